```python
import jax, jax.numpy as jnp
from jax import lax
import numpy as np

D_MODEL = 2048
BATCH = 4
SEQ = 2048
DEPTH = 1
DEC_BATCH = 128
DEC_SEQ = 8
PAST_LEN = 16384
PAGE_SIZE = 128

N_META = 16
MIX_W = D_MODEL
RET_HEADS = 4
RET_DK = (MIX_W // 2) // RET_HEADS
RET_DV = (MIX_W // 2) // RET_HEADS
RET_CHUNK = 128
RET_THETA = 10000.0
SWA_HD = 64
SWA_HEADS = (MIX_W // 2) // SWA_HD
SWA_KV_HEADS = 2
SWA_GROUP = SWA_HEADS // SWA_KV_HEADS
WINDOW = 128
BUF_LEN = min(WINDOW, PAST_LEN)
ROPE_THETA = 500000.0
ROT_DIM = SWA_HD // 4
FFN_HIDDEN = -(-8 * D_MODEL // (3 * 256)) * 256
ALPHA = (2.0 * DEPTH) ** 0.25
BETA = (8.0 * DEPTH) ** -0.25
LN_EPS = 1e-5
NEG_INF = -1e30

PROJ_SIZES = (RET_HEADS * RET_DK, RET_HEADS * RET_DK, RET_HEADS * RET_DV, RET_HEADS * RET_DV,
              SWA_HEADS * SWA_HD, SWA_KV_HEADS * SWA_HD, SWA_KV_HEADS * SWA_HD)
PROJ_W = sum(PROJ_SIZES)

kernel_name = "hymba_retention_swa_sink_deepnorm_step"


def layer_norm(x, g, b):
    xf = x.astype(jnp.float32)
    mu = xf.mean(-1, keepdims=True)
    var = jnp.square(xf - mu).mean(-1, keepdims=True)
    return ((xf - mu) * lax.rsqrt(var + LN_EPS) * g + b).astype(x.dtype)


def head_group_norm(o):
    mu = o.mean(-1, keepdims=True)
    var = jnp.square(o - mu).mean(-1, keepdims=True)
    return (o - mu) * lax.rsqrt(var + LN_EPS)


def rope(x, pos, inv_freq):
    ang = pos.astype(jnp.float32)[:, None] * inv_freq[None, :]
    cos = jnp.cos(ang)[:, None, :]
    sin = jnp.sin(ang)[:, None, :]
    x1, x2 = jnp.split(x.astype(jnp.float32), 2, axis=-1)
    return jnp.concatenate([x1 * cos - x2 * sin, x2 * cos + x1 * sin], -1).astype(x.dtype)


def mixer_projections(x, pos, w_in):
    B, T, _ = x.shape
    z = jnp.einsum('btd,de->bte', x, w_in)
    split_at = np.cumsum(PROJ_SIZES)[:-1].tolist()
    rq, rk, rv, rg, sq, sk, sv = jnp.split(z, split_at, axis=-1)
    ret_freq = jnp.power(RET_THETA, -jnp.linspace(0.0, 1.0, RET_DK // 2, dtype=jnp.float32))
    rq = rope(rq.reshape(B, T, RET_HEADS, RET_DK), pos, ret_freq)
    rk = rope(rk.reshape(B, T, RET_HEADS, RET_DK), pos, ret_freq) * (RET_DK ** -0.5)
    rv = rv.reshape(B, T, RET_HEADS, RET_DV)
    rg = rg.reshape(B, T, RET_HEADS, RET_DV)
    swa_freq = jnp.power(ROPE_THETA, -jnp.arange(0, ROT_DIM, 2, dtype=jnp.float32) / ROT_DIM)
    sq = sq.reshape(B, T, SWA_HEADS, SWA_HD)
    sk = sk.reshape(B, T, SWA_KV_HEADS, SWA_HD)
    sq = jnp.concatenate([rope(sq[..., :ROT_DIM], pos, swa_freq), sq[..., ROT_DIM:]], -1)
    sk = jnp.concatenate([rope(sk[..., :ROT_DIM], pos, swa_freq), sk[..., ROT_DIM:]], -1)
    sv = sv.reshape(B, T, SWA_KV_HEADS, SWA_HD)
    return rq, rk, rv, rg, sq, sk, sv


def retention_log_gamma():
    return jnp.log(1.0 - jnp.power(2.0, -5.0 - jnp.arange(RET_HEADS, dtype=jnp.float32)))


def retention_chunk(q, k, v, s_prev, log_gamma):
    C = q.shape[1]
    qf, kf, vf = (a.astype(jnp.float32) for a in (q, k, v))
    s_prev = s_prev.astype(jnp.float32)
    idx = jnp.arange(C, dtype=jnp.float32)
    rel = idx[:, None] - idx[None, :]
    decay = jnp.where(rel[None] >= 0,
                      jnp.exp(jnp.maximum(rel, 0.0)[None] * log_gamma[:, None, None]), 0.0)
    scores = jnp.einsum('bihd,bjhd->bhij', qf, kf) * decay[None]
    inner = jnp.einsum('bhij,bjhe->bihe', scores, vf)
    q_decay = jnp.exp((idx + 1.0)[:, None] * log_gamma[None, :])
    cross = jnp.einsum('bihd,bhde->bihe', qf, s_prev) * q_decay[None, :, :, None]
    k_decay = jnp.exp((C - 1.0 - idx)[:, None] * log_gamma[None, :])
    s_new = (jnp.exp(C * log_gamma)[None, :, None, None] * s_prev
             + jnp.einsum('bjhd,bjhe,jh->bhde', kf, vf, k_decay))
    return inner + cross, s_new


def retention_prompt(q, k, v, log_gamma):
    B, L = q.shape[:2]
    s0 = jnp.zeros((B, RET_HEADS, RET_DK, RET_DV), jnp.float32)
    o_meta, s = retention_chunk(q[:, :N_META], k[:, :N_META], v[:, :N_META], s0, log_gamma)
    n_chunks = (L - N_META) // RET_CHUNK

    def to_chunks(a):
        return a[:, N_META:].reshape(B, n_chunks, RET_CHUNK, *a.shape[2:]).swapaxes(0, 1)

    def step(s_c, qkv):
        o_c, s_c = retention_chunk(qkv[0], qkv[1], qkv[2], s_c, log_gamma)
        return s_c, o_c

    s, o = lax.scan(step, s, (to_chunks(q), to_chunks(k), to_chunks(v)))
    o = o.swapaxes(0, 1).reshape(B, L - N_META, RET_HEADS, RET_DV)
    return jnp.concatenate([o_meta, o], axis=1), s


def sink_softmax(logits, mask, sink):
    logits = jnp.where(mask, logits, NEG_INF)
    m = jnp.maximum(logits.max(-1, keepdims=True), sink)
    p = jnp.exp(logits - m)
    return p / (p.sum(-1, keepdims=True) + jnp.exp(sink - m))


def swa_prompt(q, k, v, sinks):
    B, L = q.shape[:2]
    nb = -(-L // WINDOW)
    Lp = nb * WINDOW
    pad_end = Lp - L
    qb = jnp.pad(q, ((0, 0), (0, pad_end), (0, 0), (0, 0))).reshape(
        B, nb, WINDOW, SWA_KV_HEADS, SWA_GROUP, SWA_HD)

    def band(a):
        ap = jnp.pad(a, ((0, 0), (WINDOW, pad_end), (0, 0), (0, 0)))
        prev = ap[:, :Lp].reshape(B, nb, WINDOW, SWA_KV_HEADS, SWA_HD)
        cur = ap[:, WINDOW:].reshape(B, nb, WINDOW, SWA_KV_HEADS, SWA_HD)
        return jnp.concatenate([prev, cur], axis=2)

    kb, vb = band(k), band(v)
    blk = jnp.arange(nb)[:, None] * WINDOW
    qpos = blk + jnp.arange(WINDOW)[None]
    kpos = blk - WINDOW + jnp.arange(2 * WINDOW)[None]
    diff = qpos[:, :, None] - kpos[:, None, :]
    mask = (diff >= 0) & (diff < WINDOW) & (kpos[:, None, :] >= 0)
    logits = jnp.einsum('bnqgrd,bnkgd->bngrqk', qb.astype(jnp.float32),
                        kb.astype(jnp.float32)) * (SWA_HD ** -0.5)
    sink = sinks.astype(jnp.float32).reshape(SWA_KV_HEADS, SWA_GROUP)[None, None, :, :, None, None]
    probs = sink_softmax(logits, mask[None, :, None, None], sink)
    o = jnp.einsum('bngrqk,bnkgd->bnqgrd', probs, vb.astype(jnp.float32))
    return o.reshape(B, Lp, SWA_HEADS * SWA_HD)[:, :L].astype(v.dtype)


def swa_sample(q, k_new, v_new, k_buf, v_buf, sinks):
    B, T = q.shape[:2]
    kk = jnp.concatenate([k_buf.astype(k_new.dtype), k_new], axis=1)
    vv = jnp.concatenate([v_buf.astype(v_new.dtype), v_new], axis=1)
    qpos = BUF_LEN + jnp.arange(T)
    kpos = jnp.arange(BUF_LEN + T)
    diff = qpos[:, None] - kpos[None, :]
    mask = (diff >= 0) & (diff < WINDOW)
    qg = q.reshape(B, T, SWA_KV_HEADS, SWA_GROUP, SWA_HD).astype(jnp.float32)
    logits = jnp.einsum('btgrd,bkgd->bgrtk', qg, kk.astype(jnp.float32)) * (SWA_HD ** -0.5)
    sink = sinks.astype(jnp.float32).reshape(SWA_KV_HEADS, SWA_GROUP)[None, :, :, None, None]
    probs = sink_softmax(logits, mask[None, None, None], sink)
    o = jnp.einsum('bgrtk,bkgd->btgrd', probs, vv.astype(jnp.float32))
    return (o.reshape(B, T, SWA_HEADS * SWA_HD).astype(v_new.dtype),
            kk[:, -BUF_LEN:], vv[:, -BUF_LEN:])


def layer_tail(x, ret_o, ret_gate, swa_o, w_out, ln_mix_g, ln_mix_b,
               w_ffn_gate, w_ffn_up, w_ffn_down, ln_ffn_g, ln_ffn_b):
    B, T, _ = x.shape
    ret_o = head_group_norm(ret_o).astype(x.dtype) * jax.nn.silu(ret_gate)
    mixed = jnp.concatenate([ret_o.reshape(B, T, -1), swa_o.reshape(B, T, -1)], axis=-1)
    h = layer_norm(ALPHA * x + mixed @ w_out, ln_mix_g, ln_mix_b)
    f = (jax.nn.silu(h @ w_ffn_gate) * (h @ w_ffn_up)) @ w_ffn_down
    return layer_norm(ALPHA * h + f, ln_ffn_g, ln_ffn_b)


def setup_inputs(seed: int = 0) -> dict:
    key = jax.random.key(seed)
    ks = jax.random.split(key, 20)
    f32 = jnp.float32
    nrm = lambda k, shape, s: jax.random.normal(k, shape, f32) * s
    return {
        "x_prompt": nrm(ks[0], (BATCH, SEQ, D_MODEL), 1.0),
        "x_sample": nrm(ks[1], (DEC_BATCH, DEC_SEQ, D_MODEL), 1.0),
        "state_ret": nrm(ks[2], (DEPTH, DEC_BATCH, RET_HEADS, RET_DK, RET_DV), 0.5),
        "cache_swa_k": nrm(ks[3], (DEPTH, DEC_BATCH, BUF_LEN, SWA_KV_HEADS, SWA_HD), 1.0),
        "cache_swa_v": nrm(ks[4], (DEPTH, DEC_BATCH, BUF_LEN, SWA_KV_HEADS, SWA_HD), 1.0),
        "meta_tokens": nrm(ks[5], (N_META, D_MODEL), 1.0),
        "ln_emb_g": 1.0 + nrm(ks[6], (D_MODEL,), 0.02),
        "ln_emb_b": nrm(ks[7], (D_MODEL,), 0.02),
        "w_in": nrm(ks[8], (DEPTH, D_MODEL, PROJ_W), D_MODEL ** -0.5),
        "w_out": nrm(ks[9], (DEPTH, MIX_W, D_MODEL), BETA * MIX_W ** -0.5),
        "swa_sinks": nrm(ks[10], (DEPTH, SWA_HEADS), 0.5),
        "ln_mix_g": 1.0 + nrm(ks[11], (DEPTH, D_MODEL), 0.02),
        "ln_mix_b": nrm(ks[12], (DEPTH, D_MODEL), 0.02),
        "w_ffn_gate": nrm(ks[13], (DEPTH, D_MODEL, FFN_HIDDEN), D_MODEL ** -0.5),
        "w_ffn_up": nrm(ks[14], (DEPTH, D_MODEL, FFN_HIDDEN), D_MODEL ** -0.5),
        "w_ffn_down": nrm(ks[15], (DEPTH, FFN_HIDDEN, D_MODEL), BETA * FFN_HIDDEN ** -0.5),
        "ln_ffn_g": 1.0 + nrm(ks[16], (DEPTH, D_MODEL), 0.02),
        "ln_ffn_b": nrm(ks[17], (DEPTH, D_MODEL), 0.02),
    }


def reference(x_prompt, x_sample, state_ret, cache_swa_k, cache_swa_v, meta_tokens,
              ln_emb_g, ln_emb_b, w_in, w_out, swa_sinks, ln_mix_g, ln_mix_b,
              w_ffn_gate, w_ffn_up, w_ffn_down, ln_ffn_g, ln_ffn_b):
    B, S, D = x_prompt.shape
    T = x_sample.shape[1]
    log_gamma = retention_log_gamma()
    meta = jnp.broadcast_to(meta_tokens.astype(x_prompt.dtype)[None], (B, N_META, D))
    h_p = layer_norm(jnp.concatenate([meta, x_prompt], axis=1), ln_emb_g, ln_emb_b)
    h_s = layer_norm(x_sample, ln_emb_g, ln_emb_b)
    pos_p = jnp.arange(N_META + S)
    pos_s = PAST_LEN + jnp.arange(T)
    ret_p, k_p, v_p, ret_s, k_s, v_s = [], [], [], [], [], []
    for l in range(DEPTH):
        rq, rk, rv, rg, sq, sk, sv = mixer_projections(h_p, pos_p, w_in[l])
        ret_o, s_new = retention_prompt(rq, rk, rv, log_gamma)
        swa_o = swa_prompt(sq, sk, sv, swa_sinks[l])
        ret_p.append(s_new.astype(h_p.dtype))
        k_p.append(sk[:, -BUF_LEN:])
        v_p.append(sv[:, -BUF_LEN:])
        h_p = layer_tail(h_p, ret_o, rg, swa_o, w_out[l], ln_mix_g[l], ln_mix_b[l],
                         w_ffn_gate[l], w_ffn_up[l], w_ffn_down[l], ln_ffn_g[l], ln_ffn_b[l])
        rq, rk, rv, rg, sq, sk, sv = mixer_projections(h_s, pos_s, w_in[l])
        ret_o, s_new = retention_chunk(rq, rk, rv, state_ret[l], log_gamma)
        swa_o, kb_new, vb_new = swa_sample(sq, sk, sv, cache_swa_k[l], cache_swa_v[l], swa_sinks[l])
        ret_s.append(s_new.astype(h_s.dtype))
        k_s.append(kb_new)
        v_s.append(vb_new)
        h_s = layer_tail(h_s, ret_o, rg, swa_o, w_out[l], ln_mix_g[l], ln_mix_b[l],
                         w_ffn_gate[l], w_ffn_up[l], w_ffn_down[l], ln_ffn_g[l], ln_ffn_b[l])
    y_prompt = h_p[:, N_META:]
    y_sample = h_s
    return (y_prompt, y_sample, jnp.stack(ret_p), jnp.stack(k_p), jnp.stack(v_p),
            jnp.stack(ret_s), jnp.stack(k_s), jnp.stack(v_s))
```

```python
import functools

import jax
import jax.numpy as jnp
from jax import lax
from jax.experimental import pallas as pl
from jax.experimental.pallas import tpu as pltpu

F32 = jnp.float32
BF16 = jnp.bfloat16

D_MODEL = 2048
N_META = 16
RET_HEADS = 4
RET_DK = 256
RET_DV = 256
RET_CHUNK = 128
RET_THETA = 10000.0
SWA_HD = 64
SWA_HEADS = 16
SWA_KV_HEADS = 2
SWA_GROUP = SWA_HEADS // SWA_KV_HEADS
WINDOW = 128
ROPE_THETA = 500000.0
ROT_DIM = SWA_HD // 4
FFN_HIDDEN = 5632
PROJ_W = 5376
DEPTH = 1
ALPHA = (2.0 * DEPTH) ** 0.25
LN_EPS = 1e-5
NEG_INF = -1e30

LANES = 128
RET_W = RET_HEADS * RET_DK
SWA_Q_COL = 4 * RET_W
SWA_KV_COL = SWA_Q_COL + SWA_HEADS * SWA_HD
VMEM_LIMIT = 56 * 1024 * 1024


def _cparams(sem):
    return pltpu.CompilerParams(dimension_semantics=sem, vmem_limit_bytes=VMEM_LIMIT)


def _layer_norm(x, g, b):
    mu = jnp.mean(x, axis=-1, keepdims=True)
    xc = x - mu
    var = jnp.mean(xc * xc, axis=-1, keepdims=True)
    return xc * lax.rsqrt(var + LN_EPS) * g + b


def _silu(x):
    return x / (1.0 + jnp.exp(-x))


def _dot(a, b):
    return jnp.dot(a, b, preferred_element_type=F32)


def _dot_nt(a, b):
    return lax.dot_general(a, b, (((1,), (1,)), ((), ())), preferred_element_type=F32)


def _dot_tn(a, b):
    return lax.dot_general(a, b, (((0,), (0,)), ((), ())), preferred_element_type=F32)


def _ln_proj_kernel(x_ref, g_ref, b_ref, w_ref, z_ref, h_scr):
    @pl.when(pl.program_id(1) == 0)
    def _():
        h_scr[...] = _layer_norm(x_ref[...], g_ref[...], b_ref[...]).astype(BF16)

    z_ref[...] = _dot(h_scr[...], w_ref[...])


def _ln_proj(x, g, b, w, tm, tn):
    n = x.shape[0]
    return pl.pallas_call(
        _ln_proj_kernel,
        grid=(n // tm, PROJ_W // tn),
        in_specs=[
            pl.BlockSpec((tm, D_MODEL), lambda i, j: (i, 0)),
            pl.BlockSpec((1, D_MODEL), lambda i, j: (0, 0)),
            pl.BlockSpec((1, D_MODEL), lambda i, j: (0, 0)),
            pl.BlockSpec((D_MODEL, tn), lambda i, j: (0, j)),
        ],
        out_specs=pl.BlockSpec((tm, tn), lambda i, j: (i, j)),
        out_shape=jax.ShapeDtypeStruct((n, PROJ_W), F32),
        scratch_shapes=[pltpu.VMEM((tm, D_MODEL), BF16)],
        compiler_params=_cparams(("parallel", "arbitrary")),
        name="ln_proj",
    )(x, g, b, w)


def _ret_rope(x, cos, sin):
    x1 = x[:, :LANES]
    x2 = x[:, LANES:]
    return jnp.concatenate([x1 * cos - x2 * sin, x2 * cos + x1 * sin], axis=1)


def _group_norm_gate(o, gate):
    mu = jnp.mean(o, axis=-1, keepdims=True)
    oc = o - mu
    var = jnp.mean(oc * oc, axis=-1, keepdims=True)
    return oc * lax.rsqrt(var + LN_EPS) * _silu(gate)


def _ret_prompt_kernel(lg_ref, q_ref, k_ref, v_ref, g_ref, km_ref, vm_ref,
                       cos_ref, sin_ref, cosm_ref, sinm_ref, o_ref, s_ref, s_scr):
    C = RET_CHUNK
    lg = lg_ref[pl.program_id(1)]
    ri = lax.broadcasted_iota(jnp.int32, (C, C), 0)
    ci = lax.broadcasted_iota(jnp.int32, (C, C), 1)
    rel = (ri - ci).astype(F32)
    decay = jnp.where(rel >= 0.0, jnp.exp(jnp.maximum(rel, 0.0) * lg), 0.0)
    row = lax.broadcasted_iota(jnp.int32, (C, 1), 0).astype(F32)
    q_decay = jnp.exp((row + 1.0) * lg)
    k_decay = jnp.exp((C - 1.0 - row) * lg)
    chunk_decay = jnp.exp(jnp.full((1, RET_DV), C * lg, F32))
    scale = RET_DK ** -0.5

    meta_decay = jnp.where(row >= C - N_META, k_decay, 0.0)
    km = _ret_rope(km_ref[...], cosm_ref[...], sinm_ref[...]) * scale
    s_scr[...] = _dot_tn((km * meta_decay).astype(BF16), vm_ref[...].astype(BF16))

    def chunk(c, carry):
        rows = pl.ds(pl.multiple_of(c * C, C), C)
        cos = cos_ref[rows, :]
        sin = sin_ref[rows, :]
        q = _ret_rope(q_ref[rows, :], cos, sin)
        k = _ret_rope(k_ref[rows, :], cos, sin) * scale
        qb = q.astype(BF16)
        vb = v_ref[rows, :].astype(BF16)
        s_prev = s_scr[...]
        scores = _dot_nt(qb, k.astype(BF16)) * decay
        inner = _dot(scores.astype(BF16), vb)
        cross = _dot(qb, s_prev.astype(BF16)) * q_decay
        s_scr[...] = chunk_decay * s_prev + _dot_tn((k * k_decay).astype(BF16), vb)
        o_ref[rows, :] = _group_norm_gate(inner + cross, g_ref[rows, :]).astype(BF16)
        return carry

    lax.fori_loop(0, q_ref.shape[0] // C, chunk, 0)
    s_ref[...] = s_scr[...]


def _ret_prompt(lg, z, zm, cos, sin, cosm, sinm):
    bsz, seq, _ = z.shape
    hb = RET_DK // RET_DK
    col = lambda base: (lambda b, h: (b, 0, base + h * hb))
    mcol = lambda base: (lambda b, h: (0, base + h * hb))
    full = lambda b, h: (0, 0)
    return pl.pallas_call(
        _ret_prompt_kernel,
        grid=(bsz, RET_HEADS),
        in_specs=[
            pl.BlockSpec(memory_space=pltpu.SMEM),
            pl.BlockSpec((None, seq, RET_DK), col(0)),
            pl.BlockSpec((None, seq, RET_DK), col(RET_HEADS)),
            pl.BlockSpec((None, seq, RET_DV), col(2 * RET_HEADS)),
            pl.BlockSpec((None, seq, RET_DV), col(3 * RET_HEADS)),
            pl.BlockSpec((RET_CHUNK, RET_DK), mcol(RET_HEADS)),
            pl.BlockSpec((RET_CHUNK, RET_DV), mcol(2 * RET_HEADS)),
            pl.BlockSpec((seq, LANES), full),
            pl.BlockSpec((seq, LANES), full),
            pl.BlockSpec((RET_CHUNK, LANES), full),
            pl.BlockSpec((RET_CHUNK, LANES), full),
        ],
        out_specs=[
            pl.BlockSpec((None, seq, RET_DV), lambda b, h: (b, 0, h)),
            pl.BlockSpec((None, None, RET_DK, RET_DV), lambda b, h: (b, h, 0, 0)),
        ],
        out_shape=[
            jax.ShapeDtypeStruct((bsz, seq, RET_W), BF16),
            jax.ShapeDtypeStruct((bsz, RET_HEADS, RET_DK, RET_DV), F32),
        ],
        scratch_shapes=[pltpu.VMEM((RET_DK, RET_DV), F32)],
        compiler_params=_cparams(("parallel", "parallel")),
        name="ret_prompt",
    )(lg, z, z, z, z, zm, zm, cos, sin, cosm, sinm)


def _ret_sample_kernel(lg_ref, z_ref, s_ref, cos_ref, sin_ref, o_ref, so_ref, *, nb, t):
    rows = nb * t
    R = RET_HEADS * rows
    scale = RET_DK ** -0.5
    cos = cos_ref[...]
    sin = sin_ref[...]

    def stack(base, rope, mul=1.0):
        parts = []
        for h in range(RET_HEADS):
            x = z_ref[:, base + h * RET_DK: base + (h + 1) * RET_DK]
            parts.append(_ret_rope(x, cos, sin) * mul if rope else x)
        return jnp.concatenate(parts, axis=0)

    q = stack(0, True)
    k = stack(RET_W, True, scale)
    v = stack(2 * RET_W, False)
    gate = stack(3 * RET_W, False)

    ri = lax.broadcasted_iota(jnp.int32, (R, R), 0)
    ci = lax.broadcasted_iota(jnp.int32, (R, R), 1)
    rcol = lax.broadcasted_iota(jnp.int32, (R, 1), 0)
    lg_col = jnp.zeros((R, 1), F32)
    for h in range(RET_HEADS):
        lg_col = jnp.where((rcol >= h * rows) & (rcol < (h + 1) * rows), lg_ref[h], lg_col)
    tcol = (rcol & (t - 1)).astype(F32)
    rel = (ri - ci).astype(F32)
    same = ((ri & -t) == (ci & -t)) & (ri >= ci)
    decay = jnp.where(same, jnp.exp(jnp.maximum(rel, 0.0) * lg_col), 0.0)
    q_decay = jnp.exp((tcol + 1.0) * lg_col)
    k_decay = jnp.exp((t - 1.0 - tcol) * lg_col)

    qb = q.astype(BF16)
    vb = v.astype(BF16)
    scores = _dot_nt(qb, k.astype(BF16)) * decay
    inner = _dot(scores.astype(BF16), vb)
    kw = k * k_decay

    cross_parts = []
    for h in range(RET_HEADS):
        step_decay = jnp.exp(jnp.full((1, RET_DV), t * lg_ref[h], F32))
        for db in range(nb):
            r0 = h * rows + db * t
            s_prev = s_ref[db, h]
            cross_parts.append(_dot(q[r0:r0 + t, :].astype(BF16), s_prev.astype(BF16)))
            mine = (rcol >= r0) & (rcol < r0 + t)
            upd = _dot_tn(jnp.where(mine, kw, 0.0).astype(BF16), vb)
            so_ref[db, h] = step_decay * s_prev + upd
    cross = jnp.concatenate(cross_parts, axis=0) * q_decay
    out = _group_norm_gate(inner + cross, gate).astype(BF16)
    for h in range(RET_HEADS):
        o_ref[:, h * RET_DV:(h + 1) * RET_DV] = out[h * rows:(h + 1) * rows, :]


def _ret_sample(lg, z, state, cos, sin, nb, t):
    n = z.shape[0]
    dbs = n // t
    rows = nb * t
    return pl.pallas_call(
        functools.partial(_ret_sample_kernel, nb=nb, t=t),
        grid=(dbs // nb,),
        in_specs=[
            pl.BlockSpec(memory_space=pltpu.SMEM),
            pl.BlockSpec((rows, 4 * RET_W), lambda i: (i, 0)),
            pl.BlockSpec((nb, RET_HEADS, RET_DK, RET_DV), lambda i: (i, 0, 0, 0)),
            pl.BlockSpec((rows, LANES), lambda i: (0, 0)),
            pl.BlockSpec((rows, LANES), lambda i: (0, 0)),
        ],
        out_specs=[
            pl.BlockSpec((rows, RET_W), lambda i: (i, 0)),
            pl.BlockSpec((nb, RET_HEADS, RET_DK, RET_DV), lambda i: (i, 0, 0, 0)),
        ],
        out_shape=[
            jax.ShapeDtypeStruct((n, RET_W), BF16),
            jax.ShapeDtypeStruct(state.shape, F32),
        ],
        compiler_params=_cparams(("parallel",)),
        name="ret_sample",
    )(lg, z, state, cos, sin)


def _swa_rope(x, c, s1, s2):
    return x * c + pltpu.roll(x, 8, 1) * s1 + pltpu.roll(x, LANES - 8, 1) * s2


def _dup_head(x, g, low):
    swapped = pltpu.roll(x, SWA_HD, 1)
    return jnp.where(low, x, swapped) if g == 0 else jnp.where(low, swapped, x)


def _sink_attention(qm, kd, vd, mask, sink):
    logits = _dot_nt(qm, kd) * (SWA_HD ** -0.5)
    logits = jnp.where(mask, logits, NEG_INF)
    m = jnp.maximum(jnp.max(logits, axis=-1, keepdims=True), sink)
    p = jnp.exp(logits - m)
    den = jnp.sum(p, axis=-1, keepdims=True) + jnp.exp(sink - m)
    return _dot(p.astype(BF16), vd) / den


def _swa_prompt_kernel(sink_ref, q_ref, kv_ref, kvm_ref, c_ref, s1_ref, s2_ref,
                       cm_ref, s1m_ref, s2m_ref, o_ref, kp_ref, vp_ref, kprev, vprev):
    W = WINDOW
    m_id = pl.program_id(1)

    @pl.when(m_id == 0)
    def _():
        kprev[...] = _swa_rope(kvm_ref[:, :LANES], cm_ref[...], s1m_ref[...], s2m_ref[...])
        vprev[...] = kvm_ref[:, LANES:]

    c = c_ref[...]
    s1 = s1_ref[...]
    s2 = s2_ref[...]
    kcur = _swa_rope(kv_ref[:, :LANES], c, s1, s2)
    vcur = kv_ref[:, LANES:]
    k2 = jnp.concatenate([kprev[...], kcur], axis=0)
    v2 = jnp.concatenate([vprev[...], vcur], axis=0)

    r = lax.broadcasted_iota(jnp.int32, (W, 2 * W), 0)
    cc = lax.broadcasted_iota(jnp.int32, (W, 2 * W), 1)
    diff = W + r - cc
    mask = (diff >= 0) & (diff < W) & ((m_id > 0) | (cc >= W - N_META))
    low_k = lax.broadcasted_iota(jnp.int32, (2 * W, LANES), 1) < SWA_HD
    low_q = lax.broadcasted_iota(jnp.int32, (W, LANES), 1) < SWA_HD

    for g in range(SWA_KV_HEADS):
        kd = _dup_head(k2, g, low_k).astype(BF16)
        vd = _dup_head(v2, g, low_k).astype(BF16)
        for pair in range(SWA_GROUP // 2):
            p = g * (SWA_GROUP // 2) + pair
            qp = _swa_rope(q_ref[:, p * LANES:(p + 1) * LANES], c, s1, s2)
            o_lo = _sink_attention(jnp.where(low_q, qp, 0.0).astype(BF16), kd, vd, mask, sink_ref[2 * p])
            o_hi = _sink_attention(jnp.where(low_q, 0.0, qp).astype(BF16), kd, vd, mask, sink_ref[2 * p + 1])
            o_ref[:, p * LANES:(p + 1) * LANES] = jnp.where(low_q, o_lo, o_hi).astype(BF16)

    kprev[...] = kcur
    vprev[...] = vcur
    kp_ref[...] = kcur
    vp_ref[...] = vcur


def _swa_prompt(sinks, z, zm, c, s1, s2, cm, s1m, s2m):
    bsz, seq, _ = z.shape
    W = WINDOW
    qw = SWA_HEADS * SWA_HD
    tab = pl.BlockSpec((W, LANES), lambda b, m: (m, 0))
    mtab = pl.BlockSpec((W, LANES), lambda b, m: (0, 0))
    return pl.pallas_call(
        _swa_prompt_kernel,
        grid=(bsz, seq // W),
        in_specs=[
            pl.BlockSpec(memory_space=pltpu.SMEM),
            pl.BlockSpec((None, W, qw), lambda b, m: (b, m, SWA_Q_COL // qw)),
            pl.BlockSpec((None, W, 2 * LANES), lambda b, m: (b, m, SWA_KV_COL // (2 * LANES))),
            pl.BlockSpec((W, 2 * LANES), lambda b, m: (0, SWA_KV_COL // (2 * LANES))),
            tab, tab, tab, mtab, mtab, mtab,
        ],
        out_specs=[
            pl.BlockSpec((None, W, qw), lambda b, m: (b, m, 0)),
            pl.BlockSpec((None, W, LANES), lambda b, m: (b, 0, 0)),
            pl.BlockSpec((None, W, LANES), lambda b, m: (b, 0, 0)),
        ],
        out_shape=[
            jax.ShapeDtypeStruct((bsz, seq, qw), BF16),
            jax.ShapeDtypeStruct((bsz, W, LANES), F32),
            jax.ShapeDtypeStruct((bsz, W, LANES), F32),
        ],
        scratch_shapes=[pltpu.VMEM((W, LANES), F32), pltpu.VMEM((W, LANES), F32)],
        compiler_params=_cparams(("parallel", "arbitrary")),
        name="swa_prompt",
    )(sinks, z, z, zm, c, s1, s2, cm, s1m, s2m)


def _swa_sample_kernel(sink_ref, q_ref, kv_ref, kc_ref, vc_ref, c_ref, s1_ref, s2_ref,
                       o_ref, ko_ref, vo_ref, *, nb, t):
    W = WINDOW
    keys = 2 * W
    c = c_ref[...]
    s1 = s1_ref[...]
    s2 = s2_ref[...]
    knew = _swa_rope(kv_ref[:, :LANES], c, s1, s2)
    vnew = kv_ref[:, LANES:]
    pad = jnp.zeros((keys - W - t, LANES), F32)

    rq = SWA_GROUP * t
    tq = lax.broadcasted_iota(jnp.int32, (rq, keys), 0) & (t - 1)
    cc = lax.broadcasted_iota(jnp.int32, (rq, keys), 1)
    diff = W + tq - cc
    mask = (diff >= 0) & (diff < W)
    low_k = lax.broadcasted_iota(jnp.int32, (keys, LANES), 1) < SWA_HD
    low_q = lax.broadcasted_iota(jnp.int32, (t, LANES), 1) < SWA_HD
    npair = SWA_HEADS // 2

    qrot = [_swa_rope(q_ref[:, p * LANES:(p + 1) * LANES], c, s1, s2) for p in range(npair)]
    outs = [[] for _ in range(npair)]
    for db in range(nb):
        rows = slice(db * t, (db + 1) * t)
        kn = knew[rows, :]
        vn = vnew[rows, :]
        ko_ref[db, 0:W - t, :] = kc_ref[db, t:W, :]
        ko_ref[db, W - t:W, :] = kn
        vo_ref[db, 0:W - t, :] = vc_ref[db, t:W, :]
        vo_ref[db, W - t:W, :] = vn
        k2 = jnp.concatenate([kc_ref[db], kn, pad], axis=0)
        v2 = jnp.concatenate([vc_ref[db], vn, pad], axis=0)
        for g in range(SWA_KV_HEADS):
            kd = _dup_head(k2, g, low_k).astype(BF16)
            vd = _dup_head(v2, g, low_k).astype(BF16)
            qs, sinks = [], []
            for rr in range(SWA_GROUP):
                p = (g * SWA_GROUP + rr) // 2
                qp = qrot[p][rows, :]
                qs.append(jnp.where(low_q, qp, 0.0) if rr % 2 == 0 else jnp.where(low_q, 0.0, qp))
                sinks.append(jnp.full((t, 1), sink_ref[g * SWA_GROUP + rr], F32))
            o = _sink_attention(jnp.concatenate(qs, axis=0).astype(BF16), kd, vd, mask,
                                jnp.concatenate(sinks, axis=0))
            for pair in range(SWA_GROUP // 2):
                p = g * (SWA_GROUP // 2) + pair
                lo = o[(2 * pair) * t:(2 * pair + 1) * t, :]
                hi = o[(2 * pair + 1) * t:(2 * pair + 2) * t, :]
                outs[p].append(jnp.where(low_q, lo, hi))
    for p in range(npair):
        o_ref[:, p * LANES:(p + 1) * LANES] = jnp.concatenate(outs[p], axis=0).astype(BF16)


def _swa_sample(sinks, z, kc, vc, c, s1, s2, nb, t):
    n = z.shape[0]
    dbs = n // t
    rows = nb * t
    W = WINDOW
    qw = SWA_HEADS * SWA_HD
    tab = pl.BlockSpec((rows, LANES), lambda i: (0, 0))
    cache = pl.BlockSpec((nb, W, LANES), lambda i: (i, 0, 0))
    return pl.pallas_call(
        functools.partial(_swa_sample_kernel, nb=nb, t=t),
        grid=(dbs // nb,),
        in_specs=[
            pl.BlockSpec(memory_space=pltpu.SMEM),
            pl.BlockSpec((rows, qw), lambda i: (i, SWA_Q_COL // qw)),
            pl.BlockSpec((rows, 2 * LANES), lambda i: (i, SWA_KV_COL // (2 * LANES))),
            cache, cache, tab, tab, tab,
        ],
        out_specs=[pl.BlockSpec((rows, qw), lambda i: (i, 0)), cache, cache],
        out_shape=[
            jax.ShapeDtypeStruct((n, qw), BF16),
            jax.ShapeDtypeStruct((dbs, W, LANES), F32),
            jax.ShapeDtypeStruct((dbs, W, LANES), F32),
        ],
        compiler_params=_cparams(("parallel",)),
        name="swa_sample",
    )(sinks, z, z, kc, vc, c, s1, s2)


def _out_proj_ln_kernel(ro_ref, so_ref, x_ref, eg_ref, eb_ref, w_ref, mg_ref, mb_ref, h_ref):
    x_in = _layer_norm(x_ref[...], eg_ref[...], eb_ref[...])
    mixed = _dot(ro_ref[...], w_ref[:RET_W, :]) + _dot(so_ref[...], w_ref[RET_W:, :])
    h_ref[...] = _layer_norm(ALPHA * x_in + mixed, mg_ref[...], mb_ref[...])


def _out_proj_ln(ro, so, x, eg, eb, w, mg, mb, tm):
    n = x.shape[0]
    vec = pl.BlockSpec((1, D_MODEL), lambda i: (0, 0))
    return pl.pallas_call(
        _out_proj_ln_kernel,
        grid=(n // tm,),
        in_specs=[
            pl.BlockSpec((tm, RET_W), lambda i: (i, 0)),
            pl.BlockSpec((tm, RET_W), lambda i: (i, 0)),
            pl.BlockSpec((tm, D_MODEL), lambda i: (i, 0)),
            vec, vec,
            pl.BlockSpec((D_MODEL, D_MODEL), lambda i: (0, 0)),
            vec, vec,
        ],
        out_specs=pl.BlockSpec((tm, D_MODEL), lambda i: (i, 0)),
        out_shape=jax.ShapeDtypeStruct((n, D_MODEL), F32),
        compiler_params=_cparams(("parallel",)),
        name="out_proj_ln",
    )(ro, so, x, eg, eb, w, mg, mb)


def _ffn_ln_kernel(h_ref, wg_ref, wu_ref, wd_ref, g_ref, b_ref, y_ref, hb_scr):
    j = pl.program_id(1)

    @pl.when(j == 0)
    def _():
        hb_scr[...] = h_ref[...].astype(BF16)
        y_ref[...] = jnp.zeros_like(y_ref)

    hb = hb_scr[...]
    act = _silu(_dot(hb, wg_ref[...])) * _dot(hb, wu_ref[...])
    y_ref[...] += _dot(act.astype(BF16), wd_ref[...])

    @pl.when(j == pl.num_programs(1) - 1)
    def _():
        y_ref[...] = _layer_norm(ALPHA * h_ref[...] + y_ref[...], g_ref[...], b_ref[...])


def _ffn_ln(h, wg, wu, wd, g, b, tm, th):
    n = h.shape[0]
    vec = pl.BlockSpec((1, D_MODEL), lambda i, j: (0, 0))
    return pl.pallas_call(
        _ffn_ln_kernel,
        grid=(n // tm, FFN_HIDDEN // th),
        in_specs=[
            pl.BlockSpec((tm, D_MODEL), lambda i, j: (i, 0)),
            pl.BlockSpec((D_MODEL, th), lambda i, j: (0, j)),
            pl.BlockSpec((D_MODEL, th), lambda i, j: (0, j)),
            pl.BlockSpec((th, D_MODEL), lambda i, j: (j, 0)),
            vec, vec,
        ],
        out_specs=pl.BlockSpec((tm, D_MODEL), lambda i, j: (i, 0)),
        out_shape=jax.ShapeDtypeStruct((n, D_MODEL), F32),
        scratch_shapes=[pltpu.VMEM((tm, D_MODEL), BF16)],
        compiler_params=_cparams(("parallel", "arbitrary")),
        name="ffn_ln",
    )(h, wg, wu, wd, g, b)


def _ret_tables(pos):
    freq = jnp.power(RET_THETA, -jnp.linspace(0.0, 1.0, RET_DK // 2, dtype=F32))
    ang = pos.astype(F32)[:, None] * freq[None, :]
    return jnp.cos(ang), jnp.sin(ang)


def _swa_tables(pos):
    half = ROT_DIM // 2
    freq = jnp.power(ROPE_THETA, -jnp.arange(0, ROT_DIM, 2, dtype=F32) / ROT_DIM)
    ang = pos.astype(F32)[:, None] * freq[None, :]
    cos, sin = jnp.cos(ang), jnp.sin(ang)
    n = pos.shape[0]
    rest = SWA_HD - ROT_DIM
    c = jnp.concatenate([cos, cos, jnp.ones((n, rest), F32)], axis=1)
    s1 = jnp.concatenate([jnp.zeros((n, half), F32), sin, jnp.zeros((n, rest), F32)], axis=1)
    s2 = jnp.concatenate([-sin, jnp.zeros((n, half + rest), F32)], axis=1)
    return tuple(jnp.tile(a, (1, LANES // SWA_HD)) for a in (c, s1, s2))


def _front_pad(a, rows):
    return jnp.pad(a, ((rows - a.shape[0], 0), (0, 0)))


def kernel(x_prompt, x_sample, state_ret, cache_swa_k, cache_swa_v, meta_tokens, ln_emb_g, ln_emb_b,
           w_in, w_out, swa_sinks, ln_mix_g, ln_mix_b, w_ffn_gate, w_ffn_up, w_ffn_down, ln_ffn_g, ln_ffn_b):
    bsz, seq, d = x_prompt.shape
    dbs, t, _ = x_sample.shape
    assert w_in.shape[0] == DEPTH and d == D_MODEL and seq % RET_CHUNK == 0 and t & (t - 1) == 0
    past_len = 16384
    row = lambda a: a.reshape(1, -1)

    w_in_b = w_in[0].astype(BF16)
    w_out_b = w_out[0].astype(BF16)
    w_gate_b = w_ffn_gate[0].astype(BF16)
    w_up_b = w_ffn_up[0].astype(BF16)
    w_down_b = w_ffn_down[0].astype(BF16)
    eg, eb = row(ln_emb_g), row(ln_emb_b)
    sinks = swa_sinks[0]
    lg = jnp.log(1.0 - jnp.power(2.0, -5.0 - jnp.arange(RET_HEADS, dtype=F32)))

    xp = x_prompt.reshape(bsz * seq, d)
    xs = x_sample.reshape(dbs * t, d)
    xm = _front_pad(meta_tokens, RET_CHUNK)

    z_p = _ln_proj(xp, eg, eb, w_in_b, 512, 768).reshape(bsz, seq, PROJ_W)
    z_s = _ln_proj(xs, eg, eb, w_in_b, 512, 768)
    z_m = _ln_proj(xm, eg, eb, w_in_b, RET_CHUNK, 768)

    pos_main = N_META + jnp.arange(seq)
    pos_meta = jnp.maximum(jnp.arange(RET_CHUNK) - (RET_CHUNK - N_META), 0)
    nb_ret, nb_swa = 4, 8
    pos_s = past_len + jnp.arange(t)
    cos_p, sin_p = _ret_tables(pos_main)
    cos_m, sin_m = _ret_tables(pos_meta)
    cos_s, sin_s = (jnp.tile(a, (nb_ret, 1)) for a in _ret_tables(pos_s))
    swa_tab_p = _swa_tables(pos_main)
    swa_tab_m = _swa_tables(pos_meta)
    swa_tab_s = tuple(jnp.tile(a, (nb_swa, 1)) for a in _swa_tables(pos_s))

    ret_o_p, ret_state_p = _ret_prompt(lg, z_p, z_m, cos_p, sin_p, cos_m, sin_m)
    swa_o_p, k_p, v_p = _swa_prompt(sinks, z_p, z_m, *swa_tab_p, *swa_tab_m)
    ret_o_s, ret_state_s = _ret_sample(lg, z_s, state_ret[0], cos_s, sin_s, nb_ret, t)
    kc = cache_swa_k[0].reshape(dbs, WINDOW, LANES)
    vc = cache_swa_v[0].reshape(dbs, WINDOW, LANES)
    swa_o_s, k_s, v_s = _swa_sample(sinks, z_s, kc, vc, *swa_tab_s, nb_swa, t)

    mg, mb = row(ln_mix_g[0]), row(ln_mix_b[0])
    fg, fb = row(ln_ffn_g[0]), row(ln_ffn_b[0])
    h_p = _out_proj_ln(ret_o_p.reshape(bsz * seq, RET_W), swa_o_p.reshape(bsz * seq, -1), xp, eg, eb, w_out_b, mg, mb, 512)
    h_s = _out_proj_ln(ret_o_s, swa_o_s, xs, eg, eb, w_out_b, mg, mb, 512)
    y_p = _ffn_ln(h_p, w_gate_b, w_up_b, w_down_b, fg, fb, 512, 512)
    y_s = _ffn_ln(h_s, w_gate_b, w_up_b, w_down_b, fg, fb, 512, 512)

    kv_shape = (DEPTH, -1, WINDOW, SWA_KV_HEADS, SWA_HD)
    return (y_p.reshape(bsz, seq, d), y_s.reshape(dbs, t, d),
            ret_state_p[None], k_p.reshape(kv_shape), v_p.reshape(kv_shape),
            ret_state_s[None], k_s.reshape(kv_shape), v_s.reshape(kv_shape))
```

```python
import functools

import jax
import jax.numpy as jnp
from jax import lax
from jax.experimental import pallas as pl
from jax.experimental.pallas import tpu as pltpu

F32 = jnp.float32
BF16 = jnp.bfloat16

D_MODEL = 2048
N_META = 16
RET_HEADS = 4
RET_DK = 256
RET_DV = 256
RET_CHUNK = 128
RET_THETA = 10000.0
SWA_HD = 64
SWA_HEADS = 16
SWA_KV_HEADS = 2
SWA_GROUP = SWA_HEADS // SWA_KV_HEADS
WINDOW = 128
ROPE_THETA = 500000.0
ROT_DIM = SWA_HD // 4
FFN_HIDDEN = 5632
PROJ_W = 5376
DEPTH = 1
ALPHA = (2.0 * DEPTH) ** 0.25
LN_EPS = 1e-5
NEG_INF = -1e30

LANES = 128
RET_W = RET_HEADS * RET_DK
SWA_Q_COL = 4 * RET_W
SWA_KV_COL = SWA_Q_COL + SWA_HEADS * SWA_HD
VMEM_LIMIT = 56 * 1024 * 1024
TOK_TILE = 1024
WEIGHT_STAGE_ROWS = 256


def _cparams(sem):
    return pltpu.CompilerParams(dimension_semantics=sem, vmem_limit_bytes=VMEM_LIMIT)


def _layer_norm(x, g, b):
    mu = jnp.mean(x, axis=-1, keepdims=True)
    xc = x - mu
    var = jnp.mean(xc * xc, axis=-1, keepdims=True)
    return xc * lax.rsqrt(var + LN_EPS) * g + b


def _silu(x):
    return x / (1.0 + jnp.exp(-x))


def _dot(a, b):
    return jnp.dot(a, b, preferred_element_type=F32)


def _dot_nt(a, b):
    return lax.dot_general(a, b, (((1,), (1,)), ((), ())), preferred_element_type=F32)


def _dot_tn(a, b):
    return lax.dot_general(a, b, (((0,), (0,)), ((), ())), preferred_element_type=F32)


def _ln_proj_kernel(xp_ref, xs_ref, xm_ref, g_ref, b_ref, w_ref, z_ref, h_scr, *, np_tiles, ns_tiles):
    i = pl.program_id(0)
    first = pl.program_id(1) == 0
    meta_rows = xm_ref.shape[0]
    is_meta = i == np_tiles + ns_tiles

    def norm(x_ref):
        return _layer_norm(x_ref[...], g_ref[...], b_ref[...]).astype(BF16)

    @pl.when(first & (i < np_tiles))
    def _():
        h_scr[...] = norm(xp_ref)

    @pl.when(first & (i >= np_tiles) & jnp.logical_not(is_meta))
    def _():
        h_scr[...] = norm(xs_ref)

    @pl.when(first & is_meta)
    def _():
        h_scr[:meta_rows, :] = norm(xm_ref)

    wb = w_ref[...].astype(BF16)

    @pl.when(jnp.logical_not(is_meta))
    def _():
        z_ref[...] = _dot(h_scr[...], wb)

    @pl.when(is_meta)
    def _():
        z_ref[:meta_rows, :] = _dot(h_scr[:meta_rows, :], wb)


def _ln_proj(xp, xs, xm, g, b, w, tm, tn):
    np_tiles = xp.shape[0] // tm
    ns_tiles = xs.shape[0] // tm
    n = xp.shape[0] + xs.shape[0] + xm.shape[0]
    once = pl.Buffered(1)
    return pl.pallas_call(
        functools.partial(_ln_proj_kernel, np_tiles=np_tiles, ns_tiles=ns_tiles),
        grid=(np_tiles + ns_tiles + 1, PROJ_W // tn),
        in_specs=[
            pl.BlockSpec((tm, D_MODEL), lambda i, j: (jnp.minimum(i, np_tiles - 1), 0)),
            pl.BlockSpec((tm, D_MODEL), lambda i, j: (jnp.clip(i - np_tiles, 0, ns_tiles - 1), 0),
                         pipeline_mode=once if ns_tiles == 1 else None),
            pl.BlockSpec(xm.shape, lambda i, j: (0, 0), pipeline_mode=once),
            pl.BlockSpec((1, D_MODEL), lambda i, j: (0, 0)),
            pl.BlockSpec((1, D_MODEL), lambda i, j: (0, 0)),
            pl.BlockSpec((D_MODEL, tn), lambda i, j: (0, j)),
        ],
        out_specs=pl.BlockSpec((tm, tn), lambda i, j: (i, j)),
        out_shape=jax.ShapeDtypeStruct((n, PROJ_W), F32),
        scratch_shapes=[pltpu.VMEM((tm, D_MODEL), BF16)],
        compiler_params=_cparams(("arbitrary", "arbitrary")),
        name="ln_proj",
    )(xp, xs, xm, g, b, w)


def _ret_rope(x, cos, sin):
    x1 = x[:, :LANES]
    x2 = x[:, LANES:]
    return jnp.concatenate([x1 * cos - x2 * sin, x2 * cos + x1 * sin], axis=1)


def _group_norm_gate(o, gate):
    mu = jnp.mean(o, axis=-1, keepdims=True)
    oc = o - mu
    var = jnp.mean(oc * oc, axis=-1, keepdims=True)
    return oc * lax.rsqrt(var + LN_EPS) * _silu(gate)


def _ret_prompt_kernel(lg_ref, q_ref, k_ref, v_ref, g_ref, km_ref, vm_ref,
                       cos_ref, sin_ref, cosm_ref, sinm_ref, o_ref, s_ref, s_scr):
    C = RET_CHUNK
    lg = lg_ref[pl.program_id(1)]
    ri = lax.broadcasted_iota(jnp.int32, (C, C), 0)
    ci = lax.broadcasted_iota(jnp.int32, (C, C), 1)
    rel = (ri - ci).astype(F32)
    decay = jnp.where(rel >= 0.0, jnp.exp(jnp.maximum(rel, 0.0) * lg), 0.0)
    row = lax.broadcasted_iota(jnp.int32, (C, 1), 0).astype(F32)
    q_decay = jnp.exp((row + 1.0) * lg)
    k_decay = jnp.exp((C - 1.0 - row) * lg)
    chunk_decay = jnp.exp(jnp.full((1, RET_DV), C * lg, F32))
    scale = RET_DK ** -0.5

    meta_decay = jnp.where(row >= C - N_META, k_decay, 0.0)
    km = _ret_rope(km_ref[...], cosm_ref[...], sinm_ref[...]) * scale
    s_scr[...] = _dot_tn((km * meta_decay).astype(BF16), vm_ref[...].astype(BF16))

    def chunk(c, carry):
        rows = pl.ds(pl.multiple_of(c * C, C), C)
        cos = cos_ref[rows, :]
        sin = sin_ref[rows, :]
        q = _ret_rope(q_ref[rows, :], cos, sin)
        k = _ret_rope(k_ref[rows, :], cos, sin) * scale
        qb = q.astype(BF16)
        vb = v_ref[rows, :].astype(BF16)
        s_prev = s_scr[...]
        scores = _dot_nt(qb, k.astype(BF16)) * decay
        inner = _dot(scores.astype(BF16), vb)
        cross = _dot(qb, s_prev.astype(BF16)) * q_decay
        s_scr[...] = chunk_decay * s_prev + _dot_tn((k * k_decay).astype(BF16), vb)
        o_ref[rows, :] = _group_norm_gate(inner + cross, g_ref[rows, :]).astype(BF16)
        return carry

    lax.fori_loop(0, q_ref.shape[0] // C, chunk, 0)
    s_ref[...] = s_scr[...]


def _ret_prompt(lg, z, bsz, seq, meta_row, cos, sin, cosm, sinm):
    col = lambda base: (lambda b, h: (b, base + h))
    mcol = lambda base: (lambda b, h: (meta_row // RET_CHUNK, base + h))
    full = lambda b, h: (0, 0)
    return pl.pallas_call(
        _ret_prompt_kernel,
        grid=(bsz, RET_HEADS),
        in_specs=[
            pl.BlockSpec(memory_space=pltpu.SMEM),
            pl.BlockSpec((seq, RET_DK), col(0)),
            pl.BlockSpec((seq, RET_DK), col(RET_HEADS)),
            pl.BlockSpec((seq, RET_DV), col(2 * RET_HEADS)),
            pl.BlockSpec((seq, RET_DV), col(3 * RET_HEADS)),
            pl.BlockSpec((RET_CHUNK, RET_DK), mcol(RET_HEADS)),
            pl.BlockSpec((RET_CHUNK, RET_DV), mcol(2 * RET_HEADS)),
            pl.BlockSpec((seq, LANES), full),
            pl.BlockSpec((seq, LANES), full),
            pl.BlockSpec((RET_CHUNK, LANES), full),
            pl.BlockSpec((RET_CHUNK, LANES), full),
        ],
        out_specs=[
            pl.BlockSpec((seq, RET_DV), lambda b, h: (b, h)),
            pl.BlockSpec((None, None, RET_DK, RET_DV), lambda b, h: (b, h, 0, 0)),
        ],
        out_shape=[
            jax.ShapeDtypeStruct((bsz * seq, RET_W), BF16),
            jax.ShapeDtypeStruct((bsz, RET_HEADS, RET_DK, RET_DV), F32),
        ],
        scratch_shapes=[pltpu.VMEM((RET_DK, RET_DV), F32)],
        compiler_params=_cparams(("parallel", "parallel")),
        name="ret_prompt",
    )(lg, z, z, z, z, z, z, cos, sin, cosm, sinm)


def _ret_sample_kernel(lg_ref, z_ref, s_ref, cos_ref, sin_ref, o_ref, so_ref, *, nb, t):
    rows = nb * t
    R = RET_HEADS * rows
    scale = RET_DK ** -0.5
    cos = cos_ref[...]
    sin = sin_ref[...]

    def stack(base, rope, mul=1.0):
        parts = []
        for h in range(RET_HEADS):
            x = z_ref[:, base + h * RET_DK: base + (h + 1) * RET_DK]
            parts.append(_ret_rope(x, cos, sin) * mul if rope else x)
        return jnp.concatenate(parts, axis=0)

    q = stack(0, True)
    k = stack(RET_W, True, scale)
    v = stack(2 * RET_W, False)
    gate = stack(3 * RET_W, False)

    ri = lax.broadcasted_iota(jnp.int32, (R, R), 0)
    ci = lax.broadcasted_iota(jnp.int32, (R, R), 1)
    rcol = lax.broadcasted_iota(jnp.int32, (R, 1), 0)
    lg_col = jnp.zeros((R, 1), F32)
    for h in range(RET_HEADS):
        lg_col = jnp.where((rcol >= h * rows) & (rcol < (h + 1) * rows), lg_ref[h], lg_col)
    tcol = (rcol & (t - 1)).astype(F32)
    rel = (ri - ci).astype(F32)
    same = ((ri & -t) == (ci & -t)) & (ri >= ci)
    decay = jnp.where(same, jnp.exp(jnp.maximum(rel, 0.0) * lg_col), 0.0)
    q_decay = jnp.exp((tcol + 1.0) * lg_col)
    k_decay = jnp.exp((t - 1.0 - tcol) * lg_col)

    qb = q.astype(BF16)
    vb = v.astype(BF16)
    scores = _dot_nt(qb, k.astype(BF16)) * decay
    inner = _dot(scores.astype(BF16), vb)
    kw = k * k_decay

    cross_parts = []
    for h in range(RET_HEADS):
        step_decay = jnp.exp(jnp.full((1, RET_DV), t * lg_ref[h], F32))
        for db in range(nb):
            r0 = h * rows + db * t
            s_prev = s_ref[db, h]
            cross_parts.append(_dot(q[r0:r0 + t, :].astype(BF16), s_prev.astype(BF16)))
            mine = (rcol >= r0) & (rcol < r0 + t)
            upd = _dot_tn(jnp.where(mine, kw, 0.0).astype(BF16), vb)
            so_ref[db, h] = step_decay * s_prev + upd
    cross = jnp.concatenate(cross_parts, axis=0) * q_decay
    out = _group_norm_gate(inner + cross, gate).astype(BF16)
    for h in range(RET_HEADS):
        o_ref[:, h * RET_DV:(h + 1) * RET_DV] = out[h * rows:(h + 1) * rows, :]


def _ret_sample(lg, z, row0, state, cos, sin, nb, t):
    dbs = state.shape[0]
    n = dbs * t
    rows = nb * t
    return pl.pallas_call(
        functools.partial(_ret_sample_kernel, nb=nb, t=t),
        grid=(dbs // nb,),
        in_specs=[
            pl.BlockSpec(memory_space=pltpu.SMEM),
            pl.BlockSpec((rows, 4 * RET_W), lambda i: (row0 // rows + i, 0)),
            pl.BlockSpec((nb, RET_HEADS, RET_DK, RET_DV), lambda i: (i, 0, 0, 0)),
            pl.BlockSpec((rows, LANES), lambda i: (0, 0)),
            pl.BlockSpec((rows, LANES), lambda i: (0, 0)),
        ],
        out_specs=[
            pl.BlockSpec((rows, RET_W), lambda i: (i, 0)),
            pl.BlockSpec((nb, RET_HEADS, RET_DK, RET_DV), lambda i: (i, 0, 0, 0)),
        ],
        out_shape=[
            jax.ShapeDtypeStruct((n, RET_W), BF16),
            jax.ShapeDtypeStruct(state.shape, F32),
        ],
        compiler_params=_cparams(("parallel",)),
        name="ret_sample",
    )(lg, z, state, cos, sin)


def _swa_rope(x, c, s1, s2):
    return x * c + pltpu.roll(x, 8, 1) * s1 + pltpu.roll(x, LANES - 8, 1) * s2


def _dup_head(x, g, low):
    swapped = pltpu.roll(x, SWA_HD, 1)
    return jnp.where(low, x, swapped) if g == 0 else jnp.where(low, swapped, x)


def _sink_attention(qm, kd, vd, mask, sink):
    logits = _dot_nt(qm, kd) * (SWA_HD ** -0.5)
    logits = jnp.where(mask, logits, NEG_INF)
    m = jnp.maximum(jnp.max(logits, axis=-1, keepdims=True), sink)
    p = jnp.exp(logits - m)
    den = jnp.sum(p, axis=-1, keepdims=True) + jnp.exp(sink - m)
    return _dot(p.astype(BF16), vd) / den


def _swa_prompt_kernel(sink_ref, q_ref, kv_ref, kvm_ref, c_ref, s1_ref, s2_ref,
                       cm_ref, s1m_ref, s2m_ref, o_ref, kp_ref, vp_ref, kprev, vprev):
    W = WINDOW
    m_id = pl.program_id(1)

    @pl.when(m_id == 0)
    def _():
        kprev[...] = _swa_rope(kvm_ref[:, :LANES], cm_ref[...], s1m_ref[...], s2m_ref[...])
        vprev[...] = kvm_ref[:, LANES:]

    c = c_ref[...]
    s1 = s1_ref[...]
    s2 = s2_ref[...]
    kcur = _swa_rope(kv_ref[:, :LANES], c, s1, s2)
    vcur = kv_ref[:, LANES:]
    k2 = jnp.concatenate([kprev[...], kcur], axis=0)
    v2 = jnp.concatenate([vprev[...], vcur], axis=0)

    r = lax.broadcasted_iota(jnp.int32, (W, 2 * W), 0)
    cc = lax.broadcasted_iota(jnp.int32, (W, 2 * W), 1)
    diff = W + r - cc
    mask = (diff >= 0) & (diff < W) & ((m_id > 0) | (cc >= W - N_META))
    low_k = lax.broadcasted_iota(jnp.int32, (2 * W, LANES), 1) < SWA_HD
    low_q = lax.broadcasted_iota(jnp.int32, (W, LANES), 1) < SWA_HD

    for g in range(SWA_KV_HEADS):
        kd = _dup_head(k2, g, low_k).astype(BF16)
        vd = _dup_head(v2, g, low_k).astype(BF16)
        for pair in range(SWA_GROUP // 2):
            p = g * (SWA_GROUP // 2) + pair
            qp = _swa_rope(q_ref[:, p * LANES:(p + 1) * LANES], c, s1, s2)
            o_lo = _sink_attention(jnp.where(low_q, qp, 0.0).astype(BF16), kd, vd, mask, sink_ref[2 * p])
            o_hi = _sink_attention(jnp.where(low_q, 0.0, qp).astype(BF16), kd, vd, mask, sink_ref[2 * p + 1])
            o_ref[:, p * LANES:(p + 1) * LANES] = jnp.where(low_q, o_lo, o_hi).astype(BF16)

    kprev[...] = kcur
    vprev[...] = vcur
    kp_ref[...] = kcur
    vp_ref[...] = vcur


def _swa_prompt(sinks, z, bsz, seq, meta_row, c, s1, s2, cm, s1m, s2m):
    W = WINDOW
    nblk = seq // W
    qw = SWA_HEADS * SWA_HD
    kv_col = SWA_KV_COL // (2 * LANES)
    tab = pl.BlockSpec((W, LANES), lambda b, m: (m, 0))
    mtab = pl.BlockSpec((W, LANES), lambda b, m: (0, 0))
    return pl.pallas_call(
        _swa_prompt_kernel,
        grid=(bsz, nblk),
        in_specs=[
            pl.BlockSpec(memory_space=pltpu.SMEM),
            pl.BlockSpec((W, qw), lambda b, m: (b * nblk + m, SWA_Q_COL // qw)),
            pl.BlockSpec((W, 2 * LANES), lambda b, m: (b * nblk + m, kv_col)),
            pl.BlockSpec((W, 2 * LANES), lambda b, m: (meta_row // W, kv_col)),
            tab, tab, tab, mtab, mtab, mtab,
        ],
        out_specs=[
            pl.BlockSpec((W, qw), lambda b, m: (b * nblk + m, 0)),
            pl.BlockSpec((None, W, LANES), lambda b, m: (b, 0, 0)),
            pl.BlockSpec((None, W, LANES), lambda b, m: (b, 0, 0)),
        ],
        out_shape=[
            jax.ShapeDtypeStruct((bsz * seq, qw), BF16),
            jax.ShapeDtypeStruct((bsz, W, LANES), F32),
            jax.ShapeDtypeStruct((bsz, W, LANES), F32),
        ],
        scratch_shapes=[pltpu.VMEM((W, LANES), F32), pltpu.VMEM((W, LANES), F32)],
        compiler_params=_cparams(("parallel", "arbitrary")),
        name="swa_prompt",
    )(sinks, z, z, z, c, s1, s2, cm, s1m, s2m)


def _swa_sample_kernel(sink_ref, q_ref, kv_ref, kc_ref, vc_ref, c_ref, s1_ref, s2_ref,
                       o_ref, ko_ref, vo_ref, *, nb, t):
    W = WINDOW
    keys = 2 * W
    c = c_ref[...]
    s1 = s1_ref[...]
    s2 = s2_ref[...]
    knew = _swa_rope(kv_ref[:, :LANES], c, s1, s2)
    vnew = kv_ref[:, LANES:]
    pad = jnp.zeros((keys - W - t, LANES), F32)

    rq = SWA_GROUP * t
    tq = lax.broadcasted_iota(jnp.int32, (rq, keys), 0) & (t - 1)
    cc = lax.broadcasted_iota(jnp.int32, (rq, keys), 1)
    diff = W + tq - cc
    mask = (diff >= 0) & (diff < W)
    low_k = lax.broadcasted_iota(jnp.int32, (keys, LANES), 1) < SWA_HD
    low_q = lax.broadcasted_iota(jnp.int32, (t, LANES), 1) < SWA_HD
    npair = SWA_HEADS // 2

    qrot = [_swa_rope(q_ref[:, p * LANES:(p + 1) * LANES], c, s1, s2) for p in range(npair)]
    outs = [[] for _ in range(npair)]
    for db in range(nb):
        rows = slice(db * t, (db + 1) * t)
        kn = knew[rows, :]
        vn = vnew[rows, :]
        ko_ref[db, 0:W - t, :] = kc_ref[db, t:W, :]
        ko_ref[db, W - t:W, :] = kn
        vo_ref[db, 0:W - t, :] = vc_ref[db, t:W, :]
        vo_ref[db, W - t:W, :] = vn
        k2 = jnp.concatenate([kc_ref[db], kn, pad], axis=0)
        v2 = jnp.concatenate([vc_ref[db], vn, pad], axis=0)
        for g in range(SWA_KV_HEADS):
            kd = _dup_head(k2, g, low_k).astype(BF16)
            vd = _dup_head(v2, g, low_k).astype(BF16)
            qs, sinks = [], []
            for rr in range(SWA_GROUP):
                p = (g * SWA_GROUP + rr) // 2
                qp = qrot[p][rows, :]
                qs.append(jnp.where(low_q, qp, 0.0) if rr % 2 == 0 else jnp.where(low_q, 0.0, qp))
                sinks.append(jnp.full((t, 1), sink_ref[g * SWA_GROUP + rr], F32))
            o = _sink_attention(jnp.concatenate(qs, axis=0).astype(BF16), kd, vd, mask,
                                jnp.concatenate(sinks, axis=0))
            for pair in range(SWA_GROUP // 2):
                p = g * (SWA_GROUP // 2) + pair
                lo = o[(2 * pair) * t:(2 * pair + 1) * t, :]
                hi = o[(2 * pair + 1) * t:(2 * pair + 2) * t, :]
                outs[p].append(jnp.where(low_q, lo, hi))
    for p in range(npair):
        o_ref[:, p * LANES:(p + 1) * LANES] = jnp.concatenate(outs[p], axis=0).astype(BF16)


def _swa_sample(sinks, z, row0, kc, vc, c, s1, s2, nb, t):
    dbs = kc.shape[0]
    n = dbs * t
    rows = nb * t
    W = WINDOW
    qw = SWA_HEADS * SWA_HD
    tab = pl.BlockSpec((rows, LANES), lambda i: (0, 0))
    cache = pl.BlockSpec((nb, W, LANES), lambda i: (i, 0, 0))
    return pl.pallas_call(
        functools.partial(_swa_sample_kernel, nb=nb, t=t),
        grid=(dbs // nb,),
        in_specs=[
            pl.BlockSpec(memory_space=pltpu.SMEM),
            pl.BlockSpec((rows, qw), lambda i: (row0 // rows + i, SWA_Q_COL // qw)),
            pl.BlockSpec((rows, 2 * LANES), lambda i: (row0 // rows + i, SWA_KV_COL // (2 * LANES))),
            cache, cache, tab, tab, tab,
        ],
        out_specs=[pl.BlockSpec((rows, qw), lambda i: (i, 0)), cache, cache],
        out_shape=[
            jax.ShapeDtypeStruct((n, qw), BF16),
            jax.ShapeDtypeStruct((dbs, W, LANES), F32),
            jax.ShapeDtypeStruct((dbs, W, LANES), F32),
        ],
        compiler_params=_cparams(("parallel",)),
        name="swa_sample",
    )(sinks, z, z, kc, vc, c, s1, s2)


def _out_proj_ln_kernel(ro_ref, so_ref, x_ref, eg_ref, eb_ref, w_hbm, mg_ref, mb_ref, h_ref, hb_ref,
                        wb_scr, stage, sem):
    chunk = stage.shape[1]
    n_chunks = w_hbm.shape[0] // chunk

    def weight_copy(k):
        return pltpu.make_async_copy(w_hbm.at[pl.ds(k * chunk, chunk), :], stage.at[k % 2], sem.at[k % 2])

    @pl.when(pl.program_id(0) == 0)
    def _():
        weight_copy(0).start()
        for k in range(n_chunks):
            if k + 1 < n_chunks:
                weight_copy(k + 1).start()
            weight_copy(k).wait()
            wb_scr[k * chunk:(k + 1) * chunk, :] = stage[k % 2].astype(BF16)

    x_in = _layer_norm(x_ref[...], eg_ref[...], eb_ref[...])
    mixed = _dot(ro_ref[...], wb_scr[:RET_W, :]) + _dot(so_ref[...], wb_scr[RET_W:, :])
    h = _layer_norm(ALPHA * x_in + mixed, mg_ref[...], mb_ref[...])
    h_ref[...] = h
    hb_ref[...] = h.astype(BF16)


def _out_proj_ln(ro, so, x, eg, eb, w, mg, mb, tm):
    n = x.shape[0]
    vec = pl.BlockSpec((1, D_MODEL), lambda i: (0, 0))
    tile = pl.BlockSpec((tm, D_MODEL), lambda i: (i, 0))
    return pl.pallas_call(
        _out_proj_ln_kernel,
        grid=(n // tm,),
        in_specs=[
            pl.BlockSpec((tm, RET_W), lambda i: (i, 0)),
            pl.BlockSpec((tm, RET_W), lambda i: (i, 0)),
            tile, vec, vec,
            pl.BlockSpec(memory_space=pl.ANY),
            vec, vec,
        ],
        out_specs=[tile, tile],
        out_shape=[jax.ShapeDtypeStruct((n, D_MODEL), F32), jax.ShapeDtypeStruct((n, D_MODEL), BF16)],
        scratch_shapes=[pltpu.VMEM((D_MODEL, D_MODEL), BF16),
                        pltpu.VMEM((2, WEIGHT_STAGE_ROWS, D_MODEL), F32),
                        pltpu.SemaphoreType.DMA((2,))],
        compiler_params=_cparams(("arbitrary",)),
        name="out_proj_ln",
    )(ro, so, x, eg, eb, w, mg, mb)


def _ffn_ln_kernel(hb_ref, h_hbm, wg_ref, wu_ref, wd_ref, g_ref, b_ref, y_ref, h_res, sem):
    i = pl.program_id(0)
    j = pl.program_id(1)
    tm = h_res.shape[0]
    residual_copy = pltpu.make_async_copy(h_hbm.at[pl.ds(i * tm, tm), :], h_res, sem)

    @pl.when(j == 0)
    def _():
        residual_copy.start()
        y_ref[...] = jnp.zeros_like(y_ref)

    hb = hb_ref[...]
    act = _silu(_dot(hb, wg_ref[...].astype(BF16))) * _dot(hb, wu_ref[...].astype(BF16))
    y_ref[...] += _dot(act.astype(BF16), wd_ref[...].astype(BF16))

    @pl.when(j == pl.num_programs(1) - 1)
    def _():
        residual_copy.wait()
        y_ref[...] = _layer_norm(ALPHA * h_res[...] + y_ref[...], g_ref[...], b_ref[...])


def _ffn_ln(h, hb, wg, wu, wd, g, b, tm, th):
    n = h.shape[0]
    vec = pl.BlockSpec((1, D_MODEL), lambda i, j: (0, 0))
    return pl.pallas_call(
        _ffn_ln_kernel,
        grid=(n // tm, FFN_HIDDEN // th),
        in_specs=[
            pl.BlockSpec((tm, D_MODEL), lambda i, j: (i, 0)),
            pl.BlockSpec(memory_space=pl.ANY),
            pl.BlockSpec((D_MODEL, th), lambda i, j: (0, j)),
            pl.BlockSpec((D_MODEL, th), lambda i, j: (0, j)),
            pl.BlockSpec((th, D_MODEL), lambda i, j: (j, 0)),
            vec, vec,
        ],
        out_specs=pl.BlockSpec((tm, D_MODEL), lambda i, j: (i, 0)),
        out_shape=jax.ShapeDtypeStruct((n, D_MODEL), F32),
        scratch_shapes=[pltpu.VMEM((tm, D_MODEL), F32), pltpu.SemaphoreType.DMA(())],
        compiler_params=_cparams(("arbitrary", "arbitrary")),
        name="ffn_ln",
    )(hb, h, wg, wu, wd, g, b)


def _ret_tables(pos):
    freq = jnp.power(RET_THETA, -jnp.linspace(0.0, 1.0, RET_DK // 2, dtype=F32))
    ang = pos.astype(F32)[:, None] * freq[None, :]
    return jnp.cos(ang), jnp.sin(ang)


def _swa_tables(pos):
    half = ROT_DIM // 2
    freq = jnp.power(ROPE_THETA, -jnp.arange(0, ROT_DIM, 2, dtype=F32) / ROT_DIM)
    ang = pos.astype(F32)[:, None] * freq[None, :]
    cos, sin = jnp.cos(ang), jnp.sin(ang)
    n = pos.shape[0]
    rest = SWA_HD - ROT_DIM
    c = jnp.concatenate([cos, cos, jnp.ones((n, rest), F32)], axis=1)
    s1 = jnp.concatenate([jnp.zeros((n, half), F32), sin, jnp.zeros((n, rest), F32)], axis=1)
    s2 = jnp.concatenate([-sin, jnp.zeros((n, half + rest), F32)], axis=1)
    return tuple(jnp.tile(a, (1, LANES // SWA_HD)) for a in (c, s1, s2))


def _front_pad(a, rows):
    return jnp.pad(a, ((rows - a.shape[0], 0), (0, 0)))


def kernel(x_prompt, x_sample, state_ret, cache_swa_k, cache_swa_v, meta_tokens, ln_emb_g, ln_emb_b,
           w_in, w_out, swa_sinks, ln_mix_g, ln_mix_b, w_ffn_gate, w_ffn_up, w_ffn_down, ln_ffn_g, ln_ffn_b):
    bsz, seq, d = x_prompt.shape
    dbs, t, _ = x_sample.shape
    assert w_in.shape[0] == DEPTH and d == D_MODEL and seq % RET_CHUNK == 0 and t & (t - 1) == 0
    past_len = 16384
    row = lambda a: a.reshape(1, -1)

    eg, eb = row(ln_emb_g), row(ln_emb_b)
    sinks = swa_sinks[0]
    lg = jnp.log(1.0 - jnp.power(2.0, -5.0 - jnp.arange(RET_HEADS, dtype=F32)))

    xp = x_prompt.reshape(bsz * seq, d)
    xs = x_sample.reshape(dbs * t, d)
    xm = _front_pad(meta_tokens, RET_CHUNK)

    z = _ln_proj(xp, xs, xm, eg, eb, w_in[0], TOK_TILE, 768)
    sample_row = bsz * seq
    meta_row = sample_row + dbs * t

    pos_main = N_META + jnp.arange(seq)
    pos_meta = jnp.maximum(jnp.arange(RET_CHUNK) - (RET_CHUNK - N_META), 0)
    nb_ret, nb_swa = 4, 8
    pos_s = past_len + jnp.arange(t)
    cos_p, sin_p = _ret_tables(pos_main)
    cos_m, sin_m = _ret_tables(pos_meta)
    cos_s, sin_s = (jnp.tile(a, (nb_ret, 1)) for a in _ret_tables(pos_s))
    swa_tab_p = _swa_tables(pos_main)
    swa_tab_m = _swa_tables(pos_meta)
    swa_tab_s = tuple(jnp.tile(a, (nb_swa, 1)) for a in _swa_tables(pos_s))

    ret_o_p, ret_state_p = _ret_prompt(lg, z, bsz, seq, meta_row, cos_p, sin_p, cos_m, sin_m)
    swa_o_p, k_p, v_p = _swa_prompt(sinks, z, bsz, seq, meta_row, *swa_tab_p, *swa_tab_m)
    ret_o_s, ret_state_s = _ret_sample(lg, z, sample_row, state_ret[0], cos_s, sin_s, nb_ret, t)
    kc = cache_swa_k[0].reshape(dbs, WINDOW, LANES)
    vc = cache_swa_v[0].reshape(dbs, WINDOW, LANES)
    swa_o_s, k_s, v_s = _swa_sample(sinks, z, sample_row, kc, vc, *swa_tab_s, nb_swa, t)

    mg, mb = row(ln_mix_g[0]), row(ln_mix_b[0])
    fg, fb = row(ln_ffn_g[0]), row(ln_ffn_b[0])
    ffn_w = (w_ffn_gate[0], w_ffn_up[0], w_ffn_down[0])
    h_p, hb_p = _out_proj_ln(ret_o_p, swa_o_p, xp, eg, eb, w_out[0], mg, mb, 512)
    h_s, hb_s = _out_proj_ln(ret_o_s, swa_o_s, xs, eg, eb, w_out[0], mg, mb, 512)
    y_p = _ffn_ln(h_p, hb_p, *ffn_w, fg, fb, TOK_TILE, 256)
    y_s = _ffn_ln(h_s, hb_s, *ffn_w, fg, fb, TOK_TILE, 256)

    kv_shape = (DEPTH, -1, WINDOW, SWA_KV_HEADS, SWA_HD)
    return (y_p.reshape(bsz, seq, d), y_s.reshape(dbs, t, d),
            ret_state_p[None], k_p.reshape(kv_shape), v_p.reshape(kv_shape),
            ret_state_s[None], k_s.reshape(kv_shape), v_s.reshape(kv_shape))
```

```python
import functools

import jax
import jax.numpy as jnp
from jax import lax
from jax.experimental import pallas as pl
from jax.experimental.pallas import tpu as pltpu

F32 = jnp.float32
BF16 = jnp.bfloat16

D_MODEL = 2048
N_META = 16
RET_HEADS = 4
RET_DK = 256
RET_DV = 256
RET_CHUNK = 128
RET_THETA = 10000.0
SWA_HD = 64
SWA_HEADS = 16
SWA_KV_HEADS = 2
SWA_GROUP = SWA_HEADS // SWA_KV_HEADS
WINDOW = 128
ROPE_THETA = 500000.0
ROT_DIM = SWA_HD // 4
FFN_HIDDEN = 5632
PROJ_W = 5376
DEPTH = 1
ALPHA = (2.0 * DEPTH) ** 0.25
LN_EPS = 1e-5
NEG_INF = -1e30

LANES = 128
RET_W = RET_HEADS * RET_DK
SWA_Q_COL = 4 * RET_W
SWA_KV_COL = SWA_Q_COL + SWA_HEADS * SWA_HD
VMEM_LIMIT = 56 * 1024 * 1024
TOK_TILE = 1024
WEIGHT_STAGE_ROWS = 256


def _cparams(sem):
    return pltpu.CompilerParams(dimension_semantics=sem, vmem_limit_bytes=VMEM_LIMIT)


def _layer_norm(x, g, b):
    mu = jnp.mean(x, axis=-1, keepdims=True)
    xc = x - mu
    var = jnp.mean(xc * xc, axis=-1, keepdims=True)
    return xc * lax.rsqrt(var + LN_EPS) * g + b


def _silu(x):
    return x / (1.0 + jnp.exp(-x))


def _dot(a, b):
    return jnp.dot(a, b, preferred_element_type=F32)


def _dot_nt(a, b):
    return lax.dot_general(a, b, (((1,), (1,)), ((), ())), preferred_element_type=F32)


def _dot_tn(a, b):
    return lax.dot_general(a, b, (((0,), (0,)), ((), ())), preferred_element_type=F32)


def _ln_proj_kernel(xp_ref, xs_ref, xm_ref, g_ref, b_ref, w_ref, z_ref, zm_ref, h_scr, hm_scr, *, np_tiles):
    i = pl.program_id(0)
    first = pl.program_id(1) == 0
    is_sample = i == np_tiles

    def norm(x_ref):
        return _layer_norm(x_ref[...], g_ref[...], b_ref[...]).astype(BF16)

    @pl.when(first & jnp.logical_not(is_sample))
    def _():
        h_scr[...] = norm(xp_ref)

    @pl.when(first & is_sample)
    def _():
        h_scr[...] = norm(xs_ref)
        hm_scr[...] = norm(xm_ref)

    @pl.when(jnp.logical_not(is_sample))
    def _():
        z_ref[...] = _dot(h_scr[...], w_ref[...].astype(BF16))

    @pl.when(is_sample)
    def _():
        wb = w_ref[...].astype(BF16)
        z_ref[...] = _dot(h_scr[...], wb)
        zm_ref[...] = _dot(hm_scr[...], wb)


def _ln_proj(xp, xs, xm, g, b, w, tm, tn):
    np_tiles = xp.shape[0] // tm
    assert xs.shape[0] == tm
    once = pl.Buffered(1)
    return pl.pallas_call(
        functools.partial(_ln_proj_kernel, np_tiles=np_tiles),
        grid=(np_tiles + 1, PROJ_W // tn),
        in_specs=[
            pl.BlockSpec((tm, D_MODEL), lambda i, j: (jnp.minimum(i, np_tiles - 1), 0)),
            pl.BlockSpec((tm, D_MODEL), lambda i, j: (0, 0), pipeline_mode=once),
            pl.BlockSpec(xm.shape, lambda i, j: (0, 0), pipeline_mode=once),
            pl.BlockSpec((1, D_MODEL), lambda i, j: (0, 0)),
            pl.BlockSpec((1, D_MODEL), lambda i, j: (0, 0)),
            pl.BlockSpec((D_MODEL, tn), lambda i, j: (0, j)),
        ],
        out_specs=[
            pl.BlockSpec((tm, tn), lambda i, j: (i, j)),
            pl.BlockSpec((xm.shape[0], tn), lambda i, j: (0, jnp.where(i == np_tiles, j, 0))),
        ],
        out_shape=[
            jax.ShapeDtypeStruct((xp.shape[0] + xs.shape[0], PROJ_W), F32),
            jax.ShapeDtypeStruct((xm.shape[0], PROJ_W), F32),
        ],
        scratch_shapes=[pltpu.VMEM((tm, D_MODEL), BF16), pltpu.VMEM((xm.shape[0], D_MODEL), BF16)],
        compiler_params=_cparams(("arbitrary", "arbitrary")),
        name="ln_proj",
    )(xp, xs, xm, g, b, w)


def _ret_rope(x, cos, sin):
    x1 = x[:, :LANES]
    x2 = x[:, LANES:]
    return jnp.concatenate([x1 * cos - x2 * sin, x2 * cos + x1 * sin], axis=1)


def _group_norm_gate(o, gate):
    mu = jnp.mean(o, axis=-1, keepdims=True)
    oc = o - mu
    var = jnp.mean(oc * oc, axis=-1, keepdims=True)
    return oc * lax.rsqrt(var + LN_EPS) * _silu(gate)


def _ret_prompt_kernel(lg_ref, q_ref, k_ref, v_ref, g_ref, km_ref, vm_ref,
                       cos_ref, sin_ref, cosm_ref, sinm_ref, o_ref, s_ref, s_scr):
    C = RET_CHUNK
    lg = lg_ref[pl.program_id(1)]
    ri = lax.broadcasted_iota(jnp.int32, (C, C), 0)
    ci = lax.broadcasted_iota(jnp.int32, (C, C), 1)
    rel = (ri - ci).astype(F32)
    decay = jnp.where(rel >= 0.0, jnp.exp(jnp.maximum(rel, 0.0) * lg), 0.0)
    row = lax.broadcasted_iota(jnp.int32, (C, 1), 0).astype(F32)
    q_decay = jnp.exp((row + 1.0) * lg)
    k_decay = jnp.exp((C - 1.0 - row) * lg)
    chunk_decay = jnp.exp(jnp.full((1, RET_DV), C * lg, F32))
    scale = RET_DK ** -0.5

    meta_decay = jnp.where(row >= C - N_META, k_decay, 0.0)
    km = _ret_rope(km_ref[...], cosm_ref[...], sinm_ref[...]) * scale
    s_scr[...] = _dot_tn((km * meta_decay).astype(BF16), vm_ref[...].astype(BF16))

    def chunk(c, carry):
        rows = pl.ds(pl.multiple_of(c * C, C), C)
        cos = cos_ref[rows, :]
        sin = sin_ref[rows, :]
        q = _ret_rope(q_ref[rows, :], cos, sin)
        k = _ret_rope(k_ref[rows, :], cos, sin) * scale
        qb = q.astype(BF16)
        vb = v_ref[rows, :].astype(BF16)
        s_prev = s_scr[...]
        scores = _dot_nt(qb, k.astype(BF16)) * decay
        inner = _dot(scores.astype(BF16), vb)
        cross = _dot(qb, s_prev.astype(BF16)) * q_decay
        s_scr[...] = chunk_decay * s_prev + _dot_tn((k * k_decay).astype(BF16), vb)
        o_ref[rows, :] = _group_norm_gate(inner + cross, g_ref[rows, :]).astype(BF16)
        return carry

    lax.fori_loop(0, q_ref.shape[0] // C, chunk, 0)
    s_ref[...] = s_scr[...]


def _ret_prompt(lg, z, zm, bsz, seq, cos, sin, cosm, sinm):
    col = lambda base: (lambda b, h: (b, base + h))
    mcol = lambda base: (lambda b, h: (0, base + h))
    full = lambda b, h: (0, 0)
    return pl.pallas_call(
        _ret_prompt_kernel,
        grid=(bsz, RET_HEADS),
        in_specs=[
            pl.BlockSpec(memory_space=pltpu.SMEM),
            pl.BlockSpec((seq, RET_DK), col(0)),
            pl.BlockSpec((seq, RET_DK), col(RET_HEADS)),
            pl.BlockSpec((seq, RET_DV), col(2 * RET_HEADS)),
            pl.BlockSpec((seq, RET_DV), col(3 * RET_HEADS)),
            pl.BlockSpec((RET_CHUNK, RET_DK), mcol(RET_HEADS)),
            pl.BlockSpec((RET_CHUNK, RET_DV), mcol(2 * RET_HEADS)),
            pl.BlockSpec((seq, LANES), full),
            pl.BlockSpec((seq, LANES), full),
            pl.BlockSpec((RET_CHUNK, LANES), full),
            pl.BlockSpec((RET_CHUNK, LANES), full),
        ],
        out_specs=[
            pl.BlockSpec((seq, RET_DV), lambda b, h: (b, h)),
            pl.BlockSpec((None, None, RET_DK, RET_DV), lambda b, h: (b, h, 0, 0)),
        ],
        out_shape=[
            jax.ShapeDtypeStruct((bsz * seq, RET_W), BF16),
            jax.ShapeDtypeStruct((bsz, RET_HEADS, RET_DK, RET_DV), F32),
        ],
        scratch_shapes=[pltpu.VMEM((RET_DK, RET_DV), F32)],
        compiler_params=_cparams(("parallel", "parallel")),
        name="ret_prompt",
    )(lg, z, z, z, z, zm, zm, cos, sin, cosm, sinm)


def _ret_sample_kernel(lg_ref, z_ref, s_ref, cos_ref, sin_ref, o_ref, so_ref, *, nb, t):
    rows = nb * t
    R = RET_HEADS * rows
    scale = RET_DK ** -0.5
    cos = cos_ref[...]
    sin = sin_ref[...]

    def stack(base, rope, mul=1.0):
        parts = []
        for h in range(RET_HEADS):
            x = z_ref[:, base + h * RET_DK: base + (h + 1) * RET_DK]
            parts.append(_ret_rope(x, cos, sin) * mul if rope else x)
        return jnp.concatenate(parts, axis=0)

    q = stack(0, True)
    k = stack(RET_W, True, scale)
    v = stack(2 * RET_W, False)
    gate = stack(3 * RET_W, False)

    ri = lax.broadcasted_iota(jnp.int32, (R, R), 0)
    ci = lax.broadcasted_iota(jnp.int32, (R, R), 1)
    rcol = lax.broadcasted_iota(jnp.int32, (R, 1), 0)
    lg_col = jnp.zeros((R, 1), F32)
    for h in range(RET_HEADS):
        lg_col = jnp.where((rcol >= h * rows) & (rcol < (h + 1) * rows), lg_ref[h], lg_col)
    tcol = (rcol & (t - 1)).astype(F32)
    rel = (ri - ci).astype(F32)
    same = ((ri & -t) == (ci & -t)) & (ri >= ci)
    decay = jnp.where(same, jnp.exp(jnp.maximum(rel, 0.0) * lg_col), 0.0)
    q_decay = jnp.exp((tcol + 1.0) * lg_col)
    k_decay = jnp.exp((t - 1.0 - tcol) * lg_col)

    qb = q.astype(BF16)
    vb = v.astype(BF16)
    scores = _dot_nt(qb, k.astype(BF16)) * decay
    inner = _dot(scores.astype(BF16), vb)
    kw = k * k_decay

    cross_parts = []
    for h in range(RET_HEADS):
        step_decay = jnp.exp(jnp.full((1, RET_DV), t * lg_ref[h], F32))
        for db in range(nb):
            r0 = h * rows + db * t
            s_prev = s_ref[db, h]
            cross_parts.append(_dot(q[r0:r0 + t, :].astype(BF16), s_prev.astype(BF16)))
            mine = (rcol >= r0) & (rcol < r0 + t)
            upd = _dot_tn(jnp.where(mine, kw, 0.0).astype(BF16), vb)
            so_ref[db, h] = step_decay * s_prev + upd
    cross = jnp.concatenate(cross_parts, axis=0) * q_decay
    out = _group_norm_gate(inner + cross, gate).astype(BF16)
    for h in range(RET_HEADS):
        o_ref[:, h * RET_DV:(h + 1) * RET_DV] = out[h * rows:(h + 1) * rows, :]


def _ret_sample(lg, z, row0, state, cos, sin, nb, t):
    dbs = state.shape[0]
    n = dbs * t
    rows = nb * t
    return pl.pallas_call(
        functools.partial(_ret_sample_kernel, nb=nb, t=t),
        grid=(dbs // nb,),
        in_specs=[
            pl.BlockSpec(memory_space=pltpu.SMEM),
            pl.BlockSpec((rows, 4 * RET_W), lambda i: (row0 // rows + i, 0)),
            pl.BlockSpec((nb, RET_HEADS, RET_DK, RET_DV), lambda i: (i, 0, 0, 0)),
            pl.BlockSpec((rows, LANES), lambda i: (0, 0)),
            pl.BlockSpec((rows, LANES), lambda i: (0, 0)),
        ],
        out_specs=[
            pl.BlockSpec((rows, RET_W), lambda i: (i, 0)),
            pl.BlockSpec((nb, RET_HEADS, RET_DK, RET_DV), lambda i: (i, 0, 0, 0)),
        ],
        out_shape=[
            jax.ShapeDtypeStruct((n, RET_W), BF16),
            jax.ShapeDtypeStruct(state.shape, F32),
        ],
        compiler_params=_cparams(("parallel",)),
        name="ret_sample",
    )(lg, z, state, cos, sin)


def _swa_rope(x, c, s1, s2):
    return x * c + pltpu.roll(x, 8, 1) * s1 + pltpu.roll(x, LANES - 8, 1) * s2


def _dup_head(x, g, low):
    swapped = pltpu.roll(x, SWA_HD, 1)
    return jnp.where(low, x, swapped) if g == 0 else jnp.where(low, swapped, x)


def _to_kv_half(slab, head, low):
    g = head // SWA_GROUP
    src = slab if head % 2 == g else pltpu.roll(slab, SWA_HD, 1)
    return jnp.where(low, src, 0.0) if g == 0 else jnp.where(low, 0.0, src)


def _from_kv_half(o_even, o_odd, g, low):
    if g == 0:
        return jnp.where(low, o_even, pltpu.roll(o_odd, SWA_HD, 1))
    return jnp.where(low, pltpu.roll(o_even, SWA_HD, 1), o_odd)


def _sink_softmax(logits, sink):
    m = jnp.maximum(jnp.max(logits, axis=-1, keepdims=True), sink)
    p = jnp.exp(logits - m)
    return p, jnp.sum(p, axis=-1, keepdims=True) + jnp.exp(sink - m)


def _swa_prompt_kernel(sink_ref, q_ref, kv_ref, kvm_ref, c_ref, s1_ref, s2_ref,
                       cm_ref, s1m_ref, s2m_ref, o_ref, kp_ref, vp_ref, kprev, vprev):
    W = WINDOW
    m_id = pl.program_id(1)

    @pl.when(m_id == 0)
    def _():
        kprev[...] = _swa_rope(kvm_ref[:, :LANES], cm_ref[...], s1m_ref[...], s2m_ref[...])
        vprev[...] = kvm_ref[:, LANES:]

    c = c_ref[...]
    s1 = s1_ref[...]
    s2 = s2_ref[...]
    kcur = _swa_rope(kv_ref[:, :LANES], c, s1, s2)
    vcur = kv_ref[:, LANES:]
    k2 = jnp.concatenate([kprev[...], kcur], axis=0)
    v2 = jnp.concatenate([vprev[...], vcur], axis=0)

    r = lax.broadcasted_iota(jnp.int32, (W, 2 * W), 0)
    cc = lax.broadcasted_iota(jnp.int32, (W, 2 * W), 1)
    diff = W + r - cc
    mask = (diff >= 0) & (diff < W) & ((m_id > 0) | (cc >= W - N_META))
    low_k = lax.broadcasted_iota(jnp.int32, (2 * W, LANES), 1) < SWA_HD
    low = lax.broadcasted_iota(jnp.int32, (W, LANES), 1) < SWA_HD
    scale = SWA_HD ** -0.5

    def attend(q, kd, vd, h):
        p, den = _sink_softmax(jnp.where(mask, _dot_nt(q.astype(BF16), kd), NEG_INF), sink_ref[h])
        return _dot(p.astype(BF16), vd) / den

    for g in range(SWA_KV_HEADS):
        kd = _dup_head(k2, g, low_k).astype(BF16)
        vd = _dup_head(v2, g, low_k).astype(BF16)
        for p in range(g * SWA_GROUP // 2, (g + 1) * SWA_GROUP // 2):
            slab = _swa_rope(q_ref[:, p * LANES:(p + 1) * LANES], c, s1, s2) * scale
            o_even = attend(jnp.where(low, slab, 0.0), kd, vd, 2 * p)
            o_odd = attend(jnp.where(low, 0.0, slab), kd, vd, 2 * p + 1)
            o_ref[:, p * LANES:(p + 1) * LANES] = jnp.where(low, o_even, o_odd).astype(BF16)

    kprev[...] = kcur
    vprev[...] = vcur
    kp_ref[...] = kcur
    vp_ref[...] = vcur


def _swa_prompt(sinks, z, zm, bsz, seq, c, s1, s2, cm, s1m, s2m):
    W = WINDOW
    nblk = seq // W
    qw = SWA_HEADS * SWA_HD
    kv_col = SWA_KV_COL // (2 * LANES)
    tab = pl.BlockSpec((W, LANES), lambda b, m: (m, 0))
    mtab = pl.BlockSpec((W, LANES), lambda b, m: (0, 0))
    return pl.pallas_call(
        _swa_prompt_kernel,
        grid=(bsz, nblk),
        in_specs=[
            pl.BlockSpec(memory_space=pltpu.SMEM),
            pl.BlockSpec((W, qw), lambda b, m: (b * nblk + m, SWA_Q_COL // qw)),
            pl.BlockSpec((W, 2 * LANES), lambda b, m: (b * nblk + m, kv_col)),
            pl.BlockSpec((W, 2 * LANES), lambda b, m: (0, kv_col)),
            tab, tab, tab, mtab, mtab, mtab,
        ],
        out_specs=[
            pl.BlockSpec((W, qw), lambda b, m: (b * nblk + m, 0)),
            pl.BlockSpec((None, W, LANES), lambda b, m: (b, 0, 0)),
            pl.BlockSpec((None, W, LANES), lambda b, m: (b, 0, 0)),
        ],
        out_shape=[
            jax.ShapeDtypeStruct((bsz * seq, qw), BF16),
            jax.ShapeDtypeStruct((bsz, W, LANES), F32),
            jax.ShapeDtypeStruct((bsz, W, LANES), F32),
        ],
        scratch_shapes=[pltpu.VMEM((W, LANES), F32), pltpu.VMEM((W, LANES), F32)],
        compiler_params=_cparams(("parallel", "arbitrary")),
        name="swa_prompt",
    )(sinks, z, z, zm, c, s1, s2, cm, s1m, s2m)


def _swa_sample_kernel(sink_ref, q_ref, kv_ref, kc_ref, vc_ref, c_ref, s1_ref, s2_ref,
                       o_ref, ko_ref, vo_ref, *, nb, t):
    W = WINDOW
    keys = 2 * W
    c = c_ref[...]
    s1 = s1_ref[...]
    s2 = s2_ref[...]
    knew = _swa_rope(kv_ref[:, :LANES], c, s1, s2)
    vnew = kv_ref[:, LANES:]
    pad = jnp.zeros((keys - W - t, LANES), F32)

    rq = SWA_HEADS * t
    tq = lax.broadcasted_iota(jnp.int32, (rq, keys), 0) & (t - 1)
    cc = lax.broadcasted_iota(jnp.int32, (rq, keys), 1)
    diff = W + tq - cc
    mask = (diff >= 0) & (diff < W)
    low = lax.broadcasted_iota(jnp.int32, (nb * t, LANES), 1) < SWA_HD
    low_t = lax.broadcasted_iota(jnp.int32, (t, LANES), 1) < SWA_HD
    sink_col = jnp.concatenate([jnp.full((t, 1), sink_ref[h], F32) for h in range(SWA_HEADS)], axis=0)
    npair = SWA_HEADS // 2

    scale = SWA_HD ** -0.5
    slabs = [_swa_rope(q_ref[:, p * LANES:(p + 1) * LANES], c, s1, s2) * scale for p in range(npair)]
    q_heads = [_to_kv_half(slabs[h // 2], h, low) for h in range(SWA_HEADS)]

    logits, values = [], []
    for db in range(nb):
        rows = slice(db * t, (db + 1) * t)
        kn = knew[rows, :]
        vn = vnew[rows, :]
        ko_ref[db, 0:W - t, :] = kc_ref[db, t:W, :]
        ko_ref[db, W - t:W, :] = kn
        vo_ref[db, 0:W - t, :] = vc_ref[db, t:W, :]
        vo_ref[db, W - t:W, :] = vn
        k2 = jnp.concatenate([kc_ref[db], kn, pad], axis=0).astype(BF16)
        values.append(jnp.concatenate([vc_ref[db], vn, pad], axis=0).astype(BF16))
        q_db = jnp.concatenate([qh[rows, :] for qh in q_heads], axis=0).astype(BF16)
        logits.append(jnp.where(mask, _dot_nt(q_db, k2), NEG_INF))
    p, den = _sink_softmax(jnp.concatenate(logits, axis=0), jnp.concatenate([sink_col] * nb, axis=0))
    p = p.astype(BF16)
    outs = [[] for _ in range(npair)]
    for db in range(nb):
        o = _dot(p[db * rq:(db + 1) * rq, :], values[db]) / den[db * rq:(db + 1) * rq, :]
        for pr in range(npair):
            o_even = o[(2 * pr) * t:(2 * pr + 1) * t, :]
            o_odd = o[(2 * pr + 1) * t:(2 * pr + 2) * t, :]
            outs[pr].append(_from_kv_half(o_even, o_odd, 2 * pr // SWA_GROUP, low_t))
    for pr in range(npair):
        o_ref[:, pr * LANES:(pr + 1) * LANES] = jnp.concatenate(outs[pr], axis=0).astype(BF16)


def _swa_sample(sinks, z, row0, kc, vc, c, s1, s2, nb, t):
    dbs = kc.shape[0]
    n = dbs * t
    rows = nb * t
    W = WINDOW
    qw = SWA_HEADS * SWA_HD
    tab = pl.BlockSpec((rows, LANES), lambda i: (0, 0))
    cache = pl.BlockSpec((nb, W, LANES), lambda i: (i, 0, 0))
    return pl.pallas_call(
        functools.partial(_swa_sample_kernel, nb=nb, t=t),
        grid=(dbs // nb,),
        in_specs=[
            pl.BlockSpec(memory_space=pltpu.SMEM),
            pl.BlockSpec((rows, qw), lambda i: (row0 // rows + i, SWA_Q_COL // qw)),
            pl.BlockSpec((rows, 2 * LANES), lambda i: (row0 // rows + i, SWA_KV_COL // (2 * LANES))),
            cache, cache, tab, tab, tab,
        ],
        out_specs=[pl.BlockSpec((rows, qw), lambda i: (i, 0)), cache, cache],
        out_shape=[
            jax.ShapeDtypeStruct((n, qw), BF16),
            jax.ShapeDtypeStruct((dbs, W, LANES), F32),
            jax.ShapeDtypeStruct((dbs, W, LANES), F32),
        ],
        compiler_params=_cparams(("parallel",)),
        name="swa_sample",
    )(sinks, z, z, kc, vc, c, s1, s2)


def _out_proj_ln_kernel(ro_ref, so_ref, x_ref, eg_ref, eb_ref, w_hbm, mg_ref, mb_ref, h_ref, hb_ref,
                        wb_scr, stage, sem):
    chunk = stage.shape[1]
    n_chunks = w_hbm.shape[0] // chunk

    def weight_copy(k):
        return pltpu.make_async_copy(w_hbm.at[pl.ds(k * chunk, chunk), :], stage.at[k % 2], sem.at[k % 2])

    @pl.when(pl.program_id(0) == 0)
    def _():
        weight_copy(0).start()
        for k in range(n_chunks):
            if k + 1 < n_chunks:
                weight_copy(k + 1).start()
            weight_copy(k).wait()
            wb_scr[k * chunk:(k + 1) * chunk, :] = stage[k % 2].astype(BF16)

    x_in = _layer_norm(x_ref[...], eg_ref[...], eb_ref[...])
    mixed = _dot(ro_ref[...], wb_scr[:RET_W, :]) + _dot(so_ref[...], wb_scr[RET_W:, :])
    h = _layer_norm(ALPHA * x_in + mixed, mg_ref[...], mb_ref[...])
    h_ref[...] = h
    hb_ref[...] = h.astype(BF16)


def _out_proj_ln(ro, so, x, eg, eb, w, mg, mb, tm):
    n = x.shape[0]
    vec = pl.BlockSpec((1, D_MODEL), lambda i: (0, 0))
    tile = pl.BlockSpec((tm, D_MODEL), lambda i: (i, 0))
    return pl.pallas_call(
        _out_proj_ln_kernel,
        grid=(n // tm,),
        in_specs=[
            pl.BlockSpec((tm, RET_W), lambda i: (i, 0)),
            pl.BlockSpec((tm, RET_W), lambda i: (i, 0)),
            tile, vec, vec,
            pl.BlockSpec(memory_space=pl.ANY),
            vec, vec,
        ],
        out_specs=[tile, tile],
        out_shape=[jax.ShapeDtypeStruct((n, D_MODEL), F32), jax.ShapeDtypeStruct((n, D_MODEL), BF16)],
        scratch_shapes=[pltpu.VMEM((D_MODEL, D_MODEL), BF16),
                        pltpu.VMEM((2, WEIGHT_STAGE_ROWS, D_MODEL), F32),
                        pltpu.SemaphoreType.DMA((2,))],
        compiler_params=_cparams(("arbitrary",)),
        name="out_proj_ln",
    )(ro, so, x, eg, eb, w, mg, mb)


def _ffn_ln_kernel(hb_ref, h_hbm, wg_ref, wu_ref, wd_ref, g_ref, b_ref, y_ref, h_res, sem):
    i = pl.program_id(0)
    j = pl.program_id(1)
    tm = h_res.shape[0]
    residual_copy = pltpu.make_async_copy(h_hbm.at[pl.ds(i * tm, tm), :], h_res, sem)

    @pl.when(j == 0)
    def _():
        residual_copy.start()
        y_ref[...] = jnp.zeros_like(y_ref)

    hb = hb_ref[...]
    act = _silu(_dot(hb, wg_ref[...].astype(BF16))) * _dot(hb, wu_ref[...].astype(BF16))
    y_ref[...] += _dot(act.astype(BF16), wd_ref[...].astype(BF16))

    @pl.when(j == pl.num_programs(1) - 1)
    def _():
        residual_copy.wait()
        y_ref[...] = _layer_norm(ALPHA * h_res[...] + y_ref[...], g_ref[...], b_ref[...])


def _ffn_ln(h, hb, wg, wu, wd, g, b, tm, th):
    n = h.shape[0]
    vec = pl.BlockSpec((1, D_MODEL), lambda i, j: (0, 0))
    return pl.pallas_call(
        _ffn_ln_kernel,
        grid=(n // tm, FFN_HIDDEN // th),
        in_specs=[
            pl.BlockSpec((tm, D_MODEL), lambda i, j: (i, 0)),
            pl.BlockSpec(memory_space=pl.ANY),
            pl.BlockSpec((D_MODEL, th), lambda i, j: (0, j)),
            pl.BlockSpec((D_MODEL, th), lambda i, j: (0, j)),
            pl.BlockSpec((th, D_MODEL), lambda i, j: (j, 0)),
            vec, vec,
        ],
        out_specs=pl.BlockSpec((tm, D_MODEL), lambda i, j: (i, 0)),
        out_shape=jax.ShapeDtypeStruct((n, D_MODEL), F32),
        scratch_shapes=[pltpu.VMEM((tm, D_MODEL), F32), pltpu.SemaphoreType.DMA(())],
        compiler_params=_cparams(("arbitrary", "arbitrary")),
        name="ffn_ln",
    )(hb, h, wg, wu, wd, g, b)


def _ret_tables(pos):
    freq = jnp.power(RET_THETA, -jnp.linspace(0.0, 1.0, RET_DK // 2, dtype=F32))
    ang = pos.astype(F32)[:, None] * freq[None, :]
    return jnp.cos(ang), jnp.sin(ang)


def _swa_tables(pos):
    half = ROT_DIM // 2
    freq = jnp.power(ROPE_THETA, -jnp.arange(0, ROT_DIM, 2, dtype=F32) / ROT_DIM)
    ang = pos.astype(F32)[:, None] * freq[None, :]
    cos, sin = jnp.cos(ang), jnp.sin(ang)
    n = pos.shape[0]
    rest = SWA_HD - ROT_DIM
    c = jnp.concatenate([cos, cos, jnp.ones((n, rest), F32)], axis=1)
    s1 = jnp.concatenate([jnp.zeros((n, half), F32), sin, jnp.zeros((n, rest), F32)], axis=1)
    s2 = jnp.concatenate([-sin, jnp.zeros((n, half + rest), F32)], axis=1)
    return tuple(jnp.tile(a, (1, LANES // SWA_HD)) for a in (c, s1, s2))


def _front_pad(a, rows):
    return jnp.pad(a, ((rows - a.shape[0], 0), (0, 0)))


def kernel(x_prompt, x_sample, state_ret, cache_swa_k, cache_swa_v, meta_tokens, ln_emb_g, ln_emb_b,
           w_in, w_out, swa_sinks, ln_mix_g, ln_mix_b, w_ffn_gate, w_ffn_up, w_ffn_down, ln_ffn_g, ln_ffn_b):
    bsz, seq, d = x_prompt.shape
    dbs, t, _ = x_sample.shape
    assert w_in.shape[0] == DEPTH and d == D_MODEL and seq % RET_CHUNK == 0 and t & (t - 1) == 0
    past_len = 16384
    row = lambda a: a.reshape(1, -1)

    eg, eb = row(ln_emb_g), row(ln_emb_b)
    sinks = swa_sinks[0]
    lg = jnp.log(1.0 - jnp.power(2.0, -5.0 - jnp.arange(RET_HEADS, dtype=F32)))

    xp = x_prompt.reshape(bsz * seq, d)
    xs = x_sample.reshape(dbs * t, d)
    xm = _front_pad(meta_tokens, RET_CHUNK)

    z, zm = _ln_proj(xp, xs, xm, eg, eb, w_in[0], TOK_TILE, 768)
    sample_row = bsz * seq

    pos_main = N_META + jnp.arange(seq)
    pos_meta = jnp.maximum(jnp.arange(RET_CHUNK) - (RET_CHUNK - N_META), 0)
    nb_ret, nb_swa = 4, 8
    pos_s = past_len + jnp.arange(t)
    cos_p, sin_p = _ret_tables(pos_main)
    cos_m, sin_m = _ret_tables(pos_meta)
    cos_s, sin_s = (jnp.tile(a, (nb_ret, 1)) for a in _ret_tables(pos_s))
    swa_tab_p = _swa_tables(pos_main)
    swa_tab_m = _swa_tables(pos_meta)
    swa_tab_s = tuple(jnp.tile(a, (nb_swa, 1)) for a in _swa_tables(pos_s))

    ret_o_p, ret_state_p = _ret_prompt(lg, z, zm, bsz, seq, cos_p, sin_p, cos_m, sin_m)
    swa_o_p, k_p, v_p = _swa_prompt(sinks, z, zm, bsz, seq, *swa_tab_p, *swa_tab_m)
    ret_o_s, ret_state_s = _ret_sample(lg, z, sample_row, state_ret[0], cos_s, sin_s, nb_ret, t)
    kc = cache_swa_k[0].reshape(dbs, WINDOW, LANES)
    vc = cache_swa_v[0].reshape(dbs, WINDOW, LANES)
    swa_o_s, k_s, v_s = _swa_sample(sinks, z, sample_row, kc, vc, *swa_tab_s, nb_swa, t)

    mg, mb = row(ln_mix_g[0]), row(ln_mix_b[0])
    fg, fb = row(ln_ffn_g[0]), row(ln_ffn_b[0])
    ffn_w = (w_ffn_gate[0], w_ffn_up[0], w_ffn_down[0])
    h_p, hb_p = _out_proj_ln(ret_o_p, swa_o_p, xp, eg, eb, w_out[0], mg, mb, 512)
    h_s, hb_s = _out_proj_ln(ret_o_s, swa_o_s, xs, eg, eb, w_out[0], mg, mb, 512)
    y_p = _ffn_ln(h_p, hb_p, *ffn_w, fg, fb, TOK_TILE, 256)
    y_s = _ffn_ln(h_s, hb_s, *ffn_w, fg, fb, TOK_TILE, 256)

    kv_shape = (DEPTH, -1, WINDOW, SWA_KV_HEADS, SWA_HD)
    return (y_p.reshape(bsz, seq, d), y_s.reshape(dbs, t, d),
            ret_state_p[None], k_p.reshape(kv_shape), v_p.reshape(kv_shape),
            ret_state_s[None], k_s.reshape(kv_shape), v_s.reshape(kv_shape))
```

```python
import functools

import jax
import jax.numpy as jnp
from jax import lax
from jax.experimental import pallas as pl
from jax.experimental.pallas import tpu as pltpu

F32 = jnp.float32
BF16 = jnp.bfloat16

D_MODEL = 2048
N_META = 16
RET_HEADS = 4
RET_DK = 256
RET_DV = 256
RET_CHUNK = 128
RET_THETA = 10000.0
SWA_HD = 64
SWA_HEADS = 16
SWA_KV_HEADS = 2
SWA_GROUP = SWA_HEADS // SWA_KV_HEADS
WINDOW = 128
ROPE_THETA = 500000.0
ROT_DIM = SWA_HD // 4
FFN_HIDDEN = 5632
PROJ_W = 5376
DEPTH = 1
ALPHA = (2.0 * DEPTH) ** 0.25
LN_EPS = 1e-5
NEG_INF = -1e30

LANES = 128
RET_W = RET_HEADS * RET_DK
SWA_Q_COL = 4 * RET_W
SWA_KV_COL = SWA_Q_COL + SWA_HEADS * SWA_HD
VMEM_LIMIT = 56 * 1024 * 1024
VMEM_LIMIT_LARGE = 60 * 1024 * 1024
TOK_TILE = 1024
RET_CHUNK_UNROLL = 16
WEIGHT_STAGE_ROWS = 256


def _cparams(sem):
    return pltpu.CompilerParams(dimension_semantics=sem, vmem_limit_bytes=VMEM_LIMIT)


def _layer_norm(x, g, b):
    mu = jnp.mean(x, axis=-1, keepdims=True)
    xc = x - mu
    var = jnp.mean(xc * xc, axis=-1, keepdims=True)
    return xc * lax.rsqrt(var + LN_EPS) * g + b


def _silu(x):
    return x / (1.0 + jnp.exp(-x))


def _dot(a, b):
    return jnp.dot(a, b, preferred_element_type=F32)


def _dot_nt(a, b):
    return lax.dot_general(a, b, (((1,), (1,)), ((), ())), preferred_element_type=F32)


def _dot_tn(a, b):
    return lax.dot_general(a, b, (((0,), (0,)), ((), ())), preferred_element_type=F32)


def _ln_proj_kernel(xp_ref, xs_ref, xm_ref, g_ref, b_ref, w_hbm, z_ref, zm_ref,
                    h_scr, hm_scr, wb_scr, stage, sem, *, np_tiles, ns_tiles):
    i = pl.program_id(0)
    j = pl.program_id(1)
    first = j == 0
    is_sample = i >= np_tiles
    is_last = i == np_tiles + ns_tiles - 1
    n_col, _, tn = wb_scr.shape
    chunk = stage.shape[1]
    n_chunks = w_hbm.shape[0] // chunk

    def weight_copy(k):
        return pltpu.make_async_copy(w_hbm.at[pl.ds(k * chunk, chunk), :], stage.at[k % 2], sem.at[k % 2])

    @pl.when(first & (i == 0))
    def _():
        weight_copy(0).start()
        for k in range(n_chunks):
            if k + 1 < n_chunks:
                weight_copy(k + 1).start()
            weight_copy(k).wait()
            for jt in range(n_col):
                wb_scr[jt, k * chunk:(k + 1) * chunk, :] = stage[k % 2, :, jt * tn:(jt + 1) * tn].astype(BF16)

    def norm(x_ref):
        return _layer_norm(x_ref[...], g_ref[...], b_ref[...]).astype(BF16)

    @pl.when(first & jnp.logical_not(is_sample))
    def _():
        h_scr[...] = norm(xp_ref)

    @pl.when(first & is_sample)
    def _():
        h_scr[...] = norm(xs_ref)

    @pl.when(first & is_last)
    def _():
        hm_scr[...] = norm(xm_ref)

    z_ref[...] = _dot(h_scr[...], wb_scr[j])

    @pl.when(is_last)
    def _():
        zm_ref[...] = _dot(hm_scr[...], wb_scr[j])


def _ln_proj(xp, xs, xm, g, b, w, tm, tn):
    np_tiles = xp.shape[0] // tm
    ns_tiles = xs.shape[0] // tm
    last = np_tiles + ns_tiles - 1
    n_col = PROJ_W // tn
    return pl.pallas_call(
        functools.partial(_ln_proj_kernel, np_tiles=np_tiles, ns_tiles=ns_tiles),
        grid=(np_tiles + ns_tiles, n_col),
        in_specs=[
            pl.BlockSpec((tm, D_MODEL), lambda i, j: (jnp.minimum(i, np_tiles - 1), 0)),
            pl.BlockSpec((tm, D_MODEL), lambda i, j: (jnp.clip(i - np_tiles, 0, ns_tiles - 1), 0)),
            pl.BlockSpec(xm.shape, lambda i, j: (0, 0), pipeline_mode=pl.Buffered(1)),
            pl.BlockSpec((1, D_MODEL), lambda i, j: (0, 0)),
            pl.BlockSpec((1, D_MODEL), lambda i, j: (0, 0)),
            pl.BlockSpec(memory_space=pl.ANY),
        ],
        out_specs=[
            pl.BlockSpec((tm, tn), lambda i, j: (i, j)),
            pl.BlockSpec((xm.shape[0], tn), lambda i, j: (0, jnp.where(i == last, j, 0))),
        ],
        out_shape=[
            jax.ShapeDtypeStruct((xp.shape[0] + xs.shape[0], PROJ_W), F32),
            jax.ShapeDtypeStruct((xm.shape[0], PROJ_W), F32),
        ],
        scratch_shapes=[
            pltpu.VMEM((tm, D_MODEL), BF16),
            pltpu.VMEM((xm.shape[0], D_MODEL), BF16),
            pltpu.VMEM((n_col, D_MODEL, tn), BF16),
            pltpu.VMEM((2, WEIGHT_STAGE_ROWS // 4, PROJ_W), F32),
            pltpu.SemaphoreType.DMA((2,)),
        ],
        compiler_params=pltpu.CompilerParams(dimension_semantics=("arbitrary", "arbitrary"),
                                             vmem_limit_bytes=VMEM_LIMIT_LARGE),
        name="ln_proj",
    )(xp, xs, xm, g, b, w)


def _ret_rope(x, cos, sin):
    x1 = x[:, :LANES]
    x2 = x[:, LANES:]
    return jnp.concatenate([x1 * cos - x2 * sin, x2 * cos + x1 * sin], axis=1)


def _group_norm_gate(o, gate):
    mu = jnp.mean(o, axis=-1, keepdims=True)
    oc = o - mu
    var = jnp.mean(oc * oc, axis=-1, keepdims=True)
    return oc * lax.rsqrt(var + LN_EPS) * _silu(gate)


def _ret_prompt_kernel(lg_ref, q_ref, k_ref, v_ref, g_ref, km_ref, vm_ref,
                       cos_ref, sin_ref, cosm_ref, sinm_ref, o_ref, s_ref, s_scr):
    C = RET_CHUNK
    lg = lg_ref[pl.program_id(1)]
    ri = lax.broadcasted_iota(jnp.int32, (C, C), 0)
    ci = lax.broadcasted_iota(jnp.int32, (C, C), 1)
    rel = (ri - ci).astype(F32)
    decay = jnp.where(rel >= 0.0, jnp.exp(jnp.maximum(rel, 0.0) * lg), 0.0)
    row = lax.broadcasted_iota(jnp.int32, (C, 1), 0).astype(F32)
    q_decay = jnp.exp((row + 1.0) * lg)
    k_decay = jnp.exp((C - 1.0 - row) * lg)
    chunk_decay = jnp.exp(jnp.full((1, RET_DV), C * lg, F32))
    scale = RET_DK ** -0.5

    meta_decay = jnp.where(row >= C - N_META, k_decay, 0.0)
    km = _ret_rope(km_ref[...], cosm_ref[...], sinm_ref[...]) * scale
    s_scr[...] = _dot_tn((km * meta_decay).astype(BF16), vm_ref[...].astype(BF16))

    def chunk(c, carry):
        rows = pl.ds(pl.multiple_of(c * C, C), C)
        cos = cos_ref[rows, :]
        sin = sin_ref[rows, :]
        q = _ret_rope(q_ref[rows, :], cos, sin)
        k = _ret_rope(k_ref[rows, :], cos, sin) * scale
        qb = q.astype(BF16)
        vb = v_ref[rows, :].astype(BF16)
        s_prev = s_scr[...]
        scores = _dot_nt(qb, k.astype(BF16)) * decay
        inner = _dot(scores.astype(BF16), vb)
        cross = _dot(qb, s_prev.astype(BF16)) * q_decay
        s_scr[...] = chunk_decay * s_prev + _dot_tn((k * k_decay).astype(BF16), vb)
        o_ref[rows, :] = _group_norm_gate(inner + cross, g_ref[rows, :]).astype(BF16)
        return carry

    lax.fori_loop(0, q_ref.shape[0] // C, chunk, 0, unroll=RET_CHUNK_UNROLL)
    s_ref[...] = s_scr[...]


def _ret_prompt(lg, z, zm, bsz, seq, cos, sin, cosm, sinm):
    col = lambda base: (lambda b, h: (b, base + h))
    mcol = lambda base: (lambda b, h: (0, base + h))
    full = lambda b, h: (0, 0)
    return pl.pallas_call(
        _ret_prompt_kernel,
        grid=(bsz, RET_HEADS),
        in_specs=[
            pl.BlockSpec(memory_space=pltpu.SMEM),
            pl.BlockSpec((seq, RET_DK), col(0)),
            pl.BlockSpec((seq, RET_DK), col(RET_HEADS)),
            pl.BlockSpec((seq, RET_DV), col(2 * RET_HEADS)),
            pl.BlockSpec((seq, RET_DV), col(3 * RET_HEADS)),
            pl.BlockSpec((RET_CHUNK, RET_DK), mcol(RET_HEADS)),
            pl.BlockSpec((RET_CHUNK, RET_DV), mcol(2 * RET_HEADS)),
            pl.BlockSpec((seq, LANES), full),
            pl.BlockSpec((seq, LANES), full),
            pl.BlockSpec((RET_CHUNK, LANES), full),
            pl.BlockSpec((RET_CHUNK, LANES), full),
        ],
        out_specs=[
            pl.BlockSpec((seq, RET_DV), lambda b, h: (b, h)),
            pl.BlockSpec((None, None, RET_DK, RET_DV), lambda b, h: (b, h, 0, 0)),
        ],
        out_shape=[
            jax.ShapeDtypeStruct((bsz * seq, RET_W), BF16),
            jax.ShapeDtypeStruct((bsz, RET_HEADS, RET_DK, RET_DV), F32),
        ],
        scratch_shapes=[pltpu.VMEM((RET_DK, RET_DV), F32)],
        compiler_params=_cparams(("parallel", "parallel")),
        name="ret_prompt",
    )(lg, z, z, z, z, zm, zm, cos, sin, cosm, sinm)


def _ret_sample_kernel(lg_ref, z_ref, s_ref, cos_ref, sin_ref, o_ref, so_ref, *, nb, t):
    rows = nb * t
    R = RET_HEADS * rows
    scale = RET_DK ** -0.5
    cos = cos_ref[...]
    sin = sin_ref[...]

    def stack(base, rope, mul=1.0):
        parts = []
        for h in range(RET_HEADS):
            x = z_ref[:, base + h * RET_DK: base + (h + 1) * RET_DK]
            parts.append(_ret_rope(x, cos, sin) * mul if rope else x)
        return jnp.concatenate(parts, axis=0)

    q = stack(0, True)
    k = stack(RET_W, True, scale)
    v = stack(2 * RET_W, False)
    gate = stack(3 * RET_W, False)

    ri = lax.broadcasted_iota(jnp.int32, (R, R), 0)
    ci = lax.broadcasted_iota(jnp.int32, (R, R), 1)
    rcol = lax.broadcasted_iota(jnp.int32, (R, 1), 0)
    lg_col = jnp.zeros((R, 1), F32)
    for h in range(RET_HEADS):
        lg_col = jnp.where((rcol >= h * rows) & (rcol < (h + 1) * rows), lg_ref[h], lg_col)
    tcol = (rcol & (t - 1)).astype(F32)
    rel = (ri - ci).astype(F32)
    same = ((ri & -t) == (ci & -t)) & (ri >= ci)
    decay = jnp.where(same, jnp.exp(jnp.maximum(rel, 0.0) * lg_col), 0.0)
    q_decay = jnp.exp((tcol + 1.0) * lg_col)
    k_decay = jnp.exp((t - 1.0 - tcol) * lg_col)

    qb = q.astype(BF16)
    vb = v.astype(BF16)
    scores = _dot_nt(qb, k.astype(BF16)) * decay
    inner = _dot(scores.astype(BF16), vb)
    kw = k * k_decay

    cross_parts = []
    for h in range(RET_HEADS):
        step_decay = jnp.exp(jnp.full((1, RET_DV), t * lg_ref[h], F32))
        for db in range(nb):
            r0 = h * rows + db * t
            s_prev = s_ref[db, h]
            cross_parts.append(_dot(q[r0:r0 + t, :].astype(BF16), s_prev.astype(BF16)))
            mine = (rcol >= r0) & (rcol < r0 + t)
            upd = _dot_tn(jnp.where(mine, kw, 0.0).astype(BF16), vb)
            so_ref[db, h] = step_decay * s_prev + upd
    cross = jnp.concatenate(cross_parts, axis=0) * q_decay
    out = _group_norm_gate(inner + cross, gate).astype(BF16)
    for h in range(RET_HEADS):
        o_ref[:, h * RET_DV:(h + 1) * RET_DV] = out[h * rows:(h + 1) * rows, :]


def _ret_sample(lg, z, row0, state, cos, sin, nb, t):
    dbs = state.shape[0]
    n = dbs * t
    rows = nb * t
    return pl.pallas_call(
        functools.partial(_ret_sample_kernel, nb=nb, t=t),
        grid=(dbs // nb,),
        in_specs=[
            pl.BlockSpec(memory_space=pltpu.SMEM),
            pl.BlockSpec((rows, 4 * RET_W), lambda i: (row0 // rows + i, 0)),
            pl.BlockSpec((nb, RET_HEADS, RET_DK, RET_DV), lambda i: (i, 0, 0, 0)),
            pl.BlockSpec((rows, LANES), lambda i: (0, 0)),
            pl.BlockSpec((rows, LANES), lambda i: (0, 0)),
        ],
        out_specs=[
            pl.BlockSpec((rows, RET_W), lambda i: (i, 0)),
            pl.BlockSpec((nb, RET_HEADS, RET_DK, RET_DV), lambda i: (i, 0, 0, 0)),
        ],
        out_shape=[
            jax.ShapeDtypeStruct((n, RET_W), BF16),
            jax.ShapeDtypeStruct(state.shape, F32),
        ],
        compiler_params=_cparams(("parallel",)),
        name="ret_sample",
    )(lg, z, state, cos, sin)


def _swa_rope(x, c, s1, s2):
    return x * c + pltpu.roll(x, 8, 1) * s1 + pltpu.roll(x, LANES - 8, 1) * s2


def _dup_head(x, g, low):
    swapped = pltpu.roll(x, SWA_HD, 1)
    return jnp.where(low, x, swapped) if g == 0 else jnp.where(low, swapped, x)


def _to_kv_half(slab, head, low):
    g = head // SWA_GROUP
    src = slab if head % 2 == g else pltpu.roll(slab, SWA_HD, 1)
    return jnp.where(low, src, 0.0) if g == 0 else jnp.where(low, 0.0, src)


def _from_kv_half(o_even, o_odd, g, low):
    if g == 0:
        return jnp.where(low, o_even, pltpu.roll(o_odd, SWA_HD, 1))
    return jnp.where(low, pltpu.roll(o_even, SWA_HD, 1), o_odd)


def _sink_softmax(logits, sink):
    m = jnp.maximum(jnp.max(logits, axis=-1, keepdims=True), sink)
    p = jnp.exp(logits - m)
    return p, jnp.sum(p, axis=-1, keepdims=True) + jnp.exp(sink - m)


def _swa_prompt_kernel(sink_ref, q_ref, kv_ref, kvm_ref, c_ref, s1_ref, s2_ref,
                       cm_ref, s1m_ref, s2m_ref, o_ref, kp_ref, vp_ref, kprev, vprev):
    W = WINDOW
    m_id = pl.program_id(1)

    @pl.when(m_id == 0)
    def _():
        kprev[...] = _swa_rope(kvm_ref[:, :LANES], cm_ref[...], s1m_ref[...], s2m_ref[...])
        vprev[...] = kvm_ref[:, LANES:]

    c = c_ref[...]
    s1 = s1_ref[...]
    s2 = s2_ref[...]
    kcur = _swa_rope(kv_ref[:, :LANES], c, s1, s2)
    vcur = kv_ref[:, LANES:]
    k2 = jnp.concatenate([kprev[...], kcur], axis=0)
    v2 = jnp.concatenate([vprev[...], vcur], axis=0)

    r = lax.broadcasted_iota(jnp.int32, (W, 2 * W), 0)
    cc = lax.broadcasted_iota(jnp.int32, (W, 2 * W), 1)
    diff = W + r - cc
    mask = (diff >= 0) & (diff < W) & ((m_id > 0) | (cc >= W - N_META))
    low_k = lax.broadcasted_iota(jnp.int32, (2 * W, LANES), 1) < SWA_HD
    low = lax.broadcasted_iota(jnp.int32, (W, LANES), 1) < SWA_HD
    scale = SWA_HD ** -0.5

    def attend(q, kd, vd, h):
        p, den = _sink_softmax(jnp.where(mask, _dot_nt(q.astype(BF16), kd), NEG_INF), sink_ref[h])
        return _dot(p.astype(BF16), vd) / den

    for g in range(SWA_KV_HEADS):
        kd = _dup_head(k2, g, low_k).astype(BF16)
        vd = _dup_head(v2, g, low_k).astype(BF16)
        for p in range(g * SWA_GROUP // 2, (g + 1) * SWA_GROUP // 2):
            slab = _swa_rope(q_ref[:, p * LANES:(p + 1) * LANES], c, s1, s2) * scale
            o_even = attend(jnp.where(low, slab, 0.0), kd, vd, 2 * p)
            o_odd = attend(jnp.where(low, 0.0, slab), kd, vd, 2 * p + 1)
            o_ref[:, p * LANES:(p + 1) * LANES] = jnp.where(low, o_even, o_odd).astype(BF16)

    kprev[...] = kcur
    vprev[...] = vcur
    kp_ref[...] = kcur
    vp_ref[...] = vcur


def _swa_prompt(sinks, z, zm, bsz, seq, c, s1, s2, cm, s1m, s2m):
    W = WINDOW
    nblk = seq // W
    qw = SWA_HEADS * SWA_HD
    kv_col = SWA_KV_COL // (2 * LANES)
    tab = pl.BlockSpec((W, LANES), lambda b, m: (m, 0))
    mtab = pl.BlockSpec((W, LANES), lambda b, m: (0, 0))
    return pl.pallas_call(
        _swa_prompt_kernel,
        grid=(bsz, nblk),
        in_specs=[
            pl.BlockSpec(memory_space=pltpu.SMEM),
            pl.BlockSpec((W, qw), lambda b, m: (b * nblk + m, SWA_Q_COL // qw)),
            pl.BlockSpec((W, 2 * LANES), lambda b, m: (b * nblk + m, kv_col)),
            pl.BlockSpec((W, 2 * LANES), lambda b, m: (0, kv_col)),
            tab, tab, tab, mtab, mtab, mtab,
        ],
        out_specs=[
            pl.BlockSpec((W, qw), lambda b, m: (b * nblk + m, 0)),
            pl.BlockSpec((None, W, LANES), lambda b, m: (b, 0, 0)),
            pl.BlockSpec((None, W, LANES), lambda b, m: (b, 0, 0)),
        ],
        out_shape=[
            jax.ShapeDtypeStruct((bsz * seq, qw), BF16),
            jax.ShapeDtypeStruct((bsz, W, LANES), F32),
            jax.ShapeDtypeStruct((bsz, W, LANES), F32),
        ],
        scratch_shapes=[pltpu.VMEM((W, LANES), F32), pltpu.VMEM((W, LANES), F32)],
        compiler_params=_cparams(("parallel", "arbitrary")),
        name="swa_prompt",
    )(sinks, z, z, zm, c, s1, s2, cm, s1m, s2m)


def _swa_sample_kernel(sink_ref, q_ref, kv_ref, kc_ref, vc_ref, c_ref, s1_ref, s2_ref,
                       o_ref, ko_ref, vo_ref, *, nb, t):
    W = WINDOW
    keys = 2 * W
    c = c_ref[...]
    s1 = s1_ref[...]
    s2 = s2_ref[...]
    knew = _swa_rope(kv_ref[:, :LANES], c, s1, s2)
    vnew = kv_ref[:, LANES:]
    pad = jnp.zeros((keys - W - t, LANES), F32)

    rq = SWA_HEADS * t
    tq = lax.broadcasted_iota(jnp.int32, (rq, keys), 0) & (t - 1)
    cc = lax.broadcasted_iota(jnp.int32, (rq, keys), 1)
    diff = W + tq - cc
    mask = (diff >= 0) & (diff < W)
    low = lax.broadcasted_iota(jnp.int32, (nb * t, LANES), 1) < SWA_HD
    low_t = lax.broadcasted_iota(jnp.int32, (t, LANES), 1) < SWA_HD
    sink_col = jnp.concatenate([jnp.full((t, 1), sink_ref[h], F32) for h in range(SWA_HEADS)], axis=0)
    npair = SWA_HEADS // 2

    scale = SWA_HD ** -0.5
    slabs = [_swa_rope(q_ref[:, p * LANES:(p + 1) * LANES], c, s1, s2) * scale for p in range(npair)]
    q_heads = [_to_kv_half(slabs[h // 2], h, low) for h in range(SWA_HEADS)]

    logits, values = [], []
    for db in range(nb):
        rows = slice(db * t, (db + 1) * t)
        kn = knew[rows, :]
        vn = vnew[rows, :]
        ko_ref[db, 0:W - t, :] = kc_ref[db, t:W, :]
        ko_ref[db, W - t:W, :] = kn
        vo_ref[db, 0:W - t, :] = vc_ref[db, t:W, :]
        vo_ref[db, W - t:W, :] = vn
        k2 = jnp.concatenate([kc_ref[db], kn, pad], axis=0).astype(BF16)
        values.append(jnp.concatenate([vc_ref[db], vn, pad], axis=0).astype(BF16))
        q_db = jnp.concatenate([qh[rows, :] for qh in q_heads], axis=0).astype(BF16)
        logits.append(jnp.where(mask, _dot_nt(q_db, k2), NEG_INF))
    p, den = _sink_softmax(jnp.concatenate(logits, axis=0), jnp.concatenate([sink_col] * nb, axis=0))
    p = p.astype(BF16)
    outs = [[] for _ in range(npair)]
    for db in range(nb):
        o = _dot(p[db * rq:(db + 1) * rq, :], values[db]) / den[db * rq:(db + 1) * rq, :]
        for pr in range(npair):
            o_even = o[(2 * pr) * t:(2 * pr + 1) * t, :]
            o_odd = o[(2 * pr + 1) * t:(2 * pr + 2) * t, :]
            outs[pr].append(_from_kv_half(o_even, o_odd, 2 * pr // SWA_GROUP, low_t))
    for pr in range(npair):
        o_ref[:, pr * LANES:(pr + 1) * LANES] = jnp.concatenate(outs[pr], axis=0).astype(BF16)


def _swa_sample(sinks, z, row0, kc, vc, c, s1, s2, nb, t):
    dbs = kc.shape[0]
    n = dbs * t
    rows = nb * t
    W = WINDOW
    qw = SWA_HEADS * SWA_HD
    tab = pl.BlockSpec((rows, LANES), lambda i: (0, 0))
    cache = pl.BlockSpec((nb, W, LANES), lambda i: (i, 0, 0))
    return pl.pallas_call(
        functools.partial(_swa_sample_kernel, nb=nb, t=t),
        grid=(dbs // nb,),
        in_specs=[
            pl.BlockSpec(memory_space=pltpu.SMEM),
            pl.BlockSpec((rows, qw), lambda i: (row0 // rows + i, SWA_Q_COL // qw)),
            pl.BlockSpec((rows, 2 * LANES), lambda i: (row0 // rows + i, SWA_KV_COL // (2 * LANES))),
            cache, cache, tab, tab, tab,
        ],
        out_specs=[pl.BlockSpec((rows, qw), lambda i: (i, 0)), cache, cache],
        out_shape=[
            jax.ShapeDtypeStruct((n, qw), BF16),
            jax.ShapeDtypeStruct((dbs, W, LANES), F32),
            jax.ShapeDtypeStruct((dbs, W, LANES), F32),
        ],
        compiler_params=_cparams(("parallel",)),
        name="swa_sample",
    )(sinks, z, z, kc, vc, c, s1, s2)


def _out_proj_ln_kernel(ro_ref, so_ref, x_ref, eg_ref, eb_ref, w_hbm, mg_ref, mb_ref, h_ref, hb_ref,
                        wb_scr, stage, sem):
    chunk = stage.shape[1]
    n_chunks = w_hbm.shape[0] // chunk

    def weight_copy(k):
        return pltpu.make_async_copy(w_hbm.at[pl.ds(k * chunk, chunk), :], stage.at[k % 2], sem.at[k % 2])

    @pl.when(pl.program_id(0) == 0)
    def _():
        weight_copy(0).start()
        for k in range(n_chunks):
            if k + 1 < n_chunks:
                weight_copy(k + 1).start()
            weight_copy(k).wait()
            wb_scr[k * chunk:(k + 1) * chunk, :] = stage[k % 2].astype(BF16)

    x_in = _layer_norm(x_ref[...], eg_ref[...], eb_ref[...])
    mixed = _dot(ro_ref[...], wb_scr[:RET_W, :]) + _dot(so_ref[...], wb_scr[RET_W:, :])
    h = _layer_norm(ALPHA * x_in + mixed, mg_ref[...], mb_ref[...])
    h_ref[...] = h
    hb_ref[...] = h.astype(BF16)


def _out_proj_ln(ro, so, x, eg, eb, w, mg, mb, tm):
    n = x.shape[0]
    vec = pl.BlockSpec((1, D_MODEL), lambda i: (0, 0))
    tile = pl.BlockSpec((tm, D_MODEL), lambda i: (i, 0))
    return pl.pallas_call(
        _out_proj_ln_kernel,
        grid=(n // tm,),
        in_specs=[
            pl.BlockSpec((tm, RET_W), lambda i: (i, 0)),
            pl.BlockSpec((tm, RET_W), lambda i: (i, 0)),
            tile, vec, vec,
            pl.BlockSpec(memory_space=pl.ANY),
            vec, vec,
        ],
        out_specs=[tile, tile],
        out_shape=[jax.ShapeDtypeStruct((n, D_MODEL), F32), jax.ShapeDtypeStruct((n, D_MODEL), BF16)],
        scratch_shapes=[pltpu.VMEM((D_MODEL, D_MODEL), BF16),
                        pltpu.VMEM((2, WEIGHT_STAGE_ROWS, D_MODEL), F32),
                        pltpu.SemaphoreType.DMA((2,))],
        compiler_params=_cparams(("arbitrary",)),
        name="out_proj_ln",
    )(ro, so, x, eg, eb, w, mg, mb)


def _ffn_ln_kernel(hb_ref, h_hbm, wg_ref, wu_ref, wd_ref, g_ref, b_ref, y_ref, h_res, sem):
    i = pl.program_id(0)
    j = pl.program_id(1)
    tm = h_res.shape[0]
    residual_copy = pltpu.make_async_copy(h_hbm.at[pl.ds(i * tm, tm), :], h_res, sem)

    @pl.when(j == 0)
    def _():
        residual_copy.start()
        y_ref[...] = jnp.zeros_like(y_ref)

    hb = hb_ref[...]
    act = _silu(_dot(hb, wg_ref[...].astype(BF16))) * _dot(hb, wu_ref[...].astype(BF16))
    y_ref[...] += _dot(act.astype(BF16), wd_ref[...].astype(BF16))

    @pl.when(j == pl.num_programs(1) - 1)
    def _():
        residual_copy.wait()
        y_ref[...] = _layer_norm(ALPHA * h_res[...] + y_ref[...], g_ref[...], b_ref[...])


def _ffn_ln(h, hb, wg, wu, wd, g, b, tm, th):
    n = h.shape[0]
    vec = pl.BlockSpec((1, D_MODEL), lambda i, j: (0, 0))
    return pl.pallas_call(
        _ffn_ln_kernel,
        grid=(n // tm, FFN_HIDDEN // th),
        in_specs=[
            pl.BlockSpec((tm, D_MODEL), lambda i, j: (i, 0)),
            pl.BlockSpec(memory_space=pl.ANY),
            pl.BlockSpec((D_MODEL, th), lambda i, j: (0, j)),
            pl.BlockSpec((D_MODEL, th), lambda i, j: (0, j)),
            pl.BlockSpec((th, D_MODEL), lambda i, j: (j, 0)),
            vec, vec,
        ],
        out_specs=pl.BlockSpec((tm, D_MODEL), lambda i, j: (i, 0)),
        out_shape=jax.ShapeDtypeStruct((n, D_MODEL), F32),
        scratch_shapes=[pltpu.VMEM((tm, D_MODEL), F32), pltpu.SemaphoreType.DMA(())],
        compiler_params=_cparams(("arbitrary", "arbitrary")),
        name="ffn_ln",
    )(hb, h, wg, wu, wd, g, b)


def _ret_tables(pos):
    freq = jnp.power(RET_THETA, -jnp.linspace(0.0, 1.0, RET_DK // 2, dtype=F32))
    ang = pos.astype(F32)[:, None] * freq[None, :]
    return jnp.cos(ang), jnp.sin(ang)


def _swa_tables(pos):
    half = ROT_DIM // 2
    freq = jnp.power(ROPE_THETA, -jnp.arange(0, ROT_DIM, 2, dtype=F32) / ROT_DIM)
    ang = pos.astype(F32)[:, None] * freq[None, :]
    cos, sin = jnp.cos(ang), jnp.sin(ang)
    n = pos.shape[0]
    rest = SWA_HD - ROT_DIM
    c = jnp.concatenate([cos, cos, jnp.ones((n, rest), F32)], axis=1)
    s1 = jnp.concatenate([jnp.zeros((n, half), F32), sin, jnp.zeros((n, rest), F32)], axis=1)
    s2 = jnp.concatenate([-sin, jnp.zeros((n, half + rest), F32)], axis=1)
    return tuple(jnp.tile(a, (1, LANES // SWA_HD)) for a in (c, s1, s2))


def _front_pad(a, rows):
    return jnp.pad(a, ((rows - a.shape[0], 0), (0, 0)))


def kernel(x_prompt, x_sample, state_ret, cache_swa_k, cache_swa_v, meta_tokens, ln_emb_g, ln_emb_b,
           w_in, w_out, swa_sinks, ln_mix_g, ln_mix_b, w_ffn_gate, w_ffn_up, w_ffn_down, ln_ffn_g, ln_ffn_b):
    bsz, seq, d = x_prompt.shape
    dbs, t, _ = x_sample.shape
    assert w_in.shape[0] == DEPTH and d == D_MODEL and seq % RET_CHUNK == 0 and t & (t - 1) == 0
    past_len = 16384
    row = lambda a: a.reshape(1, -1)

    eg, eb = row(ln_emb_g), row(ln_emb_b)
    sinks = swa_sinks[0]
    lg = jnp.log(1.0 - jnp.power(2.0, -5.0 - jnp.arange(RET_HEADS, dtype=F32)))

    xp = x_prompt.reshape(bsz * seq, d)
    xs = x_sample.reshape(dbs * t, d)
    xm = _front_pad(meta_tokens, RET_CHUNK)

    z, zm = _ln_proj(xp, xs, xm, eg, eb, w_in[0], 512, PROJ_W // 3)
    sample_row = bsz * seq

    pos_main = N_META + jnp.arange(seq)
    pos_meta = jnp.maximum(jnp.arange(RET_CHUNK) - (RET_CHUNK - N_META), 0)
    nb_ret, nb_swa = 8, 8
    pos_s = past_len + jnp.arange(t)
    cos_p, sin_p = _ret_tables(pos_main)
    cos_m, sin_m = _ret_tables(pos_meta)
    cos_s, sin_s = (jnp.tile(a, (nb_ret, 1)) for a in _ret_tables(pos_s))
    swa_tab_p = _swa_tables(pos_main)
    swa_tab_m = _swa_tables(pos_meta)
    swa_tab_s = tuple(jnp.tile(a, (nb_swa, 1)) for a in _swa_tables(pos_s))

    ret_o_p, ret_state_p = _ret_prompt(lg, z, zm, bsz, seq, cos_p, sin_p, cos_m, sin_m)
    swa_o_p, k_p, v_p = _swa_prompt(sinks, z, zm, bsz, seq, *swa_tab_p, *swa_tab_m)
    ret_o_s, ret_state_s = _ret_sample(lg, z, sample_row, state_ret[0], cos_s, sin_s, nb_ret, t)
    kc = cache_swa_k[0].reshape(dbs, WINDOW, LANES)
    vc = cache_swa_v[0].reshape(dbs, WINDOW, LANES)
    swa_o_s, k_s, v_s = _swa_sample(sinks, z, sample_row, kc, vc, *swa_tab_s, nb_swa, t)

    mg, mb = row(ln_mix_g[0]), row(ln_mix_b[0])
    fg, fb = row(ln_ffn_g[0]), row(ln_ffn_b[0])
    ffn_w = (w_ffn_gate[0], w_ffn_up[0], w_ffn_down[0])
    h_p, hb_p = _out_proj_ln(ret_o_p, swa_o_p, xp, eg, eb, w_out[0], mg, mb, 512)
    h_s, hb_s = _out_proj_ln(ret_o_s, swa_o_s, xs, eg, eb, w_out[0], mg, mb, 512)
    y_p = _ffn_ln(h_p, hb_p, *ffn_w, fg, fb, TOK_TILE, 256)
    y_s = _ffn_ln(h_s, hb_s, *ffn_w, fg, fb, TOK_TILE, 256)

    kv_shape = (DEPTH, -1, WINDOW, SWA_KV_HEADS, SWA_HD)
    return (y_p.reshape(bsz, seq, d), y_s.reshape(dbs, t, d),
            ret_state_p[None], k_p.reshape(kv_shape), v_p.reshape(kv_shape),
            ret_state_s[None], k_s.reshape(kv_shape), v_s.reshape(kv_shape))
```

```python
import functools

import jax
import jax.numpy as jnp
from jax import lax
from jax.experimental import pallas as pl
from jax.experimental.pallas import tpu as pltpu

F32 = jnp.float32
BF16 = jnp.bfloat16

D_MODEL = 2048
N_META = 16
RET_HEADS = 4
RET_DK = 256
RET_DV = 256
RET_CHUNK = 128
RET_THETA = 10000.0
SWA_HD = 64
SWA_HEADS = 16
SWA_KV_HEADS = 2
SWA_GROUP = SWA_HEADS // SWA_KV_HEADS
WINDOW = 128
ROPE_THETA = 500000.0
ROT_DIM = SWA_HD // 4
FFN_HIDDEN = 5632
PROJ_W = 5376
DEPTH = 1
ALPHA = (2.0 * DEPTH) ** 0.25
LN_EPS = 1e-5
NEG_INF = -1e30

LANES = 128
RET_W = RET_HEADS * RET_DK
SWA_Q_COL = 4 * RET_W
SWA_KV_COL = SWA_Q_COL + SWA_HEADS * SWA_HD
VMEM_LIMIT = 56 * 1024 * 1024
VMEM_LIMIT_LARGE = 60 * 1024 * 1024
TOK_TILE = 1024
RET_CHUNK_UNROLL = 16
WEIGHT_STAGE_ROWS = 256


def _cparams(sem):
    return pltpu.CompilerParams(dimension_semantics=sem, vmem_limit_bytes=VMEM_LIMIT)


def _layer_norm(x, g, b):
    mu = jnp.mean(x, axis=-1, keepdims=True)
    xc = x - mu
    var = jnp.mean(xc * xc, axis=-1, keepdims=True)
    return xc * lax.rsqrt(var + LN_EPS) * g + b


def _silu(x):
    return x / (1.0 + jnp.exp(-x))


def _dot(a, b):
    return jnp.dot(a, b, preferred_element_type=F32)


def _dot_nt(a, b):
    return lax.dot_general(a, b, (((1,), (1,)), ((), ())), preferred_element_type=F32)


def _dot_tn(a, b):
    return lax.dot_general(a, b, (((0,), (0,)), ((), ())), preferred_element_type=F32)


def _ln_proj_kernel(xp_ref, xs_ref, xm_ref, g_ref, b_ref, w_hbm, z_ref, zm_ref,
                    h_scr, hm_scr, wb_scr, stage, sem, *, np_tiles, ns_tiles):
    i = pl.program_id(0)
    j = pl.program_id(1)
    first = j == 0
    is_sample = i >= np_tiles
    is_last = i == np_tiles + ns_tiles - 1
    n_col, _, tn = wb_scr.shape
    chunk = stage.shape[1]
    n_chunks = w_hbm.shape[0] // chunk

    def weight_copy(k):
        return pltpu.make_async_copy(w_hbm.at[pl.ds(k * chunk, chunk), :], stage.at[k % 2], sem.at[k % 2])

    @pl.when(first & (i == 0))
    def _():
        weight_copy(0).start()
        for k in range(n_chunks):
            if k + 1 < n_chunks:
                weight_copy(k + 1).start()
            weight_copy(k).wait()
            for jt in range(n_col):
                wb_scr[jt, k * chunk:(k + 1) * chunk, :] = stage[k % 2, :, jt * tn:(jt + 1) * tn].astype(BF16)

    def norm(x_ref):
        return _layer_norm(x_ref[...], g_ref[...], b_ref[...]).astype(BF16)

    @pl.when(first & jnp.logical_not(is_sample))
    def _():
        h_scr[...] = norm(xp_ref)

    @pl.when(first & is_sample)
    def _():
        h_scr[...] = norm(xs_ref)

    @pl.when(first & is_last)
    def _():
        hm_scr[...] = norm(xm_ref)

    z_ref[...] = _dot(h_scr[...], wb_scr[j])

    @pl.when(is_last)
    def _():
        zm_ref[...] = _dot(hm_scr[...], wb_scr[j])


def _ln_proj(xp, xs, xm, g, b, w, tm, tn):
    np_tiles = xp.shape[0] // tm
    ns_tiles = xs.shape[0] // tm
    last = np_tiles + ns_tiles - 1
    n_col = PROJ_W // tn
    return pl.pallas_call(
        functools.partial(_ln_proj_kernel, np_tiles=np_tiles, ns_tiles=ns_tiles),
        grid=(np_tiles + ns_tiles, n_col),
        in_specs=[
            pl.BlockSpec((tm, D_MODEL), lambda i, j: (jnp.minimum(i, np_tiles - 1), 0)),
            pl.BlockSpec((tm, D_MODEL), lambda i, j: (jnp.clip(i - np_tiles, 0, ns_tiles - 1), 0)),
            pl.BlockSpec(xm.shape, lambda i, j: (0, 0), pipeline_mode=pl.Buffered(1)),
            pl.BlockSpec((1, D_MODEL), lambda i, j: (0, 0)),
            pl.BlockSpec((1, D_MODEL), lambda i, j: (0, 0)),
            pl.BlockSpec(memory_space=pl.ANY),
        ],
        out_specs=[
            pl.BlockSpec((tm, tn), lambda i, j: (i, j)),
            pl.BlockSpec((xm.shape[0], tn), lambda i, j: (0, jnp.where(i == last, j, 0))),
        ],
        out_shape=[
            jax.ShapeDtypeStruct((xp.shape[0] + xs.shape[0], PROJ_W), F32),
            jax.ShapeDtypeStruct((xm.shape[0], PROJ_W), F32),
        ],
        scratch_shapes=[
            pltpu.VMEM((tm, D_MODEL), BF16),
            pltpu.VMEM((xm.shape[0], D_MODEL), BF16),
            pltpu.VMEM((n_col, D_MODEL, tn), BF16),
            pltpu.VMEM((2, WEIGHT_STAGE_ROWS // 4, PROJ_W), F32),
            pltpu.SemaphoreType.DMA((2,)),
        ],
        compiler_params=pltpu.CompilerParams(dimension_semantics=("arbitrary", "arbitrary"),
                                             vmem_limit_bytes=VMEM_LIMIT_LARGE),
        name="ln_proj",
    )(xp, xs, xm, g, b, w)


def _ret_rope(x, cos, sin):
    x1 = x[:, :LANES]
    x2 = x[:, LANES:]
    return jnp.concatenate([x1 * cos - x2 * sin, x2 * cos + x1 * sin], axis=1)


def _group_norm_gate(o, gate):
    mu = jnp.mean(o, axis=-1, keepdims=True)
    oc = o - mu
    var = jnp.mean(oc * oc, axis=-1, keepdims=True)
    return oc * lax.rsqrt(var + LN_EPS) * _silu(gate)


def _ret_prompt_kernel(lg_ref, q_ref, k_ref, v_ref, g_ref, km_ref, vm_ref,
                       cos_ref, sin_ref, cosm_ref, sinm_ref, o_ref, s_ref, s_scr):
    C = RET_CHUNK
    lg = lg_ref[pl.program_id(1)]
    ri = lax.broadcasted_iota(jnp.int32, (C, C), 0)
    ci = lax.broadcasted_iota(jnp.int32, (C, C), 1)
    rel = (ri - ci).astype(F32)
    decay = jnp.where(rel >= 0.0, jnp.exp(jnp.maximum(rel, 0.0) * lg), 0.0)
    row = lax.broadcasted_iota(jnp.int32, (C, 1), 0).astype(F32)
    q_decay = jnp.exp((row + 1.0) * lg)
    k_decay = jnp.exp((C - 1.0 - row) * lg)
    chunk_decay = jnp.exp(jnp.full((1, RET_DV), C * lg, F32))
    scale = RET_DK ** -0.5

    meta_decay = jnp.where(row >= C - N_META, k_decay, 0.0)
    km = _ret_rope(km_ref[...], cosm_ref[...], sinm_ref[...]) * scale
    s_scr[...] = _dot_tn((km * meta_decay).astype(BF16), vm_ref[...].astype(BF16))

    def chunk(c, carry):
        rows = pl.ds(pl.multiple_of(c * C, C), C)
        cos = cos_ref[rows, :]
        sin = sin_ref[rows, :]
        q = _ret_rope(q_ref[rows, :], cos, sin)
        k = _ret_rope(k_ref[rows, :], cos, sin) * scale
        qb = q.astype(BF16)
        vb = v_ref[rows, :].astype(BF16)
        s_prev = s_scr[...]
        scores = _dot_nt(qb, k.astype(BF16)) * decay
        inner = _dot(scores.astype(BF16), vb)
        cross = _dot(qb, s_prev.astype(BF16)) * q_decay
        s_scr[...] = chunk_decay * s_prev + _dot_tn((k * k_decay).astype(BF16), vb)
        o_ref[rows, :] = _group_norm_gate(inner + cross, g_ref[rows, :]).astype(BF16)
        return carry

    lax.fori_loop(0, q_ref.shape[0] // C, chunk, 0, unroll=RET_CHUNK_UNROLL)
    s_ref[...] = s_scr[...]


def _ret_prompt(lg, z, zm, bsz, seq, cos, sin, cosm, sinm):
    col = lambda base: (lambda b, h: (b, base + h))
    mcol = lambda base: (lambda b, h: (0, base + h))
    full = lambda b, h: (0, 0)
    return pl.pallas_call(
        _ret_prompt_kernel,
        grid=(bsz, RET_HEADS),
        in_specs=[
            pl.BlockSpec(memory_space=pltpu.SMEM),
            pl.BlockSpec((seq, RET_DK), col(0)),
            pl.BlockSpec((seq, RET_DK), col(RET_HEADS)),
            pl.BlockSpec((seq, RET_DV), col(2 * RET_HEADS)),
            pl.BlockSpec((seq, RET_DV), col(3 * RET_HEADS)),
            pl.BlockSpec((RET_CHUNK, RET_DK), mcol(RET_HEADS)),
            pl.BlockSpec((RET_CHUNK, RET_DV), mcol(2 * RET_HEADS)),
            pl.BlockSpec((seq, LANES), full),
            pl.BlockSpec((seq, LANES), full),
            pl.BlockSpec((RET_CHUNK, LANES), full),
            pl.BlockSpec((RET_CHUNK, LANES), full),
        ],
        out_specs=[
            pl.BlockSpec((seq, RET_DV), lambda b, h: (b, h)),
            pl.BlockSpec((None, None, RET_DK, RET_DV), lambda b, h: (b, h, 0, 0)),
        ],
        out_shape=[
            jax.ShapeDtypeStruct((bsz * seq, RET_W), BF16),
            jax.ShapeDtypeStruct((bsz, RET_HEADS, RET_DK, RET_DV), F32),
        ],
        scratch_shapes=[pltpu.VMEM((RET_DK, RET_DV), F32)],
        compiler_params=_cparams(("parallel", "parallel")),
        name="ret_prompt",
    )(lg, z, z, z, z, zm, zm, cos, sin, cosm, sinm)


def _ret_sample_jobs(lg_ref, z_ref, s_ref, cos_ref, sin_ref, o_ref, so_ref, *, nb, t):
    rows = nb * t
    R = RET_HEADS * rows
    scale = RET_DK ** -0.5
    cos = cos_ref[...]
    sin = sin_ref[...]

    def stack(base, rope, mul=1.0):
        parts = []
        for h in range(RET_HEADS):
            x = z_ref[:, base + h * RET_DK: base + (h + 1) * RET_DK]
            parts.append(_ret_rope(x, cos, sin) * mul if rope else x)
        return jnp.concatenate(parts, axis=0)

    q = stack(0, True)
    k = stack(RET_W, True, scale)
    v = stack(2 * RET_W, False)
    gate = stack(3 * RET_W, False)

    ri = lax.broadcasted_iota(jnp.int32, (R, R), 0)
    ci = lax.broadcasted_iota(jnp.int32, (R, R), 1)
    rcol = lax.broadcasted_iota(jnp.int32, (R, 1), 0)
    lg_col = jnp.zeros((R, 1), F32)
    for h in range(RET_HEADS):
        lg_col = jnp.where((rcol >= h * rows) & (rcol < (h + 1) * rows), lg_ref[h], lg_col)
    tcol = (rcol & (t - 1)).astype(F32)
    rel = (ri - ci).astype(F32)
    same = ((ri & -t) == (ci & -t)) & (ri >= ci)
    decay = jnp.where(same, jnp.exp(jnp.maximum(rel, 0.0) * lg_col), 0.0)
    q_decay = jnp.exp((tcol + 1.0) * lg_col)
    k_decay = jnp.exp((t - 1.0 - tcol) * lg_col)

    qb = q.astype(BF16)
    vb = v.astype(BF16)
    scores = _dot_nt(qb, k.astype(BF16)) * decay
    inner = _dot(scores.astype(BF16), vb)
    kw = k * k_decay

    cross_parts = [None] * (RET_HEADS * nb)

    def state_job(h, db):
        def run():
            r0 = h * rows + db * t
            s_prev = s_ref[db, h]
            cross_parts[h * nb + db] = _dot(q[r0:r0 + t, :].astype(BF16), s_prev.astype(BF16))
            mine = (rcol >= r0) & (rcol < r0 + t)
            upd = _dot_tn(jnp.where(mine, kw, 0.0).astype(BF16), vb)
            step_decay = jnp.exp(jnp.full((1, RET_DV), t * lg_ref[h], F32))
            so_ref[db, h] = step_decay * s_prev + upd
        return run

    def finish():
        cross = jnp.concatenate(cross_parts, axis=0) * q_decay
        out = _group_norm_gate(inner + cross, gate).astype(BF16)
        for h in range(RET_HEADS):
            o_ref[:, h * RET_DV:(h + 1) * RET_DV] = out[h * rows:(h + 1) * rows, :]

    return [state_job(h, db) for h in range(RET_HEADS) for db in range(nb)], finish


def _swa_rope(x, c, s1, s2):
    return x * c + pltpu.roll(x, 8, 1) * s1 + pltpu.roll(x, LANES - 8, 1) * s2


def _dup_head(x, g, low):
    swapped = pltpu.roll(x, SWA_HD, 1)
    return jnp.where(low, x, swapped) if g == 0 else jnp.where(low, swapped, x)


def _to_kv_half(slab, head, low):
    g = head // SWA_GROUP
    src = slab if head % 2 == g else pltpu.roll(slab, SWA_HD, 1)
    return jnp.where(low, src, 0.0) if g == 0 else jnp.where(low, 0.0, src)


def _from_kv_half(o_even, o_odd, g, low):
    if g == 0:
        return jnp.where(low, o_even, pltpu.roll(o_odd, SWA_HD, 1))
    return jnp.where(low, pltpu.roll(o_even, SWA_HD, 1), o_odd)


def _sink_softmax(logits, sink):
    m = jnp.maximum(jnp.max(logits, axis=-1, keepdims=True), sink)
    p = jnp.exp(logits - m)
    return p, jnp.sum(p, axis=-1, keepdims=True) + jnp.exp(sink - m)


def _swa_prompt_kernel(sink_ref, q_ref, kv_ref, kvm_ref, c_ref, s1_ref, s2_ref,
                       cm_ref, s1m_ref, s2m_ref, o_ref, kp_ref, vp_ref, kprev, vprev, make_side_jobs=lambda: []):
    W = WINDOW
    m_id = pl.program_id(1)

    @pl.when(m_id == 0)
    def _():
        kprev[...] = _swa_rope(kvm_ref[:, :LANES], cm_ref[...], s1m_ref[...], s2m_ref[...])
        vprev[...] = kvm_ref[:, LANES:]

    side_jobs = make_side_jobs()
    c = c_ref[...]
    s1 = s1_ref[...]
    s2 = s2_ref[...]
    kcur = _swa_rope(kv_ref[:, :LANES], c, s1, s2)
    vcur = kv_ref[:, LANES:]
    k2 = jnp.concatenate([kprev[...], kcur], axis=0)
    v2 = jnp.concatenate([vprev[...], vcur], axis=0)

    r = lax.broadcasted_iota(jnp.int32, (W, 2 * W), 0)
    cc = lax.broadcasted_iota(jnp.int32, (W, 2 * W), 1)
    diff = W + r - cc
    mask = (diff >= 0) & (diff < W) & ((m_id > 0) | (cc >= W - N_META))
    low_k = lax.broadcasted_iota(jnp.int32, (2 * W, LANES), 1) < SWA_HD
    low = lax.broadcasted_iota(jnp.int32, (W, LANES), 1) < SWA_HD
    scale = SWA_HD ** -0.5

    def attend(q, kd, vd, h):
        p, den = _sink_softmax(jnp.where(mask, _dot_nt(q.astype(BF16), kd), NEG_INF), sink_ref[h])
        return _dot(p.astype(BF16), vd) / den

    for g in range(SWA_KV_HEADS):
        kd = _dup_head(k2, g, low_k).astype(BF16)
        vd = _dup_head(v2, g, low_k).astype(BF16)
        for p in range(g * SWA_GROUP // 2, (g + 1) * SWA_GROUP // 2):
            slab = _swa_rope(q_ref[:, p * LANES:(p + 1) * LANES], c, s1, s2) * scale
            o_even = attend(jnp.where(low, slab, 0.0), kd, vd, 2 * p)
            o_odd = attend(jnp.where(low, 0.0, slab), kd, vd, 2 * p + 1)
            o_ref[:, p * LANES:(p + 1) * LANES] = jnp.where(low, o_even, o_odd).astype(BF16)
    for job in side_jobs:
        job()

    kprev[...] = kcur
    vprev[...] = vcur
    kp_ref[...] = kcur
    vp_ref[...] = vcur


def _swa_prompt_ret_sample_kernel(sink_ref, q_ref, kv_ref, kvm_ref, c_ref, s1_ref, s2_ref, cm_ref, s1m_ref, s2m_ref,
                                 lg_ref, zs_ref, st_ref, cos_ref, sin_ref,
                                 o_ref, kp_ref, vp_ref, os_ref, sto_ref, kprev, vprev, *, nb, t):
    def retention_jobs():
        state_jobs, finish = _ret_sample_jobs(lg_ref, zs_ref, st_ref, cos_ref, sin_ref, os_ref, sto_ref, nb=nb, t=t)
        return state_jobs + [finish]

    _swa_prompt_kernel(sink_ref, q_ref, kv_ref, kvm_ref, c_ref, s1_ref, s2_ref, cm_ref, s1m_ref, s2m_ref,
                       o_ref, kp_ref, vp_ref, kprev, vprev, make_side_jobs=retention_jobs)


def _swa_prompt_ret_sample(sinks, z, zm, bsz, seq, c, s1, s2, cm, s1m, s2m, lg, sample_row, state, cos, sin, t):
    W = WINDOW
    nblk = seq // W
    steps = bsz * nblk
    dbs = state.shape[0]
    nb = dbs // steps
    assert nb * steps == dbs and (nb * t) % 16 == 0
    rows = nb * t
    qw = SWA_HEADS * SWA_HD
    kv_col = SWA_KV_COL // (2 * LANES)
    tab = pl.BlockSpec((W, LANES), lambda b, m: (m, 0))
    mtab = pl.BlockSpec((W, LANES), lambda b, m: (0, 0))
    stab = pl.BlockSpec((rows, LANES), lambda b, m: (0, 0))
    state_spec = pl.BlockSpec((nb, RET_HEADS, RET_DK, RET_DV), lambda b, m: (b * nblk + m, 0, 0, 0))
    return pl.pallas_call(
        functools.partial(_swa_prompt_ret_sample_kernel, nb=nb, t=t),
        grid=(bsz, nblk),
        in_specs=[
            pl.BlockSpec(memory_space=pltpu.SMEM),
            pl.BlockSpec((W, qw), lambda b, m: (b * nblk + m, SWA_Q_COL // qw)),
            pl.BlockSpec((W, 2 * LANES), lambda b, m: (b * nblk + m, kv_col)),
            pl.BlockSpec((W, 2 * LANES), lambda b, m: (0, kv_col)),
            tab, tab, tab, mtab, mtab, mtab,
            pl.BlockSpec(memory_space=pltpu.SMEM),
            pl.BlockSpec((rows, 4 * RET_W), lambda b, m: (sample_row // rows + b * nblk + m, 0)),
            state_spec, stab, stab,
        ],
        out_specs=[
            pl.BlockSpec((W, qw), lambda b, m: (b * nblk + m, 0)),
            pl.BlockSpec((None, W, LANES), lambda b, m: (b, 0, 0)),
            pl.BlockSpec((None, W, LANES), lambda b, m: (b, 0, 0)),
            pl.BlockSpec((rows, RET_W), lambda b, m: (b * nblk + m, 0)),
            state_spec,
        ],
        out_shape=[
            jax.ShapeDtypeStruct((bsz * seq, qw), BF16),
            jax.ShapeDtypeStruct((bsz, W, LANES), F32),
            jax.ShapeDtypeStruct((bsz, W, LANES), F32),
            jax.ShapeDtypeStruct((dbs * t, RET_W), BF16),
            jax.ShapeDtypeStruct(state.shape, F32),
        ],
        scratch_shapes=[pltpu.VMEM((W, LANES), F32), pltpu.VMEM((W, LANES), F32)],
        compiler_params=_cparams(("parallel", "arbitrary")),
        name="swa_prompt_ret_sample",
    )(sinks, z, z, zm, c, s1, s2, cm, s1m, s2m, lg, z, state, cos, sin)


def _swa_sample_kernel(sink_ref, q_ref, kv_ref, kc_ref, vc_ref, c_ref, s1_ref, s2_ref,
                       o_ref, ko_ref, vo_ref, *, nb, t):
    W = WINDOW
    keys = 2 * W
    c = c_ref[...]
    s1 = s1_ref[...]
    s2 = s2_ref[...]
    knew = _swa_rope(kv_ref[:, :LANES], c, s1, s2)
    vnew = kv_ref[:, LANES:]
    pad = jnp.zeros((keys - W - t, LANES), F32)

    rq = SWA_HEADS * t
    tq = lax.broadcasted_iota(jnp.int32, (rq, keys), 0) & (t - 1)
    cc = lax.broadcasted_iota(jnp.int32, (rq, keys), 1)
    diff = W + tq - cc
    mask = (diff >= 0) & (diff < W)
    low = lax.broadcasted_iota(jnp.int32, (nb * t, LANES), 1) < SWA_HD
    low_t = lax.broadcasted_iota(jnp.int32, (t, LANES), 1) < SWA_HD
    sink_col = jnp.concatenate([jnp.full((t, 1), sink_ref[h], F32) for h in range(SWA_HEADS)], axis=0)
    npair = SWA_HEADS // 2

    scale = SWA_HD ** -0.5
    slabs = [_swa_rope(q_ref[:, p * LANES:(p + 1) * LANES], c, s1, s2) * scale for p in range(npair)]
    q_heads = [_to_kv_half(slabs[h // 2], h, low) for h in range(SWA_HEADS)]

    logits, values = [], []
    for db in range(nb):
        rows = slice(db * t, (db + 1) * t)
        kn = knew[rows, :]
        vn = vnew[rows, :]
        ko_ref[db, 0:W - t, :] = kc_ref[db, t:W, :]
        ko_ref[db, W - t:W, :] = kn
        vo_ref[db, 0:W - t, :] = vc_ref[db, t:W, :]
        vo_ref[db, W - t:W, :] = vn
        k2 = jnp.concatenate([kc_ref[db], kn, pad], axis=0).astype(BF16)
        values.append(jnp.concatenate([vc_ref[db], vn, pad], axis=0).astype(BF16))
        q_db = jnp.concatenate([qh[rows, :] for qh in q_heads], axis=0).astype(BF16)
        logits.append(jnp.where(mask, _dot_nt(q_db, k2), NEG_INF))
    p, den = _sink_softmax(jnp.concatenate(logits, axis=0), jnp.concatenate([sink_col] * nb, axis=0))
    p = p.astype(BF16)
    outs = [[] for _ in range(npair)]
    for db in range(nb):
        o = _dot(p[db * rq:(db + 1) * rq, :], values[db]) / den[db * rq:(db + 1) * rq, :]
        for pr in range(npair):
            o_even = o[(2 * pr) * t:(2 * pr + 1) * t, :]
            o_odd = o[(2 * pr + 1) * t:(2 * pr + 2) * t, :]
            outs[pr].append(_from_kv_half(o_even, o_odd, 2 * pr // SWA_GROUP, low_t))
    for pr in range(npair):
        o_ref[:, pr * LANES:(pr + 1) * LANES] = jnp.concatenate(outs[pr], axis=0).astype(BF16)


def _swa_sample(sinks, z, row0, kc, vc, c, s1, s2, nb, t):
    dbs = kc.shape[0]
    n = dbs * t
    rows = nb * t
    W = WINDOW
    qw = SWA_HEADS * SWA_HD
    tab = pl.BlockSpec((rows, LANES), lambda i: (0, 0))
    cache = pl.BlockSpec((nb, W, LANES), lambda i: (i, 0, 0))
    return pl.pallas_call(
        functools.partial(_swa_sample_kernel, nb=nb, t=t),
        grid=(dbs // nb,),
        in_specs=[
            pl.BlockSpec(memory_space=pltpu.SMEM),
            pl.BlockSpec((rows, qw), lambda i: (row0 // rows + i, SWA_Q_COL // qw)),
            pl.BlockSpec((rows, 2 * LANES), lambda i: (row0 // rows + i, SWA_KV_COL // (2 * LANES))),
            cache, cache, tab, tab, tab,
        ],
        out_specs=[pl.BlockSpec((rows, qw), lambda i: (i, 0)), cache, cache],
        out_shape=[
            jax.ShapeDtypeStruct((n, qw), BF16),
            jax.ShapeDtypeStruct((dbs, W, LANES), F32),
            jax.ShapeDtypeStruct((dbs, W, LANES), F32),
        ],
        compiler_params=_cparams(("parallel",)),
        name="swa_sample",
    )(sinks, z, z, kc, vc, c, s1, s2)


def _out_proj_ln_kernel(ro_ref, so_ref, x_ref, eg_ref, eb_ref, w_hbm, mg_ref, mb_ref, h_ref, hb_ref,
                        wb_scr, stage, sem):
    chunk = stage.shape[1]
    n_chunks = w_hbm.shape[0] // chunk

    def weight_copy(k):
        return pltpu.make_async_copy(w_hbm.at[pl.ds(k * chunk, chunk), :], stage.at[k % 2], sem.at[k % 2])

    @pl.when(pl.program_id(0) == 0)
    def _():
        weight_copy(0).start()
        for k in range(n_chunks):
            if k + 1 < n_chunks:
                weight_copy(k + 1).start()
            weight_copy(k).wait()
            wb_scr[k * chunk:(k + 1) * chunk, :] = stage[k % 2].astype(BF16)

    x_in = _layer_norm(x_ref[...], eg_ref[...], eb_ref[...])
    mixed = _dot(ro_ref[...], wb_scr[:RET_W, :]) + _dot(so_ref[...], wb_scr[RET_W:, :])
    h = _layer_norm(ALPHA * x_in + mixed, mg_ref[...], mb_ref[...])
    h_ref[...] = h
    hb_ref[...] = h.astype(BF16)


def _out_proj_ln(ro, so, x, eg, eb, w, mg, mb, tm):
    n = x.shape[0]
    vec = pl.BlockSpec((1, D_MODEL), lambda i: (0, 0))
    tile = pl.BlockSpec((tm, D_MODEL), lambda i: (i, 0))
    return pl.pallas_call(
        _out_proj_ln_kernel,
        grid=(n // tm,),
        in_specs=[
            pl.BlockSpec((tm, RET_W), lambda i: (i, 0)),
            pl.BlockSpec((tm, RET_W), lambda i: (i, 0)),
            tile, vec, vec,
            pl.BlockSpec(memory_space=pl.ANY),
            vec, vec,
        ],
        out_specs=[tile, tile],
        out_shape=[jax.ShapeDtypeStruct((n, D_MODEL), F32), jax.ShapeDtypeStruct((n, D_MODEL), BF16)],
        scratch_shapes=[pltpu.VMEM((D_MODEL, D_MODEL), BF16),
                        pltpu.VMEM((2, WEIGHT_STAGE_ROWS, D_MODEL), F32),
                        pltpu.SemaphoreType.DMA((2,))],
        compiler_params=_cparams(("arbitrary",)),
        name="out_proj_ln",
    )(ro, so, x, eg, eb, w, mg, mb)


def _ffn_ln_kernel(hb_ref, h_hbm, wg_ref, wu_ref, wd_ref, g_ref, b_ref, y_ref, h_res, sem):
    i = pl.program_id(0)
    j = pl.program_id(1)
    tm = h_res.shape[0]
    residual_copy = pltpu.make_async_copy(h_hbm.at[pl.ds(i * tm, tm), :], h_res, sem)

    @pl.when(j == 0)
    def _():
        residual_copy.start()
        y_ref[...] = jnp.zeros_like(y_ref)

    hb = hb_ref[...]
    act = _silu(_dot(hb, wg_ref[...].astype(BF16))) * _dot(hb, wu_ref[...].astype(BF16))
    y_ref[...] += _dot(act.astype(BF16), wd_ref[...].astype(BF16))

    @pl.when(j == pl.num_programs(1) - 1)
    def _():
        residual_copy.wait()
        y_ref[...] = _layer_norm(ALPHA * h_res[...] + y_ref[...], g_ref[...], b_ref[...])


def _ffn_ln(h, hb, wg, wu, wd, g, b, tm, th):
    n = h.shape[0]
    vec = pl.BlockSpec((1, D_MODEL), lambda i, j: (0, 0))
    return pl.pallas_call(
        _ffn_ln_kernel,
        grid=(n // tm, FFN_HIDDEN // th),
        in_specs=[
            pl.BlockSpec((tm, D_MODEL), lambda i, j: (i, 0)),
            pl.BlockSpec(memory_space=pl.ANY),
            pl.BlockSpec((D_MODEL, th), lambda i, j: (0, j)),
            pl.BlockSpec((D_MODEL, th), lambda i, j: (0, j)),
            pl.BlockSpec((th, D_MODEL), lambda i, j: (j, 0)),
            vec, vec,
        ],
        out_specs=pl.BlockSpec((tm, D_MODEL), lambda i, j: (i, 0)),
        out_shape=jax.ShapeDtypeStruct((n, D_MODEL), F32),
        scratch_shapes=[pltpu.VMEM((tm, D_MODEL), F32), pltpu.SemaphoreType.DMA(())],
        compiler_params=_cparams(("arbitrary", "arbitrary")),
        name="ffn_ln",
    )(hb, h, wg, wu, wd, g, b)


def _ret_tables(pos):
    freq = jnp.power(RET_THETA, -jnp.linspace(0.0, 1.0, RET_DK // 2, dtype=F32))
    ang = pos.astype(F32)[:, None] * freq[None, :]
    return jnp.cos(ang), jnp.sin(ang)


def _swa_tables(pos):
    half = ROT_DIM // 2
    freq = jnp.power(ROPE_THETA, -jnp.arange(0, ROT_DIM, 2, dtype=F32) / ROT_DIM)
    ang = pos.astype(F32)[:, None] * freq[None, :]
    cos, sin = jnp.cos(ang), jnp.sin(ang)
    n = pos.shape[0]
    rest = SWA_HD - ROT_DIM
    c = jnp.concatenate([cos, cos, jnp.ones((n, rest), F32)], axis=1)
    s1 = jnp.concatenate([jnp.zeros((n, half), F32), sin, jnp.zeros((n, rest), F32)], axis=1)
    s2 = jnp.concatenate([-sin, jnp.zeros((n, half + rest), F32)], axis=1)
    return tuple(jnp.tile(a, (1, LANES // SWA_HD)) for a in (c, s1, s2))


def _front_pad(a, rows):
    return jnp.pad(a, ((rows - a.shape[0], 0), (0, 0)))


def kernel(x_prompt, x_sample, state_ret, cache_swa_k, cache_swa_v, meta_tokens, ln_emb_g, ln_emb_b,
           w_in, w_out, swa_sinks, ln_mix_g, ln_mix_b, w_ffn_gate, w_ffn_up, w_ffn_down, ln_ffn_g, ln_ffn_b):
    bsz, seq, d = x_prompt.shape
    dbs, t, _ = x_sample.shape
    assert w_in.shape[0] == DEPTH and d == D_MODEL and seq % RET_CHUNK == 0 and t & (t - 1) == 0
    past_len = 16384
    row = lambda a: a.reshape(1, -1)

    eg, eb = row(ln_emb_g), row(ln_emb_b)
    sinks = swa_sinks[0]
    lg = jnp.log(1.0 - jnp.power(2.0, -5.0 - jnp.arange(RET_HEADS, dtype=F32)))

    xp = x_prompt.reshape(bsz * seq, d)
    xs = x_sample.reshape(dbs * t, d)
    xm = _front_pad(meta_tokens, RET_CHUNK)

    z, zm = _ln_proj(xp, xs, xm, eg, eb, w_in[0], 512, PROJ_W // 3)
    sample_row = bsz * seq

    pos_main = N_META + jnp.arange(seq)
    pos_meta = jnp.maximum(jnp.arange(RET_CHUNK) - (RET_CHUNK - N_META), 0)
    nb_ret = dbs // (bsz * (seq // WINDOW))
    nb_swa = 8
    pos_s = past_len + jnp.arange(t)
    cos_p, sin_p = _ret_tables(pos_main)
    cos_m, sin_m = _ret_tables(pos_meta)
    cos_s, sin_s = (jnp.tile(a, (nb_ret, 1)) for a in _ret_tables(pos_s))
    swa_tab_p = _swa_tables(pos_main)
    swa_tab_m = _swa_tables(pos_meta)
    swa_tab_s = tuple(jnp.tile(a, (nb_swa, 1)) for a in _swa_tables(pos_s))

    ret_o_p, ret_state_p = _ret_prompt(lg, z, zm, bsz, seq, cos_p, sin_p, cos_m, sin_m)
    swa_o_p, k_p, v_p, ret_o_s, ret_state_s = _swa_prompt_ret_sample(
        sinks, z, zm, bsz, seq, *swa_tab_p, *swa_tab_m, lg, sample_row, state_ret[0], cos_s, sin_s, t)
    kc = cache_swa_k[0].reshape(dbs, WINDOW, LANES)
    vc = cache_swa_v[0].reshape(dbs, WINDOW, LANES)
    swa_o_s, k_s, v_s = _swa_sample(sinks, z, sample_row, kc, vc, *swa_tab_s, nb_swa, t)

    mg, mb = row(ln_mix_g[0]), row(ln_mix_b[0])
    fg, fb = row(ln_ffn_g[0]), row(ln_ffn_b[0])
    ffn_w = (w_ffn_gate[0], w_ffn_up[0], w_ffn_down[0])
    h_p, hb_p = _out_proj_ln(ret_o_p, swa_o_p, xp, eg, eb, w_out[0], mg, mb, 512)
    h_s, hb_s = _out_proj_ln(ret_o_s, swa_o_s, xs, eg, eb, w_out[0], mg, mb, 512)
    y_p = _ffn_ln(h_p, hb_p, *ffn_w, fg, fb, TOK_TILE, 256)
    y_s = _ffn_ln(h_s, hb_s, *ffn_w, fg, fb, TOK_TILE, 256)

    kv_shape = (DEPTH, -1, WINDOW, SWA_KV_HEADS, SWA_HD)
    return (y_p.reshape(bsz, seq, d), y_s.reshape(dbs, t, d),
            ret_state_p[None], k_p.reshape(kv_shape), v_p.reshape(kv_shape),
            ret_state_s[None], k_s.reshape(kv_shape), v_s.reshape(kv_shape))
```

```python
import functools

import jax
import jax.numpy as jnp
from jax import lax
from jax.experimental import pallas as pl
from jax.experimental.pallas import tpu as pltpu

F32 = jnp.float32
BF16 = jnp.bfloat16

D_MODEL = 2048
N_META = 16
RET_HEADS = 4
RET_DK = 256
RET_DV = 256
RET_CHUNK = 128
RET_THETA = 10000.0
SWA_HD = 64
SWA_HEADS = 16
SWA_KV_HEADS = 2
SWA_GROUP = SWA_HEADS // SWA_KV_HEADS
WINDOW = 128
ROPE_THETA = 500000.0
ROT_DIM = SWA_HD // 4
FFN_HIDDEN = 5632
PROJ_W = 5376
DEPTH = 1
ALPHA = (2.0 * DEPTH) ** 0.25
LN_EPS = 1e-5
NEG_INF = -1e30

LANES = 128
RET_W = RET_HEADS * RET_DK
SWA_Q_COL = 4 * RET_W
SWA_KV_COL = SWA_Q_COL + SWA_HEADS * SWA_HD
VMEM_LIMIT = 56 * 1024 * 1024
VMEM_LIMIT_LARGE = 60 * 1024 * 1024
TOK_TILE = 1024
RET_CHUNK_UNROLL = 16
WEIGHT_STAGE_ROWS = 256


def _cparams(sem):
    return pltpu.CompilerParams(dimension_semantics=sem, vmem_limit_bytes=VMEM_LIMIT)


def _layer_norm(x, g, b):
    mu = jnp.mean(x, axis=-1, keepdims=True)
    xc = x - mu
    var = jnp.mean(xc * xc, axis=-1, keepdims=True)
    return xc * lax.rsqrt(var + LN_EPS) * g + b


def _silu(x):
    return x / (1.0 + jnp.exp(-x))


def _dot(a, b):
    return jnp.dot(a, b, preferred_element_type=F32)


def _dot_nt(a, b):
    return lax.dot_general(a, b, (((1,), (1,)), ((), ())), preferred_element_type=F32)


def _dot_tn(a, b):
    return lax.dot_general(a, b, (((0,), (0,)), ((), ())), preferred_element_type=F32)


def _ln_proj_kernel(xp_ref, xs_ref, xm_ref, g_ref, b_ref, w_hbm, z_ref, zm_ref,
                    h_scr, hm_scr, wb_scr, stage, sem, *, np_tiles, ns_tiles):
    i = pl.program_id(0)
    j = pl.program_id(1)
    first = j == 0
    is_sample = i >= np_tiles
    is_last = i == np_tiles + ns_tiles - 1
    n_col, _, tn = wb_scr.shape
    chunk = stage.shape[1]
    n_chunks = w_hbm.shape[0] // chunk

    def weight_copy(k):
        return pltpu.make_async_copy(w_hbm.at[pl.ds(k * chunk, chunk), :], stage.at[k % 2], sem.at[k % 2])

    @pl.when(first & (i == 0))
    def _():
        weight_copy(0).start()
        for k in range(n_chunks):
            if k + 1 < n_chunks:
                weight_copy(k + 1).start()
            weight_copy(k).wait()
            for jt in range(n_col):
                wb_scr[jt, k * chunk:(k + 1) * chunk, :] = stage[k % 2, :, jt * tn:(jt + 1) * tn].astype(BF16)

    def norm(x_ref):
        return _layer_norm(x_ref[...], g_ref[...], b_ref[...]).astype(BF16)

    @pl.when(first & jnp.logical_not(is_sample))
    def _():
        h_scr[...] = norm(xp_ref)

    @pl.when(first & is_sample)
    def _():
        h_scr[...] = norm(xs_ref)

    @pl.when(first & is_last)
    def _():
        hm_scr[...] = norm(xm_ref)

    z_ref[...] = _dot(h_scr[...], wb_scr[j])

    @pl.when(is_last)
    def _():
        zm_ref[...] = _dot(hm_scr[...], wb_scr[j])


def _ln_proj(xp, xs, xm, g, b, w, tm, tn):
    np_tiles = xp.shape[0] // tm
    ns_tiles = xs.shape[0] // tm
    last = np_tiles + ns_tiles - 1
    n_col = PROJ_W // tn
    return pl.pallas_call(
        functools.partial(_ln_proj_kernel, np_tiles=np_tiles, ns_tiles=ns_tiles),
        grid=(np_tiles + ns_tiles, n_col),
        in_specs=[
            pl.BlockSpec((tm, D_MODEL), lambda i, j: (jnp.minimum(i, np_tiles - 1), 0)),
            pl.BlockSpec((tm, D_MODEL), lambda i, j: (jnp.clip(i - np_tiles, 0, ns_tiles - 1), 0)),
            pl.BlockSpec(xm.shape, lambda i, j: (0, 0), pipeline_mode=pl.Buffered(1)),
            pl.BlockSpec((1, D_MODEL), lambda i, j: (0, 0)),
            pl.BlockSpec((1, D_MODEL), lambda i, j: (0, 0)),
            pl.BlockSpec(memory_space=pl.ANY),
        ],
        out_specs=[
            pl.BlockSpec((tm, tn), lambda i, j: (i, j)),
            pl.BlockSpec((xm.shape[0], tn), lambda i, j: (0, jnp.where(i == last, j, 0))),
        ],
        out_shape=[
            jax.ShapeDtypeStruct((xp.shape[0] + xs.shape[0], PROJ_W), F32),
            jax.ShapeDtypeStruct((xm.shape[0], PROJ_W), F32),
        ],
        scratch_shapes=[
            pltpu.VMEM((tm, D_MODEL), BF16),
            pltpu.VMEM((xm.shape[0], D_MODEL), BF16),
            pltpu.VMEM((n_col, D_MODEL, tn), BF16),
            pltpu.VMEM((2, WEIGHT_STAGE_ROWS // 4, PROJ_W), F32),
            pltpu.SemaphoreType.DMA((2,)),
        ],
        compiler_params=pltpu.CompilerParams(dimension_semantics=("arbitrary", "arbitrary"),
                                             vmem_limit_bytes=VMEM_LIMIT_LARGE),
        name="ln_proj",
    )(xp, xs, xm, g, b, w)


def _ret_rope(x, cos, sin):
    x1 = x[:, :LANES]
    x2 = x[:, LANES:]
    return jnp.concatenate([x1 * cos - x2 * sin, x2 * cos + x1 * sin], axis=1)


def _group_norm_gate(o, gate):
    mu = jnp.mean(o, axis=-1, keepdims=True)
    oc = o - mu
    var = jnp.mean(oc * oc, axis=-1, keepdims=True)
    return oc * lax.rsqrt(var + LN_EPS) * _silu(gate)


def _ret_prompt_kernel(lg_ref, q_ref, k_ref, v_ref, g_ref, km_ref, vm_ref,
                       cos_ref, sin_ref, cosm_ref, sinm_ref, o_ref, s_ref, s_scr):
    C = RET_CHUNK
    lg = lg_ref[pl.program_id(1)]
    ri = lax.broadcasted_iota(jnp.int32, (C, C), 0)
    ci = lax.broadcasted_iota(jnp.int32, (C, C), 1)
    rel = (ri - ci).astype(F32)
    decay = jnp.where(rel >= 0.0, jnp.exp(jnp.maximum(rel, 0.0) * lg), 0.0)
    row = lax.broadcasted_iota(jnp.int32, (C, 1), 0).astype(F32)
    q_decay = jnp.exp((row + 1.0) * lg)
    k_decay = jnp.exp((C - 1.0 - row) * lg)
    chunk_decay = jnp.exp(jnp.full((1, RET_DV), C * lg, F32))
    scale = RET_DK ** -0.5

    meta_decay = jnp.where(row >= C - N_META, k_decay, 0.0)
    km = _ret_rope(km_ref[...], cosm_ref[...], sinm_ref[...]) * scale
    s_scr[...] = _dot_tn((km * meta_decay).astype(BF16), vm_ref[...].astype(BF16))

    def chunk(c, carry):
        rows = pl.ds(pl.multiple_of(c * C, C), C)
        cos = cos_ref[rows, :]
        sin = sin_ref[rows, :]
        q = _ret_rope(q_ref[rows, :], cos, sin)
        k = _ret_rope(k_ref[rows, :], cos, sin) * scale
        qb = q.astype(BF16)
        vb = v_ref[rows, :].astype(BF16)
        s_prev = s_scr[...]
        scores = _dot_nt(qb, k.astype(BF16)) * decay
        inner = _dot(scores.astype(BF16), vb)
        cross = _dot(qb, s_prev.astype(BF16)) * q_decay
        s_scr[...] = chunk_decay * s_prev + _dot_tn((k * k_decay).astype(BF16), vb)
        o_ref[rows, :] = _group_norm_gate(inner + cross, g_ref[rows, :]).astype(BF16)
        return carry

    lax.fori_loop(0, q_ref.shape[0] // C, chunk, 0, unroll=RET_CHUNK_UNROLL)
    s_ref[...] = s_scr[...]


def _ret_prompt(lg, z, zm, bsz, seq, cos, sin, cosm, sinm):
    col = lambda base: (lambda b, h: (b, base + h))
    mcol = lambda base: (lambda b, h: (0, base + h))
    full = lambda b, h: (0, 0)
    return pl.pallas_call(
        _ret_prompt_kernel,
        grid=(bsz, RET_HEADS),
        in_specs=[
            pl.BlockSpec(memory_space=pltpu.SMEM),
            pl.BlockSpec((seq, RET_DK), col(0)),
            pl.BlockSpec((seq, RET_DK), col(RET_HEADS)),
            pl.BlockSpec((seq, RET_DV), col(2 * RET_HEADS)),
            pl.BlockSpec((seq, RET_DV), col(3 * RET_HEADS)),
            pl.BlockSpec((RET_CHUNK, RET_DK), mcol(RET_HEADS)),
            pl.BlockSpec((RET_CHUNK, RET_DV), mcol(2 * RET_HEADS)),
            pl.BlockSpec((seq, LANES), full),
            pl.BlockSpec((seq, LANES), full),
            pl.BlockSpec((RET_CHUNK, LANES), full),
            pl.BlockSpec((RET_CHUNK, LANES), full),
        ],
        out_specs=[
            pl.BlockSpec((seq, RET_DV), lambda b, h: (b, h)),
            pl.BlockSpec((None, None, RET_DK, RET_DV), lambda b, h: (b, h, 0, 0)),
        ],
        out_shape=[
            jax.ShapeDtypeStruct((bsz * seq, RET_W), BF16),
            jax.ShapeDtypeStruct((bsz, RET_HEADS, RET_DK, RET_DV), F32),
        ],
        scratch_shapes=[pltpu.VMEM((RET_DK, RET_DV), F32)],
        compiler_params=_cparams(("parallel", "parallel")),
        name="ret_prompt",
    )(lg, z, z, z, z, zm, zm, cos, sin, cosm, sinm)


def _ret_sample_jobs(lg_ref, z_ref, s_ref, cos_ref, sin_ref, o_ref, so_ref, *, nb, t):
    rows = nb * t
    R = RET_HEADS * rows
    scale = RET_DK ** -0.5
    cos = cos_ref[...]
    sin = sin_ref[...]

    def stack(base, rope, mul=1.0):
        parts = []
        for h in range(RET_HEADS):
            x = z_ref[:, base + h * RET_DK: base + (h + 1) * RET_DK]
            parts.append(_ret_rope(x, cos, sin) * mul if rope else x)
        return jnp.concatenate(parts, axis=0)

    q = stack(0, True)
    k = stack(RET_W, True, scale)
    v = stack(2 * RET_W, False)
    gate = stack(3 * RET_W, False)

    ri = lax.broadcasted_iota(jnp.int32, (R, R), 0)
    ci = lax.broadcasted_iota(jnp.int32, (R, R), 1)
    rcol = lax.broadcasted_iota(jnp.int32, (R, 1), 0)
    lg_col = jnp.zeros((R, 1), F32)
    for h in range(RET_HEADS):
        lg_col = jnp.where((rcol >= h * rows) & (rcol < (h + 1) * rows), lg_ref[h], lg_col)
    tcol = (rcol & (t - 1)).astype(F32)
    rel = (ri - ci).astype(F32)
    same = ((ri & -t) == (ci & -t)) & (ri >= ci)
    decay = jnp.where(same, jnp.exp(jnp.maximum(rel, 0.0) * lg_col), 0.0)
    q_decay = jnp.exp((tcol + 1.0) * lg_col)
    k_decay = jnp.exp((t - 1.0 - tcol) * lg_col)

    qb = q.astype(BF16)
    vb = v.astype(BF16)
    scores = _dot_nt(qb, k.astype(BF16)) * decay
    inner = _dot(scores.astype(BF16), vb)
    kw = k * k_decay

    cross_parts = [None] * (RET_HEADS * nb)

    def state_job(h, db):
        def run():
            r0 = h * rows + db * t
            s_prev = s_ref[db, h]
            cross_parts[h * nb + db] = _dot(q[r0:r0 + t, :].astype(BF16), s_prev.astype(BF16))
            mine = (rcol >= r0) & (rcol < r0 + t)
            upd = _dot_tn(jnp.where(mine, kw, 0.0).astype(BF16), vb)
            step_decay = jnp.exp(jnp.full((1, RET_DV), t * lg_ref[h], F32))
            so_ref[db, h] = step_decay * s_prev + upd
        return run

    def finish():
        cross = jnp.concatenate(cross_parts, axis=0) * q_decay
        out = _group_norm_gate(inner + cross, gate).astype(BF16)
        for h in range(RET_HEADS):
            o_ref[:, h * RET_DV:(h + 1) * RET_DV] = out[h * rows:(h + 1) * rows, :]

    return [state_job(h, db) for h in range(RET_HEADS) for db in range(nb)], finish


def _swa_rope(x, c, s1, s2):
    return x * c + pltpu.roll(x, 8, 1) * s1 + pltpu.roll(x, LANES - 8, 1) * s2


def _dup_head(x, g, low):
    swapped = pltpu.roll(x, SWA_HD, 1)
    return jnp.where(low, x, swapped) if g == 0 else jnp.where(low, swapped, x)


def _to_kv_half(slab, head, low):
    g = head // SWA_GROUP
    src = slab if head % 2 == g else pltpu.roll(slab, SWA_HD, 1)
    return jnp.where(low, src, 0.0) if g == 0 else jnp.where(low, 0.0, src)


def _from_kv_half(o_even, o_odd, g, low):
    if g == 0:
        return jnp.where(low, o_even, pltpu.roll(o_odd, SWA_HD, 1))
    return jnp.where(low, pltpu.roll(o_even, SWA_HD, 1), o_odd)


def _sink_softmax(logits, sink):
    m = jnp.maximum(jnp.max(logits, axis=-1, keepdims=True), sink)
    p = jnp.exp(logits - m)
    return p, jnp.sum(p, axis=-1, keepdims=True) + jnp.exp(sink - m)


def _swa_prompt_kernel(sink_ref, q_ref, kv_ref, kvm_ref, c_ref, s1_ref, s2_ref,
                       cm_ref, s1m_ref, s2m_ref, o_ref, kp_ref, vp_ref, kprev, vprev, make_side_jobs=lambda: []):
    W = WINDOW
    m_id = pl.program_id(1)

    @pl.when(m_id == 0)
    def _():
        kprev[...] = _swa_rope(kvm_ref[:, :LANES], cm_ref[...], s1m_ref[...], s2m_ref[...])
        vprev[...] = kvm_ref[:, LANES:]

    side_jobs = make_side_jobs()
    c = c_ref[...]
    s1 = s1_ref[...]
    s2 = s2_ref[...]
    kcur = _swa_rope(kv_ref[:, :LANES], c, s1, s2)
    vcur = kv_ref[:, LANES:]
    k2 = jnp.concatenate([kprev[...], kcur], axis=0)
    v2 = jnp.concatenate([vprev[...], vcur], axis=0)

    r = lax.broadcasted_iota(jnp.int32, (W, 2 * W), 0)
    cc = lax.broadcasted_iota(jnp.int32, (W, 2 * W), 1)
    diff = W + r - cc
    mask = (diff >= 0) & (diff < W) & ((m_id > 0) | (cc >= W - N_META))
    low_k = lax.broadcasted_iota(jnp.int32, (2 * W, LANES), 1) < SWA_HD
    low = lax.broadcasted_iota(jnp.int32, (W, LANES), 1) < SWA_HD
    scale = SWA_HD ** -0.5

    def attend(q, kd, vd, h):
        p, den = _sink_softmax(jnp.where(mask, _dot_nt(q.astype(BF16), kd), NEG_INF), sink_ref[h])
        return _dot(p.astype(BF16), vd) / den

    for g in range(SWA_KV_HEADS):
        kd = _dup_head(k2, g, low_k).astype(BF16)
        vd = _dup_head(v2, g, low_k).astype(BF16)
        for p in range(g * SWA_GROUP // 2, (g + 1) * SWA_GROUP // 2):
            slab = _swa_rope(q_ref[:, p * LANES:(p + 1) * LANES], c, s1, s2) * scale
            o_even = attend(jnp.where(low, slab, 0.0), kd, vd, 2 * p)
            o_odd = attend(jnp.where(low, 0.0, slab), kd, vd, 2 * p + 1)
            o_ref[:, p * LANES:(p + 1) * LANES] = jnp.where(low, o_even, o_odd).astype(BF16)
    for job in side_jobs:
        job()

    kprev[...] = kcur
    vprev[...] = vcur
    kp_ref[...] = kcur
    vp_ref[...] = vcur


def _swa_prompt_ret_sample_kernel(sink_ref, q_ref, kv_ref, kvm_ref, c_ref, s1_ref, s2_ref, cm_ref, s1m_ref, s2m_ref,
                                 lg_ref, zs_ref, st_ref, cos_ref, sin_ref,
                                 o_ref, kp_ref, vp_ref, os_ref, sto_ref, kprev, vprev, *, nb, t):
    def retention_jobs():
        state_jobs, finish = _ret_sample_jobs(lg_ref, zs_ref, st_ref, cos_ref, sin_ref, os_ref, sto_ref, nb=nb, t=t)
        return state_jobs + [finish]

    _swa_prompt_kernel(sink_ref, q_ref, kv_ref, kvm_ref, c_ref, s1_ref, s2_ref, cm_ref, s1m_ref, s2m_ref,
                       o_ref, kp_ref, vp_ref, kprev, vprev, make_side_jobs=retention_jobs)


def _swa_prompt_ret_sample(sinks, z, zm, bsz, seq, c, s1, s2, cm, s1m, s2m, lg, sample_row, state, cos, sin, t):
    W = WINDOW
    nblk = seq // W
    steps = bsz * nblk
    dbs = state.shape[0]
    nb = dbs // steps
    assert nb * steps == dbs and (nb * t) % 16 == 0
    rows = nb * t
    qw = SWA_HEADS * SWA_HD
    kv_col = SWA_KV_COL // (2 * LANES)
    tab = pl.BlockSpec((W, LANES), lambda b, m: (m, 0))
    mtab = pl.BlockSpec((W, LANES), lambda b, m: (0, 0))
    stab = pl.BlockSpec((rows, LANES), lambda b, m: (0, 0))
    state_spec = pl.BlockSpec((nb, RET_HEADS, RET_DK, RET_DV), lambda b, m: (b * nblk + m, 0, 0, 0))
    return pl.pallas_call(
        functools.partial(_swa_prompt_ret_sample_kernel, nb=nb, t=t),
        grid=(bsz, nblk),
        in_specs=[
            pl.BlockSpec(memory_space=pltpu.SMEM),
            pl.BlockSpec((W, qw), lambda b, m: (b * nblk + m, SWA_Q_COL // qw)),
            pl.BlockSpec((W, 2 * LANES), lambda b, m: (b * nblk + m, kv_col)),
            pl.BlockSpec((W, 2 * LANES), lambda b, m: (0, kv_col)),
            tab, tab, tab, mtab, mtab, mtab,
            pl.BlockSpec(memory_space=pltpu.SMEM),
            pl.BlockSpec((rows, 4 * RET_W), lambda b, m: (sample_row // rows + b * nblk + m, 0)),
            state_spec, stab, stab,
        ],
        out_specs=[
            pl.BlockSpec((W, qw), lambda b, m: (b * nblk + m, 0)),
            pl.BlockSpec((None, W, LANES), lambda b, m: (b, 0, 0)),
            pl.BlockSpec((None, W, LANES), lambda b, m: (b, 0, 0)),
            pl.BlockSpec((rows, RET_W), lambda b, m: (b * nblk + m, 0)),
            state_spec,
        ],
        out_shape=[
            jax.ShapeDtypeStruct((bsz * seq, qw), BF16),
            jax.ShapeDtypeStruct((bsz, W, LANES), F32),
            jax.ShapeDtypeStruct((bsz, W, LANES), F32),
            jax.ShapeDtypeStruct((dbs * t, RET_W), BF16),
            jax.ShapeDtypeStruct(state.shape, F32),
        ],
        scratch_shapes=[pltpu.VMEM((W, LANES), F32), pltpu.VMEM((W, LANES), F32)],
        compiler_params=_cparams(("parallel", "arbitrary")),
        name="swa_prompt_ret_sample",
    )(sinks, z, z, zm, c, s1, s2, cm, s1m, s2m, lg, z, state, cos, sin)


def _swa_sample_kernel(sink_ref, q_ref, kv_ref, kc_ref, vc_ref, c_ref, s1_ref, s2_ref,
                       o_ref, ko_ref, vo_ref, *, nb, t):
    W = WINDOW
    keys = 2 * W
    c = c_ref[...]
    s1 = s1_ref[...]
    s2 = s2_ref[...]
    knew = _swa_rope(kv_ref[:, :LANES], c, s1, s2)
    vnew = kv_ref[:, LANES:]
    pad = jnp.zeros((keys - W - t, LANES), F32)

    rq = SWA_HEADS * t
    tq = lax.broadcasted_iota(jnp.int32, (rq, keys), 0) & (t - 1)
    cc = lax.broadcasted_iota(jnp.int32, (rq, keys), 1)
    diff = W + tq - cc
    mask = (diff >= 0) & (diff < W)
    low = lax.broadcasted_iota(jnp.int32, (nb * t, LANES), 1) < SWA_HD
    low_t = lax.broadcasted_iota(jnp.int32, (t, LANES), 1) < SWA_HD
    sink_col = jnp.concatenate([jnp.full((t, 1), sink_ref[h], F32) for h in range(SWA_HEADS)], axis=0)
    npair = SWA_HEADS // 2

    scale = SWA_HD ** -0.5
    slabs = [_swa_rope(q_ref[:, p * LANES:(p + 1) * LANES], c, s1, s2) * scale for p in range(npair)]
    q_heads = [_to_kv_half(slabs[h // 2], h, low) for h in range(SWA_HEADS)]

    logits, values = [], []
    for db in range(nb):
        rows = slice(db * t, (db + 1) * t)
        kn = knew[rows, :]
        vn = vnew[rows, :]
        ko_ref[db, 0:W - t, :] = kc_ref[db, t:W, :]
        ko_ref[db, W - t:W, :] = kn
        vo_ref[db, 0:W - t, :] = vc_ref[db, t:W, :]
        vo_ref[db, W - t:W, :] = vn
        k2 = jnp.concatenate([kc_ref[db], kn, pad], axis=0).astype(BF16)
        values.append(jnp.concatenate([vc_ref[db], vn, pad], axis=0).astype(BF16))
        q_db = jnp.concatenate([qh[rows, :] for qh in q_heads], axis=0).astype(BF16)
        logits.append(jnp.where(mask, _dot_nt(q_db, k2), NEG_INF))
    p, den = _sink_softmax(jnp.concatenate(logits, axis=0), jnp.concatenate([sink_col] * nb, axis=0))
    p = p.astype(BF16)
    outs = [[] for _ in range(npair)]
    for db in range(nb):
        o = _dot(p[db * rq:(db + 1) * rq, :], values[db]) / den[db * rq:(db + 1) * rq, :]
        for pr in range(npair):
            o_even = o[(2 * pr) * t:(2 * pr + 1) * t, :]
            o_odd = o[(2 * pr + 1) * t:(2 * pr + 2) * t, :]
            outs[pr].append(_from_kv_half(o_even, o_odd, 2 * pr // SWA_GROUP, low_t))
    for pr in range(npair):
        o_ref[:, pr * LANES:(pr + 1) * LANES] = jnp.concatenate(outs[pr], axis=0).astype(BF16)


def _swa_sample(sinks, z, row0, kc, vc, c, s1, s2, nb, t):
    dbs = kc.shape[0]
    n = dbs * t
    rows = nb * t
    W = WINDOW
    qw = SWA_HEADS * SWA_HD
    tab = pl.BlockSpec((rows, LANES), lambda i: (0, 0))
    cache = pl.BlockSpec((nb, W, LANES), lambda i: (i, 0, 0))
    return pl.pallas_call(
        functools.partial(_swa_sample_kernel, nb=nb, t=t),
        grid=(dbs // nb,),
        in_specs=[
            pl.BlockSpec(memory_space=pltpu.SMEM),
            pl.BlockSpec((rows, qw), lambda i: (row0 // rows + i, SWA_Q_COL // qw)),
            pl.BlockSpec((rows, 2 * LANES), lambda i: (row0 // rows + i, SWA_KV_COL // (2 * LANES))),
            cache, cache, tab, tab, tab,
        ],
        out_specs=[pl.BlockSpec((rows, qw), lambda i: (i, 0)), cache, cache],
        out_shape=[
            jax.ShapeDtypeStruct((n, qw), BF16),
            jax.ShapeDtypeStruct((dbs, W, LANES), F32),
            jax.ShapeDtypeStruct((dbs, W, LANES), F32),
        ],
        compiler_params=_cparams(("parallel",)),
        name="swa_sample",
    )(sinks, z, z, kc, vc, c, s1, s2)


def _out_proj_ln_kernel(ro_ref, so_ref, x_ref, eg_ref, eb_ref, w_hbm, mg_ref, mb_ref, h_ref, hb_ref,
                        wb_scr, stage, sem):
    chunk = stage.shape[1]
    n_chunks = w_hbm.shape[0] // chunk

    def weight_copy(k):
        return pltpu.make_async_copy(w_hbm.at[pl.ds(k * chunk, chunk), :], stage.at[k % 2], sem.at[k % 2])

    @pl.when(pl.program_id(0) == 0)
    def _():
        weight_copy(0).start()
        for k in range(n_chunks):
            if k + 1 < n_chunks:
                weight_copy(k + 1).start()
            weight_copy(k).wait()
            wb_scr[k * chunk:(k + 1) * chunk, :] = stage[k % 2].astype(BF16)

    x_in = _layer_norm(x_ref[...], eg_ref[...], eb_ref[...])
    mixed = _dot(ro_ref[...], wb_scr[:RET_W, :]) + _dot(so_ref[...], wb_scr[RET_W:, :])
    h = _layer_norm(ALPHA * x_in + mixed, mg_ref[...], mb_ref[...])
    h_ref[...] = h
    hb_ref[...] = h.astype(BF16)


def _out_proj_ln(ro, so, x, eg, eb, w, mg, mb, tm):
    n = x.shape[0]
    vec = pl.BlockSpec((1, D_MODEL), lambda i: (0, 0))
    tile = pl.BlockSpec((tm, D_MODEL), lambda i: (i, 0))
    return pl.pallas_call(
        _out_proj_ln_kernel,
        grid=(n // tm,),
        in_specs=[
            pl.BlockSpec((tm, RET_W), lambda i: (i, 0)),
            pl.BlockSpec((tm, RET_W), lambda i: (i, 0)),
            tile, vec, vec,
            pl.BlockSpec(memory_space=pl.ANY),
            vec, vec,
        ],
        out_specs=[tile, tile],
        out_shape=[jax.ShapeDtypeStruct((n, D_MODEL), F32), jax.ShapeDtypeStruct((n, D_MODEL), BF16)],
        scratch_shapes=[pltpu.VMEM((D_MODEL, D_MODEL), BF16),
                        pltpu.VMEM((2, WEIGHT_STAGE_ROWS, D_MODEL), F32),
                        pltpu.SemaphoreType.DMA((2,))],
        compiler_params=_cparams(("arbitrary",)),
        name="out_proj_ln",
    )(ro, so, x, eg, eb, w, mg, mb)


def _ffn_ln_kernel(hb_ref, h_hbm, wg_ref, wu_ref, wd_ref, g_ref, b_ref, y_ref, *rest):
    *w_out_refs, h_res, sem = rest
    i = pl.program_id(0)
    j = pl.program_id(1)
    tm = h_res.shape[0]
    residual_copy = pltpu.make_async_copy(h_hbm.at[pl.ds(i * tm, tm), :], h_res, sem)

    @pl.when(j == 0)
    def _():
        residual_copy.start()
        y_ref[...] = jnp.zeros_like(y_ref)

    wg, wu, wd = (w[...].astype(BF16) for w in (wg_ref, wu_ref, wd_ref))
    for out_ref, w in zip(w_out_refs, (wg, wu, wd)):
        out_ref[...] = w
    hb = hb_ref[...]
    act = _silu(_dot(hb, wg)) * _dot(hb, wu)
    y_ref[...] += _dot(act.astype(BF16), wd)

    @pl.when(j == pl.num_programs(1) - 1)
    def _():
        residual_copy.wait()
        y_ref[...] = _layer_norm(ALPHA * h_res[...] + y_ref[...], g_ref[...], b_ref[...])


def _ffn_ln(h, hb, wg, wu, wd, g, b, tm, th):
    n = h.shape[0]
    emit_weights = wg.dtype == F32
    assert not emit_weights or n == tm
    vec = pl.BlockSpec((1, D_MODEL), lambda i, j: (0, 0))
    w_specs = [
        pl.BlockSpec((D_MODEL, th), lambda i, j: (0, j)),
        pl.BlockSpec((D_MODEL, th), lambda i, j: (0, j)),
        pl.BlockSpec((th, D_MODEL), lambda i, j: (j, 0)),
    ]
    out_specs = [pl.BlockSpec((tm, D_MODEL), lambda i, j: (i, 0))]
    out_shape = [jax.ShapeDtypeStruct((n, D_MODEL), F32)]
    if emit_weights:
        out_specs += w_specs
        out_shape += [jax.ShapeDtypeStruct(w.shape, BF16) for w in (wg, wu, wd)]
    outs = pl.pallas_call(
        _ffn_ln_kernel,
        grid=(n // tm, FFN_HIDDEN // th),
        in_specs=[pl.BlockSpec((tm, D_MODEL), lambda i, j: (i, 0)), pl.BlockSpec(memory_space=pl.ANY)] + w_specs
        + [vec, vec],
        out_specs=out_specs,
        out_shape=out_shape,
        scratch_shapes=[pltpu.VMEM((tm, D_MODEL), F32), pltpu.SemaphoreType.DMA(())],
        compiler_params=pltpu.CompilerParams(dimension_semantics=("arbitrary", "arbitrary"),
                                             vmem_limit_bytes=VMEM_LIMIT_LARGE),
        name="ffn_ln",
    )(hb, h, wg, wu, wd, g, b)
    return tuple(outs) if emit_weights else outs[0]


def _ret_tables(pos):
    freq = jnp.power(RET_THETA, -jnp.linspace(0.0, 1.0, RET_DK // 2, dtype=F32))
    ang = pos.astype(F32)[:, None] * freq[None, :]
    return jnp.cos(ang), jnp.sin(ang)


def _swa_tables(pos):
    half = ROT_DIM // 2
    freq = jnp.power(ROPE_THETA, -jnp.arange(0, ROT_DIM, 2, dtype=F32) / ROT_DIM)
    ang = pos.astype(F32)[:, None] * freq[None, :]
    cos, sin = jnp.cos(ang), jnp.sin(ang)
    n = pos.shape[0]
    rest = SWA_HD - ROT_DIM
    c = jnp.concatenate([cos, cos, jnp.ones((n, rest), F32)], axis=1)
    s1 = jnp.concatenate([jnp.zeros((n, half), F32), sin, jnp.zeros((n, rest), F32)], axis=1)
    s2 = jnp.concatenate([-sin, jnp.zeros((n, half + rest), F32)], axis=1)
    return tuple(jnp.tile(a, (1, LANES // SWA_HD)) for a in (c, s1, s2))


def _front_pad(a, rows):
    return jnp.pad(a, ((rows - a.shape[0], 0), (0, 0)))


def kernel(x_prompt, x_sample, state_ret, cache_swa_k, cache_swa_v, meta_tokens, ln_emb_g, ln_emb_b,
           w_in, w_out, swa_sinks, ln_mix_g, ln_mix_b, w_ffn_gate, w_ffn_up, w_ffn_down, ln_ffn_g, ln_ffn_b):
    bsz, seq, d = x_prompt.shape
    dbs, t, _ = x_sample.shape
    assert w_in.shape[0] == DEPTH and d == D_MODEL and seq % RET_CHUNK == 0 and t & (t - 1) == 0
    past_len = 16384
    row = lambda a: a.reshape(1, -1)

    eg, eb = row(ln_emb_g), row(ln_emb_b)
    sinks = swa_sinks[0]
    lg = jnp.log(1.0 - jnp.power(2.0, -5.0 - jnp.arange(RET_HEADS, dtype=F32)))

    xp = x_prompt.reshape(bsz * seq, d)
    xs = x_sample.reshape(dbs * t, d)
    xm = _front_pad(meta_tokens, RET_CHUNK)

    z, zm = _ln_proj(xp, xs, xm, eg, eb, w_in[0], 512, PROJ_W // 3)
    sample_row = bsz * seq

    pos_main = N_META + jnp.arange(seq)
    pos_meta = jnp.maximum(jnp.arange(RET_CHUNK) - (RET_CHUNK - N_META), 0)
    nb_ret = dbs // (bsz * (seq // WINDOW))
    nb_swa = 8
    pos_s = past_len + jnp.arange(t)
    cos_p, sin_p = _ret_tables(pos_main)
    cos_m, sin_m = _ret_tables(pos_meta)
    cos_s, sin_s = (jnp.tile(a, (nb_ret, 1)) for a in _ret_tables(pos_s))
    swa_tab_p = _swa_tables(pos_main)
    swa_tab_m = _swa_tables(pos_meta)
    swa_tab_s = tuple(jnp.tile(a, (nb_swa, 1)) for a in _swa_tables(pos_s))

    ret_o_p, ret_state_p = _ret_prompt(lg, z, zm, bsz, seq, cos_p, sin_p, cos_m, sin_m)
    swa_o_p, k_p, v_p, ret_o_s, ret_state_s = _swa_prompt_ret_sample(
        sinks, z, zm, bsz, seq, *swa_tab_p, *swa_tab_m, lg, sample_row, state_ret[0], cos_s, sin_s, t)
    kc = cache_swa_k[0].reshape(dbs, WINDOW, LANES)
    vc = cache_swa_v[0].reshape(dbs, WINDOW, LANES)
    swa_o_s, k_s, v_s = _swa_sample(sinks, z, sample_row, kc, vc, *swa_tab_s, nb_swa, t)

    mg, mb = row(ln_mix_g[0]), row(ln_mix_b[0])
    fg, fb = row(ln_ffn_g[0]), row(ln_ffn_b[0])
    ffn_w = (w_ffn_gate[0], w_ffn_up[0], w_ffn_down[0])
    h_p, hb_p = _out_proj_ln(ret_o_p, swa_o_p, xp, eg, eb, w_out[0], mg, mb, 512)
    h_s, hb_s = _out_proj_ln(ret_o_s, swa_o_s, xs, eg, eb, w_out[0], mg, mb, 512)
    y_s, *ffn_wb = _ffn_ln(h_s, hb_s, *ffn_w, fg, fb, TOK_TILE, 256)
    y_p = _ffn_ln(h_p, hb_p, *ffn_wb, fg, fb, TOK_TILE, 512)

    kv_shape = (DEPTH, -1, WINDOW, SWA_KV_HEADS, SWA_HD)
    return (y_p.reshape(bsz, seq, d), y_s.reshape(dbs, t, d),
            ret_state_p[None], k_p.reshape(kv_shape), v_p.reshape(kv_shape),
            ret_state_s[None], k_s.reshape(kv_shape), v_s.reshape(kv_shape))
```

```python
import functools

import jax
import jax.numpy as jnp
from jax import lax
from jax.experimental import pallas as pl
from jax.experimental.pallas import tpu as pltpu

F32 = jnp.float32
BF16 = jnp.bfloat16

D_MODEL = 2048
N_META = 16
RET_HEADS = 4
RET_DK = 256
RET_DV = 256
RET_CHUNK = 128
RET_THETA = 10000.0
SWA_HD = 64
SWA_HEADS = 16
SWA_KV_HEADS = 2
SWA_GROUP = SWA_HEADS // SWA_KV_HEADS
WINDOW = 128
ROPE_THETA = 500000.0
ROT_DIM = SWA_HD // 4
FFN_HIDDEN = 5632
PROJ_W = 5376
DEPTH = 1
ALPHA = (2.0 * DEPTH) ** 0.25
LN_EPS = 1e-5
NEG_INF = -1e30

LANES = 128
RET_W = RET_HEADS * RET_DK
SWA_Q_COL = 4 * RET_W
SWA_KV_COL = SWA_Q_COL + SWA_HEADS * SWA_HD
VMEM_LIMIT = 56 * 1024 * 1024
VMEM_LIMIT_LARGE = 60 * 1024 * 1024
TOK_TILE = 1024
RET_CHUNK_UNROLL = 16
SAMPLE_TABLE_ROWS = 64
WEIGHT_STAGE_ROWS = 256


def _cparams(sem):
    return pltpu.CompilerParams(dimension_semantics=sem, vmem_limit_bytes=VMEM_LIMIT)


def _layer_norm(x, g, b):
    mu = jnp.mean(x, axis=-1, keepdims=True)
    xc = x - mu
    var = jnp.mean(xc * xc, axis=-1, keepdims=True)
    return xc * lax.rsqrt(var + LN_EPS) * g + b


def _silu(x):
    return x / (1.0 + jnp.exp(-x))


def _dot(a, b):
    return jnp.dot(a, b, preferred_element_type=F32)


def _dot_nt(a, b):
    return lax.dot_general(a, b, (((1,), (1,)), ((), ())), preferred_element_type=F32)


def _dot_tn(a, b):
    return lax.dot_general(a, b, (((0,), (0,)), ((), ())), preferred_element_type=F32)


def _ln_proj_kernel(xp_ref, xs_ref, xm_ref, g_ref, b_ref, w_hbm, z_ref, zm_ref,
                    h_scr, hm_scr, wb_scr, stage, sem, *, np_tiles, ns_tiles):
    i = pl.program_id(0)
    j = pl.program_id(1)
    first = j == 0
    is_sample = i >= np_tiles
    is_last = i == np_tiles + ns_tiles - 1
    n_col, _, tn = wb_scr.shape
    chunk = stage.shape[1]
    n_chunks = w_hbm.shape[0] // chunk

    def weight_copy(k):
        return pltpu.make_async_copy(w_hbm.at[pl.ds(k * chunk, chunk), :], stage.at[k % 2], sem.at[k % 2])

    @pl.when(first & (i == 0))
    def _():
        weight_copy(0).start()
        for k in range(n_chunks):
            if k + 1 < n_chunks:
                weight_copy(k + 1).start()
            weight_copy(k).wait()
            for jt in range(n_col):
                wb_scr[jt, k * chunk:(k + 1) * chunk, :] = stage[k % 2, :, jt * tn:(jt + 1) * tn].astype(BF16)

    def norm(x_ref):
        return _layer_norm(x_ref[...], g_ref[...], b_ref[...]).astype(BF16)

    @pl.when(first & jnp.logical_not(is_sample))
    def _():
        h_scr[...] = norm(xp_ref)

    @pl.when(first & is_sample)
    def _():
        h_scr[...] = norm(xs_ref)

    @pl.when(first & is_last)
    def _():
        lead = hm_scr.shape[0] - xm_ref.shape[0]
        hm_scr[:lead, :] = jnp.zeros((lead, D_MODEL), BF16)
        hm_scr[lead:, :] = norm(xm_ref)

    z_ref[...] = _dot(h_scr[...], wb_scr[j])

    @pl.when(is_last)
    def _():
        zm_ref[...] = _dot(hm_scr[...], wb_scr[j])


def _ln_proj(xp, xs, xm, g, b, w, tm, tn):
    np_tiles = xp.shape[0] // tm
    ns_tiles = xs.shape[0] // tm
    last = np_tiles + ns_tiles - 1
    n_col = PROJ_W // tn
    return pl.pallas_call(
        functools.partial(_ln_proj_kernel, np_tiles=np_tiles, ns_tiles=ns_tiles),
        grid=(np_tiles + ns_tiles, n_col),
        in_specs=[
            pl.BlockSpec((tm, D_MODEL), lambda i, j: (jnp.minimum(i, np_tiles - 1), 0)),
            pl.BlockSpec((tm, D_MODEL), lambda i, j: (jnp.clip(i - np_tiles, 0, ns_tiles - 1), 0)),
            pl.BlockSpec(xm.shape, lambda i, j: (0, 0), pipeline_mode=pl.Buffered(1)),
            pl.BlockSpec((1, D_MODEL), lambda i, j: (0, 0)),
            pl.BlockSpec((1, D_MODEL), lambda i, j: (0, 0)),
            pl.BlockSpec(memory_space=pl.ANY),
        ],
        out_specs=[
            pl.BlockSpec((tm, tn), lambda i, j: (i, j)),
            pl.BlockSpec((RET_CHUNK, tn), lambda i, j: (0, jnp.where(i == last, j, 0))),
        ],
        out_shape=[
            jax.ShapeDtypeStruct((xp.shape[0] + xs.shape[0], PROJ_W), F32),
            jax.ShapeDtypeStruct((RET_CHUNK, PROJ_W), F32),
        ],
        scratch_shapes=[
            pltpu.VMEM((tm, D_MODEL), BF16),
            pltpu.VMEM((RET_CHUNK, D_MODEL), BF16),
            pltpu.VMEM((n_col, D_MODEL, tn), BF16),
            pltpu.VMEM((2, WEIGHT_STAGE_ROWS // 4, PROJ_W), F32),
            pltpu.SemaphoreType.DMA((2,)),
        ],
        compiler_params=pltpu.CompilerParams(dimension_semantics=("arbitrary", "arbitrary"),
                                             vmem_limit_bytes=VMEM_LIMIT_LARGE),
        name="ln_proj",
    )(xp, xs, xm, g, b, w)


def _ret_rope(x, cos, sin):
    x1 = x[:, :LANES]
    x2 = x[:, LANES:]
    return jnp.concatenate([x1 * cos - x2 * sin, x2 * cos + x1 * sin], axis=1)


def _group_norm_gate(o, gate):
    mu = jnp.mean(o, axis=-1, keepdims=True)
    oc = o - mu
    var = jnp.mean(oc * oc, axis=-1, keepdims=True)
    return oc * lax.rsqrt(var + LN_EPS) * _silu(gate)


def _ret_prompt_kernel(lg_ref, q_ref, k_ref, v_ref, g_ref, km_ref, vm_ref,
                       cos_ref, sin_ref, cosm_ref, sinm_ref, o_ref, s_ref, s_scr):
    C = RET_CHUNK
    lg = lg_ref[pl.program_id(1)]
    ri = lax.broadcasted_iota(jnp.int32, (C, C), 0)
    ci = lax.broadcasted_iota(jnp.int32, (C, C), 1)
    rel = (ri - ci).astype(F32)
    decay = jnp.where(rel >= 0.0, jnp.exp(jnp.maximum(rel, 0.0) * lg), 0.0)
    row = lax.broadcasted_iota(jnp.int32, (C, 1), 0).astype(F32)
    q_decay = jnp.exp((row + 1.0) * lg)
    k_decay = jnp.exp((C - 1.0 - row) * lg)
    chunk_decay = jnp.exp(jnp.full((1, RET_DV), C * lg, F32))
    scale = RET_DK ** -0.5

    meta_decay = jnp.where(row >= C - N_META, k_decay, 0.0)
    km = _ret_rope(km_ref[...], cosm_ref[...], sinm_ref[...]) * scale
    s_scr[...] = _dot_tn((km * meta_decay).astype(BF16), vm_ref[...].astype(BF16))

    def chunk(c, carry):
        rows = pl.ds(pl.multiple_of(c * C, C), C)
        cos = cos_ref[rows, :]
        sin = sin_ref[rows, :]
        q = _ret_rope(q_ref[rows, :], cos, sin)
        k = _ret_rope(k_ref[rows, :], cos, sin) * scale
        qb = q.astype(BF16)
        vb = v_ref[rows, :].astype(BF16)
        s_prev = s_scr[...]
        scores = _dot_nt(qb, k.astype(BF16)) * decay
        inner = _dot(scores.astype(BF16), vb)
        cross = _dot(qb, s_prev.astype(BF16)) * q_decay
        s_scr[...] = chunk_decay * s_prev + _dot_tn((k * k_decay).astype(BF16), vb)
        o_ref[rows, :] = _group_norm_gate(inner + cross, g_ref[rows, :]).astype(BF16)
        return carry

    lax.fori_loop(0, q_ref.shape[0] // C, chunk, 0, unroll=RET_CHUNK_UNROLL)
    s_ref[...] = s_scr[...]


def _ret_prompt(lg, z, zm, bsz, seq, cos, sin):
    col = lambda base: (lambda b, h: (b, base + h))
    mcol = lambda base: (lambda b, h: (0, base + h))
    full = lambda b, h: (0, 0)
    meta_tab = lambda b, h: (seq // RET_CHUNK, 0)
    return pl.pallas_call(
        _ret_prompt_kernel,
        grid=(bsz, RET_HEADS),
        in_specs=[
            pl.BlockSpec(memory_space=pltpu.SMEM),
            pl.BlockSpec((seq, RET_DK), col(0)),
            pl.BlockSpec((seq, RET_DK), col(RET_HEADS)),
            pl.BlockSpec((seq, RET_DV), col(2 * RET_HEADS)),
            pl.BlockSpec((seq, RET_DV), col(3 * RET_HEADS)),
            pl.BlockSpec((RET_CHUNK, RET_DK), mcol(RET_HEADS)),
            pl.BlockSpec((RET_CHUNK, RET_DV), mcol(2 * RET_HEADS)),
            pl.BlockSpec((seq, LANES), full),
            pl.BlockSpec((seq, LANES), full),
            pl.BlockSpec((RET_CHUNK, LANES), meta_tab),
            pl.BlockSpec((RET_CHUNK, LANES), meta_tab),
        ],
        out_specs=[
            pl.BlockSpec((seq, RET_DV), lambda b, h: (b, h)),
            pl.BlockSpec((None, None, RET_DK, RET_DV), lambda b, h: (b, h, 0, 0)),
        ],
        out_shape=[
            jax.ShapeDtypeStruct((bsz * seq, RET_W), BF16),
            jax.ShapeDtypeStruct((bsz, RET_HEADS, RET_DK, RET_DV), F32),
        ],
        scratch_shapes=[pltpu.VMEM((RET_DK, RET_DV), F32)],
        compiler_params=_cparams(("parallel", "parallel")),
        name="ret_prompt",
    )(lg, z, z, z, z, zm, zm, cos, sin, cos, sin)


def _ret_sample_jobs(lg_ref, z_ref, s_ref, cos_ref, sin_ref, o_ref, so_ref, *, nb, t):
    rows = nb * t
    R = RET_HEADS * rows
    scale = RET_DK ** -0.5
    cos = cos_ref[...]
    sin = sin_ref[...]

    def stack(base, rope, mul=1.0):
        parts = []
        for h in range(RET_HEADS):
            x = z_ref[:, base + h * RET_DK: base + (h + 1) * RET_DK]
            parts.append(_ret_rope(x, cos, sin) * mul if rope else x)
        return jnp.concatenate(parts, axis=0)

    q = stack(0, True)
    k = stack(RET_W, True, scale)
    v = stack(2 * RET_W, False)
    gate = stack(3 * RET_W, False)

    ri = lax.broadcasted_iota(jnp.int32, (R, R), 0)
    ci = lax.broadcasted_iota(jnp.int32, (R, R), 1)
    rcol = lax.broadcasted_iota(jnp.int32, (R, 1), 0)
    lg_col = jnp.zeros((R, 1), F32)
    for h in range(RET_HEADS):
        lg_col = jnp.where((rcol >= h * rows) & (rcol < (h + 1) * rows), lg_ref[h], lg_col)
    tcol = (rcol & (t - 1)).astype(F32)
    rel = (ri - ci).astype(F32)
    same = ((ri & -t) == (ci & -t)) & (ri >= ci)
    decay = jnp.where(same, jnp.exp(jnp.maximum(rel, 0.0) * lg_col), 0.0)
    q_decay = jnp.exp((tcol + 1.0) * lg_col)
    k_decay = jnp.exp((t - 1.0 - tcol) * lg_col)

    qb = q.astype(BF16)
    vb = v.astype(BF16)
    scores = _dot_nt(qb, k.astype(BF16)) * decay
    inner = _dot(scores.astype(BF16), vb)
    kw = k * k_decay

    cross_parts = [None] * (RET_HEADS * nb)

    def state_job(h, db):
        def run():
            r0 = h * rows + db * t
            s_prev = s_ref[db, h]
            cross_parts[h * nb + db] = _dot(q[r0:r0 + t, :].astype(BF16), s_prev.astype(BF16))
            mine = (rcol >= r0) & (rcol < r0 + t)
            upd = _dot_tn(jnp.where(mine, kw, 0.0).astype(BF16), vb)
            step_decay = jnp.exp(jnp.full((1, RET_DV), t * lg_ref[h], F32))
            so_ref[db, h] = step_decay * s_prev + upd
        return run

    def finish():
        cross = jnp.concatenate(cross_parts, axis=0) * q_decay
        out = _group_norm_gate(inner + cross, gate).astype(BF16)
        for h in range(RET_HEADS):
            o_ref[:, h * RET_DV:(h + 1) * RET_DV] = out[h * rows:(h + 1) * rows, :]

    return [state_job(h, db) for h in range(RET_HEADS) for db in range(nb)], finish


def _swa_rope(x, c, s1, s2):
    return x * c + pltpu.roll(x, 8, 1) * s1 + pltpu.roll(x, LANES - 8, 1) * s2


def _dup_head(x, g, low):
    swapped = pltpu.roll(x, SWA_HD, 1)
    return jnp.where(low, x, swapped) if g == 0 else jnp.where(low, swapped, x)


def _to_kv_half(slab, head, low):
    g = head // SWA_GROUP
    src = slab if head % 2 == g else pltpu.roll(slab, SWA_HD, 1)
    return jnp.where(low, src, 0.0) if g == 0 else jnp.where(low, 0.0, src)


def _from_kv_half(o_even, o_odd, g, low):
    if g == 0:
        return jnp.where(low, o_even, pltpu.roll(o_odd, SWA_HD, 1))
    return jnp.where(low, pltpu.roll(o_even, SWA_HD, 1), o_odd)


def _sink_softmax(logits, sink):
    m = jnp.maximum(jnp.max(logits, axis=-1, keepdims=True), sink)
    p = jnp.exp(logits - m)
    return p, jnp.sum(p, axis=-1, keepdims=True) + jnp.exp(sink - m)


def _swa_prompt_kernel(sink_ref, q_ref, kv_ref, kvm_ref, c_ref, s1_ref, s2_ref,
                       cm_ref, s1m_ref, s2m_ref, o_ref, kp_ref, vp_ref, kprev, vprev, make_side_jobs=lambda: []):
    W = WINDOW
    m_id = pl.program_id(1)

    @pl.when(m_id == 0)
    def _():
        kprev[...] = _swa_rope(kvm_ref[:, :LANES], cm_ref[...], s1m_ref[...], s2m_ref[...])
        vprev[...] = kvm_ref[:, LANES:]

    side_jobs = make_side_jobs()
    c = c_ref[...]
    s1 = s1_ref[...]
    s2 = s2_ref[...]
    kcur = _swa_rope(kv_ref[:, :LANES], c, s1, s2)
    vcur = kv_ref[:, LANES:]
    k2 = jnp.concatenate([kprev[...], kcur], axis=0)
    v2 = jnp.concatenate([vprev[...], vcur], axis=0)

    r = lax.broadcasted_iota(jnp.int32, (W, 2 * W), 0)
    cc = lax.broadcasted_iota(jnp.int32, (W, 2 * W), 1)
    diff = W + r - cc
    mask = (diff >= 0) & (diff < W) & ((m_id > 0) | (cc >= W - N_META))
    low_k = lax.broadcasted_iota(jnp.int32, (2 * W, LANES), 1) < SWA_HD
    low = lax.broadcasted_iota(jnp.int32, (W, LANES), 1) < SWA_HD
    scale = SWA_HD ** -0.5

    def attend(q, kd, vd, h):
        p, den = _sink_softmax(jnp.where(mask, _dot_nt(q.astype(BF16), kd), NEG_INF), sink_ref[h])
        return _dot(p.astype(BF16), vd) / den

    for g in range(SWA_KV_HEADS):
        kd = _dup_head(k2, g, low_k).astype(BF16)
        vd = _dup_head(v2, g, low_k).astype(BF16)
        for p in range(g * SWA_GROUP // 2, (g + 1) * SWA_GROUP // 2):
            slab = _swa_rope(q_ref[:, p * LANES:(p + 1) * LANES], c, s1, s2) * scale
            o_even = attend(jnp.where(low, slab, 0.0), kd, vd, 2 * p)
            o_odd = attend(jnp.where(low, 0.0, slab), kd, vd, 2 * p + 1)
            o_ref[:, p * LANES:(p + 1) * LANES] = jnp.where(low, o_even, o_odd).astype(BF16)
    for job in side_jobs:
        job()

    kprev[...] = kcur
    vprev[...] = vcur
    kp_ref[...] = kcur
    vp_ref[...] = vcur


def _swa_prompt_ret_sample_kernel(sink_ref, q_ref, kv_ref, kvm_ref, c_ref, s1_ref, s2_ref, cm_ref, s1m_ref, s2m_ref,
                                 lg_ref, zs_ref, st_ref, cos_ref, sin_ref,
                                 o_ref, kp_ref, vp_ref, os_ref, sto_ref, kprev, vprev, *, nb, t):
    def retention_jobs():
        state_jobs, finish = _ret_sample_jobs(lg_ref, zs_ref, st_ref, cos_ref, sin_ref, os_ref, sto_ref, nb=nb, t=t)
        return state_jobs + [finish]

    _swa_prompt_kernel(sink_ref, q_ref, kv_ref, kvm_ref, c_ref, s1_ref, s2_ref, cm_ref, s1m_ref, s2m_ref,
                       o_ref, kp_ref, vp_ref, kprev, vprev, make_side_jobs=retention_jobs)


def _swa_prompt_ret_sample(sinks, z, zm, bsz, seq, c, s1, s2, lg, sample_row, state, cos, sin, t):
    W = WINDOW
    nblk = seq // W
    steps = bsz * nblk
    dbs = state.shape[0]
    nb = dbs // steps
    assert nb * steps == dbs and (nb * t) % 16 == 0
    rows = nb * t
    qw = SWA_HEADS * SWA_HD
    kv_col = SWA_KV_COL // (2 * LANES)
    tab = pl.BlockSpec((W, LANES), lambda b, m: (m, 0))
    mtab = pl.BlockSpec((W, LANES), lambda b, m: (seq // W, 0))
    stab = pl.BlockSpec((rows, LANES), lambda b, m: ((seq + RET_CHUNK) // rows, 0))
    state_spec = pl.BlockSpec((nb, RET_HEADS, RET_DK, RET_DV), lambda b, m: (b * nblk + m, 0, 0, 0))
    return pl.pallas_call(
        functools.partial(_swa_prompt_ret_sample_kernel, nb=nb, t=t),
        grid=(bsz, nblk),
        in_specs=[
            pl.BlockSpec(memory_space=pltpu.SMEM),
            pl.BlockSpec((W, qw), lambda b, m: (b * nblk + m, SWA_Q_COL // qw)),
            pl.BlockSpec((W, 2 * LANES), lambda b, m: (b * nblk + m, kv_col)),
            pl.BlockSpec((W, 2 * LANES), lambda b, m: (0, kv_col)),
            tab, tab, tab, mtab, mtab, mtab,
            pl.BlockSpec(memory_space=pltpu.SMEM),
            pl.BlockSpec((rows, 4 * RET_W), lambda b, m: (sample_row // rows + b * nblk + m, 0)),
            state_spec, stab, stab,
        ],
        out_specs=[
            pl.BlockSpec((W, qw), lambda b, m: (b * nblk + m, 0)),
            pl.BlockSpec((None, W, LANES), lambda b, m: (b, 0, 0)),
            pl.BlockSpec((None, W, LANES), lambda b, m: (b, 0, 0)),
            pl.BlockSpec((rows, RET_W), lambda b, m: (b * nblk + m, 0)),
            state_spec,
        ],
        out_shape=[
            jax.ShapeDtypeStruct((bsz * seq, qw), BF16),
            jax.ShapeDtypeStruct((bsz, W, LANES), F32),
            jax.ShapeDtypeStruct((bsz, W, LANES), F32),
            jax.ShapeDtypeStruct((dbs * t, RET_W), BF16),
            jax.ShapeDtypeStruct(state.shape, F32),
        ],
        scratch_shapes=[pltpu.VMEM((W, LANES), F32), pltpu.VMEM((W, LANES), F32)],
        compiler_params=_cparams(("parallel", "arbitrary")),
        name="swa_prompt_ret_sample",
    )(sinks, z, z, zm, c, s1, s2, c, s1, s2, lg, z, state, cos, sin)


def _swa_sample_kernel(sink_ref, q_ref, kv_ref, kc_ref, vc_ref, c_ref, s1_ref, s2_ref,
                       o_ref, ko_ref, vo_ref, *, nb, t):
    W = WINDOW
    keys = 2 * W
    c = c_ref[...]
    s1 = s1_ref[...]
    s2 = s2_ref[...]
    knew = _swa_rope(kv_ref[:, :LANES], c, s1, s2)
    vnew = kv_ref[:, LANES:]
    pad = jnp.zeros((keys - W - t, LANES), F32)

    rq = SWA_HEADS * t
    tq = lax.broadcasted_iota(jnp.int32, (rq, keys), 0) & (t - 1)
    cc = lax.broadcasted_iota(jnp.int32, (rq, keys), 1)
    diff = W + tq - cc
    mask = (diff >= 0) & (diff < W)
    low = lax.broadcasted_iota(jnp.int32, (nb * t, LANES), 1) < SWA_HD
    low_t = lax.broadcasted_iota(jnp.int32, (t, LANES), 1) < SWA_HD
    sink_col = jnp.concatenate([jnp.full((t, 1), sink_ref[h], F32) for h in range(SWA_HEADS)], axis=0)
    npair = SWA_HEADS // 2

    scale = SWA_HD ** -0.5
    slabs = [_swa_rope(q_ref[:, p * LANES:(p + 1) * LANES], c, s1, s2) * scale for p in range(npair)]
    q_heads = [_to_kv_half(slabs[h // 2], h, low) for h in range(SWA_HEADS)]

    logits, values = [], []
    for db in range(nb):
        rows = slice(db * t, (db + 1) * t)
        kn = knew[rows, :]
        vn = vnew[rows, :]
        ko_ref[db, 0:W - t, :] = kc_ref[db, t:W, :]
        ko_ref[db, W - t:W, :] = kn
        vo_ref[db, 0:W - t, :] = vc_ref[db, t:W, :]
        vo_ref[db, W - t:W, :] = vn
        k2 = jnp.concatenate([kc_ref[db], kn, pad], axis=0).astype(BF16)
        values.append(jnp.concatenate([vc_ref[db], vn, pad], axis=0).astype(BF16))
        q_db = jnp.concatenate([qh[rows, :] for qh in q_heads], axis=0).astype(BF16)
        logits.append(jnp.where(mask, _dot_nt(q_db, k2), NEG_INF))
    p, den = _sink_softmax(jnp.concatenate(logits, axis=0), jnp.concatenate([sink_col] * nb, axis=0))
    p = p.astype(BF16)
    outs = [[] for _ in range(npair)]
    for db in range(nb):
        o = _dot(p[db * rq:(db + 1) * rq, :], values[db]) / den[db * rq:(db + 1) * rq, :]
        for pr in range(npair):
            o_even = o[(2 * pr) * t:(2 * pr + 1) * t, :]
            o_odd = o[(2 * pr + 1) * t:(2 * pr + 2) * t, :]
            outs[pr].append(_from_kv_half(o_even, o_odd, 2 * pr // SWA_GROUP, low_t))
    for pr in range(npair):
        o_ref[:, pr * LANES:(pr + 1) * LANES] = jnp.concatenate(outs[pr], axis=0).astype(BF16)


def _swa_sample(sinks, z, row0, kc, vc, c, s1, s2, tab_row0, t):
    dbs = kc.shape[0]
    n = dbs * t
    rows = SAMPLE_TABLE_ROWS
    nb = rows // t
    W = WINDOW
    qw = SWA_HEADS * SWA_HD
    tab = pl.BlockSpec((rows, LANES), lambda i: (tab_row0 // rows, 0))
    cache = pl.BlockSpec((nb, W, LANES), lambda i: (i, 0, 0))
    return pl.pallas_call(
        functools.partial(_swa_sample_kernel, nb=nb, t=t),
        grid=(dbs // nb,),
        in_specs=[
            pl.BlockSpec(memory_space=pltpu.SMEM),
            pl.BlockSpec((rows, qw), lambda i: (row0 // rows + i, SWA_Q_COL // qw)),
            pl.BlockSpec((rows, 2 * LANES), lambda i: (row0 // rows + i, SWA_KV_COL // (2 * LANES))),
            cache, cache, tab, tab, tab,
        ],
        out_specs=[pl.BlockSpec((rows, qw), lambda i: (i, 0)), cache, cache],
        out_shape=[
            jax.ShapeDtypeStruct((n, qw), BF16),
            jax.ShapeDtypeStruct((dbs, W, LANES), F32),
            jax.ShapeDtypeStruct((dbs, W, LANES), F32),
        ],
        compiler_params=_cparams(("parallel",)),
        name="swa_sample",
    )(sinks, z, z, kc, vc, c, s1, s2)


def _out_proj_ln_kernel(ro_ref, so_ref, x_ref, eg_ref, eb_ref, w_hbm, mg_ref, mb_ref, h_ref, hb_ref,
                        wb_scr, stage, sem):
    chunk = stage.shape[1]
    n_chunks = w_hbm.shape[0] // chunk

    def weight_copy(k):
        return pltpu.make_async_copy(w_hbm.at[pl.ds(k * chunk, chunk), :], stage.at[k % 2], sem.at[k % 2])

    @pl.when(pl.program_id(0) == 0)
    def _():
        weight_copy(0).start()
        for k in range(n_chunks):
            if k + 1 < n_chunks:
                weight_copy(k + 1).start()
            weight_copy(k).wait()
            wb_scr[k * chunk:(k + 1) * chunk, :] = stage[k % 2].astype(BF16)

    sub = h_ref.shape[0] // 2
    for r in range(2):
        rows = slice(r * sub, (r + 1) * sub)
        x_in = _layer_norm(x_ref[rows, :], eg_ref[...], eb_ref[...])
        mixed = _dot(ro_ref[rows, :], wb_scr[:RET_W, :]) + _dot(so_ref[rows, :], wb_scr[RET_W:, :])
        h = _layer_norm(ALPHA * x_in + mixed, mg_ref[...], mb_ref[...])
        h_ref[rows, :] = h
        hb_ref[rows, :] = h.astype(BF16)


def _out_proj_ln(ro, so, x, eg, eb, w, mg, mb, tm):
    n = x.shape[0]
    vec = pl.BlockSpec((1, D_MODEL), lambda i: (0, 0))
    tile = pl.BlockSpec((tm, D_MODEL), lambda i: (i, 0))
    return pl.pallas_call(
        _out_proj_ln_kernel,
        grid=(n // tm,),
        in_specs=[
            pl.BlockSpec((tm, RET_W), lambda i: (i, 0)),
            pl.BlockSpec((tm, RET_W), lambda i: (i, 0)),
            tile, vec, vec,
            pl.BlockSpec(memory_space=pl.ANY),
            vec, vec,
        ],
        out_specs=[tile, tile],
        out_shape=[jax.ShapeDtypeStruct((n, D_MODEL), F32), jax.ShapeDtypeStruct((n, D_MODEL), BF16)],
        scratch_shapes=[pltpu.VMEM((D_MODEL, D_MODEL), BF16),
                        pltpu.VMEM((2, WEIGHT_STAGE_ROWS, D_MODEL), F32),
                        pltpu.SemaphoreType.DMA((2,))],
        compiler_params=_cparams(("arbitrary",)),
        name="out_proj_ln",
    )(ro, so, x, eg, eb, w, mg, mb)


def _ffn_ln_kernel(hb_ref, h_hbm, wg_ref, wu_ref, wd_ref, g_ref, b_ref, y_ref, *rest):
    *w_out_refs, h_res, sem = rest
    i = pl.program_id(0)
    j = pl.program_id(1)
    tm = h_res.shape[0]
    residual_copy = pltpu.make_async_copy(h_hbm.at[pl.ds(i * tm, tm), :], h_res, sem)

    @pl.when(j == 0)
    def _():
        residual_copy.start()
        y_ref[...] = jnp.zeros_like(y_ref)

    wg, wu, wd = (w[...].astype(BF16) for w in (wg_ref, wu_ref, wd_ref))
    for out_ref, w in zip(w_out_refs, (wg, wu, wd)):
        out_ref[...] = w
    hb = hb_ref[...]
    act = _silu(_dot(hb, wg)) * _dot(hb, wu)
    y_ref[...] += _dot(act.astype(BF16), wd)

    @pl.when(j == pl.num_programs(1) - 1)
    def _():
        residual_copy.wait()
        y_ref[...] = _layer_norm(ALPHA * h_res[...] + y_ref[...], g_ref[...], b_ref[...])


def _ffn_ln(h, hb, wg, wu, wd, g, b, tm, th):
    n = h.shape[0]
    emit_weights = wg.dtype == F32
    assert not emit_weights or n == tm
    vec = pl.BlockSpec((1, D_MODEL), lambda i, j: (0, 0))
    w_specs = [
        pl.BlockSpec((D_MODEL, th), lambda i, j: (0, j)),
        pl.BlockSpec((D_MODEL, th), lambda i, j: (0, j)),
        pl.BlockSpec((th, D_MODEL), lambda i, j: (j, 0)),
    ]
    out_specs = [pl.BlockSpec((tm, D_MODEL), lambda i, j: (i, 0))]
    out_shape = [jax.ShapeDtypeStruct((n, D_MODEL), F32)]
    if emit_weights:
        out_specs += w_specs
        out_shape += [jax.ShapeDtypeStruct(w.shape, BF16) for w in (wg, wu, wd)]
    outs = pl.pallas_call(
        _ffn_ln_kernel,
        grid=(n // tm, FFN_HIDDEN // th),
        in_specs=[pl.BlockSpec((tm, D_MODEL), lambda i, j: (i, 0)), pl.BlockSpec(memory_space=pl.ANY)] + w_specs
        + [vec, vec],
        out_specs=out_specs,
        out_shape=out_shape,
        scratch_shapes=[pltpu.VMEM((tm, D_MODEL), F32), pltpu.SemaphoreType.DMA(())],
        compiler_params=pltpu.CompilerParams(dimension_semantics=("arbitrary", "arbitrary"),
                                             vmem_limit_bytes=VMEM_LIMIT_LARGE),
        name="ffn_ln",
    )(hb, h, wg, wu, wd, g, b)
    return tuple(outs) if emit_weights else outs[0]


def _position_tables(seq, t, past_len):
    r = jnp.arange(seq + RET_CHUNK + SAMPLE_TABLE_ROWS)
    meta_r = r - seq
    pos = jnp.where(r < seq, N_META + r,
                    jnp.where(meta_r < RET_CHUNK, jnp.maximum(meta_r - (RET_CHUNK - N_META), 0),
                              past_len + (meta_r - RET_CHUNK) % t)).astype(F32)[:, None]
    ret_freq = jnp.power(RET_THETA, -jnp.linspace(0.0, 1.0, RET_DK // 2, dtype=F32))
    ret_ang = pos * ret_freq[None, :]
    half = ROT_DIM // 2
    swa_freq = jnp.power(ROPE_THETA, -jnp.arange(0, ROT_DIM, 2, dtype=F32) / ROT_DIM)
    swa_ang = pos * swa_freq[None, :]
    cos = jnp.tile(jnp.cos(swa_ang), (1, LANES // half))
    sin = jnp.tile(jnp.sin(swa_ang), (1, LANES // half))
    d = jnp.arange(LANES) % SWA_HD
    c = jnp.where(d < ROT_DIM, cos, 1.0)
    s1 = jnp.where((d >= half) & (d < ROT_DIM), sin, 0.0)
    s2 = jnp.where(d < half, -sin, 0.0)
    return (jnp.cos(ret_ang), jnp.sin(ret_ang)), (c, s1, s2)


def kernel(x_prompt, x_sample, state_ret, cache_swa_k, cache_swa_v, meta_tokens, ln_emb_g, ln_emb_b,
           w_in, w_out, swa_sinks, ln_mix_g, ln_mix_b, w_ffn_gate, w_ffn_up, w_ffn_down, ln_ffn_g, ln_ffn_b):
    bsz, seq, d = x_prompt.shape
    dbs, t, _ = x_sample.shape
    assert w_in.shape[0] == DEPTH and d == D_MODEL and seq % RET_CHUNK == 0 and t & (t - 1) == 0
    past_len = 16384
    row = lambda a: a.reshape(1, -1)

    eg, eb = row(ln_emb_g), row(ln_emb_b)
    sinks = swa_sinks[0]
    lg = jnp.log(1.0 - jnp.power(2.0, -5.0 - jnp.arange(RET_HEADS, dtype=F32)))

    xp = x_prompt.reshape(bsz * seq, d)
    xs = x_sample.reshape(dbs * t, d)

    z, zm = _ln_proj(xp, xs, meta_tokens, eg, eb, w_in[0], 512, PROJ_W // 3)
    sample_row = bsz * seq

    (cos, sin), swa_tabs = _position_tables(seq, t, past_len)
    ret_o_p, ret_state_p = _ret_prompt(lg, z, zm, bsz, seq, cos, sin)
    swa_o_p, k_p, v_p, ret_o_s, ret_state_s = _swa_prompt_ret_sample(
        sinks, z, zm, bsz, seq, *swa_tabs, lg, sample_row, state_ret[0], cos, sin, t)
    kc = cache_swa_k[0].reshape(dbs, WINDOW, LANES)
    vc = cache_swa_v[0].reshape(dbs, WINDOW, LANES)
    swa_o_s, k_s, v_s = _swa_sample(sinks, z, sample_row, kc, vc, *swa_tabs, seq + RET_CHUNK, t)

    mg, mb = row(ln_mix_g[0]), row(ln_mix_b[0])
    fg, fb = row(ln_ffn_g[0]), row(ln_ffn_b[0])
    ffn_w = (w_ffn_gate[0], w_ffn_up[0], w_ffn_down[0])
    h_p, hb_p = _out_proj_ln(ret_o_p, swa_o_p, xp, eg, eb, w_out[0], mg, mb, 512)
    h_s, hb_s = _out_proj_ln(ret_o_s, swa_o_s, xs, eg, eb, w_out[0], mg, mb, 512)
    y_s, *ffn_wb = _ffn_ln(h_s, hb_s, *ffn_w, fg, fb, TOK_TILE, 256)
    y_p = _ffn_ln(h_p, hb_p, *ffn_wb, fg, fb, TOK_TILE, 512)

    kv_shape = (DEPTH, -1, WINDOW, SWA_KV_HEADS, SWA_HD)
    return (y_p.reshape(bsz, seq, d), y_s.reshape(dbs, t, d),
            ret_state_p[None], k_p.reshape(kv_shape), v_p.reshape(kv_shape),
            ret_state_s[None], k_s.reshape(kv_shape), v_s.reshape(kv_shape))
```

```python
import functools

import jax
import jax.numpy as jnp
from jax import lax
from jax.experimental import pallas as pl
from jax.experimental.pallas import tpu as pltpu

F32 = jnp.float32
BF16 = jnp.bfloat16

D_MODEL = 2048
N_META = 16
RET_HEADS = 4
RET_DK = 256
RET_DV = 256
RET_CHUNK = 128
RET_THETA = 10000.0
SWA_HD = 64
SWA_HEADS = 16
SWA_KV_HEADS = 2
SWA_GROUP = SWA_HEADS // SWA_KV_HEADS
WINDOW = 128
ROPE_THETA = 500000.0
ROT_DIM = SWA_HD // 4
FFN_HIDDEN = 5632
PROJ_W = 5376
DEPTH = 1
ALPHA = (2.0 * DEPTH) ** 0.25
LN_EPS = 1e-5
NEG_INF = -1e30

LANES = 128
RET_W = RET_HEADS * RET_DK
SWA_Q_COL = 4 * RET_W
SWA_KV_COL = SWA_Q_COL + SWA_HEADS * SWA_HD
VMEM_LIMIT = 56 * 1024 * 1024
VMEM_LIMIT_LARGE = 60 * 1024 * 1024
TOK_TILE = 1024
RET_CHUNK_UNROLL = 16
RET_SAMPLE_EVERY = 2
SAMPLE_TABLE_ROWS = 64
WEIGHT_STAGE_ROWS = 256


def _cparams(sem):
    return pltpu.CompilerParams(dimension_semantics=sem, vmem_limit_bytes=VMEM_LIMIT)


def _layer_norm(x, g, b):
    mu = jnp.mean(x, axis=-1, keepdims=True)
    xc = x - mu
    var = jnp.mean(xc * xc, axis=-1, keepdims=True)
    return xc * lax.rsqrt(var + LN_EPS) * g + b


def _silu(x):
    return x / (1.0 + jnp.exp(-x))


def _dot(a, b):
    return jnp.dot(a, b, preferred_element_type=F32)


def _dot_nt(a, b):
    return lax.dot_general(a, b, (((1,), (1,)), ((), ())), preferred_element_type=F32)


def _dot_tn(a, b):
    return lax.dot_general(a, b, (((0,), (0,)), ((), ())), preferred_element_type=F32)


def _ln_proj_kernel(xp_ref, xs_ref, xm_ref, g_ref, b_ref, w_hbm, z_ref, zm_ref,
                    h_scr, hm_scr, wb_scr, stage, sem, *, np_tiles, ns_tiles):
    i = pl.program_id(0)
    j = pl.program_id(1)
    first = j == 0
    is_sample = i >= np_tiles
    is_last = i == np_tiles + ns_tiles - 1
    n_col, _, tn = wb_scr.shape
    chunk = stage.shape[1]
    n_chunks = w_hbm.shape[0] // chunk

    def weight_copy(k):
        return pltpu.make_async_copy(w_hbm.at[pl.ds(k * chunk, chunk), :], stage.at[k % 2], sem.at[k % 2])

    @pl.when(first & (i == 0))
    def _():
        weight_copy(0).start()
        for k in range(n_chunks):
            if k + 1 < n_chunks:
                weight_copy(k + 1).start()
            weight_copy(k).wait()
            for jt in range(n_col):
                wb_scr[jt, k * chunk:(k + 1) * chunk, :] = stage[k % 2, :, jt * tn:(jt + 1) * tn].astype(BF16)

    def norm(x_ref):
        return _layer_norm(x_ref[...], g_ref[...], b_ref[...]).astype(BF16)

    @pl.when(first & jnp.logical_not(is_sample))
    def _():
        h_scr[...] = norm(xp_ref)

    @pl.when(first & is_sample)
    def _():
        h_scr[...] = norm(xs_ref)

    @pl.when(first & is_last)
    def _():
        lead = hm_scr.shape[0] - xm_ref.shape[0]
        hm_scr[:lead, :] = jnp.zeros((lead, D_MODEL), BF16)
        hm_scr[lead:, :] = norm(xm_ref)

    z_ref[...] = _dot(h_scr[...], wb_scr[j])

    @pl.when(is_last)
    def _():
        zm_ref[...] = _dot(hm_scr[...], wb_scr[j])


def _ln_proj(xp, xs, xm, g, b, w, tm, tn):
    np_tiles = xp.shape[0] // tm
    ns_tiles = xs.shape[0] // tm
    last = np_tiles + ns_tiles - 1
    n_col = PROJ_W // tn
    return pl.pallas_call(
        functools.partial(_ln_proj_kernel, np_tiles=np_tiles, ns_tiles=ns_tiles),
        grid=(np_tiles + ns_tiles, n_col),
        in_specs=[
            pl.BlockSpec((tm, D_MODEL), lambda i, j: (jnp.minimum(i, np_tiles - 1), 0)),
            pl.BlockSpec((tm, D_MODEL), lambda i, j: (jnp.clip(i - np_tiles, 0, ns_tiles - 1), 0)),
            pl.BlockSpec(xm.shape, lambda i, j: (0, 0), pipeline_mode=pl.Buffered(1)),
            pl.BlockSpec((1, D_MODEL), lambda i, j: (0, 0)),
            pl.BlockSpec((1, D_MODEL), lambda i, j: (0, 0)),
            pl.BlockSpec(memory_space=pl.ANY),
        ],
        out_specs=[
            pl.BlockSpec((tm, tn), lambda i, j: (i, j)),
            pl.BlockSpec((RET_CHUNK, tn), lambda i, j: (0, jnp.where(i == last, j, 0))),
        ],
        out_shape=[
            jax.ShapeDtypeStruct((xp.shape[0] + xs.shape[0], PROJ_W), F32),
            jax.ShapeDtypeStruct((RET_CHUNK, PROJ_W), F32),
        ],
        scratch_shapes=[
            pltpu.VMEM((tm, D_MODEL), BF16),
            pltpu.VMEM((RET_CHUNK, D_MODEL), BF16),
            pltpu.VMEM((n_col, D_MODEL, tn), BF16),
            pltpu.VMEM((2, WEIGHT_STAGE_ROWS // 4, PROJ_W), F32),
            pltpu.SemaphoreType.DMA((2,)),
        ],
        compiler_params=pltpu.CompilerParams(dimension_semantics=("arbitrary", "arbitrary"),
                                             vmem_limit_bytes=VMEM_LIMIT_LARGE),
        name="ln_proj",
    )(xp, xs, xm, g, b, w)


def _ret_rope(x, cos, sin):
    x1 = x[:, :LANES]
    x2 = x[:, LANES:]
    return jnp.concatenate([x1 * cos - x2 * sin, x2 * cos + x1 * sin], axis=1)


def _group_norm_gate(o, gate):
    mu = jnp.mean(o, axis=-1, keepdims=True)
    oc = o - mu
    var = jnp.mean(oc * oc, axis=-1, keepdims=True)
    return oc * lax.rsqrt(var + LN_EPS) * _silu(gate)


def _ret_prompt_kernel(lg_ref, q_ref, k_ref, v_ref, g_ref, km_ref, vm_ref,
                       cos_ref, sin_ref, cosm_ref, sinm_ref, o_ref, s_ref, s_scr):
    C = RET_CHUNK
    lg = lg_ref[pl.program_id(1)]
    ri = lax.broadcasted_iota(jnp.int32, (C, C), 0)
    ci = lax.broadcasted_iota(jnp.int32, (C, C), 1)
    rel = (ri - ci).astype(F32)
    decay = jnp.where(rel >= 0.0, jnp.exp(jnp.maximum(rel, 0.0) * lg), 0.0)
    row = lax.broadcasted_iota(jnp.int32, (C, 1), 0).astype(F32)
    q_decay = jnp.exp((row + 1.0) * lg)
    k_decay = jnp.exp((C - 1.0 - row) * lg)
    chunk_decay = jnp.exp(jnp.full((1, RET_DV), C * lg, F32))
    scale = RET_DK ** -0.5

    meta_decay = jnp.where(row >= C - N_META, k_decay, 0.0)
    km = _ret_rope(km_ref[...], cosm_ref[...], sinm_ref[...]) * scale
    s_scr[...] = _dot_tn((km * meta_decay).astype(BF16), vm_ref[...].astype(BF16))

    def chunk(c, carry):
        rows = pl.ds(pl.multiple_of(c * C, C), C)
        cos = cos_ref[rows, :]
        sin = sin_ref[rows, :]
        q = _ret_rope(q_ref[rows, :], cos, sin)
        k = _ret_rope(k_ref[rows, :], cos, sin) * scale
        qb = q.astype(BF16)
        vb = v_ref[rows, :].astype(BF16)
        s_prev = s_scr[...]
        scores = _dot_nt(qb, k.astype(BF16)) * decay
        inner = _dot(scores.astype(BF16), vb)
        cross = _dot(qb, s_prev.astype(BF16)) * q_decay
        s_scr[...] = chunk_decay * s_prev + _dot_tn((k * k_decay).astype(BF16), vb)
        o_ref[rows, :] = _group_norm_gate(inner + cross, g_ref[rows, :]).astype(BF16)
        return carry

    lax.fori_loop(0, q_ref.shape[0] // C, chunk, 0, unroll=RET_CHUNK_UNROLL)
    s_ref[...] = s_scr[...]


def _ret_prompt(lg, z, zm, bsz, seq, cos, sin):
    col = lambda base: (lambda b, h: (b, base + h))
    mcol = lambda base: (lambda b, h: (0, base + h))
    full = lambda b, h: (0, 0)
    meta_tab = lambda b, h: (seq // RET_CHUNK, 0)
    return pl.pallas_call(
        _ret_prompt_kernel,
        grid=(bsz, RET_HEADS),
        in_specs=[
            pl.BlockSpec(memory_space=pltpu.SMEM),
            pl.BlockSpec((seq, RET_DK), col(0)),
            pl.BlockSpec((seq, RET_DK), col(RET_HEADS)),
            pl.BlockSpec((seq, RET_DV), col(2 * RET_HEADS)),
            pl.BlockSpec((seq, RET_DV), col(3 * RET_HEADS)),
            pl.BlockSpec((RET_CHUNK, RET_DK), mcol(RET_HEADS)),
            pl.BlockSpec((RET_CHUNK, RET_DV), mcol(2 * RET_HEADS)),
            pl.BlockSpec((seq, LANES), full),
            pl.BlockSpec((seq, LANES), full),
            pl.BlockSpec((RET_CHUNK, LANES), meta_tab),
            pl.BlockSpec((RET_CHUNK, LANES), meta_tab),
        ],
        out_specs=[
            pl.BlockSpec((seq, RET_DV), lambda b, h: (b, h)),
            pl.BlockSpec((None, None, RET_DK, RET_DV), lambda b, h: (b, h, 0, 0)),
        ],
        out_shape=[
            jax.ShapeDtypeStruct((bsz * seq, RET_W), BF16),
            jax.ShapeDtypeStruct((bsz, RET_HEADS, RET_DK, RET_DV), F32),
        ],
        scratch_shapes=[pltpu.VMEM((RET_DK, RET_DV), F32)],
        compiler_params=_cparams(("parallel", "parallel")),
        name="ret_prompt",
    )(lg, z, z, z, z, zm, zm, cos, sin, cos, sin)


def _ret_sample_jobs(lg_ref, z_ref, s_ref, cos_ref, sin_ref, o_ref, so_ref, *, nb, t):
    rows = nb * t
    R = RET_HEADS * rows
    scale = RET_DK ** -0.5
    cos = cos_ref[...]
    sin = sin_ref[...]

    def stack(base, rope, mul=1.0):
        parts = []
        for h in range(RET_HEADS):
            x = z_ref[:, base + h * RET_DK: base + (h + 1) * RET_DK]
            parts.append(_ret_rope(x, cos, sin) * mul if rope else x)
        return jnp.concatenate(parts, axis=0)

    q = stack(0, True)
    k = stack(RET_W, True, scale)
    v = stack(2 * RET_W, False)
    gate = stack(3 * RET_W, False)

    ri = lax.broadcasted_iota(jnp.int32, (R, R), 0)
    ci = lax.broadcasted_iota(jnp.int32, (R, R), 1)
    rcol = lax.broadcasted_iota(jnp.int32, (R, 1), 0)
    lg_col = jnp.zeros((R, 1), F32)
    for h in range(RET_HEADS):
        lg_col = jnp.where((rcol >= h * rows) & (rcol < (h + 1) * rows), lg_ref[h], lg_col)
    tcol = (rcol & (t - 1)).astype(F32)
    rel = (ri - ci).astype(F32)
    same = ((ri & -t) == (ci & -t)) & (ri >= ci)
    decay = jnp.where(same, jnp.exp(jnp.maximum(rel, 0.0) * lg_col), 0.0)
    q_decay = jnp.exp((tcol + 1.0) * lg_col)
    k_decay = jnp.exp((t - 1.0 - tcol) * lg_col)

    qb = q.astype(BF16)
    vb = v.astype(BF16)
    scores = _dot_nt(qb, k.astype(BF16)) * decay
    inner = _dot(scores.astype(BF16), vb)
    kw = k * k_decay

    cross_parts = [None] * (RET_HEADS * nb)

    def state_job(h, db):
        def run():
            r0 = h * rows + db * t
            s_prev = s_ref[db, h]
            cross_parts[h * nb + db] = _dot(q[r0:r0 + t, :].astype(BF16), s_prev.astype(BF16))
            mine = (rcol >= r0) & (rcol < r0 + t)
            upd = _dot_tn(jnp.where(mine, kw, 0.0).astype(BF16), vb)
            step_decay = jnp.exp(jnp.full((1, RET_DV), t * lg_ref[h], F32))
            so_ref[db, h] = step_decay * s_prev + upd
        return run

    def finish():
        cross = jnp.concatenate(cross_parts, axis=0) * q_decay
        out = _group_norm_gate(inner + cross, gate).astype(BF16)
        for h in range(RET_HEADS):
            o_ref[:, h * RET_DV:(h + 1) * RET_DV] = out[h * rows:(h + 1) * rows, :]

    return [state_job(h, db) for h in range(RET_HEADS) for db in range(nb)], finish


def _swa_rope(x, c, s1, s2):
    return x * c + pltpu.roll(x, 8, 1) * s1 + pltpu.roll(x, LANES - 8, 1) * s2


def _dup_head(x, g, low):
    swapped = pltpu.roll(x, SWA_HD, 1)
    return jnp.where(low, x, swapped) if g == 0 else jnp.where(low, swapped, x)


def _to_kv_half(slab, head, low):
    g = head // SWA_GROUP
    src = slab if head % 2 == g else pltpu.roll(slab, SWA_HD, 1)
    return jnp.where(low, src, 0.0) if g == 0 else jnp.where(low, 0.0, src)


def _from_kv_half(o_even, o_odd, g, low):
    if g == 0:
        return jnp.where(low, o_even, pltpu.roll(o_odd, SWA_HD, 1))
    return jnp.where(low, pltpu.roll(o_even, SWA_HD, 1), o_odd)


def _sink_softmax(logits, sink):
    m = jnp.maximum(jnp.max(logits, axis=-1, keepdims=True), sink)
    p = jnp.exp(logits - m)
    return p, jnp.sum(p, axis=-1, keepdims=True) + jnp.exp(sink - m)


def _swa_prompt_kernel(sink_ref, q_ref, kv_ref, kvm_ref, c_ref, s1_ref, s2_ref,
                       cm_ref, s1m_ref, s2m_ref, o_ref, kp_ref, vp_ref, kprev, vprev):
    W = WINDOW
    m_id = pl.program_id(1)

    @pl.when(m_id == 0)
    def _():
        kprev[...] = _swa_rope(kvm_ref[:, :LANES], cm_ref[...], s1m_ref[...], s2m_ref[...])
        vprev[...] = kvm_ref[:, LANES:]

    c = c_ref[...]
    s1 = s1_ref[...]
    s2 = s2_ref[...]
    kcur = _swa_rope(kv_ref[:, :LANES], c, s1, s2)
    vcur = kv_ref[:, LANES:]
    k2 = jnp.concatenate([kprev[...], kcur], axis=0)
    v2 = jnp.concatenate([vprev[...], vcur], axis=0)

    r = lax.broadcasted_iota(jnp.int32, (W, 2 * W), 0)
    cc = lax.broadcasted_iota(jnp.int32, (W, 2 * W), 1)
    diff = W + r - cc
    mask = (diff >= 0) & (diff < W) & ((m_id > 0) | (cc >= W - N_META))
    low_k = lax.broadcasted_iota(jnp.int32, (2 * W, LANES), 1) < SWA_HD
    low = lax.broadcasted_iota(jnp.int32, (W, LANES), 1) < SWA_HD
    scale = SWA_HD ** -0.5

    def attend(q, kd, vd, h):
        p, den = _sink_softmax(jnp.where(mask, _dot_nt(q.astype(BF16), kd), NEG_INF), sink_ref[h])
        return _dot(p.astype(BF16), vd) / den

    for g in range(SWA_KV_HEADS):
        kd = _dup_head(k2, g, low_k).astype(BF16)
        vd = _dup_head(v2, g, low_k).astype(BF16)
        for p in range(g * SWA_GROUP // 2, (g + 1) * SWA_GROUP // 2):
            slab = _swa_rope(q_ref[:, p * LANES:(p + 1) * LANES], c, s1, s2) * scale
            o_even = attend(jnp.where(low, slab, 0.0), kd, vd, 2 * p)
            o_odd = attend(jnp.where(low, 0.0, slab), kd, vd, 2 * p + 1)
            o_ref[:, p * LANES:(p + 1) * LANES] = jnp.where(low, o_even, o_odd).astype(BF16)

    kprev[...] = kcur
    vprev[...] = vcur
    kp_ref[...] = kcur
    vp_ref[...] = vcur


def _swa_prompt_ret_sample_kernel(sink_ref, q_ref, kv_ref, kvm_ref, c_ref, s1_ref, s2_ref, cm_ref, s1m_ref, s2m_ref,
                                 lg_ref, zs_ref, st_ref, cos_ref, sin_ref,
                                 o_ref, kp_ref, vp_ref, os_ref, sto_ref, kprev, vprev, *, nb, t, every):
    _swa_prompt_kernel(sink_ref, q_ref, kv_ref, kvm_ref, c_ref, s1_ref, s2_ref, cm_ref, s1m_ref, s2m_ref,
                       o_ref, kp_ref, vp_ref, kprev, vprev)

    @pl.when(lax.rem(pl.program_id(1), every) == every - 1)
    def _():
        state_jobs, finish = _ret_sample_jobs(lg_ref, zs_ref, st_ref, cos_ref, sin_ref, os_ref, sto_ref, nb=nb, t=t)
        for job in state_jobs + [finish]:
            job()


def _swa_prompt_ret_sample(sinks, z, zm, bsz, seq, c, s1, s2, lg, sample_row, state, cos, sin, t):
    W = WINDOW
    nblk = seq // W
    every = RET_SAMPLE_EVERY
    ret_steps = bsz * nblk // every
    dbs = state.shape[0]
    nb = dbs // ret_steps
    assert nblk % every == 0 and nb * ret_steps == dbs and (nb * t) % 16 == 0
    rows = nb * t
    qw = SWA_HEADS * SWA_HD
    kv_col = SWA_KV_COL // (2 * LANES)
    tab = pl.BlockSpec((W, LANES), lambda b, m: (m, 0))
    mtab = pl.BlockSpec((W, LANES), lambda b, m: (seq // W, 0))
    stab = pl.BlockSpec((rows, LANES), lambda b, m: ((seq + RET_CHUNK) // rows, 0))
    ret_step = lambda b, m: (b * nblk + m) // every
    state_spec = pl.BlockSpec((nb, RET_HEADS, RET_DK, RET_DV), lambda b, m: (ret_step(b, m), 0, 0, 0))
    return pl.pallas_call(
        functools.partial(_swa_prompt_ret_sample_kernel, nb=nb, t=t, every=every),
        grid=(bsz, nblk),
        in_specs=[
            pl.BlockSpec(memory_space=pltpu.SMEM),
            pl.BlockSpec((W, qw), lambda b, m: (b * nblk + m, SWA_Q_COL // qw)),
            pl.BlockSpec((W, 2 * LANES), lambda b, m: (b * nblk + m, kv_col)),
            pl.BlockSpec((W, 2 * LANES), lambda b, m: (0, kv_col)),
            tab, tab, tab, mtab, mtab, mtab,
            pl.BlockSpec(memory_space=pltpu.SMEM),
            pl.BlockSpec((rows, 4 * RET_W), lambda b, m: (sample_row // rows + ret_step(b, m), 0)),
            state_spec, stab, stab,
        ],
        out_specs=[
            pl.BlockSpec((W, qw), lambda b, m: (b * nblk + m, 0)),
            pl.BlockSpec((None, W, LANES), lambda b, m: (b, 0, 0)),
            pl.BlockSpec((None, W, LANES), lambda b, m: (b, 0, 0)),
            pl.BlockSpec((rows, RET_W), lambda b, m: (ret_step(b, m), 0)),
            state_spec,
        ],
        out_shape=[
            jax.ShapeDtypeStruct((bsz * seq, qw), BF16),
            jax.ShapeDtypeStruct((bsz, W, LANES), F32),
            jax.ShapeDtypeStruct((bsz, W, LANES), F32),
            jax.ShapeDtypeStruct((dbs * t, RET_W), BF16),
            jax.ShapeDtypeStruct(state.shape, F32),
        ],
        scratch_shapes=[pltpu.VMEM((W, LANES), F32), pltpu.VMEM((W, LANES), F32)],
        compiler_params=_cparams(("parallel", "arbitrary")),
        name="swa_prompt_ret_sample",
    )(sinks, z, z, zm, c, s1, s2, c, s1, s2, lg, z, state, cos, sin)


def _swa_sample_kernel(sink_ref, q_ref, kv_ref, kc_ref, vc_ref, c_ref, s1_ref, s2_ref,
                       o_ref, ko_ref, vo_ref, *, nb, t):
    W = WINDOW
    keys = 2 * W
    c = c_ref[...]
    s1 = s1_ref[...]
    s2 = s2_ref[...]
    knew = _swa_rope(kv_ref[:, :LANES], c, s1, s2)
    vnew = kv_ref[:, LANES:]
    pad = jnp.zeros((keys - W - t, LANES), F32)

    rq = SWA_HEADS * t
    tq = lax.broadcasted_iota(jnp.int32, (rq, keys), 0) & (t - 1)
    cc = lax.broadcasted_iota(jnp.int32, (rq, keys), 1)
    diff = W + tq - cc
    mask = (diff >= 0) & (diff < W)
    low = lax.broadcasted_iota(jnp.int32, (nb * t, LANES), 1) < SWA_HD
    low_t = lax.broadcasted_iota(jnp.int32, (t, LANES), 1) < SWA_HD
    sink_col = jnp.concatenate([jnp.full((t, 1), sink_ref[h], F32) for h in range(SWA_HEADS)], axis=0)
    npair = SWA_HEADS // 2

    scale = SWA_HD ** -0.5
    slabs = [_swa_rope(q_ref[:, p * LANES:(p + 1) * LANES], c, s1, s2) * scale for p in range(npair)]
    q_heads = [_to_kv_half(slabs[h // 2], h, low) for h in range(SWA_HEADS)]

    logits, values = [], []
    for db in range(nb):
        rows = slice(db * t, (db + 1) * t)
        kn = knew[rows, :]
        vn = vnew[rows, :]
        ko_ref[db, 0:W - t, :] = kc_ref[db, t:W, :]
        ko_ref[db, W - t:W, :] = kn
        vo_ref[db, 0:W - t, :] = vc_ref[db, t:W, :]
        vo_ref[db, W - t:W, :] = vn
        k2 = jnp.concatenate([kc_ref[db], kn, pad], axis=0).astype(BF16)
        values.append(jnp.concatenate([vc_ref[db], vn, pad], axis=0).astype(BF16))
        q_db = jnp.concatenate([qh[rows, :] for qh in q_heads], axis=0).astype(BF16)
        logits.append(jnp.where(mask, _dot_nt(q_db, k2), NEG_INF))
    p, den = _sink_softmax(jnp.concatenate(logits, axis=0), jnp.concatenate([sink_col] * nb, axis=0))
    p = p.astype(BF16)
    outs = [[] for _ in range(npair)]
    for db in range(nb):
        o = _dot(p[db * rq:(db + 1) * rq, :], values[db]) / den[db * rq:(db + 1) * rq, :]
        for pr in range(npair):
            o_even = o[(2 * pr) * t:(2 * pr + 1) * t, :]
            o_odd = o[(2 * pr + 1) * t:(2 * pr + 2) * t, :]
            outs[pr].append(_from_kv_half(o_even, o_odd, 2 * pr // SWA_GROUP, low_t))
    for pr in range(npair):
        o_ref[:, pr * LANES:(pr + 1) * LANES] = jnp.concatenate(outs[pr], axis=0).astype(BF16)


def _swa_sample(sinks, z, row0, kc, vc, c, s1, s2, tab_row0, t):
    dbs = kc.shape[0]
    n = dbs * t
    rows = SAMPLE_TABLE_ROWS
    nb = rows // t
    W = WINDOW
    qw = SWA_HEADS * SWA_HD
    tab = pl.BlockSpec((rows, LANES), lambda i: (tab_row0 // rows, 0))
    cache = pl.BlockSpec((nb, W, LANES), lambda i: (i, 0, 0))
    return pl.pallas_call(
        functools.partial(_swa_sample_kernel, nb=nb, t=t),
        grid=(dbs // nb,),
        in_specs=[
            pl.BlockSpec(memory_space=pltpu.SMEM),
            pl.BlockSpec((rows, qw), lambda i: (row0 // rows + i, SWA_Q_COL // qw)),
            pl.BlockSpec((rows, 2 * LANES), lambda i: (row0 // rows + i, SWA_KV_COL // (2 * LANES))),
            cache, cache, tab, tab, tab,
        ],
        out_specs=[pl.BlockSpec((rows, qw), lambda i: (i, 0)), cache, cache],
        out_shape=[
            jax.ShapeDtypeStruct((n, qw), BF16),
            jax.ShapeDtypeStruct((dbs, W, LANES), F32),
            jax.ShapeDtypeStruct((dbs, W, LANES), F32),
        ],
        compiler_params=_cparams(("parallel",)),
        name="swa_sample",
    )(sinks, z, z, kc, vc, c, s1, s2)


def _out_proj_ln_kernel(ro_ref, so_ref, x_ref, eg_ref, eb_ref, w_hbm, mg_ref, mb_ref, h_ref, hb_ref,
                        wb_scr, stage, sem):
    chunk = stage.shape[1]
    n_chunks = w_hbm.shape[0] // chunk

    def weight_copy(k):
        return pltpu.make_async_copy(w_hbm.at[pl.ds(k * chunk, chunk), :], stage.at[k % 2], sem.at[k % 2])

    @pl.when(pl.program_id(0) == 0)
    def _():
        weight_copy(0).start()
        for k in range(n_chunks):
            if k + 1 < n_chunks:
                weight_copy(k + 1).start()
            weight_copy(k).wait()
            wb_scr[k * chunk:(k + 1) * chunk, :] = stage[k % 2].astype(BF16)

    sub = h_ref.shape[0] // 2
    for r in range(2):
        rows = slice(r * sub, (r + 1) * sub)
        x_in = _layer_norm(x_ref[rows, :], eg_ref[...], eb_ref[...])
        mixed = _dot(ro_ref[rows, :], wb_scr[:RET_W, :]) + _dot(so_ref[rows, :], wb_scr[RET_W:, :])
        h = _layer_norm(ALPHA * x_in + mixed, mg_ref[...], mb_ref[...])
        h_ref[rows, :] = h
        hb_ref[rows, :] = h.astype(BF16)


def _out_proj_ln(ro, so, x, eg, eb, w, mg, mb, tm):
    n = x.shape[0]
    vec = pl.BlockSpec((1, D_MODEL), lambda i: (0, 0))
    tile = pl.BlockSpec((tm, D_MODEL), lambda i: (i, 0))
    return pl.pallas_call(
        _out_proj_ln_kernel,
        grid=(n // tm,),
        in_specs=[
            pl.BlockSpec((tm, RET_W), lambda i: (i, 0)),
            pl.BlockSpec((tm, RET_W), lambda i: (i, 0)),
            tile, vec, vec,
            pl.BlockSpec(memory_space=pl.ANY),
            vec, vec,
        ],
        out_specs=[tile, tile],
        out_shape=[jax.ShapeDtypeStruct((n, D_MODEL), F32), jax.ShapeDtypeStruct((n, D_MODEL), BF16)],
        scratch_shapes=[pltpu.VMEM((D_MODEL, D_MODEL), BF16),
                        pltpu.VMEM((2, WEIGHT_STAGE_ROWS, D_MODEL), F32),
                        pltpu.SemaphoreType.DMA((2,))],
        compiler_params=_cparams(("arbitrary",)),
        name="out_proj_ln",
    )(ro, so, x, eg, eb, w, mg, mb)


def _ffn_ln_kernel(hb_ref, h_hbm, wg_ref, wu_ref, wd_ref, g_ref, b_ref, y_ref, *rest):
    *w_out_refs, h_res, sem = rest
    i = pl.program_id(0)
    j = pl.program_id(1)
    tm = h_res.shape[0]
    residual_copy = pltpu.make_async_copy(h_hbm.at[pl.ds(i * tm, tm), :], h_res, sem)

    @pl.when(j == 0)
    def _():
        residual_copy.start()
        y_ref[...] = jnp.zeros_like(y_ref)

    wg, wu, wd = (w[...].astype(BF16) for w in (wg_ref, wu_ref, wd_ref))
    for out_ref, w in zip(w_out_refs, (wg, wu, wd)):
        out_ref[...] = w
    hb = hb_ref[...]
    act = _silu(_dot(hb, wg)) * _dot(hb, wu)
    y_ref[...] += _dot(act.astype(BF16), wd)

    @pl.when(j == pl.num_programs(1) - 1)
    def _():
        residual_copy.wait()
        y_ref[...] = _layer_norm(ALPHA * h_res[...] + y_ref[...], g_ref[...], b_ref[...])


def _ffn_ln(h, hb, wg, wu, wd, g, b, tm, th):
    n = h.shape[0]
    emit_weights = wg.dtype == F32
    assert not emit_weights or n == tm
    vec = pl.BlockSpec((1, D_MODEL), lambda i, j: (0, 0))
    w_specs = [
        pl.BlockSpec((D_MODEL, th), lambda i, j: (0, j)),
        pl.BlockSpec((D_MODEL, th), lambda i, j: (0, j)),
        pl.BlockSpec((th, D_MODEL), lambda i, j: (j, 0)),
    ]
    out_specs = [pl.BlockSpec((tm, D_MODEL), lambda i, j: (i, 0))]
    out_shape = [jax.ShapeDtypeStruct((n, D_MODEL), F32)]
    if emit_weights:
        out_specs += w_specs
        out_shape += [jax.ShapeDtypeStruct(w.shape, BF16) for w in (wg, wu, wd)]
    outs = pl.pallas_call(
        _ffn_ln_kernel,
        grid=(n // tm, FFN_HIDDEN // th),
        in_specs=[pl.BlockSpec((tm, D_MODEL), lambda i, j: (i, 0)), pl.BlockSpec(memory_space=pl.ANY)] + w_specs
        + [vec, vec],
        out_specs=out_specs,
        out_shape=out_shape,
        scratch_shapes=[pltpu.VMEM((tm, D_MODEL), F32), pltpu.SemaphoreType.DMA(())],
        compiler_params=pltpu.CompilerParams(dimension_semantics=("arbitrary", "arbitrary"),
                                             vmem_limit_bytes=VMEM_LIMIT_LARGE),
        name="ffn_ln",
    )(hb, h, wg, wu, wd, g, b)
    return tuple(outs) if emit_weights else outs[0]


def _position_tables(seq, t, past_len):
    r = jnp.arange(seq + RET_CHUNK + SAMPLE_TABLE_ROWS)
    meta_r = r - seq
    pos = jnp.where(r < seq, N_META + r,
                    jnp.where(meta_r < RET_CHUNK, jnp.maximum(meta_r - (RET_CHUNK - N_META), 0),
                              past_len + (meta_r - RET_CHUNK) % t)).astype(F32)[:, None]
    ret_freq = jnp.power(RET_THETA, -jnp.linspace(0.0, 1.0, RET_DK // 2, dtype=F32))
    ret_ang = pos * ret_freq[None, :]
    half = ROT_DIM // 2
    swa_freq = jnp.power(ROPE_THETA, -jnp.arange(0, ROT_DIM, 2, dtype=F32) / ROT_DIM)
    swa_ang = pos * swa_freq[None, :]
    cos = jnp.tile(jnp.cos(swa_ang), (1, LANES // half))
    sin = jnp.tile(jnp.sin(swa_ang), (1, LANES // half))
    d = jnp.arange(LANES) % SWA_HD
    c = jnp.where(d < ROT_DIM, cos, 1.0)
    s1 = jnp.where((d >= half) & (d < ROT_DIM), sin, 0.0)
    s2 = jnp.where(d < half, -sin, 0.0)
    return (jnp.cos(ret_ang), jnp.sin(ret_ang)), (c, s1, s2)


def kernel(x_prompt, x_sample, state_ret, cache_swa_k, cache_swa_v, meta_tokens, ln_emb_g, ln_emb_b,
           w_in, w_out, swa_sinks, ln_mix_g, ln_mix_b, w_ffn_gate, w_ffn_up, w_ffn_down, ln_ffn_g, ln_ffn_b):
    bsz, seq, d = x_prompt.shape
    dbs, t, _ = x_sample.shape
    assert w_in.shape[0] == DEPTH and d == D_MODEL and seq % RET_CHUNK == 0 and t & (t - 1) == 0
    past_len = 16384
    row = lambda a: a.reshape(1, -1)

    eg, eb = row(ln_emb_g), row(ln_emb_b)
    sinks = swa_sinks[0]
    lg = jnp.log(1.0 - jnp.power(2.0, -5.0 - jnp.arange(RET_HEADS, dtype=F32)))

    xp = x_prompt.reshape(bsz * seq, d)
    xs = x_sample.reshape(dbs * t, d)

    z, zm = _ln_proj(xp, xs, meta_tokens, eg, eb, w_in[0], 512, PROJ_W // 3)
    sample_row = bsz * seq

    (cos, sin), swa_tabs = _position_tables(seq, t, past_len)
    ret_o_p, ret_state_p = _ret_prompt(lg, z, zm, bsz, seq, cos, sin)
    swa_o_p, k_p, v_p, ret_o_s, ret_state_s = _swa_prompt_ret_sample(
        sinks, z, zm, bsz, seq, *swa_tabs, lg, sample_row, state_ret[0], cos, sin, t)
    kc = cache_swa_k[0].reshape(dbs, WINDOW, LANES)
    vc = cache_swa_v[0].reshape(dbs, WINDOW, LANES)
    swa_o_s, k_s, v_s = _swa_sample(sinks, z, sample_row, kc, vc, *swa_tabs, seq + RET_CHUNK, t)

    mg, mb = row(ln_mix_g[0]), row(ln_mix_b[0])
    fg, fb = row(ln_ffn_g[0]), row(ln_ffn_b[0])
    ffn_w = (w_ffn_gate[0], w_ffn_up[0], w_ffn_down[0])
    h_p, hb_p = _out_proj_ln(ret_o_p, swa_o_p, xp, eg, eb, w_out[0], mg, mb, 512)
    h_s, hb_s = _out_proj_ln(ret_o_s, swa_o_s, xs, eg, eb, w_out[0], mg, mb, 512)
    y_s, *ffn_wb = _ffn_ln(h_s, hb_s, *ffn_w, fg, fb, TOK_TILE, 256)
    y_p = _ffn_ln(h_p, hb_p, *ffn_wb, fg, fb, TOK_TILE, 512)

    kv_shape = (DEPTH, -1, WINDOW, SWA_KV_HEADS, SWA_HD)
    return (y_p.reshape(bsz, seq, d), y_s.reshape(dbs, t, d),
            ret_state_p[None], k_p.reshape(kv_shape), v_p.reshape(kv_shape),
            ret_state_s[None], k_s.reshape(kv_shape), v_s.reshape(kv_shape))
```

```python
import functools

import jax
import jax.numpy as jnp
from jax import lax
from jax.experimental import pallas as pl
from jax.experimental.pallas import tpu as pltpu

F32 = jnp.float32
BF16 = jnp.bfloat16

D_MODEL = 2048
N_META = 16
RET_HEADS = 4
RET_DK = 256
RET_DV = 256
RET_CHUNK = 128
RET_THETA = 10000.0
SWA_HD = 64
SWA_HEADS = 16
SWA_KV_HEADS = 2
SWA_GROUP = SWA_HEADS // SWA_KV_HEADS
WINDOW = 128
ROPE_THETA = 500000.0
ROT_DIM = SWA_HD // 4
FFN_HIDDEN = 5632
PROJ_W = 5376
DEPTH = 1
ALPHA = (2.0 * DEPTH) ** 0.25
LN_EPS = 1e-5
NEG_INF = -1e30

LANES = 128
RET_W = RET_HEADS * RET_DK
SWA_Q_COL = 4 * RET_W
SWA_KV_COL = SWA_Q_COL + SWA_HEADS * SWA_HD
VMEM_LIMIT = 56 * 1024 * 1024
VMEM_LIMIT_LARGE = 60 * 1024 * 1024
TOK_TILE = 1024
RET_CHUNK_UNROLL = 16
RET_SAMPLE_EVERY = 2
SAMPLE_TABLE_ROWS = 64
WEIGHT_STAGE_ROWS = 256


def _cparams(sem):
    return pltpu.CompilerParams(dimension_semantics=sem, vmem_limit_bytes=VMEM_LIMIT)


def _layer_norm(x, g, b):
    mu = jnp.mean(x, axis=-1, keepdims=True)
    xc = x - mu
    var = jnp.mean(xc * xc, axis=-1, keepdims=True)
    return xc * lax.rsqrt(var + LN_EPS) * g + b


def _silu(x):
    return x / (1.0 + jnp.exp(-x))


def _dot(a, b):
    return jnp.dot(a, b, preferred_element_type=F32)


def _dot_nt(a, b):
    return lax.dot_general(a, b, (((1,), (1,)), ((), ())), preferred_element_type=F32)


def _dot_tn(a, b):
    return lax.dot_general(a, b, (((0,), (0,)), ((), ())), preferred_element_type=F32)


def _ln_proj_kernel(xp_ref, xs_ref, xm_ref, g_ref, b_ref, w_hbm, z_ref, zm_ref,
                    h_scr, hm_scr, wb_scr, stage, sem, *, np_tiles, ns_tiles):
    i = pl.program_id(0)
    j = pl.program_id(1)
    first = j == 0
    is_sample = i >= np_tiles
    is_last = i == np_tiles + ns_tiles - 1
    n_col, _, tn = wb_scr.shape
    chunk = stage.shape[1]
    n_chunks = w_hbm.shape[0] // chunk

    def weight_copy(k):
        return pltpu.make_async_copy(w_hbm.at[pl.ds(k * chunk, chunk), :], stage.at[k % 2], sem.at[k % 2])

    @pl.when(first & (i == 0))
    def _():
        weight_copy(0).start()
        for k in range(n_chunks):
            if k + 1 < n_chunks:
                weight_copy(k + 1).start()
            weight_copy(k).wait()
            for jt in range(n_col):
                wb_scr[jt, k * chunk:(k + 1) * chunk, :] = stage[k % 2, :, jt * tn:(jt + 1) * tn].astype(BF16)

    def norm(x_ref):
        return _layer_norm(x_ref[...], g_ref[...], b_ref[...]).astype(BF16)

    @pl.when(first & jnp.logical_not(is_sample))
    def _():
        h_scr[...] = norm(xp_ref)

    @pl.when(first & is_sample)
    def _():
        h_scr[...] = norm(xs_ref)

    @pl.when(first & is_last)
    def _():
        lead = hm_scr.shape[0] - xm_ref.shape[0]
        hm_scr[:lead, :] = jnp.zeros((lead, D_MODEL), BF16)
        hm_scr[lead:, :] = norm(xm_ref)

    z_ref[...] = _dot(h_scr[...], wb_scr[j])

    @pl.when(is_last)
    def _():
        zm_ref[...] = _dot(hm_scr[...], wb_scr[j])


def _ln_proj(xp, xs, xm, g, b, w, tm, tn):
    np_tiles = xp.shape[0] // tm
    ns_tiles = xs.shape[0] // tm
    last = np_tiles + ns_tiles - 1
    n_col = PROJ_W // tn
    return pl.pallas_call(
        functools.partial(_ln_proj_kernel, np_tiles=np_tiles, ns_tiles=ns_tiles),
        grid=(np_tiles + ns_tiles, n_col),
        in_specs=[
            pl.BlockSpec((tm, D_MODEL), lambda i, j: (jnp.minimum(i, np_tiles - 1), 0)),
            pl.BlockSpec((tm, D_MODEL), lambda i, j: (jnp.clip(i - np_tiles, 0, ns_tiles - 1), 0)),
            pl.BlockSpec(xm.shape, lambda i, j: (0, 0), pipeline_mode=pl.Buffered(1)),
            pl.BlockSpec((1, D_MODEL), lambda i, j: (0, 0)),
            pl.BlockSpec((1, D_MODEL), lambda i, j: (0, 0)),
            pl.BlockSpec(memory_space=pl.ANY),
        ],
        out_specs=[
            pl.BlockSpec((tm, tn), lambda i, j: (i, j)),
            pl.BlockSpec((RET_CHUNK, tn), lambda i, j: (0, jnp.where(i == last, j, 0))),
        ],
        out_shape=[
            jax.ShapeDtypeStruct((xp.shape[0] + xs.shape[0], PROJ_W), F32),
            jax.ShapeDtypeStruct((RET_CHUNK, PROJ_W), F32),
        ],
        scratch_shapes=[
            pltpu.VMEM((tm, D_MODEL), BF16),
            pltpu.VMEM((RET_CHUNK, D_MODEL), BF16),
            pltpu.VMEM((n_col, D_MODEL, tn), BF16),
            pltpu.VMEM((2, WEIGHT_STAGE_ROWS // 4, PROJ_W), F32),
            pltpu.SemaphoreType.DMA((2,)),
        ],
        compiler_params=pltpu.CompilerParams(dimension_semantics=("arbitrary", "arbitrary"),
                                             vmem_limit_bytes=VMEM_LIMIT_LARGE),
        name="ln_proj",
    )(xp, xs, xm, g, b, w)


def _ret_rope(x, cos, sin):
    x1 = x[:, :LANES]
    x2 = x[:, LANES:]
    return jnp.concatenate([x1 * cos - x2 * sin, x2 * cos + x1 * sin], axis=1)


def _group_norm_gate(o, gate):
    mu = jnp.mean(o, axis=-1, keepdims=True)
    oc = o - mu
    var = jnp.mean(oc * oc, axis=-1, keepdims=True)
    return oc * lax.rsqrt(var + LN_EPS) * _silu(gate)


def _ret_prompt_kernel(lg_ref, q_ref, k_ref, v_ref, g_ref, km_ref, vm_ref,
                       cos_ref, sin_ref, cosm_ref, sinm_ref, o_ref, s_ref, s_scr):
    C = RET_CHUNK
    lg = lg_ref[pl.program_id(1)]
    ri = lax.broadcasted_iota(jnp.int32, (C, C), 0)
    ci = lax.broadcasted_iota(jnp.int32, (C, C), 1)
    rel = (ri - ci).astype(F32)
    decay = jnp.where(rel >= 0.0, jnp.exp(jnp.maximum(rel, 0.0) * lg), 0.0)
    row = lax.broadcasted_iota(jnp.int32, (C, 1), 0).astype(F32)
    q_decay = jnp.exp((row + 1.0) * lg)
    k_decay = jnp.exp((C - 1.0 - row) * lg)
    chunk_decay = jnp.exp(jnp.full((1, RET_DV), C * lg, F32))
    scale = RET_DK ** -0.5

    meta_decay = jnp.where(row >= C - N_META, k_decay, 0.0)
    km = _ret_rope(km_ref[...], cosm_ref[...], sinm_ref[...]) * scale
    s_scr[...] = _dot_tn((km * meta_decay).astype(BF16), vm_ref[...].astype(BF16))

    def chunk(c, carry):
        rows = pl.ds(pl.multiple_of(c * C, C), C)
        cos = cos_ref[rows, :]
        sin = sin_ref[rows, :]
        q = _ret_rope(q_ref[rows, :], cos, sin)
        k = _ret_rope(k_ref[rows, :], cos, sin) * scale
        qb = q.astype(BF16)
        vb = v_ref[rows, :].astype(BF16)
        s_prev = s_scr[...]
        scores = _dot_nt(qb, k.astype(BF16)) * decay
        inner = _dot(scores.astype(BF16), vb)
        cross = _dot(qb, s_prev.astype(BF16)) * q_decay
        s_scr[...] = chunk_decay * s_prev + _dot_tn((k * k_decay).astype(BF16), vb)
        o_ref[rows, :] = _group_norm_gate(inner + cross, g_ref[rows, :]).astype(BF16)
        return carry

    lax.fori_loop(0, q_ref.shape[0] // C, chunk, 0, unroll=RET_CHUNK_UNROLL)
    s_ref[...] = s_scr[...]


def _ret_prompt(lg, z, zm, bsz, seq, cos, sin):
    col = lambda base: (lambda b, h: (b, base + h))
    mcol = lambda base: (lambda b, h: (0, base + h))
    full = lambda b, h: (0, 0)
    meta_tab = lambda b, h: (seq // RET_CHUNK, 0)
    return pl.pallas_call(
        _ret_prompt_kernel,
        grid=(bsz, RET_HEADS),
        in_specs=[
            pl.BlockSpec(memory_space=pltpu.SMEM),
            pl.BlockSpec((seq, RET_DK), col(0)),
            pl.BlockSpec((seq, RET_DK), col(RET_HEADS)),
            pl.BlockSpec((seq, RET_DV), col(2 * RET_HEADS)),
            pl.BlockSpec((seq, RET_DV), col(3 * RET_HEADS)),
            pl.BlockSpec((RET_CHUNK, RET_DK), mcol(RET_HEADS)),
            pl.BlockSpec((RET_CHUNK, RET_DV), mcol(2 * RET_HEADS)),
            pl.BlockSpec((seq, LANES), full),
            pl.BlockSpec((seq, LANES), full),
            pl.BlockSpec((RET_CHUNK, LANES), meta_tab),
            pl.BlockSpec((RET_CHUNK, LANES), meta_tab),
        ],
        out_specs=[
            pl.BlockSpec((seq, RET_DV), lambda b, h: (b, h)),
            pl.BlockSpec((None, None, RET_DK, RET_DV), lambda b, h: (b, h, 0, 0)),
        ],
        out_shape=[
            jax.ShapeDtypeStruct((bsz * seq, RET_W), BF16),
            jax.ShapeDtypeStruct((bsz, RET_HEADS, RET_DK, RET_DV), F32),
        ],
        scratch_shapes=[pltpu.VMEM((RET_DK, RET_DV), F32)],
        compiler_params=_cparams(("parallel", "parallel")),
        name="ret_prompt",
    )(lg, z, z, z, z, zm, zm, cos, sin, cos, sin)


def _ret_sample_jobs(lg_ref, z_ref, s_ref, cos_ref, sin_ref, o_ref, so_ref, *, nb, t):
    rows = nb * t
    R = RET_HEADS * rows
    scale = RET_DK ** -0.5
    cos = cos_ref[...]
    sin = sin_ref[...]

    def stack(base, rope, mul=1.0):
        parts = []
        for h in range(RET_HEADS):
            x = z_ref[:, base + h * RET_DK: base + (h + 1) * RET_DK]
            parts.append(_ret_rope(x, cos, sin) * mul if rope else x)
        return jnp.concatenate(parts, axis=0)

    q = stack(0, True)
    k = stack(RET_W, True, scale)
    v = stack(2 * RET_W, False)
    gate = stack(3 * RET_W, False)

    ri = lax.broadcasted_iota(jnp.int32, (R, R), 0)
    ci = lax.broadcasted_iota(jnp.int32, (R, R), 1)
    rcol = lax.broadcasted_iota(jnp.int32, (R, 1), 0)
    lg_col = jnp.zeros((R, 1), F32)
    for h in range(RET_HEADS):
        lg_col = jnp.where((rcol >= h * rows) & (rcol < (h + 1) * rows), lg_ref[h], lg_col)
    tcol = (rcol & (t - 1)).astype(F32)
    rel = (ri - ci).astype(F32)
    same = ((ri & -t) == (ci & -t)) & (ri >= ci)
    decay = jnp.where(same, jnp.exp(jnp.maximum(rel, 0.0) * lg_col), 0.0)
    q_decay = jnp.exp((tcol + 1.0) * lg_col)
    k_decay = jnp.exp((t - 1.0 - tcol) * lg_col)

    qb = q.astype(BF16)
    vb = v.astype(BF16)
    scores = _dot_nt(qb, k.astype(BF16)) * decay
    inner = _dot(scores.astype(BF16), vb)
    kw = k * k_decay

    cross_parts = [None] * (RET_HEADS * nb)

    def state_job(h, db):
        def run():
            r0 = h * rows + db * t
            s_prev = s_ref[db, h]
            cross_parts[h * nb + db] = _dot(q[r0:r0 + t, :].astype(BF16), s_prev.astype(BF16))
            mine = (rcol >= r0) & (rcol < r0 + t)
            upd = _dot_tn(jnp.where(mine, kw, 0.0).astype(BF16), vb)
            step_decay = jnp.exp(jnp.full((1, RET_DV), t * lg_ref[h], F32))
            so_ref[db, h] = step_decay * s_prev + upd
        return run

    def finish():
        cross = jnp.concatenate(cross_parts, axis=0) * q_decay
        out = _group_norm_gate(inner + cross, gate).astype(BF16)
        for h in range(RET_HEADS):
            o_ref[:, h * RET_DV:(h + 1) * RET_DV] = out[h * rows:(h + 1) * rows, :]

    return [state_job(h, db) for h in range(RET_HEADS) for db in range(nb)], finish


def _swa_rope(x, c, s1, s2):
    return x * c + pltpu.roll(x, 8, 1) * s1 + pltpu.roll(x, LANES - 8, 1) * s2


def _dup_head(x, g, low):
    swapped = pltpu.roll(x, SWA_HD, 1)
    return jnp.where(low, x, swapped) if g == 0 else jnp.where(low, swapped, x)


def _to_kv_half(slab, head, low):
    g = head // SWA_GROUP
    src = slab if head % 2 == g else pltpu.roll(slab, SWA_HD, 1)
    return jnp.where(low, src, 0.0) if g == 0 else jnp.where(low, 0.0, src)


def _from_kv_half(o_even, o_odd, g, low):
    if g == 0:
        return jnp.where(low, o_even, pltpu.roll(o_odd, SWA_HD, 1))
    return jnp.where(low, pltpu.roll(o_even, SWA_HD, 1), o_odd)


def _sink_softmax(logits, sink):
    m = jnp.maximum(jnp.max(logits, axis=-1, keepdims=True), sink)
    p = jnp.exp(logits - m)
    return p, jnp.sum(p, axis=-1, keepdims=True) + jnp.exp(sink - m)


def _swa_prompt_kernel(sink_ref, q_ref, kv_ref, kvm_ref, c_ref, s1_ref, s2_ref,
                       cm_ref, s1m_ref, s2m_ref, o_ref, kp_ref, vp_ref, kprev, vprev):
    W = WINDOW
    m_id = pl.program_id(1)

    @pl.when(m_id == 0)
    def _():
        kprev[...] = _swa_rope(kvm_ref[:, :LANES], cm_ref[...], s1m_ref[...], s2m_ref[...])
        vprev[...] = kvm_ref[:, LANES:]

    c = c_ref[...]
    s1 = s1_ref[...]
    s2 = s2_ref[...]
    kcur = _swa_rope(kv_ref[:, :LANES], c, s1, s2)
    vcur = kv_ref[:, LANES:]
    k2 = jnp.concatenate([kprev[...], kcur], axis=0)
    v2 = jnp.concatenate([vprev[...], vcur], axis=0)

    r = lax.broadcasted_iota(jnp.int32, (W, 2 * W), 0)
    cc = lax.broadcasted_iota(jnp.int32, (W, 2 * W), 1)
    diff = W + r - cc
    mask = (diff >= 0) & (diff < W) & ((m_id > 0) | (cc >= W - N_META))
    low_k = lax.broadcasted_iota(jnp.int32, (2 * W, LANES), 1) < SWA_HD
    low = lax.broadcasted_iota(jnp.int32, (W, LANES), 1) < SWA_HD
    scale = SWA_HD ** -0.5

    def attend(q, kd, vd, h):
        p, den = _sink_softmax(jnp.where(mask, _dot_nt(q.astype(BF16), kd), NEG_INF), sink_ref[h])
        return _dot(p.astype(BF16), vd) / den

    for g in range(SWA_KV_HEADS):
        kd = _dup_head(k2, g, low_k).astype(BF16)
        vd = _dup_head(v2, g, low_k).astype(BF16)
        for p in range(g * SWA_GROUP // 2, (g + 1) * SWA_GROUP // 2):
            slab = _swa_rope(q_ref[:, p * LANES:(p + 1) * LANES], c, s1, s2) * scale
            o_even = attend(jnp.where(low, slab, 0.0), kd, vd, 2 * p)
            o_odd = attend(jnp.where(low, 0.0, slab), kd, vd, 2 * p + 1)
            o_ref[:, p * LANES:(p + 1) * LANES] = jnp.where(low, o_even, o_odd).astype(BF16)

    kprev[...] = kcur
    vprev[...] = vcur
    kp_ref[...] = kcur
    vp_ref[...] = vcur


def _swa_prompt_ret_sample_kernel(sink_ref, q_ref, kv_ref, kvm_ref, c_ref, s1_ref, s2_ref, cm_ref, s1m_ref, s2m_ref,
                                 lg_ref, zs_ref, st_ref, cos_ref, sin_ref,
                                 o_ref, kp_ref, vp_ref, os_ref, sto_ref, kprev, vprev, *, nb, t, every):
    _swa_prompt_kernel(sink_ref, q_ref, kv_ref, kvm_ref, c_ref, s1_ref, s2_ref, cm_ref, s1m_ref, s2m_ref,
                       o_ref, kp_ref, vp_ref, kprev, vprev)

    @pl.when(lax.rem(pl.program_id(1), every) == every - 1)
    def _():
        state_jobs, finish = _ret_sample_jobs(lg_ref, zs_ref, st_ref, cos_ref, sin_ref, os_ref, sto_ref, nb=nb, t=t)
        for job in state_jobs + [finish]:
            job()


def _swa_prompt_ret_sample(sinks, z, zm, bsz, seq, c, s1, s2, lg, sample_row, state, cos, sin, t):
    W = WINDOW
    nblk = seq // W
    every = RET_SAMPLE_EVERY
    ret_steps = bsz * nblk // every
    dbs = state.shape[0]
    nb = dbs // ret_steps
    assert nblk % every == 0 and nb * ret_steps == dbs and (nb * t) % 16 == 0
    rows = nb * t
    qw = SWA_HEADS * SWA_HD
    kv_col = SWA_KV_COL // (2 * LANES)
    tab = pl.BlockSpec((W, LANES), lambda b, m: (m, 0))
    mtab = pl.BlockSpec((W, LANES), lambda b, m: (seq // W, 0))
    stab = pl.BlockSpec((rows, LANES), lambda b, m: ((seq + RET_CHUNK) // rows, 0))
    ret_step = lambda b, m: (b * nblk + m) // every
    state_spec = pl.BlockSpec((nb, RET_HEADS, RET_DK, RET_DV), lambda b, m: (ret_step(b, m), 0, 0, 0))
    return pl.pallas_call(
        functools.partial(_swa_prompt_ret_sample_kernel, nb=nb, t=t, every=every),
        grid=(bsz, nblk),
        in_specs=[
            pl.BlockSpec(memory_space=pltpu.SMEM),
            pl.BlockSpec((W, qw), lambda b, m: (b * nblk + m, SWA_Q_COL // qw)),
            pl.BlockSpec((W, 2 * LANES), lambda b, m: (b * nblk + m, kv_col)),
            pl.BlockSpec((W, 2 * LANES), lambda b, m: (0, kv_col)),
            tab, tab, tab, mtab, mtab, mtab,
            pl.BlockSpec(memory_space=pltpu.SMEM),
            pl.BlockSpec((rows, 4 * RET_W), lambda b, m: (sample_row // rows + ret_step(b, m), 0)),
            state_spec, stab, stab,
        ],
        out_specs=[
            pl.BlockSpec((W, qw), lambda b, m: (b * nblk + m, 0)),
            pl.BlockSpec((None, W, LANES), lambda b, m: (b, 0, 0)),
            pl.BlockSpec((None, W, LANES), lambda b, m: (b, 0, 0)),
            pl.BlockSpec((rows, RET_W), lambda b, m: (ret_step(b, m), 0)),
            state_spec,
        ],
        out_shape=[
            jax.ShapeDtypeStruct((bsz * seq, qw), BF16),
            jax.ShapeDtypeStruct((bsz, W, LANES), F32),
            jax.ShapeDtypeStruct((bsz, W, LANES), F32),
            jax.ShapeDtypeStruct((dbs * t, RET_W), BF16),
            jax.ShapeDtypeStruct(state.shape, F32),
        ],
        scratch_shapes=[pltpu.VMEM((W, LANES), F32), pltpu.VMEM((W, LANES), F32)],
        compiler_params=_cparams(("parallel", "arbitrary")),
        name="swa_prompt_ret_sample",
    )(sinks, z, z, zm, c, s1, s2, c, s1, s2, lg, z, state, cos, sin)


def _swa_sample_kernel(sink_ref, q_ref, kv_ref, kc_ref, vc_ref, c_ref, s1_ref, s2_ref,
                       o_ref, ko_ref, vo_ref, *, nb, t):
    W = WINDOW
    keys = 2 * W
    c = c_ref[...]
    s1 = s1_ref[...]
    s2 = s2_ref[...]
    knew = _swa_rope(kv_ref[:, :LANES], c, s1, s2)
    vnew = kv_ref[:, LANES:]
    pad = jnp.zeros((keys - W - t, LANES), F32)

    rq = SWA_HEADS * t
    tq = lax.broadcasted_iota(jnp.int32, (rq, keys), 0) & (t - 1)
    cc = lax.broadcasted_iota(jnp.int32, (rq, keys), 1)
    diff = W + tq - cc
    mask = (diff >= 0) & (diff < W)
    low = lax.broadcasted_iota(jnp.int32, (nb * t, LANES), 1) < SWA_HD
    low_t = lax.broadcasted_iota(jnp.int32, (t, LANES), 1) < SWA_HD
    sink_col = jnp.concatenate([jnp.full((t, 1), sink_ref[h], F32) for h in range(SWA_HEADS)], axis=0)
    npair = SWA_HEADS // 2

    scale = SWA_HD ** -0.5
    slabs = [_swa_rope(q_ref[:, p * LANES:(p + 1) * LANES], c, s1, s2) * scale for p in range(npair)]
    q_heads = [_to_kv_half(slabs[h // 2], h, low) for h in range(SWA_HEADS)]

    logits, values = [], []
    for db in range(nb):
        rows = slice(db * t, (db + 1) * t)
        kn = knew[rows, :]
        vn = vnew[rows, :]
        ko_ref[db, 0:W - t, :] = kc_ref[db, t:W, :]
        ko_ref[db, W - t:W, :] = kn
        vo_ref[db, 0:W - t, :] = vc_ref[db, t:W, :]
        vo_ref[db, W - t:W, :] = vn
        k2 = jnp.concatenate([kc_ref[db], kn, pad], axis=0).astype(BF16)
        values.append(jnp.concatenate([vc_ref[db], vn, pad], axis=0).astype(BF16))
        q_db = jnp.concatenate([qh[rows, :] for qh in q_heads], axis=0).astype(BF16)
        logits.append(jnp.where(mask, _dot_nt(q_db, k2), NEG_INF))
    p, den = _sink_softmax(jnp.concatenate(logits, axis=0), jnp.concatenate([sink_col] * nb, axis=0))
    p = p.astype(BF16)
    outs = [[] for _ in range(npair)]
    for db in range(nb):
        o = _dot(p[db * rq:(db + 1) * rq, :], values[db]) / den[db * rq:(db + 1) * rq, :]
        for pr in range(npair):
            o_even = o[(2 * pr) * t:(2 * pr + 1) * t, :]
            o_odd = o[(2 * pr + 1) * t:(2 * pr + 2) * t, :]
            outs[pr].append(_from_kv_half(o_even, o_odd, 2 * pr // SWA_GROUP, low_t))
    for pr in range(npair):
        o_ref[:, pr * LANES:(pr + 1) * LANES] = jnp.concatenate(outs[pr], axis=0).astype(BF16)


def _swa_sample(sinks, z, row0, kc, vc, c, s1, s2, tab_row0, t):
    dbs = kc.shape[0]
    n = dbs * t
    rows = SAMPLE_TABLE_ROWS
    nb = rows // t
    W = WINDOW
    qw = SWA_HEADS * SWA_HD
    tab = pl.BlockSpec((rows, LANES), lambda i: (tab_row0 // rows, 0))
    cache = pl.BlockSpec((nb, W, LANES), lambda i: (i, 0, 0))
    return pl.pallas_call(
        functools.partial(_swa_sample_kernel, nb=nb, t=t),
        grid=(dbs // nb,),
        in_specs=[
            pl.BlockSpec(memory_space=pltpu.SMEM),
            pl.BlockSpec((rows, qw), lambda i: (row0 // rows + i, SWA_Q_COL // qw)),
            pl.BlockSpec((rows, 2 * LANES), lambda i: (row0 // rows + i, SWA_KV_COL // (2 * LANES))),
            cache, cache, tab, tab, tab,
        ],
        out_specs=[pl.BlockSpec((rows, qw), lambda i: (i, 0)), cache, cache],
        out_shape=[
            jax.ShapeDtypeStruct((n, qw), BF16),
            jax.ShapeDtypeStruct((dbs, W, LANES), F32),
            jax.ShapeDtypeStruct((dbs, W, LANES), F32),
        ],
        compiler_params=_cparams(("parallel",)),
        name="swa_sample",
    )(sinks, z, z, kc, vc, c, s1, s2)


def _out_proj_ln_kernel(ro_ref, so_ref, x_ref, eg_ref, eb_ref, w_hbm, mg_ref, mb_ref, h_ref, hb_ref,
                        wb_scr, stage, sem):
    chunk = stage.shape[1]
    n_chunks = w_hbm.shape[0] // chunk

    def weight_copy(k):
        return pltpu.make_async_copy(w_hbm.at[pl.ds(k * chunk, chunk), :], stage.at[k % 2], sem.at[k % 2])

    @pl.when(pl.program_id(0) == 0)
    def _():
        weight_copy(0).start()
        for k in range(n_chunks):
            if k + 1 < n_chunks:
                weight_copy(k + 1).start()
            weight_copy(k).wait()
            wb_scr[k * chunk:(k + 1) * chunk, :] = stage[k % 2].astype(BF16)

    sub = h_ref.shape[0] // 2
    for r in range(2):
        rows = slice(r * sub, (r + 1) * sub)
        x_in = _layer_norm(x_ref[rows, :], eg_ref[...], eb_ref[...])
        mixed = _dot(ro_ref[rows, :], wb_scr[:RET_W, :]) + _dot(so_ref[rows, :], wb_scr[RET_W:, :])
        h = _layer_norm(ALPHA * x_in + mixed, mg_ref[...], mb_ref[...])
        h_ref[rows, :] = h
        hb_ref[rows, :] = h.astype(BF16)


def _out_proj_ln(ro, so, x, eg, eb, w, mg, mb, tm):
    n = x.shape[0]
    vec = pl.BlockSpec((1, D_MODEL), lambda i: (0, 0))
    tile = pl.BlockSpec((tm, D_MODEL), lambda i: (i, 0))
    return pl.pallas_call(
        _out_proj_ln_kernel,
        grid=(n // tm,),
        in_specs=[
            pl.BlockSpec((tm, RET_W), lambda i: (i, 0)),
            pl.BlockSpec((tm, RET_W), lambda i: (i, 0)),
            tile, vec, vec,
            pl.BlockSpec(memory_space=pl.ANY),
            vec, vec,
        ],
        out_specs=[tile, tile],
        out_shape=[jax.ShapeDtypeStruct((n, D_MODEL), F32), jax.ShapeDtypeStruct((n, D_MODEL), BF16)],
        scratch_shapes=[pltpu.VMEM((D_MODEL, D_MODEL), BF16),
                        pltpu.VMEM((2, WEIGHT_STAGE_ROWS, D_MODEL), F32),
                        pltpu.SemaphoreType.DMA((2,))],
        compiler_params=_cparams(("arbitrary",)),
        name="out_proj_ln",
    )(ro, so, x, eg, eb, w, mg, mb)


def _ffn_ln_kernel(hb_ref, h_hbm, wg_ref, wu_ref, wd_ref, g_ref, b_ref, y_ref, *rest):
    *w_out_refs, h_res, sem = rest
    i = pl.program_id(0)
    j = pl.program_id(1)
    tm = h_res.shape[0]
    residual_copy = pltpu.make_async_copy(h_hbm.at[pl.ds(i * tm, tm), :], h_res, sem)

    @pl.when(j == 0)
    def _():
        residual_copy.start()
        y_ref[...] = jnp.zeros_like(y_ref)

    wg, wu, wd = (w[...].astype(BF16) for w in (wg_ref, wu_ref, wd_ref))
    for out_ref, w in zip(w_out_refs, (wg, wu, wd)):
        out_ref[...] = w
    hb = hb_ref[...]
    act = _silu(_dot(hb, wg)) * _dot(hb, wu)
    y_ref[...] += _dot(act.astype(BF16), wd)

    @pl.when(j == pl.num_programs(1) - 1)
    def _():
        residual_copy.wait()
        y_ref[...] = _layer_norm(ALPHA * h_res[...] + y_ref[...], g_ref[...], b_ref[...])


def _ffn_ln(h, hb, wg, wu, wd, g, b, tm, th):
    n = h.shape[0]
    emit_weights = wg.dtype == F32
    assert not emit_weights or n == tm
    vec = pl.BlockSpec((1, D_MODEL), lambda i, j: (0, 0))
    w_specs = [
        pl.BlockSpec((D_MODEL, th), lambda i, j: (0, j)),
        pl.BlockSpec((D_MODEL, th), lambda i, j: (0, j)),
        pl.BlockSpec((th, D_MODEL), lambda i, j: (j, 0)),
    ]
    out_specs = [pl.BlockSpec((tm, D_MODEL), lambda i, j: (i, 0))]
    out_shape = [jax.ShapeDtypeStruct((n, D_MODEL), F32)]
    if emit_weights:
        out_specs += w_specs
        out_shape += [jax.ShapeDtypeStruct(w.shape, BF16) for w in (wg, wu, wd)]
    outs = pl.pallas_call(
        _ffn_ln_kernel,
        grid=(n // tm, FFN_HIDDEN // th),
        in_specs=[pl.BlockSpec((tm, D_MODEL), lambda i, j: (i, 0)), pl.BlockSpec(memory_space=pl.ANY)] + w_specs
        + [vec, vec],
        out_specs=out_specs,
        out_shape=out_shape,
        scratch_shapes=[pltpu.VMEM((tm, D_MODEL), F32), pltpu.SemaphoreType.DMA(())],
        compiler_params=pltpu.CompilerParams(dimension_semantics=("arbitrary", "arbitrary"),
                                             vmem_limit_bytes=VMEM_LIMIT_LARGE),
        name="ffn_ln",
    )(hb, h, wg, wu, wd, g, b)
    return tuple(outs) if emit_weights else outs[0]


def _position_tables(seq, t, past_len):
    r = jnp.arange(seq + RET_CHUNK + SAMPLE_TABLE_ROWS)
    meta_r = r - seq
    pos = jnp.where(r < seq, N_META + r,
                    jnp.where(meta_r < RET_CHUNK, jnp.maximum(meta_r - (RET_CHUNK - N_META), 0),
                              past_len + (meta_r - RET_CHUNK) % t)).astype(F32)[:, None]
    ret_freq = jnp.power(RET_THETA, -jnp.linspace(0.0, 1.0, RET_DK // 2, dtype=F32))
    ret_ang = pos * ret_freq[None, :]
    half = ROT_DIM // 2
    swa_freq = jnp.power(ROPE_THETA, -jnp.arange(0, ROT_DIM, 2, dtype=F32) / ROT_DIM)
    swa_ang = pos * swa_freq[None, :]
    cos = jnp.tile(jnp.cos(swa_ang), (1, LANES // half))
    sin = jnp.tile(jnp.sin(swa_ang), (1, LANES // half))
    d = jnp.arange(LANES) % SWA_HD
    c = jnp.where(d < ROT_DIM, cos, 1.0)
    s1 = jnp.where((d >= half) & (d < ROT_DIM), sin, 0.0)
    s2 = jnp.where(d < half, -sin, 0.0)
    return (jnp.cos(ret_ang), jnp.sin(ret_ang)), (c, s1, s2)


def kernel(x_prompt, x_sample, state_ret, cache_swa_k, cache_swa_v, meta_tokens, ln_emb_g, ln_emb_b,
           w_in, w_out, swa_sinks, ln_mix_g, ln_mix_b, w_ffn_gate, w_ffn_up, w_ffn_down, ln_ffn_g, ln_ffn_b):
    bsz, seq, d = x_prompt.shape
    dbs, t, _ = x_sample.shape
    assert w_in.shape[0] == DEPTH and d == D_MODEL and seq % RET_CHUNK == 0 and t & (t - 1) == 0
    past_len = 16384
    row = lambda a: a.reshape(1, -1)

    eg, eb = row(ln_emb_g), row(ln_emb_b)
    sinks = swa_sinks[0]
    lg = jnp.log(1.0 - jnp.power(2.0, -5.0 - jnp.arange(RET_HEADS, dtype=F32)))

    xp = x_prompt.reshape(bsz * seq, d)
    xs = x_sample.reshape(dbs * t, d)

    z, zm = _ln_proj(xp, xs, meta_tokens, eg, eb, w_in[0], 256, PROJ_W)
    sample_row = bsz * seq

    (cos, sin), swa_tabs = _position_tables(seq, t, past_len)
    ret_o_p, ret_state_p = _ret_prompt(lg, z, zm, bsz, seq, cos, sin)
    swa_o_p, k_p, v_p, ret_o_s, ret_state_s = _swa_prompt_ret_sample(
        sinks, z, zm, bsz, seq, *swa_tabs, lg, sample_row, state_ret[0], cos, sin, t)
    kc = cache_swa_k[0].reshape(dbs, WINDOW, LANES)
    vc = cache_swa_v[0].reshape(dbs, WINDOW, LANES)
    swa_o_s, k_s, v_s = _swa_sample(sinks, z, sample_row, kc, vc, *swa_tabs, seq + RET_CHUNK, t)

    mg, mb = row(ln_mix_g[0]), row(ln_mix_b[0])
    fg, fb = row(ln_ffn_g[0]), row(ln_ffn_b[0])
    ffn_w = (w_ffn_gate[0], w_ffn_up[0], w_ffn_down[0])
    h_p, hb_p = _out_proj_ln(ret_o_p, swa_o_p, xp, eg, eb, w_out[0], mg, mb, 512)
    h_s, hb_s = _out_proj_ln(ret_o_s, swa_o_s, xs, eg, eb, w_out[0], mg, mb, 512)
    y_s, *ffn_wb = _ffn_ln(h_s, hb_s, *ffn_w, fg, fb, TOK_TILE, 256)
    y_p = _ffn_ln(h_p, hb_p, *ffn_wb, fg, fb, TOK_TILE, 512)

    kv_shape = (DEPTH, -1, WINDOW, SWA_KV_HEADS, SWA_HD)
    return (y_p.reshape(bsz, seq, d), y_s.reshape(dbs, t, d),
            ret_state_p[None], k_p.reshape(kv_shape), v_p.reshape(kv_shape),
            ret_state_s[None], k_s.reshape(kv_shape), v_s.reshape(kv_shape))
```

```python
import functools

import jax
import jax.numpy as jnp
from jax import lax
from jax.experimental import pallas as pl
from jax.experimental.pallas import tpu as pltpu

F32 = jnp.float32
BF16 = jnp.bfloat16

D_MODEL = 2048
N_META = 16
RET_HEADS = 4
RET_DK = 256
RET_DV = 256
RET_CHUNK = 128
RET_THETA = 10000.0
SWA_HD = 64
SWA_HEADS = 16
SWA_KV_HEADS = 2
SWA_GROUP = SWA_HEADS // SWA_KV_HEADS
WINDOW = 128
ROPE_THETA = 500000.0
ROT_DIM = SWA_HD // 4
FFN_HIDDEN = 5632
PROJ_W = 5376
DEPTH = 1
ALPHA = (2.0 * DEPTH) ** 0.25
LN_EPS = 1e-5
NEG_INF = -1e30

LANES = 128
RET_W = RET_HEADS * RET_DK
SWA_Q_COL = 4 * RET_W
SWA_KV_COL = SWA_Q_COL + SWA_HEADS * SWA_HD
VMEM_LIMIT = 56 * 1024 * 1024
VMEM_LIMIT_LARGE = 60 * 1024 * 1024
TOK_TILE = 1024
RET_CHUNK_UNROLL = 16
RET_SAMPLE_EVERY = 2
SAMPLE_TABLE_ROWS = 64
WEIGHT_STAGE_ROWS = 256


def _cparams(sem):
    return pltpu.CompilerParams(dimension_semantics=sem, vmem_limit_bytes=VMEM_LIMIT)


def _layer_norm(x, g, b):
    mu = jnp.mean(x, axis=-1, keepdims=True)
    xc = x - mu
    var = jnp.mean(xc * xc, axis=-1, keepdims=True)
    return xc * lax.rsqrt(var + LN_EPS) * g + b


def _silu(x):
    return x / (1.0 + jnp.exp(-x))


def _dot(a, b):
    return jnp.dot(a, b, preferred_element_type=F32)


def _dot_nt(a, b):
    return lax.dot_general(a, b, (((1,), (1,)), ((), ())), preferred_element_type=F32)


def _dot_tn(a, b):
    return lax.dot_general(a, b, (((0,), (0,)), ((), ())), preferred_element_type=F32)


def _ln_proj_kernel(xp_ref, xs_ref, xm_ref, g_ref, b_ref, w_hbm, z_ref, zm_ref,
                    h_scr, hm_scr, wb_scr, stage, sem, *, np_tiles, ns_tiles):
    i = pl.program_id(0)
    j = pl.program_id(1)
    first = j == 0
    is_sample = i >= np_tiles
    is_last = i == np_tiles + ns_tiles - 1
    n_col, _, tn = wb_scr.shape
    chunk = stage.shape[1]
    n_chunks = w_hbm.shape[0] // chunk

    def weight_copy(k):
        return pltpu.make_async_copy(w_hbm.at[pl.ds(k * chunk, chunk), :], stage.at[k % 2], sem.at[k % 2])

    @pl.when(first & (i == 0))
    def _():
        weight_copy(0).start()
        for k in range(n_chunks):
            if k + 1 < n_chunks:
                weight_copy(k + 1).start()
            weight_copy(k).wait()
            for jt in range(n_col):
                wb_scr[jt, k * chunk:(k + 1) * chunk, :] = stage[k % 2, :, jt * tn:(jt + 1) * tn].astype(BF16)

    def norm(x_ref):
        return _layer_norm(x_ref[...], g_ref[...], b_ref[...]).astype(BF16)

    @pl.when(first & jnp.logical_not(is_sample))
    def _():
        h_scr[...] = norm(xp_ref)

    @pl.when(first & is_sample)
    def _():
        h_scr[...] = norm(xs_ref)

    @pl.when(first & is_last)
    def _():
        lead = hm_scr.shape[0] - xm_ref.shape[0]
        hm_scr[:lead, :] = jnp.zeros((lead, D_MODEL), BF16)
        hm_scr[lead:, :] = norm(xm_ref)

    z_ref[...] = _dot(h_scr[...], wb_scr[j])

    @pl.when(is_last)
    def _():
        zm_ref[...] = _dot(hm_scr[...], wb_scr[j])


def _ln_proj(xp, xs, xm, g, b, w, tm, tn):
    np_tiles = xp.shape[0] // tm
    ns_tiles = xs.shape[0] // tm
    last = np_tiles + ns_tiles - 1
    n_col = PROJ_W // tn
    return pl.pallas_call(
        functools.partial(_ln_proj_kernel, np_tiles=np_tiles, ns_tiles=ns_tiles),
        grid=(np_tiles + ns_tiles, n_col),
        in_specs=[
            pl.BlockSpec((tm, D_MODEL), lambda i, j: (jnp.minimum(i, np_tiles - 1), 0)),
            pl.BlockSpec((tm, D_MODEL), lambda i, j: (jnp.clip(i - np_tiles, 0, ns_tiles - 1), 0)),
            pl.BlockSpec(xm.shape, lambda i, j: (0, 0), pipeline_mode=pl.Buffered(1)),
            pl.BlockSpec((1, D_MODEL), lambda i, j: (0, 0)),
            pl.BlockSpec((1, D_MODEL), lambda i, j: (0, 0)),
            pl.BlockSpec(memory_space=pl.ANY),
        ],
        out_specs=[
            pl.BlockSpec((tm, tn), lambda i, j: (i, j)),
            pl.BlockSpec((RET_CHUNK, tn), lambda i, j: (0, jnp.where(i == last, j, 0))),
        ],
        out_shape=[
            jax.ShapeDtypeStruct((xp.shape[0] + xs.shape[0], PROJ_W), F32),
            jax.ShapeDtypeStruct((RET_CHUNK, PROJ_W), F32),
        ],
        scratch_shapes=[
            pltpu.VMEM((tm, D_MODEL), BF16),
            pltpu.VMEM((RET_CHUNK, D_MODEL), BF16),
            pltpu.VMEM((n_col, D_MODEL, tn), BF16),
            pltpu.VMEM((2, WEIGHT_STAGE_ROWS // 4, PROJ_W), F32),
            pltpu.SemaphoreType.DMA((2,)),
        ],
        compiler_params=pltpu.CompilerParams(dimension_semantics=("arbitrary", "arbitrary"),
                                             vmem_limit_bytes=VMEM_LIMIT_LARGE),
        name="ln_proj",
    )(xp, xs, xm, g, b, w)


def _ret_rope(x, cos, sin):
    x1 = x[:, :LANES]
    x2 = x[:, LANES:]
    return jnp.concatenate([x1 * cos - x2 * sin, x2 * cos + x1 * sin], axis=1)


def _group_norm_gate(o, gate):
    mu = jnp.mean(o, axis=-1, keepdims=True)
    oc = o - mu
    var = jnp.mean(oc * oc, axis=-1, keepdims=True)
    return oc * lax.rsqrt(var + LN_EPS) * _silu(gate)


def _ret_prompt_kernel(lg_ref, q_ref, k_ref, v_ref, g_ref, km_ref, vm_ref,
                       cos_ref, sin_ref, cosm_ref, sinm_ref, o_ref, s_ref, s_scr):
    C = RET_CHUNK
    lg = lg_ref[pl.program_id(1)]
    ri = lax.broadcasted_iota(jnp.int32, (C, C), 0)
    ci = lax.broadcasted_iota(jnp.int32, (C, C), 1)
    rel = (ri - ci).astype(F32)
    scale = RET_DK ** -0.5
    decay = jnp.where(rel >= 0.0, jnp.exp(jnp.maximum(rel, 0.0) * lg), 0.0) * scale
    row = lax.broadcasted_iota(jnp.int32, (C, 1), 0).astype(F32)
    q_decay = jnp.exp((row + 1.0) * lg)
    k_decay = jnp.exp((C - 1.0 - row) * lg) * scale
    chunk_decay = jnp.exp(jnp.full((1, RET_DV), C * lg, F32))

    meta_decay = jnp.where(row >= C - N_META, k_decay, 0.0)
    km = _ret_rope(km_ref[...], cosm_ref[...], sinm_ref[...])
    s_scr[...] = _dot_tn((km * meta_decay).astype(BF16), vm_ref[...].astype(BF16))

    def chunk(c, carry):
        rows = pl.ds(pl.multiple_of(c * C, C), C)
        cos = cos_ref[rows, :]
        sin = sin_ref[rows, :]
        q = _ret_rope(q_ref[rows, :], cos, sin)
        k = _ret_rope(k_ref[rows, :], cos, sin)
        qb = q.astype(BF16)
        vb = v_ref[rows, :].astype(BF16)
        s_prev = s_scr[...]
        scores = _dot_nt(qb, k.astype(BF16)) * decay
        inner = _dot(scores.astype(BF16), vb)
        cross = _dot(qb, s_prev.astype(BF16)) * q_decay
        s_scr[...] = chunk_decay * s_prev + _dot_tn((k * k_decay).astype(BF16), vb)
        o_ref[rows, :] = _group_norm_gate(inner + cross, g_ref[rows, :]).astype(BF16)
        return carry

    lax.fori_loop(0, q_ref.shape[0] // C, chunk, 0, unroll=RET_CHUNK_UNROLL)
    s_ref[...] = s_scr[...]


def _ret_prompt(lg, z, zm, bsz, seq, cos, sin):
    col = lambda base: (lambda b, h: (b, base + h))
    mcol = lambda base: (lambda b, h: (0, base + h))
    full = lambda b, h: (0, 0)
    meta_tab = lambda b, h: (seq // RET_CHUNK, 0)
    return pl.pallas_call(
        _ret_prompt_kernel,
        grid=(bsz, RET_HEADS),
        in_specs=[
            pl.BlockSpec(memory_space=pltpu.SMEM),
            pl.BlockSpec((seq, RET_DK), col(0)),
            pl.BlockSpec((seq, RET_DK), col(RET_HEADS)),
            pl.BlockSpec((seq, RET_DV), col(2 * RET_HEADS)),
            pl.BlockSpec((seq, RET_DV), col(3 * RET_HEADS)),
            pl.BlockSpec((RET_CHUNK, RET_DK), mcol(RET_HEADS)),
            pl.BlockSpec((RET_CHUNK, RET_DV), mcol(2 * RET_HEADS)),
            pl.BlockSpec((seq, LANES), full),
            pl.BlockSpec((seq, LANES), full),
            pl.BlockSpec((RET_CHUNK, LANES), meta_tab),
            pl.BlockSpec((RET_CHUNK, LANES), meta_tab),
        ],
        out_specs=[
            pl.BlockSpec((seq, RET_DV), lambda b, h: (b, h)),
            pl.BlockSpec((None, None, RET_DK, RET_DV), lambda b, h: (b, h, 0, 0)),
        ],
        out_shape=[
            jax.ShapeDtypeStruct((bsz * seq, RET_W), BF16),
            jax.ShapeDtypeStruct((bsz, RET_HEADS, RET_DK, RET_DV), F32),
        ],
        scratch_shapes=[pltpu.VMEM((RET_DK, RET_DV), F32)],
        compiler_params=_cparams(("parallel", "parallel")),
        name="ret_prompt",
    )(lg, z, z, z, z, zm, zm, cos, sin, cos, sin)


def _ret_sample_jobs(lg_ref, z_ref, s_ref, cos_ref, sin_ref, o_ref, so_ref, *, nb, t):
    rows = nb * t
    R = RET_HEADS * rows
    scale = RET_DK ** -0.5
    cos = cos_ref[...]
    sin = sin_ref[...]

    def stack(base, rope, mul=1.0):
        parts = []
        for h in range(RET_HEADS):
            x = z_ref[:, base + h * RET_DK: base + (h + 1) * RET_DK]
            parts.append(_ret_rope(x, cos, sin) * mul if rope else x)
        return jnp.concatenate(parts, axis=0)

    q = stack(0, True)
    k = stack(RET_W, True, scale)
    v = stack(2 * RET_W, False)
    gate = stack(3 * RET_W, False)

    ri = lax.broadcasted_iota(jnp.int32, (R, R), 0)
    ci = lax.broadcasted_iota(jnp.int32, (R, R), 1)
    rcol = lax.broadcasted_iota(jnp.int32, (R, 1), 0)
    lg_col = jnp.zeros((R, 1), F32)
    for h in range(RET_HEADS):
        lg_col = jnp.where((rcol >= h * rows) & (rcol < (h + 1) * rows), lg_ref[h], lg_col)
    tcol = (rcol & (t - 1)).astype(F32)
    rel = (ri - ci).astype(F32)
    same = ((ri & -t) == (ci & -t)) & (ri >= ci)
    decay = jnp.where(same, jnp.exp(jnp.maximum(rel, 0.0) * lg_col), 0.0)
    q_decay = jnp.exp((tcol + 1.0) * lg_col)
    k_decay = jnp.exp((t - 1.0 - tcol) * lg_col)

    qb = q.astype(BF16)
    vb = v.astype(BF16)
    scores = _dot_nt(qb, k.astype(BF16)) * decay
    inner = _dot(scores.astype(BF16), vb)
    kw = k * k_decay

    cross_parts = [None] * (RET_HEADS * nb)

    def state_job(h, db):
        def run():
            r0 = h * rows + db * t
            s_prev = s_ref[db, h]
            cross_parts[h * nb + db] = _dot(q[r0:r0 + t, :].astype(BF16), s_prev.astype(BF16))
            mine = (rcol >= r0) & (rcol < r0 + t)
            upd = _dot_tn(jnp.where(mine, kw, 0.0).astype(BF16), vb)
            step_decay = jnp.exp(jnp.full((1, RET_DV), t * lg_ref[h], F32))
            so_ref[db, h] = step_decay * s_prev + upd
        return run

    def finish():
        cross = jnp.concatenate(cross_parts, axis=0) * q_decay
        out = _group_norm_gate(inner + cross, gate).astype(BF16)
        for h in range(RET_HEADS):
            o_ref[:, h * RET_DV:(h + 1) * RET_DV] = out[h * rows:(h + 1) * rows, :]

    return [state_job(h, db) for h in range(RET_HEADS) for db in range(nb)], finish


def _swa_rope(x, c, s1, s2):
    return x * c + pltpu.roll(x, 8, 1) * s1 + pltpu.roll(x, LANES - 8, 1) * s2


def _dup_head(x, g, low):
    swapped = pltpu.roll(x, SWA_HD, 1)
    return jnp.where(low, x, swapped) if g == 0 else jnp.where(low, swapped, x)


def _to_kv_half(slab, head, low):
    g = head // SWA_GROUP
    src = slab if head % 2 == g else pltpu.roll(slab, SWA_HD, 1)
    return jnp.where(low, src, 0.0) if g == 0 else jnp.where(low, 0.0, src)


def _from_kv_half(o_even, o_odd, g, low):
    if g == 0:
        return jnp.where(low, o_even, pltpu.roll(o_odd, SWA_HD, 1))
    return jnp.where(low, pltpu.roll(o_even, SWA_HD, 1), o_odd)


def _sink_softmax(logits, sink):
    m = jnp.maximum(jnp.max(logits, axis=-1, keepdims=True), sink)
    p = jnp.exp(logits - m)
    return p, jnp.sum(p, axis=-1, keepdims=True) + jnp.exp(sink - m)


def _swa_prompt_kernel(sink_ref, q_ref, kv_ref, kvm_ref, c_ref, s1_ref, s2_ref,
                       cm_ref, s1m_ref, s2m_ref, o_ref, kp_ref, vp_ref, kprev, vprev):
    W = WINDOW
    m_id = pl.program_id(1)

    @pl.when(m_id == 0)
    def _():
        kprev[...] = _swa_rope(kvm_ref[:, :LANES], cm_ref[...], s1m_ref[...], s2m_ref[...])
        vprev[...] = kvm_ref[:, LANES:]

    c = c_ref[...]
    s1 = s1_ref[...]
    s2 = s2_ref[...]
    kcur = _swa_rope(kv_ref[:, :LANES], c, s1, s2)
    vcur = kv_ref[:, LANES:]
    k2 = jnp.concatenate([kprev[...], kcur], axis=0)
    v2 = jnp.concatenate([vprev[...], vcur], axis=0)

    r = lax.broadcasted_iota(jnp.int32, (W, 2 * W), 0)
    cc = lax.broadcasted_iota(jnp.int32, (W, 2 * W), 1)
    diff = W + r - cc
    mask = (diff >= 0) & (diff < W) & ((m_id > 0) | (cc >= W - N_META))
    low_k = lax.broadcasted_iota(jnp.int32, (2 * W, LANES), 1) < SWA_HD
    low = lax.broadcasted_iota(jnp.int32, (W, LANES), 1) < SWA_HD
    scale = SWA_HD ** -0.5

    def attend(q, kd, vd, h):
        p, den = _sink_softmax(jnp.where(mask, _dot_nt(q.astype(BF16), kd), NEG_INF), sink_ref[h])
        return _dot(p.astype(BF16), vd) / den

    for g in range(SWA_KV_HEADS):
        kd = _dup_head(k2, g, low_k).astype(BF16)
        vd = _dup_head(v2, g, low_k).astype(BF16)
        for p in range(g * SWA_GROUP // 2, (g + 1) * SWA_GROUP // 2):
            slab = _swa_rope(q_ref[:, p * LANES:(p + 1) * LANES], c, s1, s2) * scale
            o_even = attend(jnp.where(low, slab, 0.0), kd, vd, 2 * p)
            o_odd = attend(jnp.where(low, 0.0, slab), kd, vd, 2 * p + 1)
            o_ref[:, p * LANES:(p + 1) * LANES] = jnp.where(low, o_even, o_odd).astype(BF16)

    kprev[...] = kcur
    vprev[...] = vcur
    kp_ref[...] = kcur
    vp_ref[...] = vcur


def _swa_prompt_ret_sample_kernel(sink_ref, q_ref, kv_ref, kvm_ref, c_ref, s1_ref, s2_ref, cm_ref, s1m_ref, s2m_ref,
                                 lg_ref, zs_ref, st_ref, cos_ref, sin_ref, w_ref,
                                 o_ref, kp_ref, vp_ref, os_ref, sto_ref, wb_ref, kprev, vprev, *, nb, t, every):
    wb_ref[...] = w_ref[...].astype(BF16)
    _swa_prompt_kernel(sink_ref, q_ref, kv_ref, kvm_ref, c_ref, s1_ref, s2_ref, cm_ref, s1m_ref, s2m_ref,
                       o_ref, kp_ref, vp_ref, kprev, vprev)

    @pl.when(lax.rem(pl.program_id(1), every) == every - 1)
    def _():
        state_jobs, finish = _ret_sample_jobs(lg_ref, zs_ref, st_ref, cos_ref, sin_ref, os_ref, sto_ref, nb=nb, t=t)
        for job in state_jobs + [finish]:
            job()


def _swa_prompt_ret_sample(sinks, z, zm, bsz, seq, c, s1, s2, lg, sample_row, state, cos, sin, t, w):
    W = WINDOW
    nblk = seq // W
    every = RET_SAMPLE_EVERY
    ret_steps = bsz * nblk // every
    dbs = state.shape[0]
    nb = dbs // ret_steps
    assert nblk % every == 0 and nb * ret_steps == dbs and (nb * t) % 16 == 0
    w_rows = w.shape[0] // (bsz * nblk)
    assert w_rows * bsz * nblk == w.shape[0] and w_rows % 16 == 0
    w_spec = pl.BlockSpec((w_rows, w.shape[1]), lambda b, m: (b * nblk + m, 0))
    rows = nb * t
    qw = SWA_HEADS * SWA_HD
    kv_col = SWA_KV_COL // (2 * LANES)
    tab = pl.BlockSpec((W, LANES), lambda b, m: (m, 0))
    mtab = pl.BlockSpec((W, LANES), lambda b, m: (seq // W, 0))
    stab = pl.BlockSpec((rows, LANES), lambda b, m: ((seq + RET_CHUNK) // rows, 0))
    ret_step = lambda b, m: (b * nblk + m) // every
    state_spec = pl.BlockSpec((nb, RET_HEADS, RET_DK, RET_DV), lambda b, m: (ret_step(b, m), 0, 0, 0))
    return pl.pallas_call(
        functools.partial(_swa_prompt_ret_sample_kernel, nb=nb, t=t, every=every),
        grid=(bsz, nblk),
        in_specs=[
            pl.BlockSpec(memory_space=pltpu.SMEM),
            pl.BlockSpec((W, qw), lambda b, m: (b * nblk + m, SWA_Q_COL // qw)),
            pl.BlockSpec((W, 2 * LANES), lambda b, m: (b * nblk + m, kv_col)),
            pl.BlockSpec((W, 2 * LANES), lambda b, m: (0, kv_col)),
            tab, tab, tab, mtab, mtab, mtab,
            pl.BlockSpec(memory_space=pltpu.SMEM),
            pl.BlockSpec((rows, 4 * RET_W), lambda b, m: (sample_row // rows + ret_step(b, m), 0)),
            state_spec, stab, stab, w_spec,
        ],
        out_specs=[
            pl.BlockSpec((W, qw), lambda b, m: (b * nblk + m, 0)),
            pl.BlockSpec((None, W, LANES), lambda b, m: (b, 0, 0)),
            pl.BlockSpec((None, W, LANES), lambda b, m: (b, 0, 0)),
            pl.BlockSpec((rows, RET_W), lambda b, m: (ret_step(b, m), 0)),
            state_spec, w_spec,
        ],
        out_shape=[
            jax.ShapeDtypeStruct((bsz * seq, qw), BF16),
            jax.ShapeDtypeStruct((bsz, W, LANES), F32),
            jax.ShapeDtypeStruct((bsz, W, LANES), F32),
            jax.ShapeDtypeStruct((dbs * t, RET_W), BF16),
            jax.ShapeDtypeStruct(state.shape, F32),
            jax.ShapeDtypeStruct(w.shape, BF16),
        ],
        scratch_shapes=[pltpu.VMEM((W, LANES), F32), pltpu.VMEM((W, LANES), F32)],
        compiler_params=_cparams(("parallel", "arbitrary")),
        name="swa_prompt_ret_sample",
    )(sinks, z, z, zm, c, s1, s2, c, s1, s2, lg, z, state, cos, sin, w)


def _swa_sample_kernel(sink_ref, q_ref, kv_ref, kc_ref, vc_ref, c_ref, s1_ref, s2_ref,
                       o_ref, ko_ref, vo_ref, *, nb, t):
    W = WINDOW
    keys = 2 * W
    c = c_ref[...]
    s1 = s1_ref[...]
    s2 = s2_ref[...]
    knew = _swa_rope(kv_ref[:, :LANES], c, s1, s2)
    vnew = kv_ref[:, LANES:]
    pad = jnp.zeros((keys - W - t, LANES), F32)

    rq = SWA_HEADS * t
    tq = lax.broadcasted_iota(jnp.int32, (rq, keys), 0) & (t - 1)
    cc = lax.broadcasted_iota(jnp.int32, (rq, keys), 1)
    diff = W + tq - cc
    mask = (diff >= 0) & (diff < W)
    low = lax.broadcasted_iota(jnp.int32, (nb * t, LANES), 1) < SWA_HD
    low_t = lax.broadcasted_iota(jnp.int32, (t, LANES), 1) < SWA_HD
    sink_col = jnp.concatenate([jnp.full((t, 1), sink_ref[h], F32) for h in range(SWA_HEADS)], axis=0)
    npair = SWA_HEADS // 2

    scale = SWA_HD ** -0.5
    slabs = [_swa_rope(q_ref[:, p * LANES:(p + 1) * LANES], c, s1, s2) * scale for p in range(npair)]
    q_heads = [_to_kv_half(slabs[h // 2], h, low) for h in range(SWA_HEADS)]

    logits, values = [], []
    for db in range(nb):
        rows = slice(db * t, (db + 1) * t)
        kn = knew[rows, :]
        vn = vnew[rows, :]
        ko_ref[db, 0:W - t, :] = kc_ref[db, t:W, :]
        ko_ref[db, W - t:W, :] = kn
        vo_ref[db, 0:W - t, :] = vc_ref[db, t:W, :]
        vo_ref[db, W - t:W, :] = vn
        k2 = jnp.concatenate([kc_ref[db], kn, pad], axis=0).astype(BF16)
        values.append(jnp.concatenate([vc_ref[db], vn, pad], axis=0).astype(BF16))
        q_db = jnp.concatenate([qh[rows, :] for qh in q_heads], axis=0).astype(BF16)
        logits.append(jnp.where(mask, _dot_nt(q_db, k2), NEG_INF))
    p, den = _sink_softmax(jnp.concatenate(logits, axis=0), jnp.concatenate([sink_col] * nb, axis=0))
    p = p.astype(BF16)
    outs = [[] for _ in range(npair)]
    for db in range(nb):
        o = _dot(p[db * rq:(db + 1) * rq, :], values[db]) / den[db * rq:(db + 1) * rq, :]
        for pr in range(npair):
            o_even = o[(2 * pr) * t:(2 * pr + 1) * t, :]
            o_odd = o[(2 * pr + 1) * t:(2 * pr + 2) * t, :]
            outs[pr].append(_from_kv_half(o_even, o_odd, 2 * pr // SWA_GROUP, low_t))
    for pr in range(npair):
        o_ref[:, pr * LANES:(pr + 1) * LANES] = jnp.concatenate(outs[pr], axis=0).astype(BF16)


def _swa_sample(sinks, z, row0, kc, vc, c, s1, s2, tab_row0, t):
    dbs = kc.shape[0]
    n = dbs * t
    rows = SAMPLE_TABLE_ROWS
    nb = rows // t
    W = WINDOW
    qw = SWA_HEADS * SWA_HD
    tab = pl.BlockSpec((rows, LANES), lambda i: (tab_row0 // rows, 0))
    cache = pl.BlockSpec((nb, W, LANES), lambda i: (i, 0, 0))
    return pl.pallas_call(
        functools.partial(_swa_sample_kernel, nb=nb, t=t),
        grid=(dbs // nb,),
        in_specs=[
            pl.BlockSpec(memory_space=pltpu.SMEM),
            pl.BlockSpec((rows, qw), lambda i: (row0 // rows + i, SWA_Q_COL // qw)),
            pl.BlockSpec((rows, 2 * LANES), lambda i: (row0 // rows + i, SWA_KV_COL // (2 * LANES))),
            cache, cache, tab, tab, tab,
        ],
        out_specs=[pl.BlockSpec((rows, qw), lambda i: (i, 0)), cache, cache],
        out_shape=[
            jax.ShapeDtypeStruct((n, qw), BF16),
            jax.ShapeDtypeStruct((dbs, W, LANES), F32),
            jax.ShapeDtypeStruct((dbs, W, LANES), F32),
        ],
        compiler_params=_cparams(("parallel",)),
        name="swa_sample",
    )(sinks, z, z, kc, vc, c, s1, s2)


def _out_proj_ln_kernel(ro_ref, so_ref, x_ref, eg_ref, eb_ref, w_hbm, mg_ref, mb_ref, h_ref, hb_ref,
                        wb_scr, stage, sem):
    chunk = stage.shape[1]
    n_chunks = w_hbm.shape[0] // chunk

    def weight_copy(k):
        return pltpu.make_async_copy(w_hbm.at[pl.ds(k * chunk, chunk), :], stage.at[k % 2], sem.at[k % 2])

    @pl.when(pl.program_id(0) == 0)
    def _():
        weight_copy(0).start()
        for k in range(n_chunks):
            if k + 1 < n_chunks:
                weight_copy(k + 1).start()
            weight_copy(k).wait()
            wb_scr[k * chunk:(k + 1) * chunk, :] = stage[k % 2].astype(BF16)

    sub = h_ref.shape[0] // 2
    for r in range(2):
        rows = slice(r * sub, (r + 1) * sub)
        x_in = _layer_norm(x_ref[rows, :], eg_ref[...], eb_ref[...])
        mixed = _dot(ro_ref[rows, :], wb_scr[:RET_W, :]) + _dot(so_ref[rows, :], wb_scr[RET_W:, :])
        h = _layer_norm(ALPHA * x_in + mixed, mg_ref[...], mb_ref[...])
        h_ref[rows, :] = h
        hb_ref[rows, :] = h.astype(BF16)


def _out_proj_ln(ro, so, x, eg, eb, w, mg, mb, tm):
    n = x.shape[0]
    vec = pl.BlockSpec((1, D_MODEL), lambda i: (0, 0))
    tile = pl.BlockSpec((tm, D_MODEL), lambda i: (i, 0))
    return pl.pallas_call(
        _out_proj_ln_kernel,
        grid=(n // tm,),
        in_specs=[
            pl.BlockSpec((tm, RET_W), lambda i: (i, 0)),
            pl.BlockSpec((tm, RET_W), lambda i: (i, 0)),
            tile, vec, vec,
            pl.BlockSpec(memory_space=pl.ANY),
            vec, vec,
        ],
        out_specs=[tile, tile],
        out_shape=[jax.ShapeDtypeStruct((n, D_MODEL), F32), jax.ShapeDtypeStruct((n, D_MODEL), BF16)],
        scratch_shapes=[pltpu.VMEM((D_MODEL, D_MODEL), BF16),
                        pltpu.VMEM((2, WEIGHT_STAGE_ROWS, D_MODEL), F32),
                        pltpu.SemaphoreType.DMA((2,))],
        compiler_params=_cparams(("arbitrary",)),
        name="out_proj_ln",
    )(ro, so, x, eg, eb, w, mg, mb)


def _ffn_ln_kernel(hb_ref, h_hbm, wg_ref, wu_ref, wd_ref, g_ref, b_ref, y_ref, *rest):
    *w_out_refs, h_res, sem = rest
    w_out_refs = iter(w_out_refs)
    i = pl.program_id(0)
    j = pl.program_id(1)
    tm = h_res.shape[0]
    residual_copy = pltpu.make_async_copy(h_hbm.at[pl.ds(i * tm, tm), :], h_res, sem)

    @pl.when(j == 0)
    def _():
        residual_copy.start()
        y_ref[...] = jnp.zeros_like(y_ref)

    wg, wu, wd = (w[...].astype(BF16) for w in (wg_ref, wu_ref, wd_ref))
    for w_ref, w in zip((wg_ref, wu_ref, wd_ref), (wg, wu, wd)):
        if w_ref.dtype == F32:
            next(w_out_refs)[...] = w
    hb = hb_ref[...]
    act = _silu(_dot(hb, wg)) * _dot(hb, wu)
    y_ref[...] += _dot(act.astype(BF16), wd)

    @pl.when(j == pl.num_programs(1) - 1)
    def _():
        residual_copy.wait()
        y_ref[...] = _layer_norm(ALPHA * h_res[...] + y_ref[...], g_ref[...], b_ref[...])


def _ffn_ln(h, hb, wg, wu, wd, g, b, tm, th):
    n = h.shape[0]
    vec = pl.BlockSpec((1, D_MODEL), lambda i, j: (0, 0))
    w_specs = [
        pl.BlockSpec((D_MODEL, th), lambda i, j: (0, j)),
        pl.BlockSpec((D_MODEL, th), lambda i, j: (0, j)),
        pl.BlockSpec((th, D_MODEL), lambda i, j: (j, 0)),
    ]
    out_specs = [pl.BlockSpec((tm, D_MODEL), lambda i, j: (i, 0))]
    out_shape = [jax.ShapeDtypeStruct((n, D_MODEL), F32)]
    for w, spec in zip((wg, wu, wd), w_specs):
        if w.dtype == F32:
            assert n == tm
            out_specs.append(spec)
            out_shape.append(jax.ShapeDtypeStruct(w.shape, BF16))
    outs = pl.pallas_call(
        _ffn_ln_kernel,
        grid=(n // tm, FFN_HIDDEN // th),
        in_specs=[pl.BlockSpec((tm, D_MODEL), lambda i, j: (i, 0)), pl.BlockSpec(memory_space=pl.ANY)] + w_specs
        + [vec, vec],
        out_specs=out_specs,
        out_shape=out_shape,
        scratch_shapes=[pltpu.VMEM((tm, D_MODEL), F32), pltpu.SemaphoreType.DMA(())],
        compiler_params=pltpu.CompilerParams(dimension_semantics=("arbitrary", "arbitrary"),
                                             vmem_limit_bytes=VMEM_LIMIT_LARGE),
        name="ffn_ln",
    )(hb, h, wg, wu, wd, g, b)
    return tuple(outs)


def _position_tables(seq, t, past_len):
    r = jnp.arange(seq + RET_CHUNK + SAMPLE_TABLE_ROWS)
    meta_r = r - seq
    pos = jnp.where(r < seq, N_META + r,
                    jnp.where(meta_r < RET_CHUNK, jnp.maximum(meta_r - (RET_CHUNK - N_META), 0),
                              past_len + (meta_r - RET_CHUNK) % t)).astype(F32)[:, None]
    ret_freq = jnp.power(RET_THETA, -jnp.linspace(0.0, 1.0, RET_DK // 2, dtype=F32))
    ret_ang = pos * ret_freq[None, :]
    half = ROT_DIM // 2
    swa_freq = jnp.power(ROPE_THETA, -jnp.arange(0, ROT_DIM, 2, dtype=F32) / ROT_DIM)
    swa_ang = pos * swa_freq[None, :]
    cos = jnp.tile(jnp.cos(swa_ang), (1, LANES // half))
    sin = jnp.tile(jnp.sin(swa_ang), (1, LANES // half))
    d = jnp.arange(LANES) % SWA_HD
    c = jnp.where(d < ROT_DIM, cos, 1.0)
    s1 = jnp.where((d >= half) & (d < ROT_DIM), sin, 0.0)
    s2 = jnp.where(d < half, -sin, 0.0)
    return (jnp.cos(ret_ang), jnp.sin(ret_ang)), (c, s1, s2)


def kernel(x_prompt, x_sample, state_ret, cache_swa_k, cache_swa_v, meta_tokens, ln_emb_g, ln_emb_b,
           w_in, w_out, swa_sinks, ln_mix_g, ln_mix_b, w_ffn_gate, w_ffn_up, w_ffn_down, ln_ffn_g, ln_ffn_b):
    bsz, seq, d = x_prompt.shape
    dbs, t, _ = x_sample.shape
    assert w_in.shape[0] == DEPTH and d == D_MODEL and seq % RET_CHUNK == 0 and t & (t - 1) == 0
    past_len = 16384
    row = lambda a: a.reshape(1, -1)

    eg, eb = row(ln_emb_g), row(ln_emb_b)
    sinks = swa_sinks[0]
    lg = jnp.log(1.0 - jnp.power(2.0, -5.0 - jnp.arange(RET_HEADS, dtype=F32)))

    xp = x_prompt.reshape(bsz * seq, d)
    xs = x_sample.reshape(dbs * t, d)

    z, zm = _ln_proj(xp, xs, meta_tokens, eg, eb, w_in[0], 256, PROJ_W)
    sample_row = bsz * seq

    (cos, sin), swa_tabs = _position_tables(seq, t, past_len)
    ret_o_p, ret_state_p = _ret_prompt(lg, z, zm, bsz, seq, cos, sin)
    swa_o_p, k_p, v_p, ret_o_s, ret_state_s, w_gate_b = _swa_prompt_ret_sample(
        sinks, z, zm, bsz, seq, *swa_tabs, lg, sample_row, state_ret[0], cos, sin, t, w_ffn_gate[0])
    kc = cache_swa_k[0].reshape(dbs, WINDOW, LANES)
    vc = cache_swa_v[0].reshape(dbs, WINDOW, LANES)
    swa_o_s, k_s, v_s = _swa_sample(sinks, z, sample_row, kc, vc, *swa_tabs, seq + RET_CHUNK, t)

    mg, mb = row(ln_mix_g[0]), row(ln_mix_b[0])
    fg, fb = row(ln_ffn_g[0]), row(ln_ffn_b[0])
    h_p, hb_p = _out_proj_ln(ret_o_p, swa_o_p, xp, eg, eb, w_out[0], mg, mb, 512)
    h_s, hb_s = _out_proj_ln(ret_o_s, swa_o_s, xs, eg, eb, w_out[0], mg, mb, 512)
    y_s, w_up_b, w_down_b = _ffn_ln(h_s, hb_s, w_gate_b, w_ffn_up[0], w_ffn_down[0], fg, fb, TOK_TILE, 256)
    y_p, = _ffn_ln(h_p, hb_p, w_gate_b, w_up_b, w_down_b, fg, fb, TOK_TILE, 512)

    kv_shape = (DEPTH, -1, WINDOW, SWA_KV_HEADS, SWA_HD)
    return (y_p.reshape(bsz, seq, d), y_s.reshape(dbs, t, d),
            ret_state_p[None], k_p.reshape(kv_shape), v_p.reshape(kv_shape),
            ret_state_s[None], k_s.reshape(kv_shape), v_s.reshape(kv_shape))
```

```python
import functools

import jax
import jax.numpy as jnp
from jax import lax
from jax.experimental import pallas as pl
from jax.experimental.pallas import tpu as pltpu

F32 = jnp.float32
BF16 = jnp.bfloat16

D_MODEL = 2048
N_META = 16
RET_HEADS = 4
RET_DK = 256
RET_DV = 256
RET_CHUNK = 128
RET_THETA = 10000.0
SWA_HD = 64
SWA_HEADS = 16
SWA_KV_HEADS = 2
SWA_GROUP = SWA_HEADS // SWA_KV_HEADS
WINDOW = 128
ROPE_THETA = 500000.0
ROT_DIM = SWA_HD // 4
FFN_HIDDEN = 5632
PROJ_W = 5376
DEPTH = 1
ALPHA = (2.0 * DEPTH) ** 0.25
LN_EPS = 1e-5
NEG_INF = -1e30

LANES = 128
RET_W = RET_HEADS * RET_DK
SWA_Q_COL = 4 * RET_W
SWA_KV_COL = SWA_Q_COL + SWA_HEADS * SWA_HD
VMEM_LIMIT = 56 * 1024 * 1024
VMEM_LIMIT_LARGE = 60 * 1024 * 1024
TOK_TILE = 1024
RET_CHUNK_UNROLL = 16
RET_SAMPLE_EVERY = 2
SAMPLE_TABLE_ROWS = 64
WEIGHT_STAGE_ROWS = 256


def _cparams(sem):
    return pltpu.CompilerParams(dimension_semantics=sem, vmem_limit_bytes=VMEM_LIMIT)


def _layer_norm(x, g, b):
    mu = jnp.mean(x, axis=-1, keepdims=True)
    xc = x - mu
    var = jnp.mean(xc * xc, axis=-1, keepdims=True)
    return xc * lax.rsqrt(var + LN_EPS) * g + b


def _silu(x):
    return x / (1.0 + jnp.exp(-x))


def _dot(a, b):
    return jnp.dot(a, b, preferred_element_type=F32)


def _dot_nt(a, b):
    return lax.dot_general(a, b, (((1,), (1,)), ((), ())), preferred_element_type=F32)


def _dot_tn(a, b):
    return lax.dot_general(a, b, (((0,), (0,)), ((), ())), preferred_element_type=F32)


def _ln_proj_kernel(xp_ref, xs_ref, xm_ref, g_ref, b_ref, w_hbm, z_ref, zm_ref,
                    h_scr, hm_scr, wb_scr, stage, sem, *, np_tiles, ns_tiles):
    i = pl.program_id(0)
    j = pl.program_id(1)
    first = j == 0
    is_sample = i >= np_tiles
    is_last = i == np_tiles + ns_tiles - 1
    n_col, _, tn = wb_scr.shape
    chunk = stage.shape[1]
    n_chunks = w_hbm.shape[0] // chunk

    def weight_copy(k):
        return pltpu.make_async_copy(w_hbm.at[pl.ds(k * chunk, chunk), :], stage.at[k % 2], sem.at[k % 2])

    @pl.when(first & (i == 0))
    def _():
        weight_copy(0).start()
        for k in range(n_chunks):
            if k + 1 < n_chunks:
                weight_copy(k + 1).start()
            weight_copy(k).wait()
            for jt in range(n_col):
                wb_scr[jt, k * chunk:(k + 1) * chunk, :] = stage[k % 2, :, jt * tn:(jt + 1) * tn].astype(BF16)

    def norm(x_ref):
        return _layer_norm(x_ref[...], g_ref[...], b_ref[...]).astype(BF16)

    @pl.when(first & jnp.logical_not(is_sample))
    def _():
        h_scr[...] = norm(xp_ref)

    @pl.when(first & is_sample)
    def _():
        h_scr[...] = norm(xs_ref)

    @pl.when(first & is_last)
    def _():
        lead = hm_scr.shape[0] - xm_ref.shape[0]
        hm_scr[:lead, :] = jnp.zeros((lead, D_MODEL), BF16)
        hm_scr[lead:, :] = norm(xm_ref)

    z_ref[...] = _dot(h_scr[...], wb_scr[j])

    @pl.when(is_last)
    def _():
        zm_ref[...] = _dot(hm_scr[...], wb_scr[j])


def _ln_proj(xp, xs, xm, g, b, w, tm, tn):
    np_tiles = xp.shape[0] // tm
    ns_tiles = xs.shape[0] // tm
    last = np_tiles + ns_tiles - 1
    n_col = PROJ_W // tn
    return pl.pallas_call(
        functools.partial(_ln_proj_kernel, np_tiles=np_tiles, ns_tiles=ns_tiles),
        grid=(np_tiles + ns_tiles, n_col),
        in_specs=[
            pl.BlockSpec((tm, D_MODEL), lambda i, j: (jnp.minimum(i, np_tiles - 1), 0)),
            pl.BlockSpec((tm, D_MODEL), lambda i, j: (jnp.clip(i - np_tiles, 0, ns_tiles - 1), 0)),
            pl.BlockSpec(xm.shape, lambda i, j: (0, 0), pipeline_mode=pl.Buffered(1)),
            pl.BlockSpec((1, D_MODEL), lambda i, j: (0, 0)),
            pl.BlockSpec((1, D_MODEL), lambda i, j: (0, 0)),
            pl.BlockSpec(memory_space=pl.ANY),
        ],
        out_specs=[
            pl.BlockSpec((tm, tn), lambda i, j: (i, j)),
            pl.BlockSpec((RET_CHUNK, tn), lambda i, j: (0, jnp.where(i == last, j, 0))),
        ],
        out_shape=[
            jax.ShapeDtypeStruct((xp.shape[0] + xs.shape[0], PROJ_W), F32),
            jax.ShapeDtypeStruct((RET_CHUNK, PROJ_W), F32),
        ],
        scratch_shapes=[
            pltpu.VMEM((tm, D_MODEL), BF16),
            pltpu.VMEM((RET_CHUNK, D_MODEL), BF16),
            pltpu.VMEM((n_col, D_MODEL, tn), BF16),
            pltpu.VMEM((2, WEIGHT_STAGE_ROWS // 4, PROJ_W), F32),
            pltpu.SemaphoreType.DMA((2,)),
        ],
        compiler_params=pltpu.CompilerParams(dimension_semantics=("arbitrary", "arbitrary"),
                                             vmem_limit_bytes=VMEM_LIMIT_LARGE),
        name="ln_proj",
    )(xp, xs, xm, g, b, w)


def _ret_rope(x, cos, sin):
    x1 = x[:, :LANES]
    x2 = x[:, LANES:]
    return jnp.concatenate([x1 * cos - x2 * sin, x2 * cos + x1 * sin], axis=1)


def _group_norm_gate(o, gate):
    mu = jnp.mean(o, axis=-1, keepdims=True)
    oc = o - mu
    var = jnp.mean(oc * oc, axis=-1, keepdims=True)
    return oc * lax.rsqrt(var + LN_EPS) * _silu(gate)


def _ret_prompt_kernel(lg_ref, q_ref, k_ref, v_ref, g_ref, km_ref, vm_ref,
                       cos_ref, sin_ref, cosm_ref, sinm_ref, o_ref, s_ref, s_scr):
    C = RET_CHUNK
    lg = lg_ref[pl.program_id(1)]
    ri = lax.broadcasted_iota(jnp.int32, (C, C), 0)
    ci = lax.broadcasted_iota(jnp.int32, (C, C), 1)
    rel = (ri - ci).astype(F32)
    decay = jnp.where(rel >= 0.0, jnp.exp(jnp.maximum(rel, 0.0) * lg), 0.0)
    row = lax.broadcasted_iota(jnp.int32, (C, 1), 0).astype(F32)
    q_decay = jnp.exp((row + 1.0) * lg)
    k_decay = jnp.exp((C - 1.0 - row) * lg)
    chunk_decay = jnp.exp(jnp.full((1, RET_DV), C * lg, F32))
    scale = RET_DK ** -0.5

    meta_decay = jnp.where(row >= C - N_META, k_decay, 0.0)
    km = _ret_rope(km_ref[...], cosm_ref[...], sinm_ref[...]) * scale
    s_scr[...] = _dot_tn((km * meta_decay).astype(BF16), vm_ref[...].astype(BF16))

    def chunk(c, carry):
        rows = pl.ds(pl.multiple_of(c * C, C), C)
        cos = cos_ref[rows, :]
        sin = sin_ref[rows, :]
        q = _ret_rope(q_ref[rows, :], cos, sin)
        k = _ret_rope(k_ref[rows, :], cos, sin) * scale
        qb = q.astype(BF16)
        vb = v_ref[rows, :].astype(BF16)
        s_prev = s_scr[...]
        scores = _dot_nt(qb, k.astype(BF16)) * decay
        inner = _dot(scores.astype(BF16), vb)
        cross = _dot(qb, s_prev.astype(BF16)) * q_decay
        s_scr[...] = chunk_decay * s_prev + _dot_tn((k * k_decay).astype(BF16), vb)
        o_ref[rows, :] = _group_norm_gate(inner + cross, g_ref[rows, :]).astype(BF16)
        return carry

    lax.fori_loop(0, q_ref.shape[0] // C, chunk, 0, unroll=RET_CHUNK_UNROLL)
    s_ref[...] = s_scr[...]


def _ret_prompt(lg, z, zm, bsz, seq, cos, sin):
    col = lambda base: (lambda b, h: (b, base + h))
    mcol = lambda base: (lambda b, h: (0, base + h))
    full = lambda b, h: (0, 0)
    meta_tab = lambda b, h: (seq // RET_CHUNK, 0)
    return pl.pallas_call(
        _ret_prompt_kernel,
        grid=(bsz, RET_HEADS),
        in_specs=[
            pl.BlockSpec(memory_space=pltpu.SMEM),
            pl.BlockSpec((seq, RET_DK), col(0)),
            pl.BlockSpec((seq, RET_DK), col(RET_HEADS)),
            pl.BlockSpec((seq, RET_DV), col(2 * RET_HEADS)),
            pl.BlockSpec((seq, RET_DV), col(3 * RET_HEADS)),
            pl.BlockSpec((RET_CHUNK, RET_DK), mcol(RET_HEADS)),
            pl.BlockSpec((RET_CHUNK, RET_DV), mcol(2 * RET_HEADS)),
            pl.BlockSpec((seq, LANES), full),
            pl.BlockSpec((seq, LANES), full),
            pl.BlockSpec((RET_CHUNK, LANES), meta_tab),
            pl.BlockSpec((RET_CHUNK, LANES), meta_tab),
        ],
        out_specs=[
            pl.BlockSpec((seq, RET_DV), lambda b, h: (b, h)),
            pl.BlockSpec((None, None, RET_DK, RET_DV), lambda b, h: (b, h, 0, 0)),
        ],
        out_shape=[
            jax.ShapeDtypeStruct((bsz * seq, RET_W), BF16),
            jax.ShapeDtypeStruct((bsz, RET_HEADS, RET_DK, RET_DV), F32),
        ],
        scratch_shapes=[pltpu.VMEM((RET_DK, RET_DV), F32)],
        compiler_params=_cparams(("parallel", "parallel")),
        name="ret_prompt",
    )(lg, z, z, z, z, zm, zm, cos, sin, cos, sin)


def _ret_sample_jobs(lg_ref, z_ref, s_ref, cos_ref, sin_ref, o_ref, so_ref, *, nb, t):
    rows = nb * t
    R = RET_HEADS * rows
    scale = RET_DK ** -0.5
    cos = cos_ref[...]
    sin = sin_ref[...]

    def stack(base, rope, mul=1.0):
        parts = []
        for h in range(RET_HEADS):
            x = z_ref[:, base + h * RET_DK: base + (h + 1) * RET_DK]
            parts.append(_ret_rope(x, cos, sin) * mul if rope else x)
        return jnp.concatenate(parts, axis=0)

    q = stack(0, True)
    k = stack(RET_W, True, scale)
    v = stack(2 * RET_W, False)
    gate = stack(3 * RET_W, False)

    ri = lax.broadcasted_iota(jnp.int32, (R, R), 0)
    ci = lax.broadcasted_iota(jnp.int32, (R, R), 1)
    rcol = lax.broadcasted_iota(jnp.int32, (R, 1), 0)
    lg_col = jnp.zeros((R, 1), F32)
    for h in range(RET_HEADS):
        lg_col = jnp.where((rcol >= h * rows) & (rcol < (h + 1) * rows), lg_ref[h], lg_col)
    tcol = (rcol & (t - 1)).astype(F32)
    rel = (ri - ci).astype(F32)
    same = ((ri & -t) == (ci & -t)) & (ri >= ci)
    decay = jnp.where(same, jnp.exp(jnp.maximum(rel, 0.0) * lg_col), 0.0)
    q_decay = jnp.exp((tcol + 1.0) * lg_col)
    k_decay = jnp.exp((t - 1.0 - tcol) * lg_col)

    qb = q.astype(BF16)
    vb = v.astype(BF16)
    scores = _dot_nt(qb, k.astype(BF16)) * decay
    inner = _dot(scores.astype(BF16), vb)
    kw = k * k_decay

    cross_parts = [None] * (RET_HEADS * nb)

    def state_job(h, db):
        def run():
            r0 = h * rows + db * t
            s_prev = s_ref[db, h]
            cross_parts[h * nb + db] = _dot(q[r0:r0 + t, :].astype(BF16), s_prev.astype(BF16))
            mine = (rcol >= r0) & (rcol < r0 + t)
            upd = _dot_tn(jnp.where(mine, kw, 0.0).astype(BF16), vb)
            step_decay = jnp.exp(jnp.full((1, RET_DV), t * lg_ref[h], F32))
            so_ref[db, h] = step_decay * s_prev + upd
        return run

    def finish():
        cross = jnp.concatenate(cross_parts, axis=0) * q_decay
        out = _group_norm_gate(inner + cross, gate).astype(BF16)
        for h in range(RET_HEADS):
            o_ref[:, h * RET_DV:(h + 1) * RET_DV] = out[h * rows:(h + 1) * rows, :]

    return [state_job(h, db) for h in range(RET_HEADS) for db in range(nb)], finish


def _swa_rope(x, c, s1, s2):
    return x * c + pltpu.roll(x, 8, 1) * s1 + pltpu.roll(x, LANES - 8, 1) * s2


def _dup_head(x, g, low):
    swapped = pltpu.roll(x, SWA_HD, 1)
    return jnp.where(low, x, swapped) if g == 0 else jnp.where(low, swapped, x)


def _to_kv_half(slab, head, low):
    g = head // SWA_GROUP
    src = slab if head % 2 == g else pltpu.roll(slab, SWA_HD, 1)
    return jnp.where(low, src, 0.0) if g == 0 else jnp.where(low, 0.0, src)


def _from_kv_half(o_even, o_odd, g, low):
    if g == 0:
        return jnp.where(low, o_even, pltpu.roll(o_odd, SWA_HD, 1))
    return jnp.where(low, pltpu.roll(o_even, SWA_HD, 1), o_odd)


def _sink_softmax(logits, sink):
    m = jnp.maximum(jnp.max(logits, axis=-1, keepdims=True), sink)
    p = jnp.exp(logits - m)
    return p, jnp.sum(p, axis=-1, keepdims=True) + jnp.exp(sink - m)


def _swa_prompt_kernel(sink_ref, q_ref, kv_ref, kvm_ref, c_ref, s1_ref, s2_ref,
                       cm_ref, s1m_ref, s2m_ref, o_ref, kp_ref, vp_ref, kprev, vprev):
    W = WINDOW
    m_id = pl.program_id(1)

    @pl.when(m_id == 0)
    def _():
        kprev[...] = _swa_rope(kvm_ref[:, :LANES], cm_ref[...], s1m_ref[...], s2m_ref[...])
        vprev[...] = kvm_ref[:, LANES:]

    c = c_ref[...]
    s1 = s1_ref[...]
    s2 = s2_ref[...]
    kcur = _swa_rope(kv_ref[:, :LANES], c, s1, s2)
    vcur = kv_ref[:, LANES:]
    k2 = jnp.concatenate([kprev[...], kcur], axis=0)
    v2 = jnp.concatenate([vprev[...], vcur], axis=0)

    r = lax.broadcasted_iota(jnp.int32, (W, 2 * W), 0)
    cc = lax.broadcasted_iota(jnp.int32, (W, 2 * W), 1)
    diff = W + r - cc
    mask = (diff >= 0) & (diff < W) & ((m_id > 0) | (cc >= W - N_META))
    low_k = lax.broadcasted_iota(jnp.int32, (2 * W, LANES), 1) < SWA_HD
    low = lax.broadcasted_iota(jnp.int32, (W, LANES), 1) < SWA_HD
    scale = SWA_HD ** -0.5

    def attend(q, kd, vd, h):
        p, den = _sink_softmax(jnp.where(mask, _dot_nt(q.astype(BF16), kd), NEG_INF), sink_ref[h])
        return _dot(p.astype(BF16), vd) / den

    for g in range(SWA_KV_HEADS):
        kd = _dup_head(k2, g, low_k).astype(BF16)
        vd = _dup_head(v2, g, low_k).astype(BF16)
        for p in range(g * SWA_GROUP // 2, (g + 1) * SWA_GROUP // 2):
            slab = _swa_rope(q_ref[:, p * LANES:(p + 1) * LANES], c, s1, s2) * scale
            o_even = attend(jnp.where(low, slab, 0.0), kd, vd, 2 * p)
            o_odd = attend(jnp.where(low, 0.0, slab), kd, vd, 2 * p + 1)
            o_ref[:, p * LANES:(p + 1) * LANES] = jnp.where(low, o_even, o_odd).astype(BF16)

    kprev[...] = kcur
    vprev[...] = vcur

    @pl.when(m_id == pl.num_programs(1) - 1)
    def _():
        kp_ref[...] = kcur.T
        vp_ref[...] = vcur.T


def _swa_prompt_ret_sample_kernel(sink_ref, q_ref, kv_ref, kvm_ref, c_ref, s1_ref, s2_ref, cm_ref, s1m_ref, s2m_ref,
                                 lg_ref, zs_ref, st_ref, cos_ref, sin_ref,
                                 o_ref, kp_ref, vp_ref, os_ref, sto_ref, kprev, vprev, *, nb, t, every):
    _swa_prompt_kernel(sink_ref, q_ref, kv_ref, kvm_ref, c_ref, s1_ref, s2_ref, cm_ref, s1m_ref, s2m_ref,
                       o_ref, kp_ref, vp_ref, kprev, vprev)

    @pl.when(lax.rem(pl.program_id(1), every) == every - 1)
    def _():
        state_jobs, finish = _ret_sample_jobs(lg_ref, zs_ref, st_ref, cos_ref, sin_ref, os_ref, sto_ref, nb=nb, t=t)
        for job in state_jobs + [finish]:
            job()


def _swa_prompt_ret_sample(sinks, z, zm, bsz, seq, c, s1, s2, lg, sample_row, state, cos, sin, t):
    W = WINDOW
    nblk = seq // W
    every = RET_SAMPLE_EVERY
    ret_steps = bsz * nblk // every
    dbs = state.shape[0]
    nb = dbs // ret_steps
    assert nblk % every == 0 and nb * ret_steps == dbs and (nb * t) % 16 == 0
    rows = nb * t
    qw = SWA_HEADS * SWA_HD
    kv_col = SWA_KV_COL // (2 * LANES)
    tab = pl.BlockSpec((W, LANES), lambda b, m: (m, 0))
    mtab = pl.BlockSpec((W, LANES), lambda b, m: (seq // W, 0))
    stab = pl.BlockSpec((rows, LANES), lambda b, m: ((seq + RET_CHUNK) // rows, 0))
    ret_step = lambda b, m: (b * nblk + m) // every
    state_spec = pl.BlockSpec((nb, RET_HEADS, RET_DK, RET_DV), lambda b, m: (ret_step(b, m), 0, 0, 0))
    return pl.pallas_call(
        functools.partial(_swa_prompt_ret_sample_kernel, nb=nb, t=t, every=every),
        grid=(bsz, nblk),
        in_specs=[
            pl.BlockSpec(memory_space=pltpu.SMEM),
            pl.BlockSpec((W, qw), lambda b, m: (b * nblk + m, SWA_Q_COL // qw)),
            pl.BlockSpec((W, 2 * LANES), lambda b, m: (b * nblk + m, kv_col)),
            pl.BlockSpec((W, 2 * LANES), lambda b, m: (0, kv_col)),
            tab, tab, tab, mtab, mtab, mtab,
            pl.BlockSpec(memory_space=pltpu.SMEM),
            pl.BlockSpec((rows, 4 * RET_W), lambda b, m: (sample_row // rows + ret_step(b, m), 0)),
            state_spec, stab, stab,
        ],
        out_specs=[
            pl.BlockSpec((W, qw), lambda b, m: (b * nblk + m, 0)),
            pl.BlockSpec((None, W, LANES), lambda b, m: (b, 0, 0)),
            pl.BlockSpec((None, W, LANES), lambda b, m: (b, 0, 0)),
            pl.BlockSpec((rows, RET_W), lambda b, m: (ret_step(b, m), 0)),
            state_spec,
        ],
        out_shape=[
            jax.ShapeDtypeStruct((bsz * seq, qw), BF16),
            jax.ShapeDtypeStruct((bsz, W, LANES), F32),
            jax.ShapeDtypeStruct((bsz, W, LANES), F32),
            jax.ShapeDtypeStruct((dbs * t, RET_W), BF16),
            jax.ShapeDtypeStruct(state.shape, F32),
        ],
        scratch_shapes=[pltpu.VMEM((W, LANES), F32), pltpu.VMEM((W, LANES), F32)],
        compiler_params=_cparams(("parallel", "arbitrary")),
        name="swa_prompt_ret_sample",
    )(sinks, z, z, zm, c, s1, s2, c, s1, s2, lg, z, state, cos, sin)


def _swa_sample_kernel(sink_ref, q_ref, kv_ref, kc_ref, vc_ref, c_ref, s1_ref, s2_ref,
                       o_ref, ko_ref, vo_ref, *, nb, t):
    W = WINDOW
    c = c_ref[...]
    s1 = s1_ref[...]
    s2 = s2_ref[...]
    row_pad = jnp.zeros((LANES - nb * t, LANES), F32)
    knew_t = jnp.concatenate([_swa_rope(kv_ref[:, :LANES], c, s1, s2), row_pad], axis=0).T
    vnew_t = jnp.concatenate([kv_ref[:, LANES:], row_pad], axis=0).T
    knew_b = knew_t.astype(BF16)
    vnew_b = vnew_t.astype(BF16)

    rq = SWA_HEADS * t
    tq = lax.broadcasted_iota(jnp.int32, (rq, 2 * W), 0) & (t - 1)
    cc = lax.broadcasted_iota(jnp.int32, (rq, 2 * W), 1)
    out_lane = lax.broadcasted_iota(jnp.int32, (LANES, LANES), 1)
    low = lax.broadcasted_iota(jnp.int32, (nb * t, LANES), 1) < SWA_HD
    low_t = lax.broadcasted_iota(jnp.int32, (t, LANES), 1) < SWA_HD
    sink_col = jnp.concatenate([jnp.full((t, 1), sink_ref[h], F32) for h in range(SWA_HEADS)], axis=0)
    npair = SWA_HEADS // 2

    scale = SWA_HD ** -0.5
    slabs = [_swa_rope(q_ref[:, p * LANES:(p + 1) * LANES], c, s1, s2) * scale for p in range(npair)]
    q_heads = [_to_kv_half(slabs[h // 2], h, low) for h in range(SWA_HEADS)]

    logits, values = [], []
    for db in range(nb):
        rows = slice(db * t, (db + 1) * t)
        kc = kc_ref[db]
        vc = vc_ref[db]
        keep = out_lane < W - t
        ko_ref[db] = jnp.where(keep, pltpu.roll(kc, W - t, 1), pltpu.roll(knew_t, (W - t - db * t) % LANES, 1))
        vo_ref[db] = jnp.where(keep, pltpu.roll(vc, W - t, 1), pltpu.roll(vnew_t, (W - t - db * t) % LANES, 1))
        k2 = jnp.concatenate([kc.astype(BF16), knew_b], axis=1)
        values.append(jnp.concatenate([vc.astype(BF16), vnew_b], axis=1))
        t_new = cc - W - db * t
        mask = ((cc < W) & (cc > tq)) | ((t_new >= 0) & (t_new <= tq))
        q_db = jnp.concatenate([qh[rows, :] for qh in q_heads], axis=0).astype(BF16)
        logits.append(jnp.where(mask, _dot(q_db, k2), NEG_INF))
    p, den = _sink_softmax(jnp.concatenate(logits, axis=0), jnp.concatenate([sink_col] * nb, axis=0))
    p = p.astype(BF16)
    outs = [[] for _ in range(npair)]
    for db in range(nb):
        o = _dot_nt(p[db * rq:(db + 1) * rq, :], values[db]) / den[db * rq:(db + 1) * rq, :]
        for pr in range(npair):
            o_even = o[(2 * pr) * t:(2 * pr + 1) * t, :]
            o_odd = o[(2 * pr + 1) * t:(2 * pr + 2) * t, :]
            outs[pr].append(_from_kv_half(o_even, o_odd, 2 * pr // SWA_GROUP, low_t))
    for pr in range(npair):
        o_ref[:, pr * LANES:(pr + 1) * LANES] = jnp.concatenate(outs[pr], axis=0).astype(BF16)


def _swa_sample(sinks, z, row0, kc, vc, c, s1, s2, tab_row0, t):
    dbs = kc.shape[0]
    n = dbs * t
    rows = SAMPLE_TABLE_ROWS
    nb = rows // t
    W = WINDOW
    qw = SWA_HEADS * SWA_HD
    tab = pl.BlockSpec((rows, LANES), lambda i: (tab_row0 // rows, 0))
    cache = pl.BlockSpec((nb, W, LANES), lambda i: (i, 0, 0))
    return pl.pallas_call(
        functools.partial(_swa_sample_kernel, nb=nb, t=t),
        grid=(dbs // nb,),
        in_specs=[
            pl.BlockSpec(memory_space=pltpu.SMEM),
            pl.BlockSpec((rows, qw), lambda i: (row0 // rows + i, SWA_Q_COL // qw)),
            pl.BlockSpec((rows, 2 * LANES), lambda i: (row0 // rows + i, SWA_KV_COL // (2 * LANES))),
            cache, cache, tab, tab, tab,
        ],
        out_specs=[pl.BlockSpec((rows, qw), lambda i: (i, 0)), cache, cache],
        out_shape=[
            jax.ShapeDtypeStruct((n, qw), BF16),
            jax.ShapeDtypeStruct((dbs, W, LANES), F32),
            jax.ShapeDtypeStruct((dbs, W, LANES), F32),
        ],
        compiler_params=_cparams(("parallel",)),
        name="swa_sample",
    )(sinks, z, z, kc, vc, c, s1, s2)


def _out_proj_ln_kernel(ro_ref, so_ref, x_ref, eg_ref, eb_ref, w_hbm, mg_ref, mb_ref, h_ref, hb_ref,
                        wb_scr, stage, sem):
    chunk = stage.shape[1]
    n_chunks = w_hbm.shape[0] // chunk

    def weight_copy(k):
        return pltpu.make_async_copy(w_hbm.at[pl.ds(k * chunk, chunk), :], stage.at[k % 2], sem.at[k % 2])

    @pl.when(pl.program_id(0) == 0)
    def _():
        weight_copy(0).start()
        for k in range(n_chunks):
            if k + 1 < n_chunks:
                weight_copy(k + 1).start()
            weight_copy(k).wait()
            wb_scr[k * chunk:(k + 1) * chunk, :] = stage[k % 2].astype(BF16)

    sub = h_ref.shape[0] // 2
    for r in range(2):
        rows = slice(r * sub, (r + 1) * sub)
        x_in = _layer_norm(x_ref[rows, :], eg_ref[...], eb_ref[...])
        mixed = _dot(ro_ref[rows, :], wb_scr[:RET_W, :]) + _dot(so_ref[rows, :], wb_scr[RET_W:, :])
        h = _layer_norm(ALPHA * x_in + mixed, mg_ref[...], mb_ref[...])
        h_ref[rows, :] = h
        hb_ref[rows, :] = h.astype(BF16)


def _out_proj_ln(ro, so, x, eg, eb, w, mg, mb, tm):
    n = x.shape[0]
    vec = pl.BlockSpec((1, D_MODEL), lambda i: (0, 0))
    tile = pl.BlockSpec((tm, D_MODEL), lambda i: (i, 0))
    return pl.pallas_call(
        _out_proj_ln_kernel,
        grid=(n // tm,),
        in_specs=[
            pl.BlockSpec((tm, RET_W), lambda i: (i, 0)),
            pl.BlockSpec((tm, RET_W), lambda i: (i, 0)),
            tile, vec, vec,
            pl.BlockSpec(memory_space=pl.ANY),
            vec, vec,
        ],
        out_specs=[tile, tile],
        out_shape=[jax.ShapeDtypeStruct((n, D_MODEL), F32), jax.ShapeDtypeStruct((n, D_MODEL), BF16)],
        scratch_shapes=[pltpu.VMEM((D_MODEL, D_MODEL), BF16),
                        pltpu.VMEM((2, WEIGHT_STAGE_ROWS, D_MODEL), F32),
                        pltpu.SemaphoreType.DMA((2,))],
        compiler_params=_cparams(("arbitrary",)),
        name="out_proj_ln",
    )(ro, so, x, eg, eb, w, mg, mb)


def _ffn_ln_kernel(hb_ref, h_hbm, wg_ref, wu_ref, wd_ref, g_ref, b_ref, y_ref, *rest):
    *w_out_refs, h_res, sem = rest
    i = pl.program_id(0)
    j = pl.program_id(1)
    tm = h_res.shape[0]
    residual_copy = pltpu.make_async_copy(h_hbm.at[pl.ds(i * tm, tm), :], h_res, sem)

    @pl.when(j == 0)
    def _():
        residual_copy.start()
        y_ref[...] = jnp.zeros_like(y_ref)

    wg, wu, wd = (w[...].astype(BF16) for w in (wg_ref, wu_ref, wd_ref))
    for out_ref, w in zip(w_out_refs, (wg, wu, wd)):
        out_ref[...] = w
    hb = hb_ref[...]
    act = _silu(_dot(hb, wg)) * _dot(hb, wu)
    y_ref[...] += _dot(act.astype(BF16), wd)

    @pl.when(j == pl.num_programs(1) - 1)
    def _():
        residual_copy.wait()
        y_ref[...] = _layer_norm(ALPHA * h_res[...] + y_ref[...], g_ref[...], b_ref[...])


def _ffn_ln(h, hb, wg, wu, wd, g, b, tm, th):
    n = h.shape[0]
    emit_weights = wg.dtype == F32
    assert not emit_weights or n == tm
    vec = pl.BlockSpec((1, D_MODEL), lambda i, j: (0, 0))
    w_specs = [
        pl.BlockSpec((D_MODEL, th), lambda i, j: (0, j)),
        pl.BlockSpec((D_MODEL, th), lambda i, j: (0, j)),
        pl.BlockSpec((th, D_MODEL), lambda i, j: (j, 0)),
    ]
    out_specs = [pl.BlockSpec((tm, D_MODEL), lambda i, j: (i, 0))]
    out_shape = [jax.ShapeDtypeStruct((n, D_MODEL), F32)]
    if emit_weights:
        out_specs += w_specs
        out_shape += [jax.ShapeDtypeStruct(w.shape, BF16) for w in (wg, wu, wd)]
    outs = pl.pallas_call(
        _ffn_ln_kernel,
        grid=(n // tm, FFN_HIDDEN // th),
        in_specs=[pl.BlockSpec((tm, D_MODEL), lambda i, j: (i, 0)), pl.BlockSpec(memory_space=pl.ANY)] + w_specs
        + [vec, vec],
        out_specs=out_specs,
        out_shape=out_shape,
        scratch_shapes=[pltpu.VMEM((tm, D_MODEL), F32), pltpu.SemaphoreType.DMA(())],
        compiler_params=pltpu.CompilerParams(dimension_semantics=("arbitrary", "arbitrary"),
                                             vmem_limit_bytes=VMEM_LIMIT_LARGE),
        name="ffn_ln",
    )(hb, h, wg, wu, wd, g, b)
    return tuple(outs) if emit_weights else outs[0]


def _position_tables(seq, t, past_len):
    r = jnp.arange(seq + RET_CHUNK + SAMPLE_TABLE_ROWS)
    meta_r = r - seq
    pos = jnp.where(r < seq, N_META + r,
                    jnp.where(meta_r < RET_CHUNK, jnp.maximum(meta_r - (RET_CHUNK - N_META), 0),
                              past_len + (meta_r - RET_CHUNK) % t)).astype(F32)[:, None]
    ret_freq = jnp.power(RET_THETA, -jnp.linspace(0.0, 1.0, RET_DK // 2, dtype=F32))
    ret_ang = pos * ret_freq[None, :]
    half = ROT_DIM // 2
    swa_freq = jnp.power(ROPE_THETA, -jnp.arange(0, ROT_DIM, 2, dtype=F32) / ROT_DIM)
    swa_ang = pos * swa_freq[None, :]
    cos = jnp.tile(jnp.cos(swa_ang), (1, LANES // half))
    sin = jnp.tile(jnp.sin(swa_ang), (1, LANES // half))
    d = jnp.arange(LANES) % SWA_HD
    c = jnp.where(d < ROT_DIM, cos, 1.0)
    s1 = jnp.where((d >= half) & (d < ROT_DIM), sin, 0.0)
    s2 = jnp.where(d < half, -sin, 0.0)
    return (jnp.cos(ret_ang), jnp.sin(ret_ang)), (c, s1, s2)


def kernel(x_prompt, x_sample, state_ret, cache_swa_k, cache_swa_v, meta_tokens, ln_emb_g, ln_emb_b,
           w_in, w_out, swa_sinks, ln_mix_g, ln_mix_b, w_ffn_gate, w_ffn_up, w_ffn_down, ln_ffn_g, ln_ffn_b):
    bsz, seq, d = x_prompt.shape
    dbs, t, _ = x_sample.shape
    assert w_in.shape[0] == DEPTH and d == D_MODEL and seq % RET_CHUNK == 0 and t & (t - 1) == 0
    past_len = 16384
    row = lambda a: a.reshape(1, -1)

    eg, eb = row(ln_emb_g), row(ln_emb_b)
    sinks = swa_sinks[0]
    lg = jnp.log(1.0 - jnp.power(2.0, -5.0 - jnp.arange(RET_HEADS, dtype=F32)))

    xp = x_prompt.reshape(bsz * seq, d)
    xs = x_sample.reshape(dbs * t, d)

    z, zm = _ln_proj(xp, xs, meta_tokens, eg, eb, w_in[0], 256, PROJ_W)
    sample_row = bsz * seq

    (cos, sin), swa_tabs = _position_tables(seq, t, past_len)
    ret_o_p, ret_state_p = _ret_prompt(lg, z, zm, bsz, seq, cos, sin)
    swa_o_p, k_p, v_p, ret_o_s, ret_state_s = _swa_prompt_ret_sample(
        sinks, z, zm, bsz, seq, *swa_tabs, lg, sample_row, state_ret[0], cos, sin, t)
    to_kernel = lambda a: jnp.transpose(a[0], (0, 2, 3, 1)).reshape(-1, LANES, WINDOW)
    from_kernel = lambda a: jnp.transpose(a.reshape(-1, SWA_KV_HEADS, SWA_HD, WINDOW), (0, 3, 1, 2))[None]
    kc, vc = to_kernel(cache_swa_k), to_kernel(cache_swa_v)
    swa_o_s, k_s, v_s = _swa_sample(sinks, z, sample_row, kc, vc, *swa_tabs, seq + RET_CHUNK, t)

    mg, mb = row(ln_mix_g[0]), row(ln_mix_b[0])
    fg, fb = row(ln_ffn_g[0]), row(ln_ffn_b[0])
    ffn_w = (w_ffn_gate[0], w_ffn_up[0], w_ffn_down[0])
    h_p, hb_p = _out_proj_ln(ret_o_p, swa_o_p, xp, eg, eb, w_out[0], mg, mb, 512)
    h_s, hb_s = _out_proj_ln(ret_o_s, swa_o_s, xs, eg, eb, w_out[0], mg, mb, 512)
    y_s, *ffn_wb = _ffn_ln(h_s, hb_s, *ffn_w, fg, fb, TOK_TILE, 256)
    y_p = _ffn_ln(h_p, hb_p, *ffn_wb, fg, fb, TOK_TILE, 512)

    return (y_p.reshape(bsz, seq, d), y_s.reshape(dbs, t, d),
            ret_state_p[None], from_kernel(k_p), from_kernel(v_p),
            ret_state_s[None], from_kernel(k_s), from_kernel(v_s))
```

```python
import functools

import jax
import jax.numpy as jnp
from jax import lax
from jax.experimental import pallas as pl
from jax.experimental.pallas import tpu as pltpu

F32 = jnp.float32
BF16 = jnp.bfloat16

D_MODEL = 2048
N_META = 16
RET_HEADS = 4
RET_DK = 256
RET_DV = 256
RET_CHUNK = 128
RET_THETA = 10000.0
SWA_HD = 64
SWA_HEADS = 16
SWA_KV_HEADS = 2
SWA_GROUP = SWA_HEADS // SWA_KV_HEADS
WINDOW = 128
ROPE_THETA = 500000.0
ROT_DIM = SWA_HD // 4
FFN_HIDDEN = 5632
PROJ_W = 5376
DEPTH = 1
ALPHA = (2.0 * DEPTH) ** 0.25
LN_EPS = 1e-5
NEG_INF = -1e30

LANES = 128
RET_W = RET_HEADS * RET_DK
SWA_Q_COL = 4 * RET_W
SWA_KV_COL = SWA_Q_COL + SWA_HEADS * SWA_HD
VMEM_LIMIT = 56 * 1024 * 1024
VMEM_LIMIT_LARGE = 60 * 1024 * 1024
TOK_TILE = 1024
LN_PROJ_SUBBLOCKS = 2
RET_CHUNK_UNROLL = 16
RET_SAMPLE_EVERY = 2
SAMPLE_TABLE_ROWS = 64
WEIGHT_STAGE_ROWS = 256


def _cparams(sem):
    return pltpu.CompilerParams(dimension_semantics=sem, vmem_limit_bytes=VMEM_LIMIT)


def _layer_norm(x, g, b):
    mu = jnp.mean(x, axis=-1, keepdims=True)
    xc = x - mu
    var = jnp.mean(xc * xc, axis=-1, keepdims=True)
    return xc * lax.rsqrt(var + LN_EPS) * g + b


def _silu(x):
    return x / (1.0 + jnp.exp(-x))


def _dot(a, b):
    return jnp.dot(a, b, preferred_element_type=F32)


def _dot_nt(a, b):
    return lax.dot_general(a, b, (((1,), (1,)), ((), ())), preferred_element_type=F32)


def _dot_tn(a, b):
    return lax.dot_general(a, b, (((0,), (0,)), ((), ())), preferred_element_type=F32)


def _ln_proj_kernel(xp_ref, xs_ref, xm_ref, g_ref, b_ref, w_hbm, z_ref, zm_ref,
                    hm_scr, wb_scr, stage, sem, *, np_tiles, ns_tiles):
    i = pl.program_id(0)
    j = pl.program_id(1)
    first = j == 0
    is_sample = i >= np_tiles
    is_last = i == np_tiles + ns_tiles - 1
    n_col, _, tn = wb_scr.shape
    chunk = stage.shape[1]
    n_chunks = w_hbm.shape[0] // chunk

    def weight_copy(k):
        return pltpu.make_async_copy(w_hbm.at[pl.ds(k * chunk, chunk), :], stage.at[k % 2], sem.at[k % 2])

    @pl.when(first & (i == 0))
    def _():
        weight_copy(0).start()
        for k in range(n_chunks):
            if k + 1 < n_chunks:
                weight_copy(k + 1).start()
            weight_copy(k).wait()
            for jt in range(n_col):
                wb_scr[jt, k * chunk:(k + 1) * chunk, :] = stage[k % 2, :, jt * tn:(jt + 1) * tn].astype(BF16)

    def norm(x):
        return _layer_norm(x, g_ref[...], b_ref[...]).astype(BF16)

    def project(x_ref):
        sub = x_ref.shape[0] // LN_PROJ_SUBBLOCKS
        for r in range(LN_PROJ_SUBBLOCKS):
            rows = slice(r * sub, (r + 1) * sub)
            z_ref[rows, :] = _dot(norm(x_ref[rows, :]), wb_scr[j])

    @pl.when(jnp.logical_not(is_sample))
    def _():
        project(xp_ref)

    @pl.when(is_sample)
    def _():
        project(xs_ref)

    @pl.when(is_last)
    def _():
        lead = hm_scr.shape[0] - xm_ref.shape[0]
        hm_scr[:lead, :] = jnp.zeros((lead, D_MODEL), BF16)
        hm_scr[lead:, :] = norm(xm_ref[...])
        zm_ref[...] = _dot(hm_scr[...], wb_scr[j])


def _ln_proj(xp, xs, xm, g, b, w, tm, tn):
    np_tiles = xp.shape[0] // tm
    ns_tiles = xs.shape[0] // tm
    last = np_tiles + ns_tiles - 1
    n_col = PROJ_W // tn
    assert n_col == 1
    return pl.pallas_call(
        functools.partial(_ln_proj_kernel, np_tiles=np_tiles, ns_tiles=ns_tiles),
        grid=(np_tiles + ns_tiles, n_col),
        in_specs=[
            pl.BlockSpec((tm, D_MODEL), lambda i, j: (jnp.minimum(i, np_tiles - 1), 0)),
            pl.BlockSpec((tm, D_MODEL), lambda i, j: (jnp.clip(i - np_tiles, 0, ns_tiles - 1), 0)),
            pl.BlockSpec(xm.shape, lambda i, j: (0, 0), pipeline_mode=pl.Buffered(1)),
            pl.BlockSpec((1, D_MODEL), lambda i, j: (0, 0)),
            pl.BlockSpec((1, D_MODEL), lambda i, j: (0, 0)),
            pl.BlockSpec(memory_space=pl.ANY),
        ],
        out_specs=[
            pl.BlockSpec((tm, tn), lambda i, j: (i, j)),
            pl.BlockSpec((RET_CHUNK, tn), lambda i, j: (0, jnp.where(i == last, j, 0))),
        ],
        out_shape=[
            jax.ShapeDtypeStruct((xp.shape[0] + xs.shape[0], PROJ_W), F32),
            jax.ShapeDtypeStruct((RET_CHUNK, PROJ_W), F32),
        ],
        scratch_shapes=[
            pltpu.VMEM((RET_CHUNK, D_MODEL), BF16),
            pltpu.VMEM((n_col, D_MODEL, tn), BF16),
            pltpu.VMEM((2, WEIGHT_STAGE_ROWS // 4, PROJ_W), F32),
            pltpu.SemaphoreType.DMA((2,)),
        ],
        compiler_params=pltpu.CompilerParams(dimension_semantics=("arbitrary", "arbitrary"),
                                             vmem_limit_bytes=VMEM_LIMIT_LARGE),
        name="ln_proj",
    )(xp, xs, xm, g, b, w)


def _ret_rope(x, cos, sin):
    x1 = x[:, :LANES]
    x2 = x[:, LANES:]
    return jnp.concatenate([x1 * cos - x2 * sin, x2 * cos + x1 * sin], axis=1)


def _group_norm_gate(o, gate):
    mu = jnp.mean(o, axis=-1, keepdims=True)
    oc = o - mu
    var = jnp.mean(oc * oc, axis=-1, keepdims=True)
    return oc * lax.rsqrt(var + LN_EPS) * _silu(gate)


def _ret_prompt_kernel(lg_ref, q_ref, k_ref, v_ref, g_ref, km_ref, vm_ref,
                       cos_ref, sin_ref, cosm_ref, sinm_ref, o_ref, s_ref, s_scr):
    C = RET_CHUNK
    lg = lg_ref[pl.program_id(1)]
    ri = lax.broadcasted_iota(jnp.int32, (C, C), 0)
    ci = lax.broadcasted_iota(jnp.int32, (C, C), 1)
    rel = (ri - ci).astype(F32)
    decay = jnp.where(rel >= 0.0, jnp.exp(jnp.maximum(rel, 0.0) * lg), 0.0)
    row = lax.broadcasted_iota(jnp.int32, (C, 1), 0).astype(F32)
    q_decay = jnp.exp((row + 1.0) * lg)
    k_decay = jnp.exp((C - 1.0 - row) * lg)
    chunk_decay = jnp.exp(jnp.full((1, RET_DV), C * lg, F32))
    scale = RET_DK ** -0.5

    meta_decay = jnp.where(row >= C - N_META, k_decay, 0.0)
    km = _ret_rope(km_ref[...], cosm_ref[...], sinm_ref[...]) * scale
    s_scr[...] = _dot_tn((km * meta_decay).astype(BF16), vm_ref[...].astype(BF16))

    def chunk(c, carry):
        rows = pl.ds(pl.multiple_of(c * C, C), C)
        cos = cos_ref[rows, :]
        sin = sin_ref[rows, :]
        q = _ret_rope(q_ref[rows, :], cos, sin)
        k = _ret_rope(k_ref[rows, :], cos, sin) * scale
        qb = q.astype(BF16)
        vb = v_ref[rows, :].astype(BF16)
        s_prev = s_scr[...]
        scores = _dot_nt(qb, k.astype(BF16)) * decay
        inner = _dot(scores.astype(BF16), vb)
        cross = _dot(qb, s_prev.astype(BF16)) * q_decay
        s_scr[...] = chunk_decay * s_prev + _dot_tn((k * k_decay).astype(BF16), vb)
        o_ref[rows, :] = _group_norm_gate(inner + cross, g_ref[rows, :]).astype(BF16)
        return carry

    lax.fori_loop(0, q_ref.shape[0] // C, chunk, 0, unroll=RET_CHUNK_UNROLL)
    s_ref[...] = s_scr[...]


def _ret_prompt(lg, z, zm, bsz, seq, cos, sin):
    col = lambda base: (lambda b, h: (b, base + h))
    mcol = lambda base: (lambda b, h: (0, base + h))
    full = lambda b, h: (0, 0)
    meta_tab = lambda b, h: (seq // RET_CHUNK, 0)
    return pl.pallas_call(
        _ret_prompt_kernel,
        grid=(bsz, RET_HEADS),
        in_specs=[
            pl.BlockSpec(memory_space=pltpu.SMEM),
            pl.BlockSpec((seq, RET_DK), col(0)),
            pl.BlockSpec((seq, RET_DK), col(RET_HEADS)),
            pl.BlockSpec((seq, RET_DV), col(2 * RET_HEADS)),
            pl.BlockSpec((seq, RET_DV), col(3 * RET_HEADS)),
            pl.BlockSpec((RET_CHUNK, RET_DK), mcol(RET_HEADS)),
            pl.BlockSpec((RET_CHUNK, RET_DV), mcol(2 * RET_HEADS)),
            pl.BlockSpec((seq, LANES), full),
            pl.BlockSpec((seq, LANES), full),
            pl.BlockSpec((RET_CHUNK, LANES), meta_tab),
            pl.BlockSpec((RET_CHUNK, LANES), meta_tab),
        ],
        out_specs=[
            pl.BlockSpec((seq, RET_DV), lambda b, h: (b, h)),
            pl.BlockSpec((None, None, RET_DK, RET_DV), lambda b, h: (b, h, 0, 0)),
        ],
        out_shape=[
            jax.ShapeDtypeStruct((bsz * seq, RET_W), BF16),
            jax.ShapeDtypeStruct((bsz, RET_HEADS, RET_DK, RET_DV), F32),
        ],
        scratch_shapes=[pltpu.VMEM((RET_DK, RET_DV), F32)],
        compiler_params=_cparams(("parallel", "parallel")),
        name="ret_prompt",
    )(lg, z, z, z, z, zm, zm, cos, sin, cos, sin)


def _ret_sample_jobs(lg_ref, z_ref, s_ref, cos_ref, sin_ref, o_ref, so_ref, *, nb, t):
    rows = nb * t
    R = RET_HEADS * rows
    scale = RET_DK ** -0.5
    cos = cos_ref[...]
    sin = sin_ref[...]

    def stack(base, rope, mul=1.0):
        parts = []
        for h in range(RET_HEADS):
            x = z_ref[:, base + h * RET_DK: base + (h + 1) * RET_DK]
            parts.append(_ret_rope(x, cos, sin) * mul if rope else x)
        return jnp.concatenate(parts, axis=0)

    q = stack(0, True)
    k = stack(RET_W, True, scale)
    v = stack(2 * RET_W, False)
    gate = stack(3 * RET_W, False)

    ri = lax.broadcasted_iota(jnp.int32, (R, R), 0)
    ci = lax.broadcasted_iota(jnp.int32, (R, R), 1)
    rcol = lax.broadcasted_iota(jnp.int32, (R, 1), 0)
    lg_col = jnp.zeros((R, 1), F32)
    for h in range(RET_HEADS):
        lg_col = jnp.where((rcol >= h * rows) & (rcol < (h + 1) * rows), lg_ref[h], lg_col)
    tcol = (rcol & (t - 1)).astype(F32)
    rel = (ri - ci).astype(F32)
    same = ((ri & -t) == (ci & -t)) & (ri >= ci)
    decay = jnp.where(same, jnp.exp(jnp.maximum(rel, 0.0) * lg_col), 0.0)
    q_decay = jnp.exp((tcol + 1.0) * lg_col)
    k_decay = jnp.exp((t - 1.0 - tcol) * lg_col)

    qb = q.astype(BF16)
    vb = v.astype(BF16)
    scores = _dot_nt(qb, k.astype(BF16)) * decay
    inner = _dot(scores.astype(BF16), vb)
    kw = k * k_decay

    cross_parts = [None] * (RET_HEADS * nb)

    def state_job(h, db):
        def run():
            r0 = h * rows + db * t
            s_prev = s_ref[db, h]
            cross_parts[h * nb + db] = _dot(q[r0:r0 + t, :].astype(BF16), s_prev.astype(BF16))
            mine = (rcol >= r0) & (rcol < r0 + t)
            upd = _dot_tn(jnp.where(mine, kw, 0.0).astype(BF16), vb)
            step_decay = jnp.exp(jnp.full((1, RET_DV), t * lg_ref[h], F32))
            so_ref[db, h] = step_decay * s_prev + upd
        return run

    def finish():
        cross = jnp.concatenate(cross_parts, axis=0) * q_decay
        out = _group_norm_gate(inner + cross, gate).astype(BF16)
        for h in range(RET_HEADS):
            o_ref[:, h * RET_DV:(h + 1) * RET_DV] = out[h * rows:(h + 1) * rows, :]

    return [state_job(h, db) for h in range(RET_HEADS) for db in range(nb)], finish


def _swa_rope(x, c, s1, s2):
    return x * c + pltpu.roll(x, 8, 1) * s1 + pltpu.roll(x, LANES - 8, 1) * s2


def _dup_head(x, g, low):
    swapped = pltpu.roll(x, SWA_HD, 1)
    return jnp.where(low, x, swapped) if g == 0 else jnp.where(low, swapped, x)


def _to_kv_half(slab, head, low):
    g = head // SWA_GROUP
    src = slab if head % 2 == g else pltpu.roll(slab, SWA_HD, 1)
    return jnp.where(low, src, 0.0) if g == 0 else jnp.where(low, 0.0, src)


def _from_kv_half(o_even, o_odd, g, low):
    if g == 0:
        return jnp.where(low, o_even, pltpu.roll(o_odd, SWA_HD, 1))
    return jnp.where(low, pltpu.roll(o_even, SWA_HD, 1), o_odd)


def _sink_softmax(logits, sink):
    m = jnp.maximum(jnp.max(logits, axis=-1, keepdims=True), sink)
    p = jnp.exp(logits - m)
    return p, jnp.sum(p, axis=-1, keepdims=True) + jnp.exp(sink - m)


def _swa_prompt_kernel(sink_ref, q_ref, kv_ref, kvm_ref, c_ref, s1_ref, s2_ref,
                       cm_ref, s1m_ref, s2m_ref, o_ref, kp_ref, vp_ref, kprev, vprev):
    W = WINDOW
    m_id = pl.program_id(1)

    @pl.when(m_id == 0)
    def _():
        kprev[...] = _swa_rope(kvm_ref[:, :LANES], cm_ref[...], s1m_ref[...], s2m_ref[...])
        vprev[...] = kvm_ref[:, LANES:]

    c = c_ref[...]
    s1 = s1_ref[...]
    s2 = s2_ref[...]
    kcur = _swa_rope(kv_ref[:, :LANES], c, s1, s2)
    vcur = kv_ref[:, LANES:]
    k2 = jnp.concatenate([kprev[...], kcur], axis=0)
    v2 = jnp.concatenate([vprev[...], vcur], axis=0)

    r = lax.broadcasted_iota(jnp.int32, (W, 2 * W), 0)
    cc = lax.broadcasted_iota(jnp.int32, (W, 2 * W), 1)
    diff = W + r - cc
    mask = (diff >= 0) & (diff < W) & ((m_id > 0) | (cc >= W - N_META))
    low_k = lax.broadcasted_iota(jnp.int32, (2 * W, LANES), 1) < SWA_HD
    low = lax.broadcasted_iota(jnp.int32, (W, LANES), 1) < SWA_HD
    scale = SWA_HD ** -0.5

    def attend(q, kd, vd, h):
        p, den = _sink_softmax(jnp.where(mask, _dot_nt(q.astype(BF16), kd), NEG_INF), sink_ref[h])
        return _dot(p.astype(BF16), vd) / den

    for g in range(SWA_KV_HEADS):
        kd = _dup_head(k2, g, low_k).astype(BF16)
        vd = _dup_head(v2, g, low_k).astype(BF16)
        for p in range(g * SWA_GROUP // 2, (g + 1) * SWA_GROUP // 2):
            slab = _swa_rope(q_ref[:, p * LANES:(p + 1) * LANES], c, s1, s2) * scale
            o_even = attend(jnp.where(low, slab, 0.0), kd, vd, 2 * p)
            o_odd = attend(jnp.where(low, 0.0, slab), kd, vd, 2 * p + 1)
            o_ref[:, p * LANES:(p + 1) * LANES] = jnp.where(low, o_even, o_odd).astype(BF16)

    kprev[...] = kcur
    vprev[...] = vcur

    @pl.when(m_id == pl.num_programs(1) - 1)
    def _():
        kp_ref[...] = kcur.T
        vp_ref[...] = vcur.T


def _swa_prompt_ret_sample_kernel(sink_ref, q_ref, kv_ref, kvm_ref, c_ref, s1_ref, s2_ref, cm_ref, s1m_ref, s2m_ref,
                                 lg_ref, zs_ref, st_ref, cos_ref, sin_ref,
                                 o_ref, kp_ref, vp_ref, os_ref, sto_ref, kprev, vprev, *, nb, t, every):
    _swa_prompt_kernel(sink_ref, q_ref, kv_ref, kvm_ref, c_ref, s1_ref, s2_ref, cm_ref, s1m_ref, s2m_ref,
                       o_ref, kp_ref, vp_ref, kprev, vprev)

    @pl.when(lax.rem(pl.program_id(1), every) == every - 1)
    def _():
        state_jobs, finish = _ret_sample_jobs(lg_ref, zs_ref, st_ref, cos_ref, sin_ref, os_ref, sto_ref, nb=nb, t=t)
        for job in state_jobs + [finish]:
            job()


def _swa_prompt_ret_sample(sinks, z, zm, bsz, seq, c, s1, s2, lg, sample_row, state, cos, sin, t):
    W = WINDOW
    nblk = seq // W
    every = RET_SAMPLE_EVERY
    ret_steps = bsz * nblk // every
    dbs = state.shape[0]
    nb = dbs // ret_steps
    assert nblk % every == 0 and nb * ret_steps == dbs and (nb * t) % 16 == 0
    rows = nb * t
    qw = SWA_HEADS * SWA_HD
    kv_col = SWA_KV_COL // (2 * LANES)
    tab = pl.BlockSpec((W, LANES), lambda b, m: (m, 0))
    mtab = pl.BlockSpec((W, LANES), lambda b, m: (seq // W, 0))
    stab = pl.BlockSpec((rows, LANES), lambda b, m: ((seq + RET_CHUNK) // rows, 0))
    ret_step = lambda b, m: (b * nblk + m) // every
    state_spec = pl.BlockSpec((nb, RET_HEADS, RET_DK, RET_DV), lambda b, m: (ret_step(b, m), 0, 0, 0))
    return pl.pallas_call(
        functools.partial(_swa_prompt_ret_sample_kernel, nb=nb, t=t, every=every),
        grid=(bsz, nblk),
        in_specs=[
            pl.BlockSpec(memory_space=pltpu.SMEM),
            pl.BlockSpec((W, qw), lambda b, m: (b * nblk + m, SWA_Q_COL // qw)),
            pl.BlockSpec((W, 2 * LANES), lambda b, m: (b * nblk + m, kv_col)),
            pl.BlockSpec((W, 2 * LANES), lambda b, m: (0, kv_col)),
            tab, tab, tab, mtab, mtab, mtab,
            pl.BlockSpec(memory_space=pltpu.SMEM),
            pl.BlockSpec((rows, 4 * RET_W), lambda b, m: (sample_row // rows + ret_step(b, m), 0)),
            state_spec, stab, stab,
        ],
        out_specs=[
            pl.BlockSpec((W, qw), lambda b, m: (b * nblk + m, 0)),
            pl.BlockSpec((None, W, LANES), lambda b, m: (b, 0, 0)),
            pl.BlockSpec((None, W, LANES), lambda b, m: (b, 0, 0)),
            pl.BlockSpec((rows, RET_W), lambda b, m: (ret_step(b, m), 0)),
            state_spec,
        ],
        out_shape=[
            jax.ShapeDtypeStruct((bsz * seq, qw), BF16),
            jax.ShapeDtypeStruct((bsz, W, LANES), F32),
            jax.ShapeDtypeStruct((bsz, W, LANES), F32),
            jax.ShapeDtypeStruct((dbs * t, RET_W), BF16),
            jax.ShapeDtypeStruct(state.shape, F32),
        ],
        scratch_shapes=[pltpu.VMEM((W, LANES), F32), pltpu.VMEM((W, LANES), F32)],
        compiler_params=_cparams(("parallel", "arbitrary")),
        name="swa_prompt_ret_sample",
    )(sinks, z, z, zm, c, s1, s2, c, s1, s2, lg, z, state, cos, sin)


def _swa_sample_kernel(sink_ref, q_ref, kv_ref, kc_ref, vc_ref, c_ref, s1_ref, s2_ref,
                       o_ref, ko_ref, vo_ref, *, nb, t):
    W = WINDOW
    c = c_ref[...]
    s1 = s1_ref[...]
    s2 = s2_ref[...]
    row_pad = jnp.zeros((LANES - nb * t, LANES), F32)
    knew_t = jnp.concatenate([_swa_rope(kv_ref[:, :LANES], c, s1, s2), row_pad], axis=0).T
    vnew_t = jnp.concatenate([kv_ref[:, LANES:], row_pad], axis=0).T
    knew_b = knew_t.astype(BF16)
    vnew_b = vnew_t.astype(BF16)

    rq = SWA_HEADS * t
    tq = lax.broadcasted_iota(jnp.int32, (rq, 2 * W), 0) & (t - 1)
    cc = lax.broadcasted_iota(jnp.int32, (rq, 2 * W), 1)
    out_lane = lax.broadcasted_iota(jnp.int32, (LANES, LANES), 1)
    low = lax.broadcasted_iota(jnp.int32, (nb * t, LANES), 1) < SWA_HD
    low_t = lax.broadcasted_iota(jnp.int32, (t, LANES), 1) < SWA_HD
    sink_col = jnp.concatenate([jnp.full((t, 1), sink_ref[h], F32) for h in range(SWA_HEADS)], axis=0)
    npair = SWA_HEADS // 2

    scale = SWA_HD ** -0.5
    slabs = [_swa_rope(q_ref[:, p * LANES:(p + 1) * LANES], c, s1, s2) * scale for p in range(npair)]
    q_heads = [_to_kv_half(slabs[h // 2], h, low) for h in range(SWA_HEADS)]

    logits, values = [], []
    for db in range(nb):
        rows = slice(db * t, (db + 1) * t)
        kc = kc_ref[db]
        vc = vc_ref[db]
        keep = out_lane < W - t
        ko_ref[db] = jnp.where(keep, pltpu.roll(kc, W - t, 1), pltpu.roll(knew_t, (W - t - db * t) % LANES, 1))
        vo_ref[db] = jnp.where(keep, pltpu.roll(vc, W - t, 1), pltpu.roll(vnew_t, (W - t - db * t) % LANES, 1))
        k2 = jnp.concatenate([kc.astype(BF16), knew_b], axis=1)
        values.append(jnp.concatenate([vc.astype(BF16), vnew_b], axis=1))
        t_new = cc - W - db * t
        mask = ((cc < W) & (cc > tq)) | ((t_new >= 0) & (t_new <= tq))
        q_db = jnp.concatenate([qh[rows, :] for qh in q_heads], axis=0).astype(BF16)
        logits.append(jnp.where(mask, _dot(q_db, k2), NEG_INF))
    p, den = _sink_softmax(jnp.concatenate(logits, axis=0), jnp.concatenate([sink_col] * nb, axis=0))
    p = p.astype(BF16)
    outs = [[] for _ in range(npair)]
    for db in range(nb):
        o = _dot_nt(p[db * rq:(db + 1) * rq, :], values[db]) / den[db * rq:(db + 1) * rq, :]
        for pr in range(npair):
            o_even = o[(2 * pr) * t:(2 * pr + 1) * t, :]
            o_odd = o[(2 * pr + 1) * t:(2 * pr + 2) * t, :]
            outs[pr].append(_from_kv_half(o_even, o_odd, 2 * pr // SWA_GROUP, low_t))
    for pr in range(npair):
        o_ref[:, pr * LANES:(pr + 1) * LANES] = jnp.concatenate(outs[pr], axis=0).astype(BF16)


def _swa_sample(sinks, z, row0, kc, vc, c, s1, s2, tab_row0, t):
    dbs = kc.shape[0]
    n = dbs * t
    rows = SAMPLE_TABLE_ROWS
    nb = rows // t
    W = WINDOW
    qw = SWA_HEADS * SWA_HD
    tab = pl.BlockSpec((rows, LANES), lambda i: (tab_row0 // rows, 0))
    cache = pl.BlockSpec((nb, W, LANES), lambda i: (i, 0, 0))
    return pl.pallas_call(
        functools.partial(_swa_sample_kernel, nb=nb, t=t),
        grid=(dbs // nb,),
        in_specs=[
            pl.BlockSpec(memory_space=pltpu.SMEM),
            pl.BlockSpec((rows, qw), lambda i: (row0 // rows + i, SWA_Q_COL // qw)),
            pl.BlockSpec((rows, 2 * LANES), lambda i: (row0 // rows + i, SWA_KV_COL // (2 * LANES))),
            cache, cache, tab, tab, tab,
        ],
        out_specs=[pl.BlockSpec((rows, qw), lambda i: (i, 0)), cache, cache],
        out_shape=[
            jax.ShapeDtypeStruct((n, qw), BF16),
            jax.ShapeDtypeStruct((dbs, W, LANES), F32),
            jax.ShapeDtypeStruct((dbs, W, LANES), F32),
        ],
        compiler_params=_cparams(("parallel",)),
        name="swa_sample",
    )(sinks, z, z, kc, vc, c, s1, s2)


def _out_proj_ln_kernel(ro_ref, so_ref, x_ref, eg_ref, eb_ref, w_hbm, mg_ref, mb_ref, h_ref, hb_ref,
                        wb_scr, stage, sem):
    chunk = stage.shape[1]
    n_chunks = w_hbm.shape[0] // chunk

    def weight_copy(k):
        return pltpu.make_async_copy(w_hbm.at[pl.ds(k * chunk, chunk), :], stage.at[k % 2], sem.at[k % 2])

    @pl.when(pl.program_id(0) == 0)
    def _():
        weight_copy(0).start()
        for k in range(n_chunks):
            if k + 1 < n_chunks:
                weight_copy(k + 1).start()
            weight_copy(k).wait()
            wb_scr[k * chunk:(k + 1) * chunk, :] = stage[k % 2].astype(BF16)

    sub = h_ref.shape[0] // 2
    for r in range(2):
        rows = slice(r * sub, (r + 1) * sub)
        x_in = _layer_norm(x_ref[rows, :], eg_ref[...], eb_ref[...])
        mixed = _dot(ro_ref[rows, :], wb_scr[:RET_W, :]) + _dot(so_ref[rows, :], wb_scr[RET_W:, :])
        h = _layer_norm(ALPHA * x_in + mixed, mg_ref[...], mb_ref[...])
        h_ref[rows, :] = h
        hb_ref[rows, :] = h.astype(BF16)


def _out_proj_ln(ro, so, x, eg, eb, w, mg, mb, tm):
    n = x.shape[0]
    vec = pl.BlockSpec((1, D_MODEL), lambda i: (0, 0))
    tile = pl.BlockSpec((tm, D_MODEL), lambda i: (i, 0))
    return pl.pallas_call(
        _out_proj_ln_kernel,
        grid=(n // tm,),
        in_specs=[
            pl.BlockSpec((tm, RET_W), lambda i: (i, 0)),
            pl.BlockSpec((tm, RET_W), lambda i: (i, 0)),
            tile, vec, vec,
            pl.BlockSpec(memory_space=pl.ANY),
            vec, vec,
        ],
        out_specs=[tile, tile],
        out_shape=[jax.ShapeDtypeStruct((n, D_MODEL), F32), jax.ShapeDtypeStruct((n, D_MODEL), BF16)],
        scratch_shapes=[pltpu.VMEM((D_MODEL, D_MODEL), BF16),
                        pltpu.VMEM((2, WEIGHT_STAGE_ROWS, D_MODEL), F32),
                        pltpu.SemaphoreType.DMA((2,))],
        compiler_params=_cparams(("arbitrary",)),
        name="out_proj_ln",
    )(ro, so, x, eg, eb, w, mg, mb)


def _ffn_ln_kernel(hb_ref, h_hbm, wg_ref, wu_ref, wd_ref, g_ref, b_ref, y_ref, *rest):
    *w_out_refs, h_res, sem = rest
    i = pl.program_id(0)
    j = pl.program_id(1)
    tm = h_res.shape[0]
    residual_copy = pltpu.make_async_copy(h_hbm.at[pl.ds(i * tm, tm), :], h_res, sem)

    @pl.when(j == 0)
    def _():
        residual_copy.start()
        y_ref[...] = jnp.zeros_like(y_ref)

    wg, wu, wd = (w[...].astype(BF16) for w in (wg_ref, wu_ref, wd_ref))
    for out_ref, w in zip(w_out_refs, (wg, wu, wd)):
        out_ref[...] = w
    hb = hb_ref[...]
    act = _silu(_dot(hb, wg)) * _dot(hb, wu)
    y_ref[...] += _dot(act.astype(BF16), wd)

    @pl.when(j == pl.num_programs(1) - 1)
    def _():
        residual_copy.wait()
        y_ref[...] = _layer_norm(ALPHA * h_res[...] + y_ref[...], g_ref[...], b_ref[...])


def _ffn_ln(h, hb, wg, wu, wd, g, b, tm, th):
    n = h.shape[0]
    emit_weights = wg.dtype == F32
    assert not emit_weights or n == tm
    vec = pl.BlockSpec((1, D_MODEL), lambda i, j: (0, 0))
    w_specs = [
        pl.BlockSpec((D_MODEL, th), lambda i, j: (0, j)),
        pl.BlockSpec((D_MODEL, th), lambda i, j: (0, j)),
        pl.BlockSpec((th, D_MODEL), lambda i, j: (j, 0)),
    ]
    out_specs = [pl.BlockSpec((tm, D_MODEL), lambda i, j: (i, 0))]
    out_shape = [jax.ShapeDtypeStruct((n, D_MODEL), F32)]
    if emit_weights:
        out_specs += w_specs
        out_shape += [jax.ShapeDtypeStruct(w.shape, BF16) for w in (wg, wu, wd)]
    outs = pl.pallas_call(
        _ffn_ln_kernel,
        grid=(n // tm, FFN_HIDDEN // th),
        in_specs=[pl.BlockSpec((tm, D_MODEL), lambda i, j: (i, 0)), pl.BlockSpec(memory_space=pl.ANY)] + w_specs
        + [vec, vec],
        out_specs=out_specs,
        out_shape=out_shape,
        scratch_shapes=[pltpu.VMEM((tm, D_MODEL), F32), pltpu.SemaphoreType.DMA(())],
        compiler_params=pltpu.CompilerParams(dimension_semantics=("arbitrary", "arbitrary"),
                                             vmem_limit_bytes=VMEM_LIMIT_LARGE),
        name="ffn_ln",
    )(hb, h, wg, wu, wd, g, b)
    return tuple(outs) if emit_weights else outs[0]


def _position_tables(seq, t, past_len):
    r = jnp.arange(seq + RET_CHUNK + SAMPLE_TABLE_ROWS)
    meta_r = r - seq
    pos = jnp.where(r < seq, N_META + r,
                    jnp.where(meta_r < RET_CHUNK, jnp.maximum(meta_r - (RET_CHUNK - N_META), 0),
                              past_len + (meta_r - RET_CHUNK) % t)).astype(F32)[:, None]
    ret_freq = jnp.power(RET_THETA, -jnp.linspace(0.0, 1.0, RET_DK // 2, dtype=F32))
    ret_ang = pos * ret_freq[None, :]
    half = ROT_DIM // 2
    swa_freq = jnp.power(ROPE_THETA, -jnp.arange(0, ROT_DIM, 2, dtype=F32) / ROT_DIM)
    swa_ang = pos * swa_freq[None, :]
    cos = jnp.tile(jnp.cos(swa_ang), (1, LANES // half))
    sin = jnp.tile(jnp.sin(swa_ang), (1, LANES // half))
    d = jnp.arange(LANES) % SWA_HD
    c = jnp.where(d < ROT_DIM, cos, 1.0)
    s1 = jnp.where((d >= half) & (d < ROT_DIM), sin, 0.0)
    s2 = jnp.where(d < half, -sin, 0.0)
    return (jnp.cos(ret_ang), jnp.sin(ret_ang)), (c, s1, s2)


def kernel(x_prompt, x_sample, state_ret, cache_swa_k, cache_swa_v, meta_tokens, ln_emb_g, ln_emb_b,
           w_in, w_out, swa_sinks, ln_mix_g, ln_mix_b, w_ffn_gate, w_ffn_up, w_ffn_down, ln_ffn_g, ln_ffn_b):
    bsz, seq, d = x_prompt.shape
    dbs, t, _ = x_sample.shape
    assert w_in.shape[0] == DEPTH and d == D_MODEL and seq % RET_CHUNK == 0 and t & (t - 1) == 0
    past_len = 16384
    row = lambda a: a.reshape(1, -1)

    eg, eb = row(ln_emb_g), row(ln_emb_b)
    sinks = swa_sinks[0]
    lg = jnp.log(1.0 - jnp.power(2.0, -5.0 - jnp.arange(RET_HEADS, dtype=F32)))

    xp = x_prompt.reshape(bsz * seq, d)
    xs = x_sample.reshape(dbs * t, d)

    z, zm = _ln_proj(xp, xs, meta_tokens, eg, eb, w_in[0], 256, PROJ_W)
    sample_row = bsz * seq

    (cos, sin), swa_tabs = _position_tables(seq, t, past_len)
    ret_o_p, ret_state_p = _ret_prompt(lg, z, zm, bsz, seq, cos, sin)
    swa_o_p, k_p, v_p, ret_o_s, ret_state_s = _swa_prompt_ret_sample(
        sinks, z, zm, bsz, seq, *swa_tabs, lg, sample_row, state_ret[0], cos, sin, t)
    to_kernel = lambda a: jnp.transpose(a[0], (0, 2, 3, 1)).reshape(-1, LANES, WINDOW)
    from_kernel = lambda a: jnp.transpose(a.reshape(-1, SWA_KV_HEADS, SWA_HD, WINDOW), (0, 3, 1, 2))[None]
    kc, vc = to_kernel(cache_swa_k), to_kernel(cache_swa_v)
    swa_o_s, k_s, v_s = _swa_sample(sinks, z, sample_row, kc, vc, *swa_tabs, seq + RET_CHUNK, t)

    mg, mb = row(ln_mix_g[0]), row(ln_mix_b[0])
    fg, fb = row(ln_ffn_g[0]), row(ln_ffn_b[0])
    ffn_w = (w_ffn_gate[0], w_ffn_up[0], w_ffn_down[0])
    h_p, hb_p = _out_proj_ln(ret_o_p, swa_o_p, xp, eg, eb, w_out[0], mg, mb, 512)
    h_s, hb_s = _out_proj_ln(ret_o_s, swa_o_s, xs, eg, eb, w_out[0], mg, mb, 512)
    y_s, *ffn_wb = _ffn_ln(h_s, hb_s, *ffn_w, fg, fb, TOK_TILE, 256)
    y_p = _ffn_ln(h_p, hb_p, *ffn_wb, fg, fb, TOK_TILE, 512)

    return (y_p.reshape(bsz, seq, d), y_s.reshape(dbs, t, d),
            ret_state_p[None], from_kernel(k_p), from_kernel(v_p),
            ret_state_s[None], from_kernel(k_s), from_kernel(v_s))
```

```python
import functools

import jax
import jax.numpy as jnp
from jax import lax
from jax.experimental import pallas as pl
from jax.experimental.pallas import tpu as pltpu

F32 = jnp.float32
BF16 = jnp.bfloat16

D_MODEL = 2048
N_META = 16
RET_HEADS = 4
RET_DK = 256
RET_DV = 256
RET_CHUNK = 128
RET_THETA = 10000.0
SWA_HD = 64
SWA_HEADS = 16
SWA_KV_HEADS = 2
SWA_GROUP = SWA_HEADS // SWA_KV_HEADS
WINDOW = 128
ROPE_THETA = 500000.0
ROT_DIM = SWA_HD // 4
FFN_HIDDEN = 5632
PROJ_W = 5376
DEPTH = 1
ALPHA = (2.0 * DEPTH) ** 0.25
LN_EPS = 1e-5
NEG_INF = -1e30

LANES = 128
RET_W = RET_HEADS * RET_DK
SWA_Q_COL = 4 * RET_W
SWA_KV_COL = SWA_Q_COL + SWA_HEADS * SWA_HD
VMEM_LIMIT = 56 * 1024 * 1024
VMEM_LIMIT_LARGE = 60 * 1024 * 1024
TOK_TILE = 1024
FFN_TAIL_SUBBLOCKS = 4
LN_PROJ_SUBBLOCKS = 2
RET_CHUNK_UNROLL = 16
RET_SAMPLE_EVERY = 2
SAMPLE_TABLE_ROWS = 64
WEIGHT_STAGE_ROWS = 256


def _cparams(sem):
    return pltpu.CompilerParams(dimension_semantics=sem, vmem_limit_bytes=VMEM_LIMIT)


def _layer_norm(x, g, b):
    mu = jnp.mean(x, axis=-1, keepdims=True)
    xc = x - mu
    var = jnp.mean(xc * xc, axis=-1, keepdims=True)
    return xc * lax.rsqrt(var + LN_EPS) * g + b


def _silu(x):
    return x / (1.0 + jnp.exp(-x))


def _dot(a, b):
    return jnp.dot(a, b, preferred_element_type=F32)


def _dot_nt(a, b):
    return lax.dot_general(a, b, (((1,), (1,)), ((), ())), preferred_element_type=F32)


def _dot_tn(a, b):
    return lax.dot_general(a, b, (((0,), (0,)), ((), ())), preferred_element_type=F32)


def _ln_proj_kernel(xp_ref, xs_ref, xm_ref, g_ref, b_ref, w_hbm, z_ref, zm_ref,
                    hm_scr, wb_scr, stage, sem, *, np_tiles, ns_tiles):
    i = pl.program_id(0)
    j = pl.program_id(1)
    first = j == 0
    is_sample = i >= np_tiles
    is_last = i == np_tiles + ns_tiles - 1
    n_col, _, tn = wb_scr.shape
    chunk = stage.shape[1]
    n_chunks = w_hbm.shape[0] // chunk

    def weight_copy(k):
        return pltpu.make_async_copy(w_hbm.at[pl.ds(k * chunk, chunk), :], stage.at[k % 2], sem.at[k % 2])

    @pl.when(first & (i == 0))
    def _():
        weight_copy(0).start()
        for k in range(n_chunks):
            if k + 1 < n_chunks:
                weight_copy(k + 1).start()
            weight_copy(k).wait()
            for jt in range(n_col):
                wb_scr[jt, k * chunk:(k + 1) * chunk, :] = stage[k % 2, :, jt * tn:(jt + 1) * tn].astype(BF16)

    def norm(x):
        return _layer_norm(x, g_ref[...], b_ref[...]).astype(BF16)

    def project(x_ref):
        sub = x_ref.shape[0] // LN_PROJ_SUBBLOCKS
        for r in range(LN_PROJ_SUBBLOCKS):
            rows = slice(r * sub, (r + 1) * sub)
            z_ref[rows, :] = _dot(norm(x_ref[rows, :]), wb_scr[j])

    @pl.when(jnp.logical_not(is_sample))
    def _():
        project(xp_ref)

    @pl.when(is_sample)
    def _():
        project(xs_ref)

    @pl.when(is_last)
    def _():
        lead = hm_scr.shape[0] - xm_ref.shape[0]
        hm_scr[:lead, :] = jnp.zeros((lead, D_MODEL), BF16)
        hm_scr[lead:, :] = norm(xm_ref[...])
        zm_ref[...] = _dot(hm_scr[...], wb_scr[j])


def _ln_proj(xp, xs, xm, g, b, w, tm, tn):
    np_tiles = xp.shape[0] // tm
    ns_tiles = xs.shape[0] // tm
    last = np_tiles + ns_tiles - 1
    n_col = PROJ_W // tn
    assert n_col == 1
    return pl.pallas_call(
        functools.partial(_ln_proj_kernel, np_tiles=np_tiles, ns_tiles=ns_tiles),
        grid=(np_tiles + ns_tiles, n_col),
        in_specs=[
            pl.BlockSpec((tm, D_MODEL), lambda i, j: (jnp.minimum(i, np_tiles - 1), 0)),
            pl.BlockSpec((tm, D_MODEL), lambda i, j: (jnp.clip(i - np_tiles, 0, ns_tiles - 1), 0)),
            pl.BlockSpec(xm.shape, lambda i, j: (0, 0), pipeline_mode=pl.Buffered(1)),
            pl.BlockSpec((1, D_MODEL), lambda i, j: (0, 0)),
            pl.BlockSpec((1, D_MODEL), lambda i, j: (0, 0)),
            pl.BlockSpec(memory_space=pl.ANY),
        ],
        out_specs=[
            pl.BlockSpec((tm, tn), lambda i, j: (i, j)),
            pl.BlockSpec((RET_CHUNK, tn), lambda i, j: (0, jnp.where(i == last, j, 0))),
        ],
        out_shape=[
            jax.ShapeDtypeStruct((xp.shape[0] + xs.shape[0], PROJ_W), F32),
            jax.ShapeDtypeStruct((RET_CHUNK, PROJ_W), F32),
        ],
        scratch_shapes=[
            pltpu.VMEM((RET_CHUNK, D_MODEL), BF16),
            pltpu.VMEM((n_col, D_MODEL, tn), BF16),
            pltpu.VMEM((2, WEIGHT_STAGE_ROWS // 4, PROJ_W), F32),
            pltpu.SemaphoreType.DMA((2,)),
        ],
        compiler_params=pltpu.CompilerParams(dimension_semantics=("arbitrary", "arbitrary"),
                                             vmem_limit_bytes=VMEM_LIMIT_LARGE),
        name="ln_proj",
    )(xp, xs, xm, g, b, w)


def _ret_rope(x, cos, sin):
    x1 = x[:, :LANES]
    x2 = x[:, LANES:]
    return jnp.concatenate([x1 * cos - x2 * sin, x2 * cos + x1 * sin], axis=1)


def _group_norm_gate(o, gate):
    mu = jnp.mean(o, axis=-1, keepdims=True)
    oc = o - mu
    var = jnp.mean(oc * oc, axis=-1, keepdims=True)
    return oc * lax.rsqrt(var + LN_EPS) * _silu(gate)


def _ret_prompt_kernel(lg_ref, q_ref, k_ref, v_ref, g_ref, km_ref, vm_ref,
                       cos_ref, sin_ref, cosm_ref, sinm_ref, o_ref, s_ref, s_scr):
    C = RET_CHUNK
    lg = lg_ref[pl.program_id(1)]
    ri = lax.broadcasted_iota(jnp.int32, (C, C), 0)
    ci = lax.broadcasted_iota(jnp.int32, (C, C), 1)
    rel = (ri - ci).astype(F32)
    decay = jnp.where(rel >= 0.0, jnp.exp(jnp.maximum(rel, 0.0) * lg), 0.0)
    row = lax.broadcasted_iota(jnp.int32, (C, 1), 0).astype(F32)
    q_decay = jnp.exp((row + 1.0) * lg)
    k_decay = jnp.exp((C - 1.0 - row) * lg)
    chunk_decay = jnp.exp(jnp.full((1, RET_DV), C * lg, F32))
    scale = RET_DK ** -0.5

    meta_decay = jnp.where(row >= C - N_META, k_decay, 0.0)
    km = _ret_rope(km_ref[...], cosm_ref[...], sinm_ref[...]) * scale
    s_scr[...] = _dot_tn((km * meta_decay).astype(BF16), vm_ref[...].astype(BF16))

    def chunk(c, carry):
        rows = pl.ds(pl.multiple_of(c * C, C), C)
        cos = cos_ref[rows, :]
        sin = sin_ref[rows, :]
        q = _ret_rope(q_ref[rows, :], cos, sin)
        k = _ret_rope(k_ref[rows, :], cos, sin) * scale
        qb = q.astype(BF16)
        vb = v_ref[rows, :].astype(BF16)
        s_prev = s_scr[...]
        scores = _dot_nt(qb, k.astype(BF16)) * decay
        inner = _dot(scores.astype(BF16), vb)
        cross = _dot(qb, s_prev.astype(BF16)) * q_decay
        s_scr[...] = chunk_decay * s_prev + _dot_tn((k * k_decay).astype(BF16), vb)
        o_ref[rows, :] = _group_norm_gate(inner + cross, g_ref[rows, :]).astype(BF16)
        return carry

    lax.fori_loop(0, q_ref.shape[0] // C, chunk, 0, unroll=RET_CHUNK_UNROLL)
    s_ref[...] = s_scr[...]


def _ret_prompt(lg, z, zm, bsz, seq, cos, sin):
    col = lambda base: (lambda b, h: (b, base + h))
    mcol = lambda base: (lambda b, h: (0, base + h))
    full = lambda b, h: (0, 0)
    meta_tab = lambda b, h: (seq // RET_CHUNK, 0)
    return pl.pallas_call(
        _ret_prompt_kernel,
        grid=(bsz, RET_HEADS),
        in_specs=[
            pl.BlockSpec(memory_space=pltpu.SMEM),
            pl.BlockSpec((seq, RET_DK), col(0)),
            pl.BlockSpec((seq, RET_DK), col(RET_HEADS)),
            pl.BlockSpec((seq, RET_DV), col(2 * RET_HEADS)),
            pl.BlockSpec((seq, RET_DV), col(3 * RET_HEADS)),
            pl.BlockSpec((RET_CHUNK, RET_DK), mcol(RET_HEADS)),
            pl.BlockSpec((RET_CHUNK, RET_DV), mcol(2 * RET_HEADS)),
            pl.BlockSpec((seq, LANES), full),
            pl.BlockSpec((seq, LANES), full),
            pl.BlockSpec((RET_CHUNK, LANES), meta_tab),
            pl.BlockSpec((RET_CHUNK, LANES), meta_tab),
        ],
        out_specs=[
            pl.BlockSpec((seq, RET_DV), lambda b, h: (b, h)),
            pl.BlockSpec((None, None, RET_DK, RET_DV), lambda b, h: (b, h, 0, 0)),
        ],
        out_shape=[
            jax.ShapeDtypeStruct((bsz * seq, RET_W), BF16),
            jax.ShapeDtypeStruct((bsz, RET_HEADS, RET_DK, RET_DV), F32),
        ],
        scratch_shapes=[pltpu.VMEM((RET_DK, RET_DV), F32)],
        compiler_params=_cparams(("parallel", "parallel")),
        name="ret_prompt",
    )(lg, z, z, z, z, zm, zm, cos, sin, cos, sin)


def _ret_sample_jobs(lg_ref, z_ref, s_ref, cos_ref, sin_ref, o_ref, so_ref, *, nb, t):
    rows = nb * t
    R = RET_HEADS * rows
    scale = RET_DK ** -0.5
    cos = cos_ref[...]
    sin = sin_ref[...]

    def stack(base, rope, mul=1.0):
        parts = []
        for h in range(RET_HEADS):
            x = z_ref[:, base + h * RET_DK: base + (h + 1) * RET_DK]
            parts.append(_ret_rope(x, cos, sin) * mul if rope else x)
        return jnp.concatenate(parts, axis=0)

    q = stack(0, True)
    k = stack(RET_W, True, scale)
    v = stack(2 * RET_W, False)
    gate = stack(3 * RET_W, False)

    ri = lax.broadcasted_iota(jnp.int32, (R, R), 0)
    ci = lax.broadcasted_iota(jnp.int32, (R, R), 1)
    rcol = lax.broadcasted_iota(jnp.int32, (R, 1), 0)
    lg_col = jnp.zeros((R, 1), F32)
    for h in range(RET_HEADS):
        lg_col = jnp.where((rcol >= h * rows) & (rcol < (h + 1) * rows), lg_ref[h], lg_col)
    tcol = (rcol & (t - 1)).astype(F32)
    rel = (ri - ci).astype(F32)
    same = ((ri & -t) == (ci & -t)) & (ri >= ci)
    decay = jnp.where(same, jnp.exp(jnp.maximum(rel, 0.0) * lg_col), 0.0)
    q_decay = jnp.exp((tcol + 1.0) * lg_col)
    k_decay = jnp.exp((t - 1.0 - tcol) * lg_col)

    qb = q.astype(BF16)
    vb = v.astype(BF16)
    scores = _dot_nt(qb, k.astype(BF16)) * decay
    inner = _dot(scores.astype(BF16), vb)
    kw = k * k_decay

    cross_parts = [None] * (RET_HEADS * nb)

    def state_job(h, db):
        def run():
            r0 = h * rows + db * t
            s_prev = s_ref[db, h]
            cross_parts[h * nb + db] = _dot(q[r0:r0 + t, :].astype(BF16), s_prev.astype(BF16))
            mine = (rcol >= r0) & (rcol < r0 + t)
            upd = _dot_tn(jnp.where(mine, kw, 0.0).astype(BF16), vb)
            step_decay = jnp.exp(jnp.full((1, RET_DV), t * lg_ref[h], F32))
            so_ref[db, h] = step_decay * s_prev + upd
        return run

    def finish():
        cross = jnp.concatenate(cross_parts, axis=0) * q_decay
        out = _group_norm_gate(inner + cross, gate).astype(BF16)
        for h in range(RET_HEADS):
            o_ref[:, h * RET_DV:(h + 1) * RET_DV] = out[h * rows:(h + 1) * rows, :]

    return [state_job(h, db) for h in range(RET_HEADS) for db in range(nb)], finish


def _swa_rope(x, c, s1, s2):
    return x * c + pltpu.roll(x, 8, 1) * s1 + pltpu.roll(x, LANES - 8, 1) * s2


def _dup_head(x, g, low):
    swapped = pltpu.roll(x, SWA_HD, 1)
    return jnp.where(low, x, swapped) if g == 0 else jnp.where(low, swapped, x)


def _to_kv_half(slab, head, low):
    g = head // SWA_GROUP
    src = slab if head % 2 == g else pltpu.roll(slab, SWA_HD, 1)
    return jnp.where(low, src, 0.0) if g == 0 else jnp.where(low, 0.0, src)


def _from_kv_half(o_even, o_odd, g, low):
    if g == 0:
        return jnp.where(low, o_even, pltpu.roll(o_odd, SWA_HD, 1))
    return jnp.where(low, pltpu.roll(o_even, SWA_HD, 1), o_odd)


def _sink_softmax(logits, sink):
    m = jnp.maximum(jnp.max(logits, axis=-1, keepdims=True), sink)
    p = jnp.exp(logits - m)
    return p, jnp.sum(p, axis=-1, keepdims=True) + jnp.exp(sink - m)


def _swa_prompt_kernel(sink_ref, q_ref, kv_ref, kvm_ref, c_ref, s1_ref, s2_ref,
                       cm_ref, s1m_ref, s2m_ref, o_ref, kp_ref, vp_ref, kprev, vprev):
    W = WINDOW
    m_id = pl.program_id(1)

    @pl.when(m_id == 0)
    def _():
        kprev[...] = _swa_rope(kvm_ref[:, :LANES], cm_ref[...], s1m_ref[...], s2m_ref[...])
        vprev[...] = kvm_ref[:, LANES:]

    c = c_ref[...]
    s1 = s1_ref[...]
    s2 = s2_ref[...]
    kcur = _swa_rope(kv_ref[:, :LANES], c, s1, s2)
    vcur = kv_ref[:, LANES:]
    k2 = jnp.concatenate([kprev[...], kcur], axis=0)
    v2 = jnp.concatenate([vprev[...], vcur], axis=0)

    r = lax.broadcasted_iota(jnp.int32, (W, 2 * W), 0)
    cc = lax.broadcasted_iota(jnp.int32, (W, 2 * W), 1)
    diff = W + r - cc
    mask = (diff >= 0) & (diff < W) & ((m_id > 0) | (cc >= W - N_META))
    low_k = lax.broadcasted_iota(jnp.int32, (2 * W, LANES), 1) < SWA_HD
    low = lax.broadcasted_iota(jnp.int32, (W, LANES), 1) < SWA_HD
    scale = SWA_HD ** -0.5

    def attend(q, kd, vd, h):
        p, den = _sink_softmax(jnp.where(mask, _dot_nt(q.astype(BF16), kd), NEG_INF), sink_ref[h])
        return _dot(p.astype(BF16), vd) / den

    for g in range(SWA_KV_HEADS):
        kd = _dup_head(k2, g, low_k).astype(BF16)
        vd = _dup_head(v2, g, low_k).astype(BF16)
        for p in range(g * SWA_GROUP // 2, (g + 1) * SWA_GROUP // 2):
            slab = _swa_rope(q_ref[:, p * LANES:(p + 1) * LANES], c, s1, s2) * scale
            o_even = attend(jnp.where(low, slab, 0.0), kd, vd, 2 * p)
            o_odd = attend(jnp.where(low, 0.0, slab), kd, vd, 2 * p + 1)
            o_ref[:, p * LANES:(p + 1) * LANES] = jnp.where(low, o_even, o_odd).astype(BF16)

    kprev[...] = kcur
    vprev[...] = vcur

    @pl.when(m_id == pl.num_programs(1) - 1)
    def _():
        kp_ref[...] = kcur.T
        vp_ref[...] = vcur.T


def _swa_prompt_ret_sample_kernel(sink_ref, q_ref, kv_ref, kvm_ref, c_ref, s1_ref, s2_ref, cm_ref, s1m_ref, s2m_ref,
                                 lg_ref, zs_ref, st_ref, cos_ref, sin_ref,
                                 o_ref, kp_ref, vp_ref, os_ref, sto_ref, kprev, vprev, *, nb, t, every):
    _swa_prompt_kernel(sink_ref, q_ref, kv_ref, kvm_ref, c_ref, s1_ref, s2_ref, cm_ref, s1m_ref, s2m_ref,
                       o_ref, kp_ref, vp_ref, kprev, vprev)

    @pl.when(lax.rem(pl.program_id(1), every) == every - 1)
    def _():
        state_jobs, finish = _ret_sample_jobs(lg_ref, zs_ref, st_ref, cos_ref, sin_ref, os_ref, sto_ref, nb=nb, t=t)
        for job in state_jobs + [finish]:
            job()


def _swa_prompt_ret_sample(sinks, z, zm, bsz, seq, c, s1, s2, lg, sample_row, state, cos, sin, t):
    W = WINDOW
    nblk = seq // W
    every = RET_SAMPLE_EVERY
    ret_steps = bsz * nblk // every
    dbs = state.shape[0]
    nb = dbs // ret_steps
    assert nblk % every == 0 and nb * ret_steps == dbs and (nb * t) % 16 == 0
    rows = nb * t
    qw = SWA_HEADS * SWA_HD
    kv_col = SWA_KV_COL // (2 * LANES)
    tab = pl.BlockSpec((W, LANES), lambda b, m: (m, 0))
    mtab = pl.BlockSpec((W, LANES), lambda b, m: (seq // W, 0))
    stab = pl.BlockSpec((rows, LANES), lambda b, m: ((seq + RET_CHUNK) // rows, 0))
    ret_step = lambda b, m: (b * nblk + m) // every
    state_spec = pl.BlockSpec((nb, RET_HEADS, RET_DK, RET_DV), lambda b, m: (ret_step(b, m), 0, 0, 0))
    return pl.pallas_call(
        functools.partial(_swa_prompt_ret_sample_kernel, nb=nb, t=t, every=every),
        grid=(bsz, nblk),
        in_specs=[
            pl.BlockSpec(memory_space=pltpu.SMEM),
            pl.BlockSpec((W, qw), lambda b, m: (b * nblk + m, SWA_Q_COL // qw)),
            pl.BlockSpec((W, 2 * LANES), lambda b, m: (b * nblk + m, kv_col)),
            pl.BlockSpec((W, 2 * LANES), lambda b, m: (0, kv_col)),
            tab, tab, tab, mtab, mtab, mtab,
            pl.BlockSpec(memory_space=pltpu.SMEM),
            pl.BlockSpec((rows, 4 * RET_W), lambda b, m: (sample_row // rows + ret_step(b, m), 0)),
            state_spec, stab, stab,
        ],
        out_specs=[
            pl.BlockSpec((W, qw), lambda b, m: (b * nblk + m, 0)),
            pl.BlockSpec((None, W, LANES), lambda b, m: (b, 0, 0)),
            pl.BlockSpec((None, W, LANES), lambda b, m: (b, 0, 0)),
            pl.BlockSpec((rows, RET_W), lambda b, m: (ret_step(b, m), 0)),
            state_spec,
        ],
        out_shape=[
            jax.ShapeDtypeStruct((bsz * seq, qw), BF16),
            jax.ShapeDtypeStruct((bsz, W, LANES), F32),
            jax.ShapeDtypeStruct((bsz, W, LANES), F32),
            jax.ShapeDtypeStruct((dbs * t, RET_W), BF16),
            jax.ShapeDtypeStruct(state.shape, F32),
        ],
        scratch_shapes=[pltpu.VMEM((W, LANES), F32), pltpu.VMEM((W, LANES), F32)],
        compiler_params=_cparams(("parallel", "arbitrary")),
        name="swa_prompt_ret_sample",
    )(sinks, z, z, zm, c, s1, s2, c, s1, s2, lg, z, state, cos, sin)


def _swa_sample_kernel(sink_ref, q_ref, kv_ref, kc_ref, vc_ref, c_ref, s1_ref, s2_ref,
                       o_ref, ko_ref, vo_ref, *, nb, t):
    W = WINDOW
    c = c_ref[...]
    s1 = s1_ref[...]
    s2 = s2_ref[...]
    row_pad = jnp.zeros((LANES - nb * t, LANES), F32)
    knew_t = jnp.concatenate([_swa_rope(kv_ref[:, :LANES], c, s1, s2), row_pad], axis=0).T
    vnew_t = jnp.concatenate([kv_ref[:, LANES:], row_pad], axis=0).T
    knew_b = knew_t.astype(BF16)
    vnew_b = vnew_t.astype(BF16)

    rq = SWA_HEADS * t
    tq = lax.broadcasted_iota(jnp.int32, (rq, 2 * W), 0) & (t - 1)
    cc = lax.broadcasted_iota(jnp.int32, (rq, 2 * W), 1)
    out_lane = lax.broadcasted_iota(jnp.int32, (LANES, LANES), 1)
    low = lax.broadcasted_iota(jnp.int32, (nb * t, LANES), 1) < SWA_HD
    low_t = lax.broadcasted_iota(jnp.int32, (t, LANES), 1) < SWA_HD
    sink_col = jnp.concatenate([jnp.full((t, 1), sink_ref[h], F32) for h in range(SWA_HEADS)], axis=0)
    npair = SWA_HEADS // 2

    scale = SWA_HD ** -0.5
    slabs = [_swa_rope(q_ref[:, p * LANES:(p + 1) * LANES], c, s1, s2) * scale for p in range(npair)]
    q_heads = [_to_kv_half(slabs[h // 2], h, low) for h in range(SWA_HEADS)]

    logits, values = [], []
    for db in range(nb):
        rows = slice(db * t, (db + 1) * t)
        kc = kc_ref[db]
        vc = vc_ref[db]
        keep = out_lane < W - t
        ko_ref[db] = jnp.where(keep, pltpu.roll(kc, W - t, 1), pltpu.roll(knew_t, (W - t - db * t) % LANES, 1))
        vo_ref[db] = jnp.where(keep, pltpu.roll(vc, W - t, 1), pltpu.roll(vnew_t, (W - t - db * t) % LANES, 1))
        k2 = jnp.concatenate([kc.astype(BF16), knew_b], axis=1)
        values.append(jnp.concatenate([vc.astype(BF16), vnew_b], axis=1))
        t_new = cc - W - db * t
        mask = ((cc < W) & (cc > tq)) | ((t_new >= 0) & (t_new <= tq))
        q_db = jnp.concatenate([qh[rows, :] for qh in q_heads], axis=0).astype(BF16)
        logits.append(jnp.where(mask, _dot(q_db, k2), NEG_INF))
    p, den = _sink_softmax(jnp.concatenate(logits, axis=0), jnp.concatenate([sink_col] * nb, axis=0))
    p = p.astype(BF16)
    outs = [[] for _ in range(npair)]
    for db in range(nb):
        o = _dot_nt(p[db * rq:(db + 1) * rq, :], values[db]) / den[db * rq:(db + 1) * rq, :]
        for pr in range(npair):
            o_even = o[(2 * pr) * t:(2 * pr + 1) * t, :]
            o_odd = o[(2 * pr + 1) * t:(2 * pr + 2) * t, :]
            outs[pr].append(_from_kv_half(o_even, o_odd, 2 * pr // SWA_GROUP, low_t))
    for pr in range(npair):
        o_ref[:, pr * LANES:(pr + 1) * LANES] = jnp.concatenate(outs[pr], axis=0).astype(BF16)


def _swa_sample(sinks, z, row0, kc, vc, c, s1, s2, tab_row0, t):
    dbs = kc.shape[0]
    n = dbs * t
    rows = SAMPLE_TABLE_ROWS
    nb = rows // t
    W = WINDOW
    qw = SWA_HEADS * SWA_HD
    tab = pl.BlockSpec((rows, LANES), lambda i: (tab_row0 // rows, 0))
    cache = pl.BlockSpec((nb, W, LANES), lambda i: (i, 0, 0))
    return pl.pallas_call(
        functools.partial(_swa_sample_kernel, nb=nb, t=t),
        grid=(dbs // nb,),
        in_specs=[
            pl.BlockSpec(memory_space=pltpu.SMEM),
            pl.BlockSpec((rows, qw), lambda i: (row0 // rows + i, SWA_Q_COL // qw)),
            pl.BlockSpec((rows, 2 * LANES), lambda i: (row0 // rows + i, SWA_KV_COL // (2 * LANES))),
            cache, cache, tab, tab, tab,
        ],
        out_specs=[pl.BlockSpec((rows, qw), lambda i: (i, 0)), cache, cache],
        out_shape=[
            jax.ShapeDtypeStruct((n, qw), BF16),
            jax.ShapeDtypeStruct((dbs, W, LANES), F32),
            jax.ShapeDtypeStruct((dbs, W, LANES), F32),
        ],
        compiler_params=_cparams(("parallel",)),
        name="swa_sample",
    )(sinks, z, z, kc, vc, c, s1, s2)


def _out_proj_ln_kernel(ro_ref, so_ref, x_ref, eg_ref, eb_ref, w_hbm, mg_ref, mb_ref, h_ref, hb_ref,
                        wb_scr, stage, sem):
    chunk = stage.shape[1]
    n_chunks = w_hbm.shape[0] // chunk

    def weight_copy(k):
        return pltpu.make_async_copy(w_hbm.at[pl.ds(k * chunk, chunk), :], stage.at[k % 2], sem.at[k % 2])

    @pl.when(pl.program_id(0) == 0)
    def _():
        weight_copy(0).start()
        for k in range(n_chunks):
            if k + 1 < n_chunks:
                weight_copy(k + 1).start()
            weight_copy(k).wait()
            wb_scr[k * chunk:(k + 1) * chunk, :] = stage[k % 2].astype(BF16)

    sub = h_ref.shape[0] // 2
    for r in range(2):
        rows = slice(r * sub, (r + 1) * sub)
        x_in = _layer_norm(x_ref[rows, :], eg_ref[...], eb_ref[...])
        mixed = _dot(ro_ref[rows, :], wb_scr[:RET_W, :]) + _dot(so_ref[rows, :], wb_scr[RET_W:, :])
        h = _layer_norm(ALPHA * x_in + mixed, mg_ref[...], mb_ref[...])
        h_ref[rows, :] = h
        hb_ref[rows, :] = h.astype(BF16)


def _out_proj_ln(ro, so, x, eg, eb, w, mg, mb, tm):
    n = x.shape[0]
    vec = pl.BlockSpec((1, D_MODEL), lambda i: (0, 0))
    tile = pl.BlockSpec((tm, D_MODEL), lambda i: (i, 0))
    return pl.pallas_call(
        _out_proj_ln_kernel,
        grid=(n // tm,),
        in_specs=[
            pl.BlockSpec((tm, RET_W), lambda i: (i, 0)),
            pl.BlockSpec((tm, RET_W), lambda i: (i, 0)),
            tile, vec, vec,
            pl.BlockSpec(memory_space=pl.ANY),
            vec, vec,
        ],
        out_specs=[tile, tile],
        out_shape=[jax.ShapeDtypeStruct((n, D_MODEL), F32), jax.ShapeDtypeStruct((n, D_MODEL), BF16)],
        scratch_shapes=[pltpu.VMEM((D_MODEL, D_MODEL), BF16),
                        pltpu.VMEM((2, WEIGHT_STAGE_ROWS, D_MODEL), F32),
                        pltpu.SemaphoreType.DMA((2,))],
        compiler_params=_cparams(("arbitrary",)),
        name="out_proj_ln",
    )(ro, so, x, eg, eb, w, mg, mb)


def _ffn_ln_kernel(hb_ref, h_hbm, wg_ref, wu_ref, wd_ref, g_ref, b_ref, y_ref, *rest):
    *w_out_refs, h_res, sem = rest
    i = pl.program_id(0)
    j = pl.program_id(1)
    tm = h_res.shape[0]
    residual_copy = pltpu.make_async_copy(h_hbm.at[pl.ds(i * tm, tm), :], h_res, sem)

    @pl.when(j == 0)
    def _():
        residual_copy.start()
        y_ref[...] = jnp.zeros_like(y_ref)

    wg, wu, wd = (w[...].astype(BF16) for w in (wg_ref, wu_ref, wd_ref))
    for out_ref, w in zip(w_out_refs, (wg, wu, wd)):
        out_ref[...] = w
    def hidden_tile(rows):
        hb = hb_ref[rows, :]
        act = _silu(_dot(hb, wg)) * _dot(hb, wu)
        return y_ref[rows, :] + _dot(act.astype(BF16), wd)

    is_last = j == pl.num_programs(1) - 1

    @pl.when(jnp.logical_not(is_last))
    def _():
        y_ref[...] = hidden_tile(slice(None))

    @pl.when(is_last)
    def _():
        residual_copy.wait()
        sub = tm // FFN_TAIL_SUBBLOCKS
        for r in range(FFN_TAIL_SUBBLOCKS):
            rows = slice(r * sub, (r + 1) * sub)
            y_ref[rows, :] = _layer_norm(ALPHA * h_res[rows, :] + hidden_tile(rows), g_ref[...], b_ref[...])


def _ffn_ln(h, hb, wg, wu, wd, g, b, tm, th):
    n = h.shape[0]
    emit_weights = wg.dtype == F32
    assert not emit_weights or n == tm
    vec = pl.BlockSpec((1, D_MODEL), lambda i, j: (0, 0))
    w_specs = [
        pl.BlockSpec((D_MODEL, th), lambda i, j: (0, j)),
        pl.BlockSpec((D_MODEL, th), lambda i, j: (0, j)),
        pl.BlockSpec((th, D_MODEL), lambda i, j: (j, 0)),
    ]
    out_specs = [pl.BlockSpec((tm, D_MODEL), lambda i, j: (i, 0))]
    out_shape = [jax.ShapeDtypeStruct((n, D_MODEL), F32)]
    if emit_weights:
        out_specs += w_specs
        out_shape += [jax.ShapeDtypeStruct(w.shape, BF16) for w in (wg, wu, wd)]
    outs = pl.pallas_call(
        _ffn_ln_kernel,
        grid=(n // tm, FFN_HIDDEN // th),
        in_specs=[pl.BlockSpec((tm, D_MODEL), lambda i, j: (i, 0)), pl.BlockSpec(memory_space=pl.ANY)] + w_specs
        + [vec, vec],
        out_specs=out_specs,
        out_shape=out_shape,
        scratch_shapes=[pltpu.VMEM((tm, D_MODEL), F32), pltpu.SemaphoreType.DMA(())],
        compiler_params=pltpu.CompilerParams(dimension_semantics=("arbitrary", "arbitrary"),
                                             vmem_limit_bytes=VMEM_LIMIT_LARGE),
        name="ffn_ln",
    )(hb, h, wg, wu, wd, g, b)
    return tuple(outs) if emit_weights else outs[0]


def _position_tables(seq, t, past_len):
    r = jnp.arange(seq + RET_CHUNK + SAMPLE_TABLE_ROWS)
    meta_r = r - seq
    pos = jnp.where(r < seq, N_META + r,
                    jnp.where(meta_r < RET_CHUNK, jnp.maximum(meta_r - (RET_CHUNK - N_META), 0),
                              past_len + (meta_r - RET_CHUNK) % t)).astype(F32)[:, None]
    ret_freq = jnp.power(RET_THETA, -jnp.linspace(0.0, 1.0, RET_DK // 2, dtype=F32))
    ret_ang = pos * ret_freq[None, :]
    half = ROT_DIM // 2
    swa_freq = jnp.power(ROPE_THETA, -jnp.arange(0, ROT_DIM, 2, dtype=F32) / ROT_DIM)
    swa_ang = pos * swa_freq[None, :]
    cos = jnp.tile(jnp.cos(swa_ang), (1, LANES // half))
    sin = jnp.tile(jnp.sin(swa_ang), (1, LANES // half))
    d = jnp.arange(LANES) % SWA_HD
    c = jnp.where(d < ROT_DIM, cos, 1.0)
    s1 = jnp.where((d >= half) & (d < ROT_DIM), sin, 0.0)
    s2 = jnp.where(d < half, -sin, 0.0)
    return (jnp.cos(ret_ang), jnp.sin(ret_ang)), (c, s1, s2)


def kernel(x_prompt, x_sample, state_ret, cache_swa_k, cache_swa_v, meta_tokens, ln_emb_g, ln_emb_b,
           w_in, w_out, swa_sinks, ln_mix_g, ln_mix_b, w_ffn_gate, w_ffn_up, w_ffn_down, ln_ffn_g, ln_ffn_b):
    bsz, seq, d = x_prompt.shape
    dbs, t, _ = x_sample.shape
    assert w_in.shape[0] == DEPTH and d == D_MODEL and seq % RET_CHUNK == 0 and t & (t - 1) == 0
    past_len = 16384
    row = lambda a: a.reshape(1, -1)

    eg, eb = row(ln_emb_g), row(ln_emb_b)
    sinks = swa_sinks[0]
    lg = jnp.log(1.0 - jnp.power(2.0, -5.0 - jnp.arange(RET_HEADS, dtype=F32)))

    xp = x_prompt.reshape(bsz * seq, d)
    xs = x_sample.reshape(dbs * t, d)

    z, zm = _ln_proj(xp, xs, meta_tokens, eg, eb, w_in[0], 256, PROJ_W)
    sample_row = bsz * seq

    (cos, sin), swa_tabs = _position_tables(seq, t, past_len)
    ret_o_p, ret_state_p = _ret_prompt(lg, z, zm, bsz, seq, cos, sin)
    swa_o_p, k_p, v_p, ret_o_s, ret_state_s = _swa_prompt_ret_sample(
        sinks, z, zm, bsz, seq, *swa_tabs, lg, sample_row, state_ret[0], cos, sin, t)
    to_kernel = lambda a: jnp.transpose(a[0], (0, 2, 3, 1)).reshape(-1, LANES, WINDOW)
    from_kernel = lambda a: jnp.transpose(a.reshape(-1, SWA_KV_HEADS, SWA_HD, WINDOW), (0, 3, 1, 2))[None]
    kc, vc = to_kernel(cache_swa_k), to_kernel(cache_swa_v)
    swa_o_s, k_s, v_s = _swa_sample(sinks, z, sample_row, kc, vc, *swa_tabs, seq + RET_CHUNK, t)

    mg, mb = row(ln_mix_g[0]), row(ln_mix_b[0])
    fg, fb = row(ln_ffn_g[0]), row(ln_ffn_b[0])
    ffn_w = (w_ffn_gate[0], w_ffn_up[0], w_ffn_down[0])
    h_p, hb_p = _out_proj_ln(ret_o_p, swa_o_p, xp, eg, eb, w_out[0], mg, mb, 512)
    h_s, hb_s = _out_proj_ln(ret_o_s, swa_o_s, xs, eg, eb, w_out[0], mg, mb, 512)
    y_s, *ffn_wb = _ffn_ln(h_s, hb_s, *ffn_w, fg, fb, TOK_TILE, 256)
    y_p = _ffn_ln(h_p, hb_p, *ffn_wb, fg, fb, TOK_TILE, 512)

    return (y_p.reshape(bsz, seq, d), y_s.reshape(dbs, t, d),
            ret_state_p[None], from_kernel(k_p), from_kernel(v_p),
            ret_state_s[None], from_kernel(k_s), from_kernel(v_s))
```

```python
import functools

import jax
import jax.numpy as jnp
from jax import lax
from jax.experimental import pallas as pl
from jax.experimental.pallas import tpu as pltpu

F32 = jnp.float32
BF16 = jnp.bfloat16

D_MODEL = 2048
N_META = 16
RET_HEADS = 4
RET_DK = 256
RET_DV = 256
RET_CHUNK = 128
RET_THETA = 10000.0
SWA_HD = 64
SWA_HEADS = 16
SWA_KV_HEADS = 2
SWA_GROUP = SWA_HEADS // SWA_KV_HEADS
WINDOW = 128
ROPE_THETA = 500000.0
ROT_DIM = SWA_HD // 4
FFN_HIDDEN = 5632
PROJ_W = 5376
DEPTH = 1
ALPHA = (2.0 * DEPTH) ** 0.25
LN_EPS = 1e-5
NEG_INF = -1e30

LANES = 128
RET_W = RET_HEADS * RET_DK
SWA_Q_COL = 4 * RET_W
SWA_KV_COL = SWA_Q_COL + SWA_HEADS * SWA_HD
VMEM_LIMIT = 56 * 1024 * 1024
VMEM_LIMIT_LARGE = 60 * 1024 * 1024
TOK_TILE = 1024
LN_PROJ_SUBBLOCKS = 2
RET_CHUNK_UNROLL = 16
RET_SAMPLE_EVERY = 2
SAMPLE_TABLE_ROWS = 64
WEIGHT_STAGE_ROWS = 256


def _cparams(sem):
    return pltpu.CompilerParams(dimension_semantics=sem, vmem_limit_bytes=VMEM_LIMIT)


def _layer_norm(x, g, b):
    mu = jnp.mean(x, axis=-1, keepdims=True)
    xc = x - mu
    var = jnp.mean(xc * xc, axis=-1, keepdims=True)
    return xc * lax.rsqrt(var + LN_EPS) * g + b


def _silu(x):
    return x / (1.0 + jnp.exp(-x))


def _dot(a, b):
    return jnp.dot(a, b, preferred_element_type=F32)


def _dot_nt(a, b):
    return lax.dot_general(a, b, (((1,), (1,)), ((), ())), preferred_element_type=F32)


def _dot_tn(a, b):
    return lax.dot_general(a, b, (((0,), (0,)), ((), ())), preferred_element_type=F32)


def _ln_proj_kernel(xp_ref, xs_ref, xm_ref, g_ref, b_ref, w_hbm, z_ref, zm_ref,
                    hm_scr, wb_scr, stage, sem, *, np_tiles, ns_tiles):
    i = pl.program_id(0)
    j = pl.program_id(1)
    first = j == 0
    is_sample = i >= np_tiles
    is_last = i == np_tiles + ns_tiles - 1
    n_col, _, tn = wb_scr.shape
    chunk = stage.shape[1]
    n_chunks = w_hbm.shape[0] // chunk

    def weight_copy(k):
        return pltpu.make_async_copy(w_hbm.at[pl.ds(k * chunk, chunk), :], stage.at[k % 2], sem.at[k % 2])

    @pl.when(first & (i == 0))
    def _():
        weight_copy(0).start()
        for k in range(n_chunks):
            if k + 1 < n_chunks:
                weight_copy(k + 1).start()
            weight_copy(k).wait()
            for jt in range(n_col):
                wb_scr[jt, k * chunk:(k + 1) * chunk, :] = stage[k % 2, :, jt * tn:(jt + 1) * tn].astype(BF16)

    def norm(x):
        return _layer_norm(x, g_ref[...], b_ref[...]).astype(BF16)

    def project(x_ref):
        sub = x_ref.shape[0] // LN_PROJ_SUBBLOCKS
        for r in range(LN_PROJ_SUBBLOCKS):
            rows = slice(r * sub, (r + 1) * sub)
            z_ref[rows, :] = _dot(norm(x_ref[rows, :]), wb_scr[j])

    @pl.when(jnp.logical_not(is_sample))
    def _():
        project(xp_ref)

    @pl.when(is_sample)
    def _():
        project(xs_ref)

    @pl.when(is_last)
    def _():
        lead = hm_scr.shape[0] - xm_ref.shape[0]
        hm_scr[:lead, :] = jnp.zeros((lead, D_MODEL), BF16)
        hm_scr[lead:, :] = norm(xm_ref[...])
        zm_ref[...] = _dot(hm_scr[...], wb_scr[j])


def _ln_proj(xp, xs, xm, g, b, w, tm, tn):
    np_tiles = xp.shape[0] // tm
    ns_tiles = xs.shape[0] // tm
    last = np_tiles + ns_tiles - 1
    n_col = PROJ_W // tn
    assert n_col == 1
    return pl.pallas_call(
        functools.partial(_ln_proj_kernel, np_tiles=np_tiles, ns_tiles=ns_tiles),
        grid=(np_tiles + ns_tiles, n_col),
        in_specs=[
            pl.BlockSpec((tm, D_MODEL), lambda i, j: (jnp.minimum(i, np_tiles - 1), 0)),
            pl.BlockSpec((tm, D_MODEL), lambda i, j: (jnp.clip(i - np_tiles, 0, ns_tiles - 1), 0)),
            pl.BlockSpec(xm.shape, lambda i, j: (0, 0), pipeline_mode=pl.Buffered(1)),
            pl.BlockSpec((1, D_MODEL), lambda i, j: (0, 0)),
            pl.BlockSpec((1, D_MODEL), lambda i, j: (0, 0)),
            pl.BlockSpec(memory_space=pl.ANY),
        ],
        out_specs=[
            pl.BlockSpec((tm, tn), lambda i, j: (i, j)),
            pl.BlockSpec((RET_CHUNK, tn), lambda i, j: (0, jnp.where(i == last, j, 0))),
        ],
        out_shape=[
            jax.ShapeDtypeStruct((xp.shape[0] + xs.shape[0], PROJ_W), F32),
            jax.ShapeDtypeStruct((RET_CHUNK, PROJ_W), F32),
        ],
        scratch_shapes=[
            pltpu.VMEM((RET_CHUNK, D_MODEL), BF16),
            pltpu.VMEM((n_col, D_MODEL, tn), BF16),
            pltpu.VMEM((2, WEIGHT_STAGE_ROWS // 4, PROJ_W), F32),
            pltpu.SemaphoreType.DMA((2,)),
        ],
        compiler_params=pltpu.CompilerParams(dimension_semantics=("arbitrary", "arbitrary"),
                                             vmem_limit_bytes=VMEM_LIMIT_LARGE),
        name="ln_proj",
    )(xp, xs, xm, g, b, w)


def _ret_rope(x, cos, sin):
    x1 = x[:, :LANES]
    x2 = x[:, LANES:]
    return jnp.concatenate([x1 * cos - x2 * sin, x2 * cos + x1 * sin], axis=1)


def _group_norm_gate(o, gate):
    mu = jnp.mean(o, axis=-1, keepdims=True)
    oc = o - mu
    var = jnp.mean(oc * oc, axis=-1, keepdims=True)
    return oc * lax.rsqrt(var + LN_EPS) * _silu(gate)


def _ret_prompt_kernel(lg_ref, q_ref, k_ref, v_ref, g_ref, km_ref, vm_ref,
                       cos_ref, sin_ref, cosm_ref, sinm_ref, o_ref, s_ref, s_scr):
    C = RET_CHUNK
    lg = lg_ref[pl.program_id(1)]
    ri = lax.broadcasted_iota(jnp.int32, (C, C), 0)
    ci = lax.broadcasted_iota(jnp.int32, (C, C), 1)
    rel = (ri - ci).astype(F32)
    decay = jnp.where(rel >= 0.0, jnp.exp(jnp.maximum(rel, 0.0) * lg), 0.0)
    row = lax.broadcasted_iota(jnp.int32, (C, 1), 0).astype(F32)
    q_decay = jnp.exp((row + 1.0) * lg)
    k_decay = jnp.exp((C - 1.0 - row) * lg)
    chunk_decay = jnp.exp(jnp.full((1, RET_DV), C * lg, F32))
    scale = RET_DK ** -0.5

    meta_decay = jnp.where(row >= C - N_META, k_decay, 0.0)
    km = _ret_rope(km_ref[...], cosm_ref[...], sinm_ref[...]) * scale
    s_scr[...] = _dot_tn((km * meta_decay).astype(BF16), vm_ref[...].astype(BF16))

    def chunk(c, carry):
        rows = pl.ds(pl.multiple_of(c * C, C), C)
        cos = cos_ref[rows, :]
        sin = sin_ref[rows, :]
        q = _ret_rope(q_ref[rows, :], cos, sin)
        k = _ret_rope(k_ref[rows, :], cos, sin) * scale
        qb = q.astype(BF16)
        vb = v_ref[rows, :].astype(BF16)
        s_prev = s_scr[...]
        scores = _dot_nt(qb, k.astype(BF16)) * decay
        inner = _dot(scores.astype(BF16), vb)
        cross = _dot(qb, s_prev.astype(BF16)) * q_decay
        s_scr[...] = chunk_decay * s_prev + _dot_tn((k * k_decay).astype(BF16), vb)
        o_ref[rows, :] = _group_norm_gate(inner + cross, g_ref[rows, :]).astype(BF16)
        return carry

    lax.fori_loop(0, q_ref.shape[0] // C, chunk, 0, unroll=RET_CHUNK_UNROLL)
    s_ref[...] = s_scr[...]


def _ret_prompt(lg, z, zm, bsz, seq, cos, sin):
    col = lambda base: (lambda b, h: (b, base + h))
    mcol = lambda base: (lambda b, h: (0, base + h))
    full = lambda b, h: (0, 0)
    meta_tab = lambda b, h: (seq // RET_CHUNK, 0)
    return pl.pallas_call(
        _ret_prompt_kernel,
        grid=(bsz, RET_HEADS),
        in_specs=[
            pl.BlockSpec(memory_space=pltpu.SMEM),
            pl.BlockSpec((seq, RET_DK), col(0)),
            pl.BlockSpec((seq, RET_DK), col(RET_HEADS)),
            pl.BlockSpec((seq, RET_DV), col(2 * RET_HEADS)),
            pl.BlockSpec((seq, RET_DV), col(3 * RET_HEADS)),
            pl.BlockSpec((RET_CHUNK, RET_DK), mcol(RET_HEADS)),
            pl.BlockSpec((RET_CHUNK, RET_DV), mcol(2 * RET_HEADS)),
            pl.BlockSpec((seq, LANES), full),
            pl.BlockSpec((seq, LANES), full),
            pl.BlockSpec((RET_CHUNK, LANES), meta_tab),
            pl.BlockSpec((RET_CHUNK, LANES), meta_tab),
        ],
        out_specs=[
            pl.BlockSpec((seq, RET_DV), lambda b, h: (b, h)),
            pl.BlockSpec((None, None, RET_DK, RET_DV), lambda b, h: (b, h, 0, 0)),
        ],
        out_shape=[
            jax.ShapeDtypeStruct((bsz * seq, RET_W), BF16),
            jax.ShapeDtypeStruct((bsz, RET_HEADS, RET_DK, RET_DV), F32),
        ],
        scratch_shapes=[pltpu.VMEM((RET_DK, RET_DV), F32)],
        compiler_params=_cparams(("parallel", "parallel")),
        name="ret_prompt",
    )(lg, z, z, z, z, zm, zm, cos, sin, cos, sin)


def _ret_sample_jobs(lg_ref, z_ref, s_ref, cos_ref, sin_ref, o_ref, so_ref, *, nb, t):
    rows = nb * t
    R = RET_HEADS * rows
    scale = RET_DK ** -0.5
    cos = cos_ref[...]
    sin = sin_ref[...]

    def stack(base, rope, mul=1.0):
        parts = []
        for h in range(RET_HEADS):
            x = z_ref[:, base + h * RET_DK: base + (h + 1) * RET_DK]
            parts.append(_ret_rope(x, cos, sin) * mul if rope else x)
        return jnp.concatenate(parts, axis=0)

    q = stack(0, True)
    k = stack(RET_W, True, scale)
    v = stack(2 * RET_W, False)
    gate = stack(3 * RET_W, False)

    ri = lax.broadcasted_iota(jnp.int32, (R, R), 0)
    ci = lax.broadcasted_iota(jnp.int32, (R, R), 1)
    rcol = lax.broadcasted_iota(jnp.int32, (R, 1), 0)
    lg_col = jnp.zeros((R, 1), F32)
    for h in range(RET_HEADS):
        lg_col = jnp.where((rcol >= h * rows) & (rcol < (h + 1) * rows), lg_ref[h], lg_col)
    tcol = (rcol & (t - 1)).astype(F32)
    rel = (ri - ci).astype(F32)
    same = ((ri & -t) == (ci & -t)) & (ri >= ci)
    decay = jnp.where(same, jnp.exp(jnp.maximum(rel, 0.0) * lg_col), 0.0)
    q_decay = jnp.exp((tcol + 1.0) * lg_col)
    k_decay = jnp.exp((t - 1.0 - tcol) * lg_col)

    qb = q.astype(BF16)
    vb = v.astype(BF16)
    scores = _dot_nt(qb, k.astype(BF16)) * decay
    inner = _dot(scores.astype(BF16), vb)
    kw = k * k_decay

    cross_parts = [None] * (RET_HEADS * nb)

    def state_job(h, db):
        def run():
            r0 = h * rows + db * t
            s_prev = s_ref[db, h]
            cross_parts[h * nb + db] = _dot(q[r0:r0 + t, :].astype(BF16), s_prev.astype(BF16))
            mine = (rcol >= r0) & (rcol < r0 + t)
            upd = _dot_tn(jnp.where(mine, kw, 0.0).astype(BF16), vb)
            step_decay = jnp.exp(jnp.full((1, RET_DV), t * lg_ref[h], F32))
            so_ref[db, h] = step_decay * s_prev + upd
        return run

    def finish():
        cross = jnp.concatenate(cross_parts, axis=0) * q_decay
        out = _group_norm_gate(inner + cross, gate).astype(BF16)
        for h in range(RET_HEADS):
            o_ref[:, h * RET_DV:(h + 1) * RET_DV] = out[h * rows:(h + 1) * rows, :]

    return [state_job(h, db) for h in range(RET_HEADS) for db in range(nb)], finish


def _swa_rope(x, c, s1, s2):
    return x * c + pltpu.roll(x, 8, 1) * s1 + pltpu.roll(x, LANES - 8, 1) * s2


def _dup_head(x, g, low):
    swapped = pltpu.roll(x, SWA_HD, 1)
    return jnp.where(low, x, swapped) if g == 0 else jnp.where(low, swapped, x)


def _to_kv_half(slab, head, low):
    g = head // SWA_GROUP
    src = slab if head % 2 == g else pltpu.roll(slab, SWA_HD, 1)
    return jnp.where(low, src, 0.0) if g == 0 else jnp.where(low, 0.0, src)


def _from_kv_half(o_even, o_odd, g, low):
    if g == 0:
        return jnp.where(low, o_even, pltpu.roll(o_odd, SWA_HD, 1))
    return jnp.where(low, pltpu.roll(o_even, SWA_HD, 1), o_odd)


def _sink_softmax(logits, sink):
    m = jnp.maximum(jnp.max(logits, axis=-1, keepdims=True), sink)
    p = jnp.exp(logits - m)
    return p, jnp.sum(p, axis=-1, keepdims=True) + jnp.exp(sink - m)


def _swa_prompt_kernel(sink_ref, q_ref, kv_ref, kvm_ref, c_ref, s1_ref, s2_ref,
                       cm_ref, s1m_ref, s2m_ref, o_ref, kp_ref, vp_ref, kprev, vprev):
    W = WINDOW
    m_id = pl.program_id(1)

    @pl.when(m_id == 0)
    def _():
        kprev[...] = _swa_rope(kvm_ref[:, :LANES], cm_ref[...], s1m_ref[...], s2m_ref[...])
        vprev[...] = kvm_ref[:, LANES:]

    c = c_ref[...]
    s1 = s1_ref[...]
    s2 = s2_ref[...]
    kcur = _swa_rope(kv_ref[:, :LANES], c, s1, s2)
    vcur = kv_ref[:, LANES:]
    k2 = jnp.concatenate([kprev[...], kcur], axis=0)
    v2 = jnp.concatenate([vprev[...], vcur], axis=0)

    r = lax.broadcasted_iota(jnp.int32, (W, 2 * W), 0)
    cc = lax.broadcasted_iota(jnp.int32, (W, 2 * W), 1)
    diff = W + r - cc
    mask = (diff >= 0) & (diff < W) & ((m_id > 0) | (cc >= W - N_META))
    low_k = lax.broadcasted_iota(jnp.int32, (2 * W, LANES), 1) < SWA_HD
    low = lax.broadcasted_iota(jnp.int32, (W, LANES), 1) < SWA_HD
    scale = SWA_HD ** -0.5

    def attend(q, kd, vd, h):
        p, den = _sink_softmax(jnp.where(mask, _dot_nt(q.astype(BF16), kd), NEG_INF), sink_ref[h])
        return _dot(p.astype(BF16), vd) / den

    for g in range(SWA_KV_HEADS):
        kd = _dup_head(k2, g, low_k).astype(BF16)
        vd = _dup_head(v2, g, low_k).astype(BF16)
        for p in range(g * SWA_GROUP // 2, (g + 1) * SWA_GROUP // 2):
            slab = _swa_rope(q_ref[:, p * LANES:(p + 1) * LANES], c, s1, s2) * scale
            o_even = attend(jnp.where(low, slab, 0.0), kd, vd, 2 * p)
            o_odd = attend(jnp.where(low, 0.0, slab), kd, vd, 2 * p + 1)
            o_ref[:, p * LANES:(p + 1) * LANES] = jnp.where(low, o_even, o_odd).astype(BF16)

    kprev[...] = kcur
    vprev[...] = vcur

    @pl.when(m_id == pl.num_programs(1) - 1)
    def _():
        kp_ref[...] = kcur.T
        vp_ref[...] = vcur.T


def _swa_prompt_ret_sample_kernel(sink_ref, q_ref, kv_ref, kvm_ref, c_ref, s1_ref, s2_ref, cm_ref, s1m_ref, s2m_ref,
                                 lg_ref, zs_ref, st_ref, cos_ref, sin_ref,
                                 o_ref, kp_ref, vp_ref, os_ref, sto_ref, kprev, vprev, *, nb, t, every):
    _swa_prompt_kernel(sink_ref, q_ref, kv_ref, kvm_ref, c_ref, s1_ref, s2_ref, cm_ref, s1m_ref, s2m_ref,
                       o_ref, kp_ref, vp_ref, kprev, vprev)

    @pl.when(lax.rem(pl.program_id(1), every) == every - 1)
    def _():
        state_jobs, finish = _ret_sample_jobs(lg_ref, zs_ref, st_ref, cos_ref, sin_ref, os_ref, sto_ref, nb=nb, t=t)
        for job in state_jobs + [finish]:
            job()


def _swa_prompt_ret_sample(sinks, z, zm, bsz, seq, c, s1, s2, lg, sample_row, state, cos, sin, t):
    W = WINDOW
    nblk = seq // W
    every = RET_SAMPLE_EVERY
    ret_steps = bsz * nblk // every
    dbs = state.shape[0]
    nb = dbs // ret_steps
    assert nblk % every == 0 and nb * ret_steps == dbs and (nb * t) % 16 == 0
    rows = nb * t
    qw = SWA_HEADS * SWA_HD
    kv_col = SWA_KV_COL // (2 * LANES)
    tab = pl.BlockSpec((W, LANES), lambda b, m: (m, 0))
    mtab = pl.BlockSpec((W, LANES), lambda b, m: (seq // W, 0))
    stab = pl.BlockSpec((rows, LANES), lambda b, m: ((seq + RET_CHUNK) // rows, 0))
    ret_step = lambda b, m: (b * nblk + m) // every
    state_spec = pl.BlockSpec((nb, RET_HEADS, RET_DK, RET_DV), lambda b, m: (ret_step(b, m), 0, 0, 0))
    return pl.pallas_call(
        functools.partial(_swa_prompt_ret_sample_kernel, nb=nb, t=t, every=every),
        grid=(bsz, nblk),
        in_specs=[
            pl.BlockSpec(memory_space=pltpu.SMEM),
            pl.BlockSpec((W, qw), lambda b, m: (b * nblk + m, SWA_Q_COL // qw)),
            pl.BlockSpec((W, 2 * LANES), lambda b, m: (b * nblk + m, kv_col)),
            pl.BlockSpec((W, 2 * LANES), lambda b, m: (0, kv_col)),
            tab, tab, tab, mtab, mtab, mtab,
            pl.BlockSpec(memory_space=pltpu.SMEM),
            pl.BlockSpec((rows, 4 * RET_W), lambda b, m: (sample_row // rows + ret_step(b, m), 0)),
            state_spec, stab, stab,
        ],
        out_specs=[
            pl.BlockSpec((W, qw), lambda b, m: (b * nblk + m, 0)),
            pl.BlockSpec((None, W, LANES), lambda b, m: (b, 0, 0)),
            pl.BlockSpec((None, W, LANES), lambda b, m: (b, 0, 0)),
            pl.BlockSpec((rows, RET_W), lambda b, m: (ret_step(b, m), 0)),
            state_spec,
        ],
        out_shape=[
            jax.ShapeDtypeStruct((bsz * seq, qw), BF16),
            jax.ShapeDtypeStruct((bsz, W, LANES), F32),
            jax.ShapeDtypeStruct((bsz, W, LANES), F32),
            jax.ShapeDtypeStruct((dbs * t, RET_W), BF16),
            jax.ShapeDtypeStruct(state.shape, F32),
        ],
        scratch_shapes=[pltpu.VMEM((W, LANES), F32), pltpu.VMEM((W, LANES), F32)],
        compiler_params=_cparams(("parallel", "arbitrary")),
        name="swa_prompt_ret_sample",
    )(sinks, z, z, zm, c, s1, s2, c, s1, s2, lg, z, state, cos, sin)


def _swa_sample_kernel(sink_ref, q_ref, kv_ref, kc_ref, vc_ref, c_ref, s1_ref, s2_ref,
                       o_ref, ko_ref, vo_ref, *, nb, t):
    W = WINDOW
    c = c_ref[...]
    s1 = s1_ref[...]
    s2 = s2_ref[...]
    row_pad = jnp.zeros((LANES - nb * t, LANES), F32)
    knew_t = jnp.concatenate([_swa_rope(kv_ref[:, :LANES], c, s1, s2), row_pad], axis=0).T
    vnew_t = jnp.concatenate([kv_ref[:, LANES:], row_pad], axis=0).T
    knew_b = knew_t.astype(BF16)
    vnew_b = vnew_t.astype(BF16)

    rq = SWA_HEADS * t
    tq = lax.broadcasted_iota(jnp.int32, (rq, 2 * W), 0) & (t - 1)
    cc = lax.broadcasted_iota(jnp.int32, (rq, 2 * W), 1)
    out_lane = lax.broadcasted_iota(jnp.int32, (LANES, LANES), 1)
    low = lax.broadcasted_iota(jnp.int32, (nb * t, LANES), 1) < SWA_HD
    low_t = lax.broadcasted_iota(jnp.int32, (t, LANES), 1) < SWA_HD
    sink_col = jnp.concatenate([jnp.full((t, 1), sink_ref[h], F32) for h in range(SWA_HEADS)], axis=0)
    npair = SWA_HEADS // 2

    scale = SWA_HD ** -0.5
    slabs = [_swa_rope(q_ref[:, p * LANES:(p + 1) * LANES], c, s1, s2) * scale for p in range(npair)]
    q_heads = [_to_kv_half(slabs[h // 2], h, low) for h in range(SWA_HEADS)]

    logits, values = [], []
    for db in range(nb):
        rows = slice(db * t, (db + 1) * t)
        kc = kc_ref[db]
        vc = vc_ref[db]
        keep = out_lane < W - t
        ko_ref[db] = jnp.where(keep, pltpu.roll(kc, W - t, 1), pltpu.roll(knew_t, (W - t - db * t) % LANES, 1))
        vo_ref[db] = jnp.where(keep, pltpu.roll(vc, W - t, 1), pltpu.roll(vnew_t, (W - t - db * t) % LANES, 1))
        k2 = jnp.concatenate([kc.astype(BF16), knew_b], axis=1)
        values.append(jnp.concatenate([vc.astype(BF16), vnew_b], axis=1))
        t_new = cc - W - db * t
        mask = ((cc < W) & (cc > tq)) | ((t_new >= 0) & (t_new <= tq))
        q_db = jnp.concatenate([qh[rows, :] for qh in q_heads], axis=0).astype(BF16)
        logits.append(jnp.where(mask, _dot(q_db, k2), NEG_INF))
    p, den = _sink_softmax(jnp.concatenate(logits, axis=0), jnp.concatenate([sink_col] * nb, axis=0))
    p = p.astype(BF16)
    outs = [[] for _ in range(npair)]
    for db in range(nb):
        o = _dot_nt(p[db * rq:(db + 1) * rq, :], values[db]) / den[db * rq:(db + 1) * rq, :]
        for pr in range(npair):
            o_even = o[(2 * pr) * t:(2 * pr + 1) * t, :]
            o_odd = o[(2 * pr + 1) * t:(2 * pr + 2) * t, :]
            outs[pr].append(_from_kv_half(o_even, o_odd, 2 * pr // SWA_GROUP, low_t))
    for pr in range(npair):
        o_ref[:, pr * LANES:(pr + 1) * LANES] = jnp.concatenate(outs[pr], axis=0).astype(BF16)


def _swa_sample(sinks, z, row0, kc, vc, c, s1, s2, tab_row0, t):
    dbs = kc.shape[0]
    n = dbs * t
    rows = SAMPLE_TABLE_ROWS
    nb = rows // t
    W = WINDOW
    qw = SWA_HEADS * SWA_HD
    tab = pl.BlockSpec((rows, LANES), lambda i: (tab_row0 // rows, 0))
    cache = pl.BlockSpec((nb, W, LANES), lambda i: (i, 0, 0))
    return pl.pallas_call(
        functools.partial(_swa_sample_kernel, nb=nb, t=t),
        grid=(dbs // nb,),
        in_specs=[
            pl.BlockSpec(memory_space=pltpu.SMEM),
            pl.BlockSpec((rows, qw), lambda i: (row0 // rows + i, SWA_Q_COL // qw)),
            pl.BlockSpec((rows, 2 * LANES), lambda i: (row0 // rows + i, SWA_KV_COL // (2 * LANES))),
            cache, cache, tab, tab, tab,
        ],
        out_specs=[pl.BlockSpec((rows, qw), lambda i: (i, 0)), cache, cache],
        out_shape=[
            jax.ShapeDtypeStruct((n, qw), BF16),
            jax.ShapeDtypeStruct((dbs, W, LANES), F32),
            jax.ShapeDtypeStruct((dbs, W, LANES), F32),
        ],
        compiler_params=_cparams(("parallel",)),
        name="swa_sample",
    )(sinks, z, z, kc, vc, c, s1, s2)


def _out_proj_ln_kernel(rop_ref, sop_ref, xp_ref, ros_ref, sos_ref, xs_ref, eg_ref, eb_ref, w_hbm, mg_ref, mb_ref,
                        h_ref, hb_ref, wb_scr, stage, sem, *, np_tiles):
    i = pl.program_id(0)
    chunk = stage.shape[1]
    n_chunks = w_hbm.shape[0] // chunk

    def weight_copy(k):
        return pltpu.make_async_copy(w_hbm.at[pl.ds(k * chunk, chunk), :], stage.at[k % 2], sem.at[k % 2])

    @pl.when(i == 0)
    def _():
        weight_copy(0).start()
        for k in range(n_chunks):
            if k + 1 < n_chunks:
                weight_copy(k + 1).start()
            weight_copy(k).wait()
            wb_scr[k * chunk:(k + 1) * chunk, :] = stage[k % 2].astype(BF16)

    def step(ro_ref, so_ref, x_ref):
        sub = h_ref.shape[0] // 2
        for r in range(2):
            rows = slice(r * sub, (r + 1) * sub)
            x_in = _layer_norm(x_ref[rows, :], eg_ref[...], eb_ref[...])
            mixed = _dot(ro_ref[rows, :], wb_scr[:RET_W, :]) + _dot(so_ref[rows, :], wb_scr[RET_W:, :])
            h = _layer_norm(ALPHA * x_in + mixed, mg_ref[...], mb_ref[...])
            h_ref[rows, :] = h
            hb_ref[rows, :] = h.astype(BF16)

    @pl.when(i < np_tiles)
    def _():
        step(rop_ref, sop_ref, xp_ref)

    @pl.when(i >= np_tiles)
    def _():
        step(ros_ref, sos_ref, xs_ref)


def _out_proj_ln(ro_p, so_p, xp, ro_s, so_s, xs, eg, eb, w, mg, mb, tm):
    np_tiles = xp.shape[0] // tm
    ns_tiles = xs.shape[0] // tm
    n = xp.shape[0] + xs.shape[0]
    vec = pl.BlockSpec((1, D_MODEL), lambda i: (0, 0))
    tile = pl.BlockSpec((tm, D_MODEL), lambda i: (i, 0))
    p_idx = lambda i: (jnp.minimum(i, np_tiles - 1), 0)
    s_idx = lambda i: (jnp.clip(i - np_tiles, 0, ns_tiles - 1), 0)
    return pl.pallas_call(
        functools.partial(_out_proj_ln_kernel, np_tiles=np_tiles),
        grid=(np_tiles + ns_tiles,),
        in_specs=[
            pl.BlockSpec((tm, RET_W), p_idx), pl.BlockSpec((tm, RET_W), p_idx), pl.BlockSpec((tm, D_MODEL), p_idx),
            pl.BlockSpec((tm, RET_W), s_idx), pl.BlockSpec((tm, RET_W), s_idx), pl.BlockSpec((tm, D_MODEL), s_idx),
            vec, vec,
            pl.BlockSpec(memory_space=pl.ANY),
            vec, vec,
        ],
        out_specs=[tile, tile],
        out_shape=[jax.ShapeDtypeStruct((n, D_MODEL), F32), jax.ShapeDtypeStruct((n, D_MODEL), BF16)],
        scratch_shapes=[pltpu.VMEM((D_MODEL, D_MODEL), BF16),
                        pltpu.VMEM((2, WEIGHT_STAGE_ROWS, D_MODEL), F32),
                        pltpu.SemaphoreType.DMA((2,))],
        compiler_params=pltpu.CompilerParams(dimension_semantics=("arbitrary",), vmem_limit_bytes=VMEM_LIMIT_LARGE),
        name="out_proj_ln",
    )(ro_p, so_p, xp, ro_s, so_s, xs, eg, eb, w, mg, mb)


def _ffn_ln_kernel(hb_ref, h_hbm, wg_ref, wu_ref, wd_ref, g_ref, b_ref, y_ref, *rest, row0):
    *w_out_refs, h_res, sem = rest
    i = pl.program_id(0)
    j = pl.program_id(1)
    tm = h_res.shape[0]
    residual_copy = pltpu.make_async_copy(h_hbm.at[pl.ds(row0 + i * tm, tm), :], h_res, sem)

    @pl.when(j == 0)
    def _():
        residual_copy.start()
        y_ref[...] = jnp.zeros_like(y_ref)

    wg, wu, wd = (w[...].astype(BF16) for w in (wg_ref, wu_ref, wd_ref))
    for out_ref, w in zip(w_out_refs, (wg, wu, wd)):
        out_ref[...] = w
    hb = hb_ref[...]
    act = _silu(_dot(hb, wg)) * _dot(hb, wu)
    y_ref[...] += _dot(act.astype(BF16), wd)

    @pl.when(j == pl.num_programs(1) - 1)
    def _():
        residual_copy.wait()
        y_ref[...] = _layer_norm(ALPHA * h_res[...] + y_ref[...], g_ref[...], b_ref[...])


def _ffn_ln(h, hb, row0, n, wg, wu, wd, g, b, tm, th):
    emit_weights = wg.dtype == F32
    assert not emit_weights or n == tm
    vec = pl.BlockSpec((1, D_MODEL), lambda i, j: (0, 0))
    w_specs = [
        pl.BlockSpec((D_MODEL, th), lambda i, j: (0, j)),
        pl.BlockSpec((D_MODEL, th), lambda i, j: (0, j)),
        pl.BlockSpec((th, D_MODEL), lambda i, j: (j, 0)),
    ]
    out_specs = [pl.BlockSpec((tm, D_MODEL), lambda i, j: (i, 0))]
    out_shape = [jax.ShapeDtypeStruct((n, D_MODEL), F32)]
    if emit_weights:
        out_specs += w_specs
        out_shape += [jax.ShapeDtypeStruct(w.shape, BF16) for w in (wg, wu, wd)]
    outs = pl.pallas_call(
        functools.partial(_ffn_ln_kernel, row0=row0),
        grid=(n // tm, FFN_HIDDEN // th),
        in_specs=[pl.BlockSpec((tm, D_MODEL), lambda i, j: (row0 // tm + i, 0)), pl.BlockSpec(memory_space=pl.ANY)]
        + w_specs + [vec, vec],
        out_specs=out_specs,
        out_shape=out_shape,
        scratch_shapes=[pltpu.VMEM((tm, D_MODEL), F32), pltpu.SemaphoreType.DMA(())],
        compiler_params=pltpu.CompilerParams(dimension_semantics=("arbitrary", "arbitrary"),
                                             vmem_limit_bytes=VMEM_LIMIT_LARGE),
        name="ffn_ln",
    )(hb, h, wg, wu, wd, g, b)
    return tuple(outs) if emit_weights else outs[0]


def _position_tables(seq, t, past_len):
    r = jnp.arange(seq + RET_CHUNK + SAMPLE_TABLE_ROWS)
    meta_r = r - seq
    pos = jnp.where(r < seq, N_META + r,
                    jnp.where(meta_r < RET_CHUNK, jnp.maximum(meta_r - (RET_CHUNK - N_META), 0),
                              past_len + (meta_r - RET_CHUNK) % t)).astype(F32)[:, None]
    ret_freq = jnp.power(RET_THETA, -jnp.linspace(0.0, 1.0, RET_DK // 2, dtype=F32))
    ret_ang = pos * ret_freq[None, :]
    half = ROT_DIM // 2
    swa_freq = jnp.power(ROPE_THETA, -jnp.arange(0, ROT_DIM, 2, dtype=F32) / ROT_DIM)
    swa_ang = pos * swa_freq[None, :]
    cos = jnp.tile(jnp.cos(swa_ang), (1, LANES // half))
    sin = jnp.tile(jnp.sin(swa_ang), (1, LANES // half))
    d = jnp.arange(LANES) % SWA_HD
    c = jnp.where(d < ROT_DIM, cos, 1.0)
    s1 = jnp.where((d >= half) & (d < ROT_DIM), sin, 0.0)
    s2 = jnp.where(d < half, -sin, 0.0)
    return (jnp.cos(ret_ang), jnp.sin(ret_ang)), (c, s1, s2)


def kernel(x_prompt, x_sample, state_ret, cache_swa_k, cache_swa_v, meta_tokens, ln_emb_g, ln_emb_b,
           w_in, w_out, swa_sinks, ln_mix_g, ln_mix_b, w_ffn_gate, w_ffn_up, w_ffn_down, ln_ffn_g, ln_ffn_b):
    bsz, seq, d = x_prompt.shape
    dbs, t, _ = x_sample.shape
    assert w_in.shape[0] == DEPTH and d == D_MODEL and seq % RET_CHUNK == 0 and t & (t - 1) == 0
    past_len = 16384
    row = lambda a: a.reshape(1, -1)

    eg, eb = row(ln_emb_g), row(ln_emb_b)
    sinks = swa_sinks[0]
    lg = jnp.log(1.0 - jnp.power(2.0, -5.0 - jnp.arange(RET_HEADS, dtype=F32)))

    xp = x_prompt.reshape(bsz * seq, d)
    xs = x_sample.reshape(dbs * t, d)

    z, zm = _ln_proj(xp, xs, meta_tokens, eg, eb, w_in[0], 256, PROJ_W)
    sample_row = bsz * seq

    (cos, sin), swa_tabs = _position_tables(seq, t, past_len)
    ret_o_p, ret_state_p = _ret_prompt(lg, z, zm, bsz, seq, cos, sin)
    swa_o_p, k_p, v_p, ret_o_s, ret_state_s = _swa_prompt_ret_sample(
        sinks, z, zm, bsz, seq, *swa_tabs, lg, sample_row, state_ret[0], cos, sin, t)
    to_kernel = lambda a: jnp.transpose(a[0], (0, 2, 3, 1)).reshape(-1, LANES, WINDOW)
    from_kernel = lambda a: jnp.transpose(a.reshape(-1, SWA_KV_HEADS, SWA_HD, WINDOW), (0, 3, 1, 2))[None]
    kc, vc = to_kernel(cache_swa_k), to_kernel(cache_swa_v)
    swa_o_s, k_s, v_s = _swa_sample(sinks, z, sample_row, kc, vc, *swa_tabs, seq + RET_CHUNK, t)

    mg, mb = row(ln_mix_g[0]), row(ln_mix_b[0])
    fg, fb = row(ln_ffn_g[0]), row(ln_ffn_b[0])
    ffn_w = (w_ffn_gate[0], w_ffn_up[0], w_ffn_down[0])
    h, hb = _out_proj_ln(ret_o_p, swa_o_p, xp, ret_o_s, swa_o_s, xs, eg, eb, w_out[0], mg, mb, 512)
    y_s, *ffn_wb = _ffn_ln(h, hb, sample_row, dbs * t, *ffn_w, fg, fb, TOK_TILE, 256)
    y_p = _ffn_ln(h, hb, 0, bsz * seq, *ffn_wb, fg, fb, TOK_TILE, 512)

    return (y_p.reshape(bsz, seq, d), y_s.reshape(dbs, t, d),
            ret_state_p[None], from_kernel(k_p), from_kernel(v_p),
            ret_state_s[None], from_kernel(k_s), from_kernel(v_s))
```

```python
import functools

import jax
import jax.numpy as jnp
from jax import lax
from jax.experimental import pallas as pl
from jax.experimental.pallas import tpu as pltpu

F32 = jnp.float32
BF16 = jnp.bfloat16

D_MODEL = 2048
N_META = 16
RET_HEADS = 4
RET_DK = 256
RET_DV = 256
RET_CHUNK = 128
RET_THETA = 10000.0
SWA_HD = 64
SWA_HEADS = 16
SWA_KV_HEADS = 2
SWA_GROUP = SWA_HEADS // SWA_KV_HEADS
WINDOW = 128
ROPE_THETA = 500000.0
ROT_DIM = SWA_HD // 4
FFN_HIDDEN = 5632
PROJ_W = 5376
DEPTH = 1
ALPHA = (2.0 * DEPTH) ** 0.25
LN_EPS = 1e-5
NEG_INF = -1e30

LANES = 128
RET_W = RET_HEADS * RET_DK
SWA_Q_COL = 4 * RET_W
SWA_KV_COL = SWA_Q_COL + SWA_HEADS * SWA_HD
VMEM_LIMIT = 56 * 1024 * 1024
VMEM_LIMIT_LARGE = 60 * 1024 * 1024
TOK_TILE = 1024
LN_PROJ_SUBBLOCKS = 2
RET_CHUNK_UNROLL = 16
RET_SAMPLE_EVERY = 2
LN_PROJ_TILE = 256
SWA_SAMPLE_BATCHES = 8
WEIGHT_STAGE_ROWS = 256


def _cparams(sem):
    return pltpu.CompilerParams(dimension_semantics=sem, vmem_limit_bytes=VMEM_LIMIT)


def _layer_norm(x, g, b):
    mu = jnp.mean(x, axis=-1, keepdims=True)
    xc = x - mu
    var = jnp.mean(xc * xc, axis=-1, keepdims=True)
    return xc * lax.rsqrt(var + LN_EPS) * g + b


def _silu(x):
    return x / (1.0 + jnp.exp(-x))


def _dot(a, b):
    return jnp.dot(a, b, preferred_element_type=F32)


def _dot_nt(a, b):
    return lax.dot_general(a, b, (((1,), (1,)), ((), ())), preferred_element_type=F32)


def _dot_tn(a, b):
    return lax.dot_general(a, b, (((0,), (0,)), ((), ())), preferred_element_type=F32)


def _store_rotated(z_ref, rows, res, tab):
    cos, sin, c, s1, s2 = (tab[:, n * LANES:(n + 1) * LANES] for n in range(5))
    for c0 in range(0, 2 * RET_W, RET_DK):
        z_ref[rows, c0:c0 + RET_DK] = _ret_rope(res[:, c0:c0 + RET_DK], cos, sin)
    z_ref[rows, 2 * RET_W:SWA_Q_COL] = res[:, 2 * RET_W:SWA_Q_COL]
    for c0 in range(SWA_Q_COL, SWA_KV_COL + LANES, LANES):
        z_ref[rows, c0:c0 + LANES] = _swa_rope(res[:, c0:c0 + LANES], c, s1, s2)
    z_ref[rows, SWA_KV_COL + LANES:] = res[:, SWA_KV_COL + LANES:]


def _ln_proj_kernel(xp_ref, xs_ref, xm_ref, g_ref, b_ref, w_hbm, tabp_ref, tabs_ref, tabm_ref, z_ref, zm_ref,
                    hm_scr, wb_scr, stage, sem, *, np_tiles, ns_tiles):
    i = pl.program_id(0)
    j = pl.program_id(1)
    first = j == 0
    is_sample = i >= np_tiles
    is_last = i == np_tiles + ns_tiles - 1
    n_col, _, tn = wb_scr.shape
    chunk = stage.shape[1]
    n_chunks = w_hbm.shape[0] // chunk

    def weight_copy(k):
        return pltpu.make_async_copy(w_hbm.at[pl.ds(k * chunk, chunk), :], stage.at[k % 2], sem.at[k % 2])

    @pl.when(first & (i == 0))
    def _():
        weight_copy(0).start()
        for k in range(n_chunks):
            if k + 1 < n_chunks:
                weight_copy(k + 1).start()
            weight_copy(k).wait()
            for jt in range(n_col):
                wb_scr[jt, k * chunk:(k + 1) * chunk, :] = stage[k % 2, :, jt * tn:(jt + 1) * tn].astype(BF16)

    def norm(x):
        return _layer_norm(x, g_ref[...], b_ref[...]).astype(BF16)

    def project(x_ref, tab_ref):
        sub = x_ref.shape[0] // LN_PROJ_SUBBLOCKS
        for r in range(LN_PROJ_SUBBLOCKS):
            rows = slice(r * sub, (r + 1) * sub)
            _store_rotated(z_ref, rows, _dot(norm(x_ref[rows, :]), wb_scr[j]), tab_ref[rows, :])

    @pl.when(jnp.logical_not(is_sample))
    def _():
        project(xp_ref, tabp_ref)

    @pl.when(is_sample)
    def _():
        project(xs_ref, tabs_ref)

    @pl.when(is_last)
    def _():
        lead = hm_scr.shape[0] - xm_ref.shape[0]
        hm_scr[:lead, :] = jnp.zeros((lead, D_MODEL), BF16)
        hm_scr[lead:, :] = norm(xm_ref[...])
        _store_rotated(zm_ref, slice(None), _dot(hm_scr[...], wb_scr[j]), tabm_ref[...])


def _ln_proj(xp, xs, xm, g, b, w, tabs, seq, tm, tn):
    np_tiles = xp.shape[0] // tm
    ns_tiles = xs.shape[0] // tm
    last = np_tiles + ns_tiles - 1
    n_col = PROJ_W // tn
    assert n_col == 1
    tab_w = tabs.shape[1]
    return pl.pallas_call(
        functools.partial(_ln_proj_kernel, np_tiles=np_tiles, ns_tiles=ns_tiles),
        grid=(np_tiles + ns_tiles, n_col),
        in_specs=[
            pl.BlockSpec((tm, D_MODEL), lambda i, j: (jnp.minimum(i, np_tiles - 1), 0)),
            pl.BlockSpec((tm, D_MODEL), lambda i, j: (jnp.clip(i - np_tiles, 0, ns_tiles - 1), 0)),
            pl.BlockSpec(xm.shape, lambda i, j: (0, 0), pipeline_mode=pl.Buffered(1)),
            pl.BlockSpec((1, D_MODEL), lambda i, j: (0, 0)),
            pl.BlockSpec((1, D_MODEL), lambda i, j: (0, 0)),
            pl.BlockSpec(memory_space=pl.ANY),
            pl.BlockSpec((tm, tab_w), lambda i, j: (lax.rem(i, seq // tm), 0)),
            pl.BlockSpec((tm, tab_w), lambda i, j: (seq // tm, 0)),
            pl.BlockSpec((RET_CHUNK, tab_w), lambda i, j: ((seq + tm) // RET_CHUNK, 0)),
        ],
        out_specs=[
            pl.BlockSpec((tm, tn), lambda i, j: (i, j)),
            pl.BlockSpec((RET_CHUNK, tn), lambda i, j: (0, jnp.where(i == last, j, 0))),
        ],
        out_shape=[
            jax.ShapeDtypeStruct((xp.shape[0] + xs.shape[0], PROJ_W), F32),
            jax.ShapeDtypeStruct((RET_CHUNK, PROJ_W), F32),
        ],
        scratch_shapes=[
            pltpu.VMEM((RET_CHUNK, D_MODEL), BF16),
            pltpu.VMEM((n_col, D_MODEL, tn), BF16),
            pltpu.VMEM((2, WEIGHT_STAGE_ROWS // 4, PROJ_W), F32),
            pltpu.SemaphoreType.DMA((2,)),
        ],
        compiler_params=pltpu.CompilerParams(dimension_semantics=("arbitrary", "arbitrary"),
                                             vmem_limit_bytes=VMEM_LIMIT_LARGE),
        name="ln_proj",
    )(xp, xs, xm, g, b, w, tabs, tabs, tabs)


def _ret_rope(x, cos, sin):
    x1 = x[:, :LANES]
    x2 = x[:, LANES:]
    return jnp.concatenate([x1 * cos - x2 * sin, x2 * cos + x1 * sin], axis=1)


def _group_norm_gate(o, gate):
    mu = jnp.mean(o, axis=-1, keepdims=True)
    oc = o - mu
    var = jnp.mean(oc * oc, axis=-1, keepdims=True)
    return oc * lax.rsqrt(var + LN_EPS) * _silu(gate)


def _ret_prompt_kernel(lg_ref, q_ref, k_ref, v_ref, g_ref, km_ref, vm_ref, o_ref, s_ref, s_scr):
    C = RET_CHUNK
    lg = lg_ref[pl.program_id(1)]
    ri = lax.broadcasted_iota(jnp.int32, (C, C), 0)
    ci = lax.broadcasted_iota(jnp.int32, (C, C), 1)
    rel = (ri - ci).astype(F32)
    decay = jnp.where(rel >= 0.0, jnp.exp(jnp.maximum(rel, 0.0) * lg), 0.0)
    row = lax.broadcasted_iota(jnp.int32, (C, 1), 0).astype(F32)
    q_decay = jnp.exp((row + 1.0) * lg)
    k_decay = jnp.exp((C - 1.0 - row) * lg)
    chunk_decay = jnp.exp(jnp.full((1, RET_DV), C * lg, F32))
    scale = RET_DK ** -0.5

    meta_decay = jnp.where(row >= C - N_META, k_decay, 0.0)
    km = km_ref[...] * scale
    s_scr[...] = _dot_tn((km * meta_decay).astype(BF16), vm_ref[...].astype(BF16))

    def chunk(c, carry):
        rows = pl.ds(pl.multiple_of(c * C, C), C)
        k = k_ref[rows, :] * scale
        qb = q_ref[rows, :].astype(BF16)
        vb = v_ref[rows, :].astype(BF16)
        s_prev = s_scr[...]
        scores = _dot_nt(qb, k.astype(BF16)) * decay
        inner = _dot(scores.astype(BF16), vb)
        cross = _dot(qb, s_prev.astype(BF16)) * q_decay
        s_scr[...] = chunk_decay * s_prev + _dot_tn((k * k_decay).astype(BF16), vb)
        o_ref[rows, :] = _group_norm_gate(inner + cross, g_ref[rows, :]).astype(BF16)
        return carry

    lax.fori_loop(0, q_ref.shape[0] // C, chunk, 0, unroll=RET_CHUNK_UNROLL)
    s_ref[...] = s_scr[...]


def _ret_prompt(lg, z, zm, bsz, seq):
    col = lambda base: (lambda b, h: (b, base + h))
    mcol = lambda base: (lambda b, h: (0, base + h))
    return pl.pallas_call(
        _ret_prompt_kernel,
        grid=(bsz, RET_HEADS),
        in_specs=[
            pl.BlockSpec(memory_space=pltpu.SMEM),
            pl.BlockSpec((seq, RET_DK), col(0)),
            pl.BlockSpec((seq, RET_DK), col(RET_HEADS)),
            pl.BlockSpec((seq, RET_DV), col(2 * RET_HEADS)),
            pl.BlockSpec((seq, RET_DV), col(3 * RET_HEADS)),
            pl.BlockSpec((RET_CHUNK, RET_DK), mcol(RET_HEADS)),
            pl.BlockSpec((RET_CHUNK, RET_DV), mcol(2 * RET_HEADS)),
        ],
        out_specs=[
            pl.BlockSpec((seq, RET_DV), lambda b, h: (b, h)),
            pl.BlockSpec((None, None, RET_DK, RET_DV), lambda b, h: (b, h, 0, 0)),
        ],
        out_shape=[
            jax.ShapeDtypeStruct((bsz * seq, RET_W), BF16),
            jax.ShapeDtypeStruct((bsz, RET_HEADS, RET_DK, RET_DV), F32),
        ],
        scratch_shapes=[pltpu.VMEM((RET_DK, RET_DV), F32)],
        compiler_params=_cparams(("parallel", "parallel")),
        name="ret_prompt",
    )(lg, z, z, z, z, zm, zm)


def _ret_sample_jobs(lg_ref, z_ref, s_ref, o_ref, so_ref, *, nb, t):
    rows = nb * t
    R = RET_HEADS * rows
    scale = RET_DK ** -0.5

    def stack(base):
        return jnp.concatenate([z_ref[:, base + h * RET_DK: base + (h + 1) * RET_DK] for h in range(RET_HEADS)],
                               axis=0)

    q = stack(0)
    k = stack(RET_W) * scale
    v = stack(2 * RET_W)
    gate = stack(3 * RET_W)

    ri = lax.broadcasted_iota(jnp.int32, (R, R), 0)
    ci = lax.broadcasted_iota(jnp.int32, (R, R), 1)
    rcol = lax.broadcasted_iota(jnp.int32, (R, 1), 0)
    lg_col = jnp.zeros((R, 1), F32)
    for h in range(RET_HEADS):
        lg_col = jnp.where((rcol >= h * rows) & (rcol < (h + 1) * rows), lg_ref[h], lg_col)
    tcol = (rcol & (t - 1)).astype(F32)
    rel = (ri - ci).astype(F32)
    same = ((ri & -t) == (ci & -t)) & (ri >= ci)
    decay = jnp.where(same, jnp.exp(jnp.maximum(rel, 0.0) * lg_col), 0.0)
    q_decay = jnp.exp((tcol + 1.0) * lg_col)
    k_decay = jnp.exp((t - 1.0 - tcol) * lg_col)

    qb = q.astype(BF16)
    vb = v.astype(BF16)
    scores = _dot_nt(qb, k.astype(BF16)) * decay
    inner = _dot(scores.astype(BF16), vb)
    kw = k * k_decay

    cross_parts = [None] * (RET_HEADS * nb)

    def state_job(h, db):
        def run():
            r0 = h * rows + db * t
            s_prev = s_ref[db, h]
            cross_parts[h * nb + db] = _dot(q[r0:r0 + t, :].astype(BF16), s_prev.astype(BF16))
            mine = (rcol >= r0) & (rcol < r0 + t)
            upd = _dot_tn(jnp.where(mine, kw, 0.0).astype(BF16), vb)
            step_decay = jnp.exp(jnp.full((1, RET_DV), t * lg_ref[h], F32))
            so_ref[db, h] = step_decay * s_prev + upd
        return run

    def finish():
        cross = jnp.concatenate(cross_parts, axis=0) * q_decay
        out = _group_norm_gate(inner + cross, gate).astype(BF16)
        for h in range(RET_HEADS):
            o_ref[:, h * RET_DV:(h + 1) * RET_DV] = out[h * rows:(h + 1) * rows, :]

    return [state_job(h, db) for h in range(RET_HEADS) for db in range(nb)], finish


def _swa_rope(x, c, s1, s2):
    return x * c + pltpu.roll(x, 8, 1) * s1 + pltpu.roll(x, LANES - 8, 1) * s2


def _dup_head(x, g, low):
    swapped = pltpu.roll(x, SWA_HD, 1)
    return jnp.where(low, x, swapped) if g == 0 else jnp.where(low, swapped, x)


def _to_kv_half(slab, head, low):
    g = head // SWA_GROUP
    src = slab if head % 2 == g else pltpu.roll(slab, SWA_HD, 1)
    return jnp.where(low, src, 0.0) if g == 0 else jnp.where(low, 0.0, src)


def _from_kv_half(o_even, o_odd, g, low):
    if g == 0:
        return jnp.where(low, o_even, pltpu.roll(o_odd, SWA_HD, 1))
    return jnp.where(low, pltpu.roll(o_even, SWA_HD, 1), o_odd)


def _sink_softmax(logits, sink):
    m = jnp.maximum(jnp.max(logits, axis=-1, keepdims=True), sink)
    p = jnp.exp(logits - m)
    return p, jnp.sum(p, axis=-1, keepdims=True) + jnp.exp(sink - m)


def _swa_prompt_kernel(sink_ref, q_ref, kv_ref, kvm_ref, o_ref, kp_ref, vp_ref, kprev, vprev):
    W = WINDOW
    m_id = pl.program_id(1)

    @pl.when(m_id == 0)
    def _():
        kprev[...] = kvm_ref[:, :LANES]
        vprev[...] = kvm_ref[:, LANES:]

    kcur = kv_ref[:, :LANES]
    vcur = kv_ref[:, LANES:]
    k2 = jnp.concatenate([kprev[...], kcur], axis=0)
    v2 = jnp.concatenate([vprev[...], vcur], axis=0)

    r = lax.broadcasted_iota(jnp.int32, (W, 2 * W), 0)
    cc = lax.broadcasted_iota(jnp.int32, (W, 2 * W), 1)
    diff = W + r - cc
    mask = (diff >= 0) & (diff < W) & ((m_id > 0) | (cc >= W - N_META))
    low_k = lax.broadcasted_iota(jnp.int32, (2 * W, LANES), 1) < SWA_HD
    low = lax.broadcasted_iota(jnp.int32, (W, LANES), 1) < SWA_HD
    scale = SWA_HD ** -0.5

    def attend(q, kd, vd, h):
        p, den = _sink_softmax(jnp.where(mask, _dot_nt(q.astype(BF16), kd), NEG_INF), sink_ref[h])
        return _dot(p.astype(BF16), vd) / den

    for g in range(SWA_KV_HEADS):
        kd = _dup_head(k2, g, low_k).astype(BF16)
        vd = _dup_head(v2, g, low_k).astype(BF16)
        for p in range(g * SWA_GROUP // 2, (g + 1) * SWA_GROUP // 2):
            slab = q_ref[:, p * LANES:(p + 1) * LANES] * scale
            o_even = attend(jnp.where(low, slab, 0.0), kd, vd, 2 * p)
            o_odd = attend(jnp.where(low, 0.0, slab), kd, vd, 2 * p + 1)
            o_ref[:, p * LANES:(p + 1) * LANES] = jnp.where(low, o_even, o_odd).astype(BF16)

    kprev[...] = kcur
    vprev[...] = vcur

    @pl.when(m_id == pl.num_programs(1) - 1)
    def _():
        kp_ref[...] = kcur.T
        vp_ref[...] = vcur.T


def _swa_prompt_ret_sample_kernel(sink_ref, q_ref, kv_ref, kvm_ref, lg_ref, zs_ref, st_ref,
                                 o_ref, kp_ref, vp_ref, os_ref, sto_ref, kprev, vprev, *, nb, t, every):
    _swa_prompt_kernel(sink_ref, q_ref, kv_ref, kvm_ref, o_ref, kp_ref, vp_ref, kprev, vprev)

    @pl.when(lax.rem(pl.program_id(1), every) == every - 1)
    def _():
        state_jobs, finish = _ret_sample_jobs(lg_ref, zs_ref, st_ref, os_ref, sto_ref, nb=nb, t=t)
        for job in state_jobs + [finish]:
            job()


def _swa_prompt_ret_sample(sinks, z, zm, bsz, seq, lg, sample_row, state, t):
    W = WINDOW
    nblk = seq // W
    every = RET_SAMPLE_EVERY
    ret_steps = bsz * nblk // every
    dbs = state.shape[0]
    nb = dbs // ret_steps
    assert nblk % every == 0 and nb * ret_steps == dbs and (nb * t) % 16 == 0
    rows = nb * t
    qw = SWA_HEADS * SWA_HD
    kv_col = SWA_KV_COL // (2 * LANES)
    ret_step =lambda b, m: (b * nblk + m) // every
    state_spec = pl.BlockSpec((nb, RET_HEADS, RET_DK, RET_DV), lambda b, m: (ret_step(b, m), 0, 0, 0))
    return pl.pallas_call(
        functools.partial(_swa_prompt_ret_sample_kernel, nb=nb, t=t, every=every),
        grid=(bsz, nblk),
        in_specs=[
            pl.BlockSpec(memory_space=pltpu.SMEM),
            pl.BlockSpec((W, qw), lambda b, m: (b * nblk + m, SWA_Q_COL // qw)),
            pl.BlockSpec((W, 2 * LANES), lambda b, m: (b * nblk + m, kv_col)),
            pl.BlockSpec((W, 2 * LANES), lambda b, m: (0, kv_col)),
            pl.BlockSpec(memory_space=pltpu.SMEM),
            pl.BlockSpec((rows, 4 * RET_W), lambda b, m: (sample_row // rows + ret_step(b, m), 0)),
            state_spec,
        ],
        out_specs=[
            pl.BlockSpec((W, qw), lambda b, m: (b * nblk + m, 0)),
            pl.BlockSpec((None, W, LANES), lambda b, m: (b, 0, 0)),
            pl.BlockSpec((None, W, LANES), lambda b, m: (b, 0, 0)),
            pl.BlockSpec((rows, RET_W), lambda b, m: (ret_step(b, m), 0)),
            state_spec,
        ],
        out_shape=[
            jax.ShapeDtypeStruct((bsz * seq, qw), BF16),
            jax.ShapeDtypeStruct((bsz, W, LANES), F32),
            jax.ShapeDtypeStruct((bsz, W, LANES), F32),
            jax.ShapeDtypeStruct((dbs * t, RET_W), BF16),
            jax.ShapeDtypeStruct(state.shape, F32),
        ],
        scratch_shapes=[pltpu.VMEM((W, LANES), F32), pltpu.VMEM((W, LANES), F32)],
        compiler_params=_cparams(("parallel", "arbitrary")),
        name="swa_prompt_ret_sample",
    )(sinks, z, z, zm, lg, z, state)


def _swa_sample_kernel(sink_ref, q_ref, kv_ref, kc_ref, vc_ref, o_ref, ko_ref, vo_ref, *, nb, t):
    W = WINDOW
    row_pad = jnp.zeros((LANES - nb * t, LANES), F32)
    knew_t = jnp.concatenate([kv_ref[:, :LANES], row_pad], axis=0).T
    vnew_t = jnp.concatenate([kv_ref[:, LANES:], row_pad], axis=0).T
    knew_b = knew_t.astype(BF16)
    vnew_b = vnew_t.astype(BF16)

    rq = SWA_HEADS * t
    tq = lax.broadcasted_iota(jnp.int32, (rq, 2 * W), 0) & (t - 1)
    cc = lax.broadcasted_iota(jnp.int32, (rq, 2 * W), 1)
    out_lane = lax.broadcasted_iota(jnp.int32, (LANES, LANES), 1)
    low = lax.broadcasted_iota(jnp.int32, (nb * t, LANES), 1) < SWA_HD
    low_t = lax.broadcasted_iota(jnp.int32, (t, LANES), 1) < SWA_HD
    sink_col = jnp.concatenate([jnp.full((t, 1), sink_ref[h], F32) for h in range(SWA_HEADS)], axis=0)
    npair = SWA_HEADS // 2

    scale = SWA_HD ** -0.5
    slabs = [q_ref[:, p * LANES:(p + 1) * LANES] * scale for p in range(npair)]
    q_heads = [_to_kv_half(slabs[h // 2], h, low) for h in range(SWA_HEADS)]

    logits, values = [], []
    for db in range(nb):
        rows = slice(db * t, (db + 1) * t)
        kc = kc_ref[db]
        vc = vc_ref[db]
        keep = out_lane < W - t
        ko_ref[db] = jnp.where(keep, pltpu.roll(kc, W - t, 1), pltpu.roll(knew_t, (W - t - db * t) % LANES, 1))
        vo_ref[db] = jnp.where(keep, pltpu.roll(vc, W - t, 1), pltpu.roll(vnew_t, (W - t - db * t) % LANES, 1))
        k2 = jnp.concatenate([kc.astype(BF16), knew_b], axis=1)
        values.append(jnp.concatenate([vc.astype(BF16), vnew_b], axis=1))
        t_new = cc - W - db * t
        mask = ((cc < W) & (cc > tq)) | ((t_new >= 0) & (t_new <= tq))
        q_db = jnp.concatenate([qh[rows, :] for qh in q_heads], axis=0).astype(BF16)
        logits.append(jnp.where(mask, _dot(q_db, k2), NEG_INF))
    p, den = _sink_softmax(jnp.concatenate(logits, axis=0), jnp.concatenate([sink_col] * nb, axis=0))
    p = p.astype(BF16)
    outs = [[] for _ in range(npair)]
    for db in range(nb):
        o = _dot_nt(p[db * rq:(db + 1) * rq, :], values[db]) / den[db * rq:(db + 1) * rq, :]
        for pr in range(npair):
            o_even = o[(2 * pr) * t:(2 * pr + 1) * t, :]
            o_odd = o[(2 * pr + 1) * t:(2 * pr + 2) * t, :]
            outs[pr].append(_from_kv_half(o_even, o_odd, 2 * pr // SWA_GROUP, low_t))
    for pr in range(npair):
        o_ref[:, pr * LANES:(pr + 1) * LANES] = jnp.concatenate(outs[pr], axis=0).astype(BF16)


def _swa_sample(sinks, z, row0, kc, vc, nb, t):
    dbs = kc.shape[0]
    n = dbs * t
    rows = nb * t
    W = WINDOW
    qw = SWA_HEADS * SWA_HD
    cache = pl.BlockSpec((nb, W, LANES), lambda i: (i, 0, 0))
    return pl.pallas_call(
        functools.partial(_swa_sample_kernel, nb=nb, t=t),
        grid=(dbs // nb,),
        in_specs=[
            pl.BlockSpec(memory_space=pltpu.SMEM),
            pl.BlockSpec((rows, qw), lambda i: (row0 // rows + i, SWA_Q_COL // qw)),
            pl.BlockSpec((rows, 2 * LANES), lambda i: (row0 // rows + i, SWA_KV_COL // (2 * LANES))),
            cache, cache,
        ],
        out_specs=[pl.BlockSpec((rows, qw), lambda i: (i, 0)), cache, cache],
        out_shape=[
            jax.ShapeDtypeStruct((n, qw), BF16),
            jax.ShapeDtypeStruct((dbs, W, LANES), F32),
            jax.ShapeDtypeStruct((dbs, W, LANES), F32),
        ],
        compiler_params=_cparams(("parallel",)),
        name="swa_sample",
    )(sinks, z, z, kc, vc)


def _out_proj_ln_kernel(rop_ref, sop_ref, xp_ref, ros_ref, sos_ref, xs_ref, eg_ref, eb_ref, w_hbm, mg_ref, mb_ref,
                        h_ref, hb_ref, wb_scr, stage, sem, *, np_tiles):
    i = pl.program_id(0)
    chunk = stage.shape[1]
    n_chunks = w_hbm.shape[0] // chunk

    def weight_copy(k):
        return pltpu.make_async_copy(w_hbm.at[pl.ds(k * chunk, chunk), :], stage.at[k % 2], sem.at[k % 2])

    @pl.when(i == 0)
    def _():
        weight_copy(0).start()
        for k in range(n_chunks):
            if k + 1 < n_chunks:
                weight_copy(k + 1).start()
            weight_copy(k).wait()
            wb_scr[k * chunk:(k + 1) * chunk, :] = stage[k % 2].astype(BF16)

    def step(ro_ref, so_ref, x_ref):
        sub = h_ref.shape[0] // 2
        for r in range(2):
            rows = slice(r * sub, (r + 1) * sub)
            x_in = _layer_norm(x_ref[rows, :], eg_ref[...], eb_ref[...])
            mixed = _dot(ro_ref[rows, :], wb_scr[:RET_W, :]) + _dot(so_ref[rows, :], wb_scr[RET_W:, :])
            h = _layer_norm(ALPHA * x_in + mixed, mg_ref[...], mb_ref[...])
            h_ref[rows, :] = h
            hb_ref[rows, :] = h.astype(BF16)

    @pl.when(i < np_tiles)
    def _():
        step(rop_ref, sop_ref, xp_ref)

    @pl.when(i >= np_tiles)
    def _():
        step(ros_ref, sos_ref, xs_ref)


def _out_proj_ln(ro_p, so_p, xp, ro_s, so_s, xs, eg, eb, w, mg, mb, tm):
    np_tiles = xp.shape[0] // tm
    ns_tiles = xs.shape[0] // tm
    n = xp.shape[0] + xs.shape[0]
    vec = pl.BlockSpec((1, D_MODEL), lambda i: (0, 0))
    tile = pl.BlockSpec((tm, D_MODEL), lambda i: (i, 0))
    p_idx = lambda i: (jnp.minimum(i, np_tiles - 1), 0)
    s_idx = lambda i: (jnp.clip(i - np_tiles, 0, ns_tiles - 1), 0)
    return pl.pallas_call(
        functools.partial(_out_proj_ln_kernel, np_tiles=np_tiles),
        grid=(np_tiles + ns_tiles,),
        in_specs=[
            pl.BlockSpec((tm, RET_W), p_idx), pl.BlockSpec((tm, RET_W), p_idx), pl.BlockSpec((tm, D_MODEL), p_idx),
            pl.BlockSpec((tm, RET_W), s_idx), pl.BlockSpec((tm, RET_W), s_idx), pl.BlockSpec((tm, D_MODEL), s_idx),
            vec, vec,
            pl.BlockSpec(memory_space=pl.ANY),
            vec, vec,
        ],
        out_specs=[tile, tile],
        out_shape=[jax.ShapeDtypeStruct((n, D_MODEL), F32), jax.ShapeDtypeStruct((n, D_MODEL), BF16)],
        scratch_shapes=[pltpu.VMEM((D_MODEL, D_MODEL), BF16),
                        pltpu.VMEM((2, WEIGHT_STAGE_ROWS, D_MODEL), F32),
                        pltpu.SemaphoreType.DMA((2,))],
        compiler_params=pltpu.CompilerParams(dimension_semantics=("arbitrary",), vmem_limit_bytes=VMEM_LIMIT_LARGE),
        name="out_proj_ln",
    )(ro_p, so_p, xp, ro_s, so_s, xs, eg, eb, w, mg, mb)


def _ffn_ln_kernel(hb_ref, h_hbm, wg_ref, wu_ref, wd_ref, g_ref, b_ref, y_ref, *rest, row0):
    *w_out_refs, h_res, sem = rest
    i = pl.program_id(0)
    j = pl.program_id(1)
    tm = h_res.shape[0]
    residual_copy = pltpu.make_async_copy(h_hbm.at[pl.ds(row0 + i * tm, tm), :], h_res, sem)

    @pl.when(j == 0)
    def _():
        residual_copy.start()
        y_ref[...] = jnp.zeros_like(y_ref)

    wg, wu, wd = (w[...].astype(BF16) for w in (wg_ref, wu_ref, wd_ref))
    for out_ref, w in zip(w_out_refs, (wg, wu, wd)):
        out_ref[...] = w
    hb = hb_ref[...]
    act = _silu(_dot(hb, wg)) * _dot(hb, wu)
    y_ref[...] += _dot(act.astype(BF16), wd)

    @pl.when(j == pl.num_programs(1) - 1)
    def _():
        residual_copy.wait()
        y_ref[...] = _layer_norm(ALPHA * h_res[...] + y_ref[...], g_ref[...], b_ref[...])


def _ffn_ln(h, hb, row0, n, wg, wu, wd, g, b, tm, th):
    emit_weights = wg.dtype == F32
    assert not emit_weights or n == tm
    vec = pl.BlockSpec((1, D_MODEL), lambda i, j: (0, 0))
    w_specs = [
        pl.BlockSpec((D_MODEL, th), lambda i, j: (0, j)),
        pl.BlockSpec((D_MODEL, th), lambda i, j: (0, j)),
        pl.BlockSpec((th, D_MODEL), lambda i, j: (j, 0)),
    ]
    out_specs = [pl.BlockSpec((tm, D_MODEL), lambda i, j: (i, 0))]
    out_shape = [jax.ShapeDtypeStruct((n, D_MODEL), F32)]
    if emit_weights:
        out_specs += w_specs
        out_shape += [jax.ShapeDtypeStruct(w.shape, BF16) for w in (wg, wu, wd)]
    outs = pl.pallas_call(
        functools.partial(_ffn_ln_kernel, row0=row0),
        grid=(n // tm, FFN_HIDDEN // th),
        in_specs=[pl.BlockSpec((tm, D_MODEL), lambda i, j: (row0 // tm + i, 0)), pl.BlockSpec(memory_space=pl.ANY)]
        + w_specs + [vec, vec],
        out_specs=out_specs,
        out_shape=out_shape,
        scratch_shapes=[pltpu.VMEM((tm, D_MODEL), F32), pltpu.SemaphoreType.DMA(())],
        compiler_params=pltpu.CompilerParams(dimension_semantics=("arbitrary", "arbitrary"),
                                             vmem_limit_bytes=VMEM_LIMIT_LARGE),
        name="ffn_ln",
    )(hb, h, wg, wu, wd, g, b)
    return tuple(outs) if emit_weights else outs[0]


def _position_tables(seq, t, past_len, sample_rows):
    r = jnp.arange(seq + sample_rows + RET_CHUNK)
    meta_r = r - seq - sample_rows
    pos = jnp.where(r < seq, N_META + r,
                    jnp.where(meta_r < 0, past_len + (r - seq) % t,
                              jnp.maximum(meta_r - (RET_CHUNK - N_META), 0))).astype(F32)[:, None]
    ret_freq = jnp.power(RET_THETA, -jnp.linspace(0.0, 1.0, RET_DK // 2, dtype=F32))
    ret_ang = pos * ret_freq[None, :]
    half = ROT_DIM // 2
    swa_freq = jnp.power(ROPE_THETA, -jnp.arange(0, ROT_DIM, 2, dtype=F32) / ROT_DIM)
    swa_ang = pos * swa_freq[None, :]
    cos = jnp.tile(jnp.cos(swa_ang), (1, LANES // half))
    sin = jnp.tile(jnp.sin(swa_ang), (1, LANES // half))
    d = jnp.arange(LANES) % SWA_HD
    c = jnp.where(d < ROT_DIM, cos, 1.0)
    s1 = jnp.where((d >= half) & (d < ROT_DIM), sin, 0.0)
    s2 = jnp.where(d < half, -sin, 0.0)
    return jnp.concatenate([jnp.cos(ret_ang), jnp.sin(ret_ang), c, s1, s2], axis=1)


def kernel(x_prompt, x_sample, state_ret, cache_swa_k, cache_swa_v, meta_tokens, ln_emb_g, ln_emb_b,
           w_in, w_out, swa_sinks, ln_mix_g, ln_mix_b, w_ffn_gate, w_ffn_up, w_ffn_down, ln_ffn_g, ln_ffn_b):
    bsz, seq, d = x_prompt.shape
    dbs, t, _ = x_sample.shape
    assert w_in.shape[0] == DEPTH and d == D_MODEL and seq % RET_CHUNK == 0 and t & (t - 1) == 0
    past_len = 16384
    row = lambda a: a.reshape(1, -1)

    eg, eb = row(ln_emb_g), row(ln_emb_b)
    sinks = swa_sinks[0]
    lg = jnp.log(1.0 - jnp.power(2.0, -5.0 - jnp.arange(RET_HEADS, dtype=F32)))

    xp = x_prompt.reshape(bsz * seq, d)
    xs = x_sample.reshape(dbs * t, d)

    tabs = _position_tables(seq, t, past_len, LN_PROJ_TILE)
    z, zm = _ln_proj(xp, xs, meta_tokens, eg, eb, w_in[0], tabs, seq, LN_PROJ_TILE, PROJ_W)
    sample_row = bsz * seq

    ret_o_p, ret_state_p = _ret_prompt(lg, z, zm, bsz, seq)
    swa_o_p, k_p, v_p, ret_o_s, ret_state_s = _swa_prompt_ret_sample(
        sinks, z, zm, bsz, seq, lg, sample_row, state_ret[0], t)
    to_kernel = lambda a: jnp.transpose(a[0], (0, 2, 3, 1)).reshape(-1, LANES, WINDOW)
    from_kernel = lambda a: jnp.transpose(a.reshape(-1, SWA_KV_HEADS, SWA_HD, WINDOW), (0, 3, 1, 2))[None]
    kc, vc = to_kernel(cache_swa_k), to_kernel(cache_swa_v)
    swa_o_s, k_s, v_s = _swa_sample(sinks, z, sample_row, kc, vc, SWA_SAMPLE_BATCHES, t)

    mg, mb = row(ln_mix_g[0]), row(ln_mix_b[0])
    fg, fb = row(ln_ffn_g[0]), row(ln_ffn_b[0])
    ffn_w = (w_ffn_gate[0], w_ffn_up[0], w_ffn_down[0])
    h, hb = _out_proj_ln(ret_o_p, swa_o_p, xp, ret_o_s, swa_o_s, xs, eg, eb, w_out[0], mg, mb, 512)
    y_s, *ffn_wb = _ffn_ln(h, hb, sample_row, dbs * t, *ffn_w, fg, fb, TOK_TILE, 256)
    y_p = _ffn_ln(h, hb, 0, bsz * seq, *ffn_wb, fg, fb, TOK_TILE, 512)

    return (y_p.reshape(bsz, seq, d), y_s.reshape(dbs, t, d),
            ret_state_p[None], from_kernel(k_p), from_kernel(v_p),
            ret_state_s[None], from_kernel(k_s), from_kernel(v_s))
```

```python
import functools

import jax
import jax.numpy as jnp
from jax import lax
from jax.experimental import pallas as pl
from jax.experimental.pallas import tpu as pltpu

F32 = jnp.float32
BF16 = jnp.bfloat16

D_MODEL = 2048
N_META = 16
RET_HEADS = 4
RET_DK = 256
RET_DV = 256
RET_CHUNK = 128
RET_THETA = 10000.0
SWA_HD = 64
SWA_HEADS = 16
SWA_KV_HEADS = 2
SWA_GROUP = SWA_HEADS // SWA_KV_HEADS
WINDOW = 128
ROPE_THETA = 500000.0
ROT_DIM = SWA_HD // 4
FFN_HIDDEN = 5632
PROJ_W = 5376
DEPTH = 1
ALPHA = (2.0 * DEPTH) ** 0.25
LN_EPS = 1e-5
NEG_INF = -1e30

LANES = 128
RET_W = RET_HEADS * RET_DK
SWA_Q_COL = 4 * RET_W
SWA_KV_COL = SWA_Q_COL + SWA_HEADS * SWA_HD
VMEM_LIMIT = 56 * 1024 * 1024
VMEM_LIMIT_LARGE = 60 * 1024 * 1024
TOK_TILE = 1024
LN_PROJ_SUBBLOCKS = 2
RET_CHUNK_UNROLL = 16
RET_SAMPLE_EVERY = 2
LN_PROJ_TILE = 256
SWA_SAMPLE_BATCHES = 8
WEIGHT_STAGE_ROWS = 256


def _cparams(sem):
    return pltpu.CompilerParams(dimension_semantics=sem, vmem_limit_bytes=VMEM_LIMIT)


def _layer_norm(x, g, b):
    mu = jnp.mean(x, axis=-1, keepdims=True)
    xc = x - mu
    var = jnp.mean(xc * xc, axis=-1, keepdims=True)
    return xc * lax.rsqrt(var + LN_EPS) * g + b


def _silu(x):
    return x / (1.0 + jnp.exp(-x))


def _dot(a, b):
    return jnp.dot(a, b, preferred_element_type=F32)


def _dot_nt(a, b):
    return lax.dot_general(a, b, (((1,), (1,)), ((), ())), preferred_element_type=F32)


def _dot_tn(a, b):
    return lax.dot_general(a, b, (((0,), (0,)), ((), ())), preferred_element_type=F32)


def _store_rotated(z_ref, rows, res, tab):
    cos, sin, c, s1, s2 = (tab[:, n * LANES:(n + 1) * LANES] for n in range(5))
    for c0 in range(0, 2 * RET_W, RET_DK):
        rot = _ret_rope(res[:, c0:c0 + RET_DK], cos, sin)
        z_ref[rows, c0:c0 + RET_DK] = rot if c0 < RET_W else rot * RET_DK ** -0.5
    z_ref[rows, 2 * RET_W:SWA_Q_COL] = res[:, 2 * RET_W:SWA_Q_COL]
    for c0 in range(SWA_Q_COL, SWA_KV_COL + LANES, LANES):
        rot = _swa_rope(res[:, c0:c0 + LANES], c, s1, s2)
        z_ref[rows, c0:c0 + LANES] = rot * SWA_HD ** -0.5 if c0 < SWA_KV_COL else rot
    z_ref[rows, SWA_KV_COL + LANES:] = res[:, SWA_KV_COL + LANES:]


def _ln_proj_kernel(xp_ref, xs_ref, xm_ref, g_ref, b_ref, w_hbm, tabp_ref, tabs_ref, tabm_ref, z_ref, zm_ref,
                    hm_scr, wb_scr, stage, sem, *, np_tiles, ns_tiles):
    i = pl.program_id(0)
    j = pl.program_id(1)
    first = j == 0
    is_sample = i >= np_tiles
    is_last = i == np_tiles + ns_tiles - 1
    n_col, _, tn = wb_scr.shape
    chunk = stage.shape[1]
    n_chunks = w_hbm.shape[0] // chunk

    def weight_copy(k):
        return pltpu.make_async_copy(w_hbm.at[pl.ds(k * chunk, chunk), :], stage.at[k % 2], sem.at[k % 2])

    @pl.when(first & (i == 0))
    def _():
        weight_copy(0).start()
        for k in range(n_chunks):
            if k + 1 < n_chunks:
                weight_copy(k + 1).start()
            weight_copy(k).wait()
            for jt in range(n_col):
                wb_scr[jt, k * chunk:(k + 1) * chunk, :] = stage[k % 2, :, jt * tn:(jt + 1) * tn].astype(BF16)

    def norm(x):
        return _layer_norm(x, g_ref[...], b_ref[...]).astype(BF16)

    def project(x_ref, tab_ref):
        sub = x_ref.shape[0] // LN_PROJ_SUBBLOCKS
        for r in range(LN_PROJ_SUBBLOCKS):
            rows = slice(r * sub, (r + 1) * sub)
            _store_rotated(z_ref, rows, _dot(norm(x_ref[rows, :]), wb_scr[j]), tab_ref[rows, :])

    @pl.when(jnp.logical_not(is_sample))
    def _():
        project(xp_ref, tabp_ref)

    @pl.when(is_sample)
    def _():
        project(xs_ref, tabs_ref)

    @pl.when(is_last)
    def _():
        lead = hm_scr.shape[0] - xm_ref.shape[0]
        hm_scr[:lead, :] = jnp.zeros((lead, D_MODEL), BF16)
        hm_scr[lead:, :] = norm(xm_ref[...])
        _store_rotated(zm_ref, slice(None), _dot(hm_scr[...], wb_scr[j]), tabm_ref[...])


def _ln_proj(xp, xs, xm, g, b, w, tabs, seq, tm, tn):
    np_tiles = xp.shape[0] // tm
    ns_tiles = xs.shape[0] // tm
    last = np_tiles + ns_tiles - 1
    n_col = PROJ_W // tn
    assert n_col == 1
    tab_w = tabs.shape[1]
    return pl.pallas_call(
        functools.partial(_ln_proj_kernel, np_tiles=np_tiles, ns_tiles=ns_tiles),
        grid=(np_tiles + ns_tiles, n_col),
        in_specs=[
            pl.BlockSpec((tm, D_MODEL), lambda i, j: (jnp.minimum(i, np_tiles - 1), 0)),
            pl.BlockSpec((tm, D_MODEL), lambda i, j: (jnp.clip(i - np_tiles, 0, ns_tiles - 1), 0)),
            pl.BlockSpec(xm.shape, lambda i, j: (0, 0), pipeline_mode=pl.Buffered(1)),
            pl.BlockSpec((1, D_MODEL), lambda i, j: (0, 0)),
            pl.BlockSpec((1, D_MODEL), lambda i, j: (0, 0)),
            pl.BlockSpec(memory_space=pl.ANY),
            pl.BlockSpec((tm, tab_w), lambda i, j: (lax.rem(i, seq // tm), 0)),
            pl.BlockSpec((tm, tab_w), lambda i, j: (seq // tm, 0)),
            pl.BlockSpec((RET_CHUNK, tab_w), lambda i, j: ((seq + tm) // RET_CHUNK, 0)),
        ],
        out_specs=[
            pl.BlockSpec((tm, tn), lambda i, j: (i, j)),
            pl.BlockSpec((RET_CHUNK, tn), lambda i, j: (0, jnp.where(i == last, j, 0))),
        ],
        out_shape=[
            jax.ShapeDtypeStruct((xp.shape[0] + xs.shape[0], PROJ_W), F32),
            jax.ShapeDtypeStruct((RET_CHUNK, PROJ_W), F32),
        ],
        scratch_shapes=[
            pltpu.VMEM((RET_CHUNK, D_MODEL), BF16),
            pltpu.VMEM((n_col, D_MODEL, tn), BF16),
            pltpu.VMEM((2, WEIGHT_STAGE_ROWS // 4, PROJ_W), F32),
            pltpu.SemaphoreType.DMA((2,)),
        ],
        compiler_params=pltpu.CompilerParams(dimension_semantics=("arbitrary", "arbitrary"),
                                             vmem_limit_bytes=VMEM_LIMIT_LARGE),
        name="ln_proj",
    )(xp, xs, xm, g, b, w, tabs, tabs, tabs)


def _ret_rope(x, cos, sin):
    x1 = x[:, :LANES]
    x2 = x[:, LANES:]
    return jnp.concatenate([x1 * cos - x2 * sin, x2 * cos + x1 * sin], axis=1)


def _group_norm_gate(o, gate):
    mu = jnp.mean(o, axis=-1, keepdims=True)
    oc = o - mu
    var = jnp.mean(oc * oc, axis=-1, keepdims=True)
    return oc * lax.rsqrt(var + LN_EPS) * _silu(gate)


def _ret_prompt_kernel(lg_ref, q_ref, k_ref, v_ref, g_ref, km_ref, vm_ref, o_ref, s_ref, s_scr):
    C = RET_CHUNK
    lg = lg_ref[pl.program_id(1)]
    ri = lax.broadcasted_iota(jnp.int32, (C, C), 0)
    ci = lax.broadcasted_iota(jnp.int32, (C, C), 1)
    rel = (ri - ci).astype(F32)
    decay = jnp.where(rel >= 0.0, jnp.exp(jnp.maximum(rel, 0.0) * lg), 0.0)
    row = lax.broadcasted_iota(jnp.int32, (C, 1), 0).astype(F32)
    q_decay = jnp.exp((row + 1.0) * lg)
    k_decay = jnp.exp((C - 1.0 - row) * lg)
    chunk_decay = jnp.exp(jnp.full((1, RET_DV), C * lg, F32))

    meta_decay = jnp.where(row >= C - N_META, k_decay, 0.0)
    s_scr[...] = _dot_tn((km_ref[...] * meta_decay).astype(BF16), vm_ref[...].astype(BF16))

    def chunk(c, carry):
        rows = pl.ds(pl.multiple_of(c * C, C), C)
        k = k_ref[rows, :]
        qb = q_ref[rows, :].astype(BF16)
        vb = v_ref[rows, :].astype(BF16)
        s_prev = s_scr[...]
        scores = _dot_nt(qb, k.astype(BF16)) * decay
        inner = _dot(scores.astype(BF16), vb)
        cross = _dot(qb, s_prev.astype(BF16)) * q_decay
        s_scr[...] = chunk_decay * s_prev + _dot_tn((k * k_decay).astype(BF16), vb)
        o_ref[rows, :] = _group_norm_gate(inner + cross, g_ref[rows, :]).astype(BF16)
        return carry

    lax.fori_loop(0, q_ref.shape[0] // C, chunk, 0, unroll=RET_CHUNK_UNROLL)
    s_ref[...] = s_scr[...]


def _ret_prompt(lg, z, zm, bsz, seq):
    col = lambda base: (lambda b, h: (b, base + h))
    mcol = lambda base: (lambda b, h: (0, base + h))
    return pl.pallas_call(
        _ret_prompt_kernel,
        grid=(bsz, RET_HEADS),
        in_specs=[
            pl.BlockSpec(memory_space=pltpu.SMEM),
            pl.BlockSpec((seq, RET_DK), col(0)),
            pl.BlockSpec((seq, RET_DK), col(RET_HEADS)),
            pl.BlockSpec((seq, RET_DV), col(2 * RET_HEADS)),
            pl.BlockSpec((seq, RET_DV), col(3 * RET_HEADS)),
            pl.BlockSpec((RET_CHUNK, RET_DK), mcol(RET_HEADS)),
            pl.BlockSpec((RET_CHUNK, RET_DV), mcol(2 * RET_HEADS)),
        ],
        out_specs=[
            pl.BlockSpec((seq, RET_DV), lambda b, h: (b, h)),
            pl.BlockSpec((None, None, RET_DK, RET_DV), lambda b, h: (b, h, 0, 0)),
        ],
        out_shape=[
            jax.ShapeDtypeStruct((bsz * seq, RET_W), BF16),
            jax.ShapeDtypeStruct((bsz, RET_HEADS, RET_DK, RET_DV), F32),
        ],
        scratch_shapes=[pltpu.VMEM((RET_DK, RET_DV), F32)],
        compiler_params=_cparams(("parallel", "parallel")),
        name="ret_prompt",
    )(lg, z, z, z, z, zm, zm)


def _ret_sample_jobs(lg_ref, z_ref, s_ref, o_ref, so_ref, *, nb, t):
    rows = nb * t
    R = RET_HEADS * rows

    def stack(base):
        return jnp.concatenate([z_ref[:, base + h * RET_DK: base + (h + 1) * RET_DK] for h in range(RET_HEADS)],
                               axis=0)

    q = stack(0)
    k = stack(RET_W)
    v = stack(2 * RET_W)
    gate = stack(3 * RET_W)

    ri = lax.broadcasted_iota(jnp.int32, (R, R), 0)
    ci = lax.broadcasted_iota(jnp.int32, (R, R), 1)
    rcol = lax.broadcasted_iota(jnp.int32, (R, 1), 0)
    lg_col = jnp.zeros((R, 1), F32)
    for h in range(RET_HEADS):
        lg_col = jnp.where((rcol >= h * rows) & (rcol < (h + 1) * rows), lg_ref[h], lg_col)
    tcol = (rcol & (t - 1)).astype(F32)
    rel = (ri - ci).astype(F32)
    same = ((ri & -t) == (ci & -t)) & (ri >= ci)
    decay = jnp.where(same, jnp.exp(jnp.maximum(rel, 0.0) * lg_col), 0.0)
    q_decay = jnp.exp((tcol + 1.0) * lg_col)
    k_decay = jnp.exp((t - 1.0 - tcol) * lg_col)

    qb = q.astype(BF16)
    vb = v.astype(BF16)
    scores = _dot_nt(qb, k.astype(BF16)) * decay
    inner = _dot(scores.astype(BF16), vb)
    kw = k * k_decay

    cross_parts = [None] * (RET_HEADS * nb)

    def state_job(h, db):
        def run():
            r0 = h * rows + db * t
            s_prev = s_ref[db, h]
            cross_parts[h * nb + db] = _dot(q[r0:r0 + t, :].astype(BF16), s_prev.astype(BF16))
            mine = (rcol >= r0) & (rcol < r0 + t)
            upd = _dot_tn(jnp.where(mine, kw, 0.0).astype(BF16), vb)
            step_decay = jnp.exp(jnp.full((1, RET_DV), t * lg_ref[h], F32))
            so_ref[db, h] = step_decay * s_prev + upd
        return run

    def finish():
        cross = jnp.concatenate(cross_parts, axis=0) * q_decay
        out = _group_norm_gate(inner + cross, gate).astype(BF16)
        for h in range(RET_HEADS):
            o_ref[:, h * RET_DV:(h + 1) * RET_DV] = out[h * rows:(h + 1) * rows, :]

    return [state_job(h, db) for h in range(RET_HEADS) for db in range(nb)], finish


def _swa_rope(x, c, s1, s2):
    return x * c + pltpu.roll(x, 8, 1) * s1 + pltpu.roll(x, LANES - 8, 1) * s2


def _dup_head(x, g, low):
    swapped = pltpu.roll(x, SWA_HD, 1)
    return jnp.where(low, x, swapped) if g == 0 else jnp.where(low, swapped, x)


def _to_kv_half(slab, head, low):
    g = head // SWA_GROUP
    src = slab if head % 2 == g else pltpu.roll(slab, SWA_HD, 1)
    return jnp.where(low, src, 0.0) if g == 0 else jnp.where(low, 0.0, src)


def _from_kv_half(o_even, o_odd, g, low):
    if g == 0:
        return jnp.where(low, o_even, pltpu.roll(o_odd, SWA_HD, 1))
    return jnp.where(low, pltpu.roll(o_even, SWA_HD, 1), o_odd)


def _sink_softmax(logits, sink):
    m = jnp.maximum(jnp.max(logits, axis=-1, keepdims=True), sink)
    p = jnp.exp(logits - m)
    return p, jnp.sum(p, axis=-1, keepdims=True) + jnp.exp(sink - m)


def _swa_prompt_kernel(sink_ref, q_ref, kv_ref, kvm_ref, o_ref, kp_ref, vp_ref, kprev, vprev):
    W = WINDOW
    m_id = pl.program_id(1)

    @pl.when(m_id == 0)
    def _():
        kprev[...] = kvm_ref[:, :LANES]
        vprev[...] = kvm_ref[:, LANES:]

    kcur = kv_ref[:, :LANES]
    vcur = kv_ref[:, LANES:]
    k2 = jnp.concatenate([kprev[...], kcur], axis=0)
    v2 = jnp.concatenate([vprev[...], vcur], axis=0)

    r = lax.broadcasted_iota(jnp.int32, (W, 2 * W), 0)
    cc = lax.broadcasted_iota(jnp.int32, (W, 2 * W), 1)
    diff = W + r - cc
    mask = (diff >= 0) & (diff < W) & ((m_id > 0) | (cc >= W - N_META))
    low_k = lax.broadcasted_iota(jnp.int32, (2 * W, LANES), 1) < SWA_HD
    low = lax.broadcasted_iota(jnp.int32, (W, LANES), 1) < SWA_HD

    def attend(q, kd, vd, h):
        p, den = _sink_softmax(jnp.where(mask, _dot_nt(q.astype(BF16), kd), NEG_INF), sink_ref[h])
        return _dot(p.astype(BF16), vd) / den

    for g in range(SWA_KV_HEADS):
        kd = _dup_head(k2, g, low_k).astype(BF16)
        vd = _dup_head(v2, g, low_k).astype(BF16)
        for p in range(g * SWA_GROUP // 2, (g + 1) * SWA_GROUP // 2):
            slab = q_ref[:, p * LANES:(p + 1) * LANES]
            o_even = attend(jnp.where(low, slab, 0.0), kd, vd, 2 * p)
            o_odd = attend(jnp.where(low, 0.0, slab), kd, vd, 2 * p + 1)
            o_ref[:, p * LANES:(p + 1) * LANES] = jnp.where(low, o_even, o_odd).astype(BF16)

    kprev[...] = kcur
    vprev[...] = vcur

    @pl.when(m_id == pl.num_programs(1) - 1)
    def _():
        kp_ref[...] = kcur.T
        vp_ref[...] = vcur.T


def _swa_prompt_ret_sample_kernel(sink_ref, q_ref, kv_ref, kvm_ref, lg_ref, zs_ref, st_ref,
                                 o_ref, kp_ref, vp_ref, os_ref, sto_ref, kprev, vprev, *, nb, t, every):
    _swa_prompt_kernel(sink_ref, q_ref, kv_ref, kvm_ref, o_ref, kp_ref, vp_ref, kprev, vprev)

    @pl.when(lax.rem(pl.program_id(1), every) == every - 1)
    def _():
        state_jobs, finish = _ret_sample_jobs(lg_ref, zs_ref, st_ref, os_ref, sto_ref, nb=nb, t=t)
        for job in state_jobs + [finish]:
            job()


def _swa_prompt_ret_sample(sinks, z, zm, bsz, seq, lg, sample_row, state, t):
    W = WINDOW
    nblk = seq // W
    every = RET_SAMPLE_EVERY
    ret_steps = bsz * nblk // every
    dbs = state.shape[0]
    nb = dbs // ret_steps
    assert nblk % every == 0 and nb * ret_steps == dbs and (nb * t) % 16 == 0
    rows = nb * t
    qw = SWA_HEADS * SWA_HD
    kv_col = SWA_KV_COL // (2 * LANES)
    ret_step =lambda b, m: (b * nblk + m) // every
    state_spec = pl.BlockSpec((nb, RET_HEADS, RET_DK, RET_DV), lambda b, m: (ret_step(b, m), 0, 0, 0))
    return pl.pallas_call(
        functools.partial(_swa_prompt_ret_sample_kernel, nb=nb, t=t, every=every),
        grid=(bsz, nblk),
        in_specs=[
            pl.BlockSpec(memory_space=pltpu.SMEM),
            pl.BlockSpec((W, qw), lambda b, m: (b * nblk + m, SWA_Q_COL // qw)),
            pl.BlockSpec((W, 2 * LANES), lambda b, m: (b * nblk + m, kv_col)),
            pl.BlockSpec((W, 2 * LANES), lambda b, m: (0, kv_col)),
            pl.BlockSpec(memory_space=pltpu.SMEM),
            pl.BlockSpec((rows, 4 * RET_W), lambda b, m: (sample_row // rows + ret_step(b, m), 0)),
            state_spec,
        ],
        out_specs=[
            pl.BlockSpec((W, qw), lambda b, m: (b * nblk + m, 0)),
            pl.BlockSpec((None, W, LANES), lambda b, m: (b, 0, 0)),
            pl.BlockSpec((None, W, LANES), lambda b, m: (b, 0, 0)),
            pl.BlockSpec((rows, RET_W), lambda b, m: (ret_step(b, m), 0)),
            state_spec,
        ],
        out_shape=[
            jax.ShapeDtypeStruct((bsz * seq, qw), BF16),
            jax.ShapeDtypeStruct((bsz, W, LANES), F32),
            jax.ShapeDtypeStruct((bsz, W, LANES), F32),
            jax.ShapeDtypeStruct((dbs * t, RET_W), BF16),
            jax.ShapeDtypeStruct(state.shape, F32),
        ],
        scratch_shapes=[pltpu.VMEM((W, LANES), F32), pltpu.VMEM((W, LANES), F32)],
        compiler_params=_cparams(("parallel", "arbitrary")),
        name="swa_prompt_ret_sample",
    )(sinks, z, z, zm, lg, z, state)


def _swa_sample_kernel(sink_ref, q_ref, kv_ref, kc_ref, vc_ref, o_ref, ko_ref, vo_ref, *, nb, t):
    W = WINDOW
    row_pad = jnp.zeros((LANES - nb * t, LANES), F32)
    knew_t = jnp.concatenate([kv_ref[:, :LANES], row_pad], axis=0).T
    vnew_t = jnp.concatenate([kv_ref[:, LANES:], row_pad], axis=0).T
    knew_b = knew_t.astype(BF16)
    vnew_b = vnew_t.astype(BF16)

    rq = SWA_HEADS * t
    tq = lax.broadcasted_iota(jnp.int32, (rq, 2 * W), 0) & (t - 1)
    cc = lax.broadcasted_iota(jnp.int32, (rq, 2 * W), 1)
    out_lane = lax.broadcasted_iota(jnp.int32, (LANES, LANES), 1)
    low = lax.broadcasted_iota(jnp.int32, (nb * t, LANES), 1) < SWA_HD
    low_t = lax.broadcasted_iota(jnp.int32, (t, LANES), 1) < SWA_HD
    sink_col = jnp.concatenate([jnp.full((t, 1), sink_ref[h], F32) for h in range(SWA_HEADS)], axis=0)
    npair = SWA_HEADS // 2

    slabs = [q_ref[:, p * LANES:(p + 1) * LANES] for p in range(npair)]
    q_heads = [_to_kv_half(slabs[h // 2], h, low) for h in range(SWA_HEADS)]

    logits, values = [], []
    for db in range(nb):
        rows = slice(db * t, (db + 1) * t)
        kc = kc_ref[db]
        vc = vc_ref[db]
        keep = out_lane < W - t
        ko_ref[db] = jnp.where(keep, pltpu.roll(kc, W - t, 1), pltpu.roll(knew_t, (W - t - db * t) % LANES, 1))
        vo_ref[db] = jnp.where(keep, pltpu.roll(vc, W - t, 1), pltpu.roll(vnew_t, (W - t - db * t) % LANES, 1))
        k2 = jnp.concatenate([kc.astype(BF16), knew_b], axis=1)
        values.append(jnp.concatenate([vc.astype(BF16), vnew_b], axis=1))
        t_new = cc - W - db * t
        mask = ((cc < W) & (cc > tq)) | ((t_new >= 0) & (t_new <= tq))
        q_db = jnp.concatenate([qh[rows, :] for qh in q_heads], axis=0).astype(BF16)
        logits.append(jnp.where(mask, _dot(q_db, k2), NEG_INF))
    p, den = _sink_softmax(jnp.concatenate(logits, axis=0), jnp.concatenate([sink_col] * nb, axis=0))
    p = p.astype(BF16)
    outs = [[] for _ in range(npair)]
    for db in range(nb):
        o = _dot_nt(p[db * rq:(db + 1) * rq, :], values[db]) / den[db * rq:(db + 1) * rq, :]
        for pr in range(npair):
            o_even = o[(2 * pr) * t:(2 * pr + 1) * t, :]
            o_odd = o[(2 * pr + 1) * t:(2 * pr + 2) * t, :]
            outs[pr].append(_from_kv_half(o_even, o_odd, 2 * pr // SWA_GROUP, low_t))
    for pr in range(npair):
        o_ref[:, pr * LANES:(pr + 1) * LANES] = jnp.concatenate(outs[pr], axis=0).astype(BF16)


def _swa_sample(sinks, z, row0, kc, vc, nb, t):
    dbs = kc.shape[0]
    n = dbs * t
    rows = nb * t
    W = WINDOW
    qw = SWA_HEADS * SWA_HD
    cache = pl.BlockSpec((nb, W, LANES), lambda i: (i, 0, 0))
    return pl.pallas_call(
        functools.partial(_swa_sample_kernel, nb=nb, t=t),
        grid=(dbs // nb,),
        in_specs=[
            pl.BlockSpec(memory_space=pltpu.SMEM),
            pl.BlockSpec((rows, qw), lambda i: (row0 // rows + i, SWA_Q_COL // qw)),
            pl.BlockSpec((rows, 2 * LANES), lambda i: (row0 // rows + i, SWA_KV_COL // (2 * LANES))),
            cache, cache,
        ],
        out_specs=[pl.BlockSpec((rows, qw), lambda i: (i, 0)), cache, cache],
        out_shape=[
            jax.ShapeDtypeStruct((n, qw), BF16),
            jax.ShapeDtypeStruct((dbs, W, LANES), F32),
            jax.ShapeDtypeStruct((dbs, W, LANES), F32),
        ],
        compiler_params=_cparams(("parallel",)),
        name="swa_sample",
    )(sinks, z, z, kc, vc)


def _out_proj_ln_kernel(rop_ref, sop_ref, xp_ref, ros_ref, sos_ref, xs_ref, eg_ref, eb_ref, w_hbm, mg_ref, mb_ref,
                        h_ref, hb_ref, wb_scr, stage, sem, *, np_tiles):
    i = pl.program_id(0)
    chunk = stage.shape[1]
    n_chunks = w_hbm.shape[0] // chunk

    def weight_copy(k):
        return pltpu.make_async_copy(w_hbm.at[pl.ds(k * chunk, chunk), :], stage.at[k % 2], sem.at[k % 2])

    @pl.when(i == 0)
    def _():
        weight_copy(0).start()
        for k in range(n_chunks):
            if k + 1 < n_chunks:
                weight_copy(k + 1).start()
            weight_copy(k).wait()
            wb_scr[k * chunk:(k + 1) * chunk, :] = stage[k % 2].astype(BF16)

    def step(ro_ref, so_ref, x_ref):
        sub = h_ref.shape[0] // 2
        for r in range(2):
            rows = slice(r * sub, (r + 1) * sub)
            x_in = _layer_norm(x_ref[rows, :], eg_ref[...], eb_ref[...])
            mixed = _dot(ro_ref[rows, :], wb_scr[:RET_W, :]) + _dot(so_ref[rows, :], wb_scr[RET_W:, :])
            h = _layer_norm(ALPHA * x_in + mixed, mg_ref[...], mb_ref[...])
            h_ref[rows, :] = h
            hb_ref[rows, :] = h.astype(BF16)

    @pl.when(i < np_tiles)
    def _():
        step(rop_ref, sop_ref, xp_ref)

    @pl.when(i >= np_tiles)
    def _():
        step(ros_ref, sos_ref, xs_ref)


def _out_proj_ln(ro_p, so_p, xp, ro_s, so_s, xs, eg, eb, w, mg, mb, tm):
    np_tiles = xp.shape[0] // tm
    ns_tiles = xs.shape[0] // tm
    n = xp.shape[0] + xs.shape[0]
    vec = pl.BlockSpec((1, D_MODEL), lambda i: (0, 0))
    tile = pl.BlockSpec((tm, D_MODEL), lambda i: (i, 0))
    p_idx = lambda i: (jnp.minimum(i, np_tiles - 1), 0)
    s_idx = lambda i: (jnp.clip(i - np_tiles, 0, ns_tiles - 1), 0)
    return pl.pallas_call(
        functools.partial(_out_proj_ln_kernel, np_tiles=np_tiles),
        grid=(np_tiles + ns_tiles,),
        in_specs=[
            pl.BlockSpec((tm, RET_W), p_idx), pl.BlockSpec((tm, RET_W), p_idx), pl.BlockSpec((tm, D_MODEL), p_idx),
            pl.BlockSpec((tm, RET_W), s_idx), pl.BlockSpec((tm, RET_W), s_idx), pl.BlockSpec((tm, D_MODEL), s_idx),
            vec, vec,
            pl.BlockSpec(memory_space=pl.ANY),
            vec, vec,
        ],
        out_specs=[tile, tile],
        out_shape=[jax.ShapeDtypeStruct((n, D_MODEL), F32), jax.ShapeDtypeStruct((n, D_MODEL), BF16)],
        scratch_shapes=[pltpu.VMEM((D_MODEL, D_MODEL), BF16),
                        pltpu.VMEM((2, WEIGHT_STAGE_ROWS, D_MODEL), F32),
                        pltpu.SemaphoreType.DMA((2,))],
        compiler_params=pltpu.CompilerParams(dimension_semantics=("arbitrary",), vmem_limit_bytes=VMEM_LIMIT_LARGE),
        name="out_proj_ln",
    )(ro_p, so_p, xp, ro_s, so_s, xs, eg, eb, w, mg, mb)


def _ffn_ln_kernel(hb_ref, h_hbm, wg_ref, wu_ref, wd_ref, g_ref, b_ref, y_ref, *rest, row0):
    *w_out_refs, h_res, sem = rest
    i = pl.program_id(0)
    j = pl.program_id(1)
    tm = h_res.shape[0]
    residual_copy = pltpu.make_async_copy(h_hbm.at[pl.ds(row0 + i * tm, tm), :], h_res, sem)

    @pl.when(j == 0)
    def _():
        residual_copy.start()
        y_ref[...] = jnp.zeros_like(y_ref)

    wg, wu, wd = (w[...].astype(BF16) for w in (wg_ref, wu_ref, wd_ref))
    for out_ref, w in zip(w_out_refs, (wg, wu, wd)):
        out_ref[...] = w
    hb = hb_ref[...]
    act = _silu(_dot(hb, wg)) * _dot(hb, wu)
    y_ref[...] += _dot(act.astype(BF16), wd)

    @pl.when(j == pl.num_programs(1) - 1)
    def _():
        residual_copy.wait()
        y_ref[...] = _layer_norm(ALPHA * h_res[...] + y_ref[...], g_ref[...], b_ref[...])


def _ffn_ln(h, hb, row0, n, wg, wu, wd, g, b, tm, th):
    emit_weights = wg.dtype == F32
    assert not emit_weights or n == tm
    vec = pl.BlockSpec((1, D_MODEL), lambda i, j: (0, 0))
    w_specs = [
        pl.BlockSpec((D_MODEL, th), lambda i, j: (0, j)),
        pl.BlockSpec((D_MODEL, th), lambda i, j: (0, j)),
        pl.BlockSpec((th, D_MODEL), lambda i, j: (j, 0)),
    ]
    out_specs = [pl.BlockSpec((tm, D_MODEL), lambda i, j: (i, 0))]
    out_shape = [jax.ShapeDtypeStruct((n, D_MODEL), F32)]
    if emit_weights:
        out_specs += w_specs
        out_shape += [jax.ShapeDtypeStruct(w.shape, BF16) for w in (wg, wu, wd)]
    outs = pl.pallas_call(
        functools.partial(_ffn_ln_kernel, row0=row0),
        grid=(n // tm, FFN_HIDDEN // th),
        in_specs=[pl.BlockSpec((tm, D_MODEL), lambda i, j: (row0 // tm + i, 0)), pl.BlockSpec(memory_space=pl.ANY)]
        + w_specs + [vec, vec],
        out_specs=out_specs,
        out_shape=out_shape,
        scratch_shapes=[pltpu.VMEM((tm, D_MODEL), F32), pltpu.SemaphoreType.DMA(())],
        compiler_params=pltpu.CompilerParams(dimension_semantics=("arbitrary", "arbitrary"),
                                             vmem_limit_bytes=VMEM_LIMIT_LARGE),
        name="ffn_ln",
    )(hb, h, wg, wu, wd, g, b)
    return tuple(outs) if emit_weights else outs[0]


def _position_tables(seq, t, past_len, sample_rows):
    r = jnp.arange(seq + sample_rows + RET_CHUNK)
    meta_r = r - seq - sample_rows
    pos = jnp.where(r < seq, N_META + r,
                    jnp.where(meta_r < 0, past_len + (r - seq) % t,
                              jnp.maximum(meta_r - (RET_CHUNK - N_META), 0))).astype(F32)[:, None]
    ret_freq = jnp.power(RET_THETA, -jnp.linspace(0.0, 1.0, RET_DK // 2, dtype=F32))[None, :]
    a = (N_META + RET_CHUNK * jnp.arange(seq // RET_CHUNK)).astype(F32)[:, None] * ret_freq
    b = jnp.arange(RET_CHUNK).astype(F32)[:, None] * ret_freq
    ca, sa, cb, sb = jnp.cos(a)[:, None, :], jnp.sin(a)[:, None, :], jnp.cos(b)[None], jnp.sin(b)[None]
    rest_ang = pos[seq:] * ret_freq
    ret_cos = jnp.concatenate([(ca * cb - sa * sb).reshape(seq, LANES), jnp.cos(rest_ang)], axis=0)
    ret_sin = jnp.concatenate([(sa * cb + ca * sb).reshape(seq, LANES), jnp.sin(rest_ang)], axis=0)
    half = ROT_DIM // 2
    swa_freq = jnp.power(ROPE_THETA, -jnp.arange(0, ROT_DIM, 2, dtype=F32) / ROT_DIM)
    swa_ang = pos * swa_freq[None, :]
    cos = jnp.tile(jnp.cos(swa_ang), (1, LANES // half))
    sin = jnp.tile(jnp.sin(swa_ang), (1, LANES // half))
    d = jnp.arange(LANES) % SWA_HD
    c = jnp.where(d < ROT_DIM, cos, 1.0)
    s1 = jnp.where((d >= half) & (d < ROT_DIM), sin, 0.0)
    s2 = jnp.where(d < half, -sin, 0.0)
    return jnp.concatenate([ret_cos, ret_sin, c, s1, s2], axis=1)


def kernel(x_prompt, x_sample, state_ret, cache_swa_k, cache_swa_v, meta_tokens, ln_emb_g, ln_emb_b,
           w_in, w_out, swa_sinks, ln_mix_g, ln_mix_b, w_ffn_gate, w_ffn_up, w_ffn_down, ln_ffn_g, ln_ffn_b):
    bsz, seq, d = x_prompt.shape
    dbs, t, _ = x_sample.shape
    assert w_in.shape[0] == DEPTH and d == D_MODEL and seq % RET_CHUNK == 0 and t & (t - 1) == 0
    past_len = 16384
    row = lambda a: a.reshape(1, -1)

    eg, eb = row(ln_emb_g), row(ln_emb_b)
    sinks = swa_sinks[0]
    lg = jnp.log(1.0 - jnp.power(2.0, -5.0 - jnp.arange(RET_HEADS, dtype=F32)))

    xp = x_prompt.reshape(bsz * seq, d)
    xs = x_sample.reshape(dbs * t, d)

    tabs = _position_tables(seq, t, past_len, LN_PROJ_TILE)
    z, zm = _ln_proj(xp, xs, meta_tokens, eg, eb, w_in[0], tabs, seq, LN_PROJ_TILE, PROJ_W)
    sample_row = bsz * seq

    ret_o_p, ret_state_p = _ret_prompt(lg, z, zm, bsz, seq)
    swa_o_p, k_p, v_p, ret_o_s, ret_state_s = _swa_prompt_ret_sample(
        sinks, z, zm, bsz, seq, lg, sample_row, state_ret[0], t)
    to_kernel = lambda a: jnp.transpose(a[0], (0, 2, 3, 1)).reshape(-1, LANES, WINDOW)
    from_kernel = lambda a: jnp.transpose(a.reshape(-1, SWA_KV_HEADS, SWA_HD, WINDOW), (0, 3, 1, 2))[None]
    kc, vc = to_kernel(cache_swa_k), to_kernel(cache_swa_v)
    swa_o_s, k_s, v_s = _swa_sample(sinks, z, sample_row, kc, vc, SWA_SAMPLE_BATCHES, t)

    mg, mb = row(ln_mix_g[0]), row(ln_mix_b[0])
    fg, fb = row(ln_ffn_g[0]), row(ln_ffn_b[0])
    ffn_w = (w_ffn_gate[0], w_ffn_up[0], w_ffn_down[0])
    h, hb = _out_proj_ln(ret_o_p, swa_o_p, xp, ret_o_s, swa_o_s, xs, eg, eb, w_out[0], mg, mb, 512)
    y_s, *ffn_wb = _ffn_ln(h, hb, sample_row, dbs * t, *ffn_w, fg, fb, TOK_TILE, 256)
    y_p = _ffn_ln(h, hb, 0, bsz * seq, *ffn_wb, fg, fb, TOK_TILE, 512)

    return (y_p.reshape(bsz, seq, d), y_s.reshape(dbs, t, d),
            ret_state_p[None], from_kernel(k_p), from_kernel(v_p),
            ret_state_s[None], from_kernel(k_s), from_kernel(v_s))
```

```python
import functools

import jax
import jax.numpy as jnp
from jax import lax
from jax.experimental import pallas as pl
from jax.experimental.pallas import tpu as pltpu

F32 = jnp.float32
BF16 = jnp.bfloat16

D_MODEL = 2048
N_META = 16
RET_HEADS = 4
RET_DK = 256
RET_DV = 256
RET_CHUNK = 128
RET_THETA = 10000.0
SWA_HD = 64
SWA_HEADS = 16
SWA_KV_HEADS = 2
SWA_GROUP = SWA_HEADS // SWA_KV_HEADS
WINDOW = 128
ROPE_THETA = 500000.0
ROT_DIM = SWA_HD // 4
FFN_HIDDEN = 5632
PROJ_W = 5376
DEPTH = 1
ALPHA = (2.0 * DEPTH) ** 0.25
LN_EPS = 1e-5
NEG_INF = -1e30

LANES = 128
RET_W = RET_HEADS * RET_DK
SWA_Q_COL = 4 * RET_W
SWA_KV_COL = SWA_Q_COL + SWA_HEADS * SWA_HD
VMEM_LIMIT = 56 * 1024 * 1024
VMEM_LIMIT_LARGE = 60 * 1024 * 1024
TOK_TILE = 1024
LN_PROJ_SUBBLOCKS = 2
RET_CHUNK_UNROLL = 16
RET_SAMPLE_EVERY = 2
LN_PROJ_TILE = 256
SWA_SAMPLE_BATCHES = 8
WEIGHT_STAGE_ROWS = 256
WEIGHT_STAGE_SLOTS = 4


def _cparams(sem):
    return pltpu.CompilerParams(dimension_semantics=sem, vmem_limit_bytes=VMEM_LIMIT)


def _layer_norm(x, g, b):
    mu = jnp.mean(x, axis=-1, keepdims=True)
    xc = x - mu
    var = jnp.mean(xc * xc, axis=-1, keepdims=True)
    return xc * lax.rsqrt(var + LN_EPS) * g + b


def _silu(x):
    return x / (1.0 + jnp.exp(-x))


def _dot(a, b):
    return jnp.dot(a, b, preferred_element_type=F32)


def _dot_nt(a, b):
    return lax.dot_general(a, b, (((1,), (1,)), ((), ())), preferred_element_type=F32)


def _dot_tn(a, b):
    return lax.dot_general(a, b, (((0,), (0,)), ((), ())), preferred_element_type=F32)


def _store_rotated(z_ref, rows, res, tab):
    cos, sin, c, s1, s2 = (tab[:, n * LANES:(n + 1) * LANES] for n in range(5))
    for c0 in range(0, 2 * RET_W, RET_DK):
        rot = _ret_rope(res[:, c0:c0 + RET_DK], cos, sin)
        z_ref[rows, c0:c0 + RET_DK] = rot if c0 < RET_W else rot * RET_DK ** -0.5
    z_ref[rows, 2 * RET_W:SWA_Q_COL] = res[:, 2 * RET_W:SWA_Q_COL]
    for c0 in range(SWA_Q_COL, SWA_KV_COL + LANES, LANES):
        rot = _swa_rope(res[:, c0:c0 + LANES], c, s1, s2)
        z_ref[rows, c0:c0 + LANES] = rot * SWA_HD ** -0.5 if c0 < SWA_KV_COL else rot
    z_ref[rows, SWA_KV_COL + LANES:] = res[:, SWA_KV_COL + LANES:]


def _stage_weight(w_hbm, stage, sem, store_chunk):
    slots, chunk = stage.shape[:2]
    n_chunks = w_hbm.shape[0] // chunk

    def weight_copy(k):
        return pltpu.make_async_copy(w_hbm.at[pl.ds(k * chunk, chunk), :], stage.at[k % slots], sem.at[k % slots])

    for k in range(min(slots - 1, n_chunks)):
        weight_copy(k).start()
    for k in range(n_chunks):
        if k + slots - 1 < n_chunks:
            weight_copy(k + slots - 1).start()
        weight_copy(k).wait()
        store_chunk(slice(k * chunk, (k + 1) * chunk), stage.at[k % slots])


def _ln_proj_kernel(xp_ref, xs_ref, xm_ref, g_ref, b_ref, w_hbm, tabp_ref, tabs_ref, tabm_ref, z_ref, zm_ref,
                    hm_scr, wb_scr, stage, sem, *, np_tiles, ns_tiles):
    i = pl.program_id(0)
    j = pl.program_id(1)
    first = j == 0
    is_sample = i >= np_tiles
    is_last = i == np_tiles + ns_tiles - 1
    n_col, _, tn = wb_scr.shape

    def store_chunk(rows, slot):
        for jt in range(n_col):
            wb_scr[jt, rows, :] = slot[:, jt * tn:(jt + 1) * tn].astype(BF16)

    @pl.when(first & (i == 0))
    def _():
        _stage_weight(w_hbm, stage, sem, store_chunk)

    def norm(x):
        return _layer_norm(x, g_ref[...], b_ref[...]).astype(BF16)

    def project(x_ref, tab_ref):
        sub = x_ref.shape[0] // LN_PROJ_SUBBLOCKS
        for r in range(LN_PROJ_SUBBLOCKS):
            rows = slice(r * sub, (r + 1) * sub)
            _store_rotated(z_ref, rows, _dot(norm(x_ref[rows, :]), wb_scr[j]), tab_ref[rows, :])

    @pl.when(jnp.logical_not(is_sample))
    def _():
        project(xp_ref, tabp_ref)

    @pl.when(is_sample)
    def _():
        project(xs_ref, tabs_ref)

    @pl.when(is_last)
    def _():
        lead = hm_scr.shape[0] - xm_ref.shape[0]
        hm_scr[:lead, :] = jnp.zeros((lead, D_MODEL), BF16)
        hm_scr[lead:, :] = norm(xm_ref[...])
        _store_rotated(zm_ref, slice(None), _dot(hm_scr[...], wb_scr[j]), tabm_ref[...])


def _ln_proj(xp, xs, xm, g, b, w, tabs, seq, tm, tn):
    np_tiles = xp.shape[0] // tm
    ns_tiles = xs.shape[0] // tm
    last = np_tiles + ns_tiles - 1
    n_col = PROJ_W // tn
    assert n_col == 1
    tab_w = tabs.shape[1]
    return pl.pallas_call(
        functools.partial(_ln_proj_kernel, np_tiles=np_tiles, ns_tiles=ns_tiles),
        grid=(np_tiles + ns_tiles, n_col),
        in_specs=[
            pl.BlockSpec((tm, D_MODEL), lambda i, j: (jnp.minimum(i, np_tiles - 1), 0)),
            pl.BlockSpec((tm, D_MODEL), lambda i, j: (jnp.clip(i - np_tiles, 0, ns_tiles - 1), 0)),
            pl.BlockSpec(xm.shape, lambda i, j: (0, 0), pipeline_mode=pl.Buffered(1)),
            pl.BlockSpec((1, D_MODEL), lambda i, j: (0, 0)),
            pl.BlockSpec((1, D_MODEL), lambda i, j: (0, 0)),
            pl.BlockSpec(memory_space=pl.ANY),
            pl.BlockSpec((tm, tab_w), lambda i, j: (lax.rem(i, seq // tm), 0)),
            pl.BlockSpec((tm, tab_w), lambda i, j: (seq // tm, 0)),
            pl.BlockSpec((RET_CHUNK, tab_w), lambda i, j: ((seq + tm) // RET_CHUNK, 0)),
        ],
        out_specs=[
            pl.BlockSpec((tm, tn), lambda i, j: (i, j)),
            pl.BlockSpec((RET_CHUNK, tn), lambda i, j: (0, jnp.where(i == last, j, 0))),
        ],
        out_shape=[
            jax.ShapeDtypeStruct((xp.shape[0] + xs.shape[0], PROJ_W), F32),
            jax.ShapeDtypeStruct((RET_CHUNK, PROJ_W), F32),
        ],
        scratch_shapes=[
            pltpu.VMEM((RET_CHUNK, D_MODEL), BF16),
            pltpu.VMEM((n_col, D_MODEL, tn), BF16),
            pltpu.VMEM((WEIGHT_STAGE_SLOTS, WEIGHT_STAGE_ROWS // 4, PROJ_W), F32),
            pltpu.SemaphoreType.DMA((WEIGHT_STAGE_SLOTS,)),
        ],
        compiler_params=pltpu.CompilerParams(dimension_semantics=("arbitrary", "arbitrary"),
                                             vmem_limit_bytes=VMEM_LIMIT_LARGE),
        name="ln_proj",
    )(xp, xs, xm, g, b, w, tabs, tabs, tabs)


def _ret_rope(x, cos, sin):
    x1 = x[:, :LANES]
    x2 = x[:, LANES:]
    return jnp.concatenate([x1 * cos - x2 * sin, x2 * cos + x1 * sin], axis=1)


def _group_norm_gate(o, gate):
    mu = jnp.mean(o, axis=-1, keepdims=True)
    oc = o - mu
    var = jnp.mean(oc * oc, axis=-1, keepdims=True)
    return oc * lax.rsqrt(var + LN_EPS) * _silu(gate)


def _ret_prompt_kernel(lg_ref, q_ref, k_ref, v_ref, g_ref, km_ref, vm_ref, o_ref, s_ref, s_scr):
    C = RET_CHUNK
    lg = lg_ref[pl.program_id(1)]
    ri = lax.broadcasted_iota(jnp.int32, (C, C), 0)
    ci = lax.broadcasted_iota(jnp.int32, (C, C), 1)
    rel = (ri - ci).astype(F32)
    decay = jnp.where(rel >= 0.0, jnp.exp(jnp.maximum(rel, 0.0) * lg), 0.0)
    row = lax.broadcasted_iota(jnp.int32, (C, 1), 0).astype(F32)
    q_decay = jnp.exp((row + 1.0) * lg)
    k_decay = jnp.exp((C - 1.0 - row) * lg)
    chunk_decay = jnp.exp(jnp.full((1, RET_DV), C * lg, F32))

    meta_decay = jnp.where(row >= C - N_META, k_decay, 0.0)
    s_scr[...] = _dot_tn((km_ref[...] * meta_decay).astype(BF16), vm_ref[...].astype(BF16))

    def chunk(c, carry):
        rows = pl.ds(pl.multiple_of(c * C, C), C)
        k = k_ref[rows, :]
        qb = q_ref[rows, :].astype(BF16)
        vb = v_ref[rows, :].astype(BF16)
        s_prev = s_scr[...]
        scores = _dot_nt(qb, k.astype(BF16)) * decay
        inner = _dot(scores.astype(BF16), vb)
        cross = _dot(qb, s_prev.astype(BF16)) * q_decay
        s_scr[...] = chunk_decay * s_prev + _dot_tn((k * k_decay).astype(BF16), vb)
        o_ref[rows, :] = _group_norm_gate(inner + cross, g_ref[rows, :]).astype(BF16)
        return carry

    lax.fori_loop(0, q_ref.shape[0] // C, chunk, 0, unroll=RET_CHUNK_UNROLL)
    s_ref[...] = s_scr[...]


def _ret_prompt(lg, z, zm, bsz, seq):
    col = lambda base: (lambda b, h: (b, base + h))
    mcol = lambda base: (lambda b, h: (0, base + h))
    return pl.pallas_call(
        _ret_prompt_kernel,
        grid=(bsz, RET_HEADS),
        in_specs=[
            pl.BlockSpec(memory_space=pltpu.SMEM),
            pl.BlockSpec((seq, RET_DK), col(0)),
            pl.BlockSpec((seq, RET_DK), col(RET_HEADS)),
            pl.BlockSpec((seq, RET_DV), col(2 * RET_HEADS)),
            pl.BlockSpec((seq, RET_DV), col(3 * RET_HEADS)),
            pl.BlockSpec((RET_CHUNK, RET_DK), mcol(RET_HEADS)),
            pl.BlockSpec((RET_CHUNK, RET_DV), mcol(2 * RET_HEADS)),
        ],
        out_specs=[
            pl.BlockSpec((seq, RET_DV), lambda b, h: (b, h)),
            pl.BlockSpec((None, None, RET_DK, RET_DV), lambda b, h: (b, h, 0, 0)),
        ],
        out_shape=[
            jax.ShapeDtypeStruct((bsz * seq, RET_W), BF16),
            jax.ShapeDtypeStruct((bsz, RET_HEADS, RET_DK, RET_DV), F32),
        ],
        scratch_shapes=[pltpu.VMEM((RET_DK, RET_DV), F32)],
        compiler_params=_cparams(("parallel", "parallel")),
        name="ret_prompt",
    )(lg, z, z, z, z, zm, zm)


def _ret_sample_jobs(lg_ref, z_ref, s_ref, o_ref, so_ref, *, nb, t):
    rows = nb * t
    R = RET_HEADS * rows

    def stack(base):
        return jnp.concatenate([z_ref[:, base + h * RET_DK: base + (h + 1) * RET_DK] for h in range(RET_HEADS)],
                               axis=0)

    q = stack(0)
    k = stack(RET_W)
    v = stack(2 * RET_W)
    gate = stack(3 * RET_W)

    ri = lax.broadcasted_iota(jnp.int32, (R, R), 0)
    ci = lax.broadcasted_iota(jnp.int32, (R, R), 1)
    rcol = lax.broadcasted_iota(jnp.int32, (R, 1), 0)
    lg_col = jnp.zeros((R, 1), F32)
    for h in range(RET_HEADS):
        lg_col = jnp.where((rcol >= h * rows) & (rcol < (h + 1) * rows), lg_ref[h], lg_col)
    tcol = (rcol & (t - 1)).astype(F32)
    rel = (ri - ci).astype(F32)
    same = ((ri & -t) == (ci & -t)) & (ri >= ci)
    decay = jnp.where(same, jnp.exp(jnp.maximum(rel, 0.0) * lg_col), 0.0)
    q_decay = jnp.exp((tcol + 1.0) * lg_col)
    k_decay = jnp.exp((t - 1.0 - tcol) * lg_col)

    qb = q.astype(BF16)
    vb = v.astype(BF16)
    scores = _dot_nt(qb, k.astype(BF16)) * decay
    inner = _dot(scores.astype(BF16), vb)
    kw = k * k_decay

    cross_parts = [None] * (RET_HEADS * nb)

    def state_job(h, db):
        def run():
            r0 = h * rows + db * t
            s_prev = s_ref[db, h]
            cross_parts[h * nb + db] = _dot(q[r0:r0 + t, :].astype(BF16), s_prev.astype(BF16))
            mine = (rcol >= r0) & (rcol < r0 + t)
            upd = _dot_tn(jnp.where(mine, kw, 0.0).astype(BF16), vb)
            step_decay = jnp.exp(jnp.full((1, RET_DV), t * lg_ref[h], F32))
            so_ref[db, h] = step_decay * s_prev + upd
        return run

    def finish():
        cross = jnp.concatenate(cross_parts, axis=0) * q_decay
        out = _group_norm_gate(inner + cross, gate).astype(BF16)
        for h in range(RET_HEADS):
            o_ref[:, h * RET_DV:(h + 1) * RET_DV] = out[h * rows:(h + 1) * rows, :]

    return [state_job(h, db) for h in range(RET_HEADS) for db in range(nb)], finish


def _swa_rope(x, c, s1, s2):
    return x * c + pltpu.roll(x, 8, 1) * s1 + pltpu.roll(x, LANES - 8, 1) * s2


def _dup_head(x, g, low):
    swapped = pltpu.roll(x, SWA_HD, 1)
    return jnp.where(low, x, swapped) if g == 0 else jnp.where(low, swapped, x)


def _to_kv_half(slab, head, low):
    g = head // SWA_GROUP
    src = slab if head % 2 == g else pltpu.roll(slab, SWA_HD, 1)
    return jnp.where(low, src, 0.0) if g == 0 else jnp.where(low, 0.0, src)


def _from_kv_half(o_even, o_odd, g, low):
    if g == 0:
        return jnp.where(low, o_even, pltpu.roll(o_odd, SWA_HD, 1))
    return jnp.where(low, pltpu.roll(o_even, SWA_HD, 1), o_odd)


def _sink_softmax(logits, sink):
    m = jnp.maximum(jnp.max(logits, axis=-1, keepdims=True), sink)
    p = jnp.exp(logits - m)
    return p, jnp.sum(p, axis=-1, keepdims=True) + jnp.exp(sink - m)


def _swa_prompt_kernel(sink_ref, q_ref, kv_ref, kvm_ref, o_ref, kp_ref, vp_ref, kprev, vprev):
    W = WINDOW
    m_id = pl.program_id(1)

    @pl.when(m_id == 0)
    def _():
        kprev[...] = kvm_ref[:, :LANES]
        vprev[...] = kvm_ref[:, LANES:]

    kcur = kv_ref[:, :LANES]
    vcur = kv_ref[:, LANES:]
    k2 = jnp.concatenate([kprev[...], kcur], axis=0)
    v2 = jnp.concatenate([vprev[...], vcur], axis=0)

    r = lax.broadcasted_iota(jnp.int32, (W, 2 * W), 0)
    cc = lax.broadcasted_iota(jnp.int32, (W, 2 * W), 1)
    diff = W + r - cc
    mask = (diff >= 0) & (diff < W) & ((m_id > 0) | (cc >= W - N_META))
    low_k = lax.broadcasted_iota(jnp.int32, (2 * W, LANES), 1) < SWA_HD
    low = lax.broadcasted_iota(jnp.int32, (W, LANES), 1) < SWA_HD

    def attend(q, kd, vd, h):
        p, den = _sink_softmax(jnp.where(mask, _dot_nt(q.astype(BF16), kd), NEG_INF), sink_ref[h])
        return _dot(p.astype(BF16), vd) / den

    for g in range(SWA_KV_HEADS):
        kd = _dup_head(k2, g, low_k).astype(BF16)
        vd = _dup_head(v2, g, low_k).astype(BF16)
        for p in range(g * SWA_GROUP // 2, (g + 1) * SWA_GROUP // 2):
            slab = q_ref[:, p * LANES:(p + 1) * LANES]
            o_even = attend(jnp.where(low, slab, 0.0), kd, vd, 2 * p)
            o_odd = attend(jnp.where(low, 0.0, slab), kd, vd, 2 * p + 1)
            o_ref[:, p * LANES:(p + 1) * LANES] = jnp.where(low, o_even, o_odd).astype(BF16)

    kprev[...] = kcur
    vprev[...] = vcur

    @pl.when(m_id == pl.num_programs(1) - 1)
    def _():
        kp_ref[...] = kcur.T
        vp_ref[...] = vcur.T


def _swa_prompt_ret_sample_kernel(sink_ref, q_ref, kv_ref, kvm_ref, lg_ref, zs_ref, st_ref,
                                 o_ref, kp_ref, vp_ref, os_ref, sto_ref, kprev, vprev, *, nb, t, every):
    _swa_prompt_kernel(sink_ref, q_ref, kv_ref, kvm_ref, o_ref, kp_ref, vp_ref, kprev, vprev)

    @pl.when(lax.rem(pl.program_id(1), every) == every - 1)
    def _():
        state_jobs, finish = _ret_sample_jobs(lg_ref, zs_ref, st_ref, os_ref, sto_ref, nb=nb, t=t)
        for job in state_jobs + [finish]:
            job()


def _swa_prompt_ret_sample(sinks, z, zm, bsz, seq, lg, sample_row, state, t):
    W = WINDOW
    nblk = seq // W
    every = RET_SAMPLE_EVERY
    ret_steps = bsz * nblk // every
    dbs = state.shape[0]
    nb = dbs // ret_steps
    assert nblk % every == 0 and nb * ret_steps == dbs and (nb * t) % 16 == 0
    rows = nb * t
    qw = SWA_HEADS * SWA_HD
    kv_col = SWA_KV_COL // (2 * LANES)
    ret_step =lambda b, m: (b * nblk + m) // every
    state_spec = pl.BlockSpec((nb, RET_HEADS, RET_DK, RET_DV), lambda b, m: (ret_step(b, m), 0, 0, 0))
    return pl.pallas_call(
        functools.partial(_swa_prompt_ret_sample_kernel, nb=nb, t=t, every=every),
        grid=(bsz, nblk),
        in_specs=[
            pl.BlockSpec(memory_space=pltpu.SMEM),
            pl.BlockSpec((W, qw), lambda b, m: (b * nblk + m, SWA_Q_COL // qw)),
            pl.BlockSpec((W, 2 * LANES), lambda b, m: (b * nblk + m, kv_col)),
            pl.BlockSpec((W, 2 * LANES), lambda b, m: (0, kv_col)),
            pl.BlockSpec(memory_space=pltpu.SMEM),
            pl.BlockSpec((rows, 4 * RET_W), lambda b, m: (sample_row // rows + ret_step(b, m), 0)),
            state_spec,
        ],
        out_specs=[
            pl.BlockSpec((W, qw), lambda b, m: (b * nblk + m, 0)),
            pl.BlockSpec((None, W, LANES), lambda b, m: (b, 0, 0)),
            pl.BlockSpec((None, W, LANES), lambda b, m: (b, 0, 0)),
            pl.BlockSpec((rows, RET_W), lambda b, m: (ret_step(b, m), 0)),
            state_spec,
        ],
        out_shape=[
            jax.ShapeDtypeStruct((bsz * seq, qw), BF16),
            jax.ShapeDtypeStruct((bsz, W, LANES), F32),
            jax.ShapeDtypeStruct((bsz, W, LANES), F32),
            jax.ShapeDtypeStruct((dbs * t, RET_W), BF16),
            jax.ShapeDtypeStruct(state.shape, F32),
        ],
        scratch_shapes=[pltpu.VMEM((W, LANES), F32), pltpu.VMEM((W, LANES), F32)],
        compiler_params=_cparams(("parallel", "arbitrary")),
        name="swa_prompt_ret_sample",
    )(sinks, z, z, zm, lg, z, state)


def _swa_sample_kernel(sink_ref, q_ref, kv_ref, kc_ref, vc_ref, o_ref, ko_ref, vo_ref, *, nb, t):
    W = WINDOW
    row_pad = jnp.zeros((LANES - nb * t, LANES), F32)
    knew_t = jnp.concatenate([kv_ref[:, :LANES], row_pad], axis=0).T
    vnew_t = jnp.concatenate([kv_ref[:, LANES:], row_pad], axis=0).T
    knew_b = knew_t.astype(BF16)
    vnew_b = vnew_t.astype(BF16)

    rq = SWA_HEADS * t
    tq = lax.broadcasted_iota(jnp.int32, (rq, 2 * W), 0) & (t - 1)
    cc = lax.broadcasted_iota(jnp.int32, (rq, 2 * W), 1)
    out_lane = lax.broadcasted_iota(jnp.int32, (LANES, LANES), 1)
    low = lax.broadcasted_iota(jnp.int32, (nb * t, LANES), 1) < SWA_HD
    low_t = lax.broadcasted_iota(jnp.int32, (t, LANES), 1) < SWA_HD
    sink_col = jnp.concatenate([jnp.full((t, 1), sink_ref[h], F32) for h in range(SWA_HEADS)], axis=0)
    npair = SWA_HEADS // 2

    slabs = [q_ref[:, p * LANES:(p + 1) * LANES] for p in range(npair)]
    q_heads = [_to_kv_half(slabs[h // 2], h, low) for h in range(SWA_HEADS)]

    logits, values = [], []
    for db in range(nb):
        rows = slice(db * t, (db + 1) * t)
        kc = kc_ref[db]
        vc = vc_ref[db]
        keep = out_lane < W - t
        ko_ref[db] = jnp.where(keep, pltpu.roll(kc, W - t, 1), pltpu.roll(knew_t, (W - t - db * t) % LANES, 1))
        vo_ref[db] = jnp.where(keep, pltpu.roll(vc, W - t, 1), pltpu.roll(vnew_t, (W - t - db * t) % LANES, 1))
        k2 = jnp.concatenate([kc.astype(BF16), knew_b], axis=1)
        values.append(jnp.concatenate([vc.astype(BF16), vnew_b], axis=1))
        t_new = cc - W - db * t
        mask = ((cc < W) & (cc > tq)) | ((t_new >= 0) & (t_new <= tq))
        q_db = jnp.concatenate([qh[rows, :] for qh in q_heads], axis=0).astype(BF16)
        logits.append(jnp.where(mask, _dot(q_db, k2), NEG_INF))
    p, den = _sink_softmax(jnp.concatenate(logits, axis=0), jnp.concatenate([sink_col] * nb, axis=0))
    p = p.astype(BF16)
    outs = [[] for _ in range(npair)]
    for db in range(nb):
        o = _dot_nt(p[db * rq:(db + 1) * rq, :], values[db]) / den[db * rq:(db + 1) * rq, :]
        for pr in range(npair):
            o_even = o[(2 * pr) * t:(2 * pr + 1) * t, :]
            o_odd = o[(2 * pr + 1) * t:(2 * pr + 2) * t, :]
            outs[pr].append(_from_kv_half(o_even, o_odd, 2 * pr // SWA_GROUP, low_t))
    for pr in range(npair):
        o_ref[:, pr * LANES:(pr + 1) * LANES] = jnp.concatenate(outs[pr], axis=0).astype(BF16)


def _swa_sample(sinks, z, row0, kc, vc, nb, t):
    dbs = kc.shape[0]
    n = dbs * t
    rows = nb * t
    W = WINDOW
    qw = SWA_HEADS * SWA_HD
    cache = pl.BlockSpec((nb, W, LANES), lambda i: (i, 0, 0))
    return pl.pallas_call(
        functools.partial(_swa_sample_kernel, nb=nb, t=t),
        grid=(dbs // nb,),
        in_specs=[
            pl.BlockSpec(memory_space=pltpu.SMEM),
            pl.BlockSpec((rows, qw), lambda i: (row0 // rows + i, SWA_Q_COL // qw)),
            pl.BlockSpec((rows, 2 * LANES), lambda i: (row0 // rows + i, SWA_KV_COL // (2 * LANES))),
            cache, cache,
        ],
        out_specs=[pl.BlockSpec((rows, qw), lambda i: (i, 0)), cache, cache],
        out_shape=[
            jax.ShapeDtypeStruct((n, qw), BF16),
            jax.ShapeDtypeStruct((dbs, W, LANES), F32),
            jax.ShapeDtypeStruct((dbs, W, LANES), F32),
        ],
        compiler_params=_cparams(("parallel",)),
        name="swa_sample",
    )(sinks, z, z, kc, vc)


def _out_proj_ln_kernel(rop_ref, sop_ref, xp_ref, ros_ref, sos_ref, xs_ref, eg_ref, eb_ref, w_hbm, mg_ref, mb_ref,
                        h_ref, hb_ref, wb_scr, stage, sem, *, np_tiles):
    i = pl.program_id(0)

    def store_chunk(rows, slot):
        wb_scr[rows, :] = slot[...].astype(BF16)

    @pl.when(i == 0)
    def _():
        _stage_weight(w_hbm, stage, sem, store_chunk)

    def step(ro_ref, so_ref, x_ref):
        sub = h_ref.shape[0] // 2
        for r in range(2):
            rows = slice(r * sub, (r + 1) * sub)
            x_in = _layer_norm(x_ref[rows, :], eg_ref[...], eb_ref[...])
            mixed = _dot(ro_ref[rows, :], wb_scr[:RET_W, :]) + _dot(so_ref[rows, :], wb_scr[RET_W:, :])
            h = _layer_norm(ALPHA * x_in + mixed, mg_ref[...], mb_ref[...])
            h_ref[rows, :] = h
            hb_ref[rows, :] = h.astype(BF16)

    @pl.when(i < np_tiles)
    def _():
        step(rop_ref, sop_ref, xp_ref)

    @pl.when(i >= np_tiles)
    def _():
        step(ros_ref, sos_ref, xs_ref)


def _out_proj_ln(ro_p, so_p, xp, ro_s, so_s, xs, eg, eb, w, mg, mb, tm):
    np_tiles = xp.shape[0] // tm
    ns_tiles = xs.shape[0] // tm
    n = xp.shape[0] + xs.shape[0]
    vec = pl.BlockSpec((1, D_MODEL), lambda i: (0, 0))
    tile = pl.BlockSpec((tm, D_MODEL), lambda i: (i, 0))
    p_idx = lambda i: (jnp.minimum(i, np_tiles - 1), 0)
    s_idx = lambda i: (jnp.clip(i - np_tiles, 0, ns_tiles - 1), 0)
    return pl.pallas_call(
        functools.partial(_out_proj_ln_kernel, np_tiles=np_tiles),
        grid=(np_tiles + ns_tiles,),
        in_specs=[
            pl.BlockSpec((tm, RET_W), p_idx), pl.BlockSpec((tm, RET_W), p_idx), pl.BlockSpec((tm, D_MODEL), p_idx),
            pl.BlockSpec((tm, RET_W), s_idx), pl.BlockSpec((tm, RET_W), s_idx), pl.BlockSpec((tm, D_MODEL), s_idx),
            vec, vec,
            pl.BlockSpec(memory_space=pl.ANY),
            vec, vec,
        ],
        out_specs=[tile, tile],
        out_shape=[jax.ShapeDtypeStruct((n, D_MODEL), F32), jax.ShapeDtypeStruct((n, D_MODEL), BF16)],
        scratch_shapes=[pltpu.VMEM((D_MODEL, D_MODEL), BF16),
                        pltpu.VMEM((WEIGHT_STAGE_SLOTS, WEIGHT_STAGE_ROWS // 2, D_MODEL), F32),
                        pltpu.SemaphoreType.DMA((WEIGHT_STAGE_SLOTS,))],
        compiler_params=pltpu.CompilerParams(dimension_semantics=("arbitrary",), vmem_limit_bytes=VMEM_LIMIT_LARGE),
        name="out_proj_ln",
    )(ro_p, so_p, xp, ro_s, so_s, xs, eg, eb, w, mg, mb)


def _ffn_ln_kernel(hb_ref, h_hbm, wg_ref, wu_ref, wd_ref, g_ref, b_ref, y_ref, *rest, row0):
    *w_out_refs, h_res, sem = rest
    i = pl.program_id(0)
    j = pl.program_id(1)
    tm = h_res.shape[0]
    residual_copy = pltpu.make_async_copy(h_hbm.at[pl.ds(row0 + i * tm, tm), :], h_res, sem)

    @pl.when(j == 0)
    def _():
        residual_copy.start()
        y_ref[...] = jnp.zeros_like(y_ref)

    wg, wu, wd = (w[...].astype(BF16) for w in (wg_ref, wu_ref, wd_ref))
    for out_ref, w in zip(w_out_refs, (wg, wu, wd)):
        out_ref[...] = w
    hb = hb_ref[...]
    act = _silu(_dot(hb, wg)) * _dot(hb, wu)
    y_ref[...] += _dot(act.astype(BF16), wd)

    @pl.when(j == pl.num_programs(1) - 1)
    def _():
        residual_copy.wait()
        y_ref[...] = _layer_norm(ALPHA * h_res[...] + y_ref[...], g_ref[...], b_ref[...])


def _ffn_ln(h, hb, row0, n, wg, wu, wd, g, b, tm, th):
    emit_weights = wg.dtype == F32
    assert not emit_weights or n == tm
    vec = pl.BlockSpec((1, D_MODEL), lambda i, j: (0, 0))
    w_specs = [
        pl.BlockSpec((D_MODEL, th), lambda i, j: (0, j)),
        pl.BlockSpec((D_MODEL, th), lambda i, j: (0, j)),
        pl.BlockSpec((th, D_MODEL), lambda i, j: (j, 0)),
    ]
    out_specs = [pl.BlockSpec((tm, D_MODEL), lambda i, j: (i, 0))]
    out_shape = [jax.ShapeDtypeStruct((n, D_MODEL), F32)]
    if emit_weights:
        out_specs += w_specs
        out_shape += [jax.ShapeDtypeStruct(w.shape, BF16) for w in (wg, wu, wd)]
    outs = pl.pallas_call(
        functools.partial(_ffn_ln_kernel, row0=row0),
        grid=(n // tm, FFN_HIDDEN // th),
        in_specs=[pl.BlockSpec((tm, D_MODEL), lambda i, j: (row0 // tm + i, 0)), pl.BlockSpec(memory_space=pl.ANY)]
        + w_specs + [vec, vec],
        out_specs=out_specs,
        out_shape=out_shape,
        scratch_shapes=[pltpu.VMEM((tm, D_MODEL), F32), pltpu.SemaphoreType.DMA(())],
        compiler_params=pltpu.CompilerParams(dimension_semantics=("arbitrary", "arbitrary"),
                                             vmem_limit_bytes=VMEM_LIMIT_LARGE),
        name="ffn_ln",
    )(hb, h, wg, wu, wd, g, b)
    return tuple(outs) if emit_weights else outs[0]


def _position_tables(seq, t, past_len, sample_rows):
    r = jnp.arange(seq + sample_rows + RET_CHUNK)
    meta_r = r - seq - sample_rows
    pos = jnp.where(r < seq, N_META + r,
                    jnp.where(meta_r < 0, past_len + (r - seq) % t,
                              jnp.maximum(meta_r - (RET_CHUNK - N_META), 0))).astype(F32)[:, None]
    ret_freq = jnp.power(RET_THETA, -jnp.linspace(0.0, 1.0, RET_DK // 2, dtype=F32))[None, :]
    a = (N_META + RET_CHUNK * jnp.arange(seq // RET_CHUNK)).astype(F32)[:, None] * ret_freq
    b = jnp.arange(RET_CHUNK).astype(F32)[:, None] * ret_freq
    ca, sa, cb, sb = jnp.cos(a)[:, None, :], jnp.sin(a)[:, None, :], jnp.cos(b)[None], jnp.sin(b)[None]
    rest_ang = pos[seq:] * ret_freq
    ret_cos = jnp.concatenate([(ca * cb - sa * sb).reshape(seq, LANES), jnp.cos(rest_ang)], axis=0)
    ret_sin = jnp.concatenate([(sa * cb + ca * sb).reshape(seq, LANES), jnp.sin(rest_ang)], axis=0)
    half = ROT_DIM // 2
    swa_freq = jnp.power(ROPE_THETA, -jnp.arange(0, ROT_DIM, 2, dtype=F32) / ROT_DIM)
    swa_ang = pos * swa_freq[None, :]
    cos = jnp.tile(jnp.cos(swa_ang), (1, LANES // half))
    sin = jnp.tile(jnp.sin(swa_ang), (1, LANES // half))
    d = jnp.arange(LANES) % SWA_HD
    c = jnp.where(d < ROT_DIM, cos, 1.0)
    s1 = jnp.where((d >= half) & (d < ROT_DIM), sin, 0.0)
    s2 = jnp.where(d < half, -sin, 0.0)
    return jnp.concatenate([ret_cos, ret_sin, c, s1, s2], axis=1)


def kernel(x_prompt, x_sample, state_ret, cache_swa_k, cache_swa_v, meta_tokens, ln_emb_g, ln_emb_b,
           w_in, w_out, swa_sinks, ln_mix_g, ln_mix_b, w_ffn_gate, w_ffn_up, w_ffn_down, ln_ffn_g, ln_ffn_b):
    bsz, seq, d = x_prompt.shape
    dbs, t, _ = x_sample.shape
    assert w_in.shape[0] == DEPTH and d == D_MODEL and seq % RET_CHUNK == 0 and t & (t - 1) == 0
    past_len = 16384
    row = lambda a: a.reshape(1, -1)

    eg, eb = row(ln_emb_g), row(ln_emb_b)
    sinks = swa_sinks[0]
    lg = jnp.log(1.0 - jnp.power(2.0, -5.0 - jnp.arange(RET_HEADS, dtype=F32)))

    xp = x_prompt.reshape(bsz * seq, d)
    xs = x_sample.reshape(dbs * t, d)

    tabs = _position_tables(seq, t, past_len, LN_PROJ_TILE)
    z, zm = _ln_proj(xp, xs, meta_tokens, eg, eb, w_in[0], tabs, seq, LN_PROJ_TILE, PROJ_W)
    sample_row = bsz * seq

    ret_o_p, ret_state_p = _ret_prompt(lg, z, zm, bsz, seq)
    swa_o_p, k_p, v_p, ret_o_s, ret_state_s = _swa_prompt_ret_sample(
        sinks, z, zm, bsz, seq, lg, sample_row, state_ret[0], t)
    to_kernel = lambda a: jnp.transpose(a[0], (0, 2, 3, 1)).reshape(-1, LANES, WINDOW)
    from_kernel = lambda a: jnp.transpose(a.reshape(-1, SWA_KV_HEADS, SWA_HD, WINDOW), (0, 3, 1, 2))[None]
    kc, vc = to_kernel(cache_swa_k), to_kernel(cache_swa_v)
    swa_o_s, k_s, v_s = _swa_sample(sinks, z, sample_row, kc, vc, SWA_SAMPLE_BATCHES, t)

    mg, mb = row(ln_mix_g[0]), row(ln_mix_b[0])
    fg, fb = row(ln_ffn_g[0]), row(ln_ffn_b[0])
    ffn_w = (w_ffn_gate[0], w_ffn_up[0], w_ffn_down[0])
    h, hb = _out_proj_ln(ret_o_p, swa_o_p, xp, ret_o_s, swa_o_s, xs, eg, eb, w_out[0], mg, mb, 512)
    y_s, *ffn_wb = _ffn_ln(h, hb, sample_row, dbs * t, *ffn_w, fg, fb, TOK_TILE, 256)
    y_p = _ffn_ln(h, hb, 0, bsz * seq, *ffn_wb, fg, fb, TOK_TILE, 512)

    return (y_p.reshape(bsz, seq, d), y_s.reshape(dbs, t, d),
            ret_state_p[None], from_kernel(k_p), from_kernel(v_p),
            ret_state_s[None], from_kernel(k_s), from_kernel(v_s))
```

```python
import functools

import jax
import jax.numpy as jnp
from jax import lax
from jax.experimental import pallas as pl
from jax.experimental.pallas import tpu as pltpu

F32 = jnp.float32
BF16 = jnp.bfloat16

D_MODEL = 2048
N_META = 16
RET_HEADS = 4
RET_DK = 256
RET_DV = 256
RET_CHUNK = 128
RET_THETA = 10000.0
SWA_HD = 64
SWA_HEADS = 16
SWA_KV_HEADS = 2
SWA_GROUP = SWA_HEADS // SWA_KV_HEADS
WINDOW = 128
ROPE_THETA = 500000.0
ROT_DIM = SWA_HD // 4
FFN_HIDDEN = 5632
PROJ_W = 5376
DEPTH = 1
ALPHA = (2.0 * DEPTH) ** 0.25
LN_EPS = 1e-5
NEG_INF = -1e30

LANES = 128
RET_W = RET_HEADS * RET_DK
SWA_Q_COL = 4 * RET_W
SWA_KV_COL = SWA_Q_COL + SWA_HEADS * SWA_HD
VMEM_LIMIT = 56 * 1024 * 1024
VMEM_LIMIT_LARGE = 60 * 1024 * 1024
TOK_TILE = 1024
LN_PROJ_SUBBLOCKS = 2
RET_CHUNK_UNROLL = 16
RET_SAMPLE_EVERY = 2
LN_PROJ_TILE = 256
SWA_SAMPLE_BATCHES = 8
WEIGHT_STAGE_ROWS = 256
WEIGHT_STAGE_SLOTS = 8


def _cparams(sem):
    return pltpu.CompilerParams(dimension_semantics=sem, vmem_limit_bytes=VMEM_LIMIT)


def _layer_norm(x, g, b):
    mu = jnp.mean(x, axis=-1, keepdims=True)
    xc = x - mu
    var = jnp.mean(xc * xc, axis=-1, keepdims=True)
    return xc * lax.rsqrt(var + LN_EPS) * g + b


def _silu(x):
    return x / (1.0 + jnp.exp(-x))


def _dot(a, b):
    return jnp.dot(a, b, preferred_element_type=F32)


def _dot_nt(a, b):
    return lax.dot_general(a, b, (((1,), (1,)), ((), ())), preferred_element_type=F32)


def _dot_tn(a, b):
    return lax.dot_general(a, b, (((0,), (0,)), ((), ())), preferred_element_type=F32)


def _store_rotated(z_ref, rows, res, tab):
    cos, sin, c, s1, s2 = (tab[:, n * LANES:(n + 1) * LANES] for n in range(5))
    for c0 in range(0, 2 * RET_W, RET_DK):
        rot = _ret_rope(res[:, c0:c0 + RET_DK], cos, sin)
        z_ref[rows, c0:c0 + RET_DK] = rot if c0 < RET_W else rot * RET_DK ** -0.5
    z_ref[rows, 2 * RET_W:SWA_Q_COL] = res[:, 2 * RET_W:SWA_Q_COL]
    for c0 in range(SWA_Q_COL, SWA_KV_COL + LANES, LANES):
        rot = _swa_rope(res[:, c0:c0 + LANES], c, s1, s2)
        z_ref[rows, c0:c0 + LANES] = rot * SWA_HD ** -0.5 if c0 < SWA_KV_COL else rot
    z_ref[rows, SWA_KV_COL + LANES:] = res[:, SWA_KV_COL + LANES:]


def _stage_weight(w_hbm, stage, sem, store_chunk):
    slots, chunk = stage.shape[:2]
    n_chunks = w_hbm.shape[0] // chunk

    def weight_copy(k):
        return pltpu.make_async_copy(w_hbm.at[pl.ds(k * chunk, chunk), :], stage.at[k % slots], sem.at[k % slots])

    for k in range(min(slots - 1, n_chunks)):
        weight_copy(k).start()
    for k in range(n_chunks):
        if k + slots - 1 < n_chunks:
            weight_copy(k + slots - 1).start()
        weight_copy(k).wait()
        store_chunk(slice(k * chunk, (k + 1) * chunk), stage.at[k % slots])


def _ln_proj_kernel(xp_ref, xs_ref, xm_ref, g_ref, b_ref, w_hbm, tabp_ref, tabs_ref, tabm_ref, z_ref, zm_ref,
                    hm_scr, wb_scr, stage, sem, *, np_tiles, ns_tiles):
    i = pl.program_id(0)
    j = pl.program_id(1)
    first = j == 0
    is_sample = i >= np_tiles
    is_last = i == np_tiles + ns_tiles - 1
    n_col, _, tn = wb_scr.shape

    def store_chunk(rows, slot):
        for jt in range(n_col):
            wb_scr[jt, rows, :] = slot[:, jt * tn:(jt + 1) * tn].astype(BF16)

    @pl.when(first & (i == 0))
    def _():
        _stage_weight(w_hbm, stage, sem, store_chunk)

    def norm(x):
        return _layer_norm(x, g_ref[...], b_ref[...]).astype(BF16)

    def project(x_ref, tab_ref):
        sub = x_ref.shape[0] // LN_PROJ_SUBBLOCKS
        for r in range(LN_PROJ_SUBBLOCKS):
            rows = slice(r * sub, (r + 1) * sub)
            _store_rotated(z_ref, rows, _dot(norm(x_ref[rows, :]), wb_scr[j]), tab_ref[rows, :])

    @pl.when(jnp.logical_not(is_sample))
    def _():
        project(xp_ref, tabp_ref)

    @pl.when(is_sample)
    def _():
        project(xs_ref, tabs_ref)

    @pl.when(is_last)
    def _():
        lead = hm_scr.shape[0] - xm_ref.shape[0]
        hm_scr[:lead, :] = jnp.zeros((lead, D_MODEL), BF16)
        hm_scr[lead:, :] = norm(xm_ref[...])
        _store_rotated(zm_ref, slice(None), _dot(hm_scr[...], wb_scr[j]), tabm_ref[...])


def _ln_proj(xp, xs, xm, g, b, w, tabs, seq, tm, tn):
    np_tiles = xp.shape[0] // tm
    ns_tiles = xs.shape[0] // tm
    last = np_tiles + ns_tiles - 1
    n_col = PROJ_W // tn
    assert n_col == 1
    tab_w = tabs.shape[1]
    return pl.pallas_call(
        functools.partial(_ln_proj_kernel, np_tiles=np_tiles, ns_tiles=ns_tiles),
        grid=(np_tiles + ns_tiles, n_col),
        in_specs=[
            pl.BlockSpec((tm, D_MODEL), lambda i, j: (jnp.minimum(i, np_tiles - 1), 0)),
            pl.BlockSpec((tm, D_MODEL), lambda i, j: (jnp.clip(i - np_tiles, 0, ns_tiles - 1), 0)),
            pl.BlockSpec(xm.shape, lambda i, j: (0, 0), pipeline_mode=pl.Buffered(1)),
            pl.BlockSpec((1, D_MODEL), lambda i, j: (0, 0)),
            pl.BlockSpec((1, D_MODEL), lambda i, j: (0, 0)),
            pl.BlockSpec(memory_space=pl.ANY),
            pl.BlockSpec((tm, tab_w), lambda i, j: (lax.rem(i, seq // tm), 0)),
            pl.BlockSpec((tm, tab_w), lambda i, j: (seq // tm, 0)),
            pl.BlockSpec((RET_CHUNK, tab_w), lambda i, j: ((seq + tm) // RET_CHUNK, 0)),
        ],
        out_specs=[
            pl.BlockSpec((tm, tn), lambda i, j: (i, j)),
            pl.BlockSpec((RET_CHUNK, tn), lambda i, j: (0, jnp.where(i == last, j, 0))),
        ],
        out_shape=[
            jax.ShapeDtypeStruct((xp.shape[0] + xs.shape[0], PROJ_W), F32),
            jax.ShapeDtypeStruct((RET_CHUNK, PROJ_W), F32),
        ],
        scratch_shapes=[
            pltpu.VMEM((RET_CHUNK, D_MODEL), BF16),
            pltpu.VMEM((n_col, D_MODEL, tn), BF16),
            pltpu.VMEM((WEIGHT_STAGE_SLOTS, WEIGHT_STAGE_ROWS // 8, PROJ_W), F32),
            pltpu.SemaphoreType.DMA((WEIGHT_STAGE_SLOTS,)),
        ],
        compiler_params=pltpu.CompilerParams(dimension_semantics=("arbitrary", "arbitrary"),
                                             vmem_limit_bytes=VMEM_LIMIT_LARGE),
        name="ln_proj",
    )(xp, xs, xm, g, b, w, tabs, tabs, tabs)


def _ret_rope(x, cos, sin):
    x1 = x[:, :LANES]
    x2 = x[:, LANES:]
    return jnp.concatenate([x1 * cos - x2 * sin, x2 * cos + x1 * sin], axis=1)


def _group_norm_gate(o, gate):
    mu = jnp.mean(o, axis=-1, keepdims=True)
    oc = o - mu
    var = jnp.mean(oc * oc, axis=-1, keepdims=True)
    return oc * lax.rsqrt(var + LN_EPS) * _silu(gate)


def _ret_prompt_kernel(lg_ref, q_ref, k_ref, v_ref, g_ref, km_ref, vm_ref, o_ref, s_ref, s_scr):
    C = RET_CHUNK
    lg = lg_ref[pl.program_id(1)]
    ri = lax.broadcasted_iota(jnp.int32, (C, C), 0)
    ci = lax.broadcasted_iota(jnp.int32, (C, C), 1)
    rel = (ri - ci).astype(F32)
    decay = jnp.where(rel >= 0.0, jnp.exp(jnp.maximum(rel, 0.0) * lg), 0.0)
    row = lax.broadcasted_iota(jnp.int32, (C, 1), 0).astype(F32)
    q_decay = jnp.exp((row + 1.0) * lg)
    k_decay = jnp.exp((C - 1.0 - row) * lg)
    chunk_decay = jnp.exp(jnp.full((1, RET_DV), C * lg, F32))

    meta_decay = jnp.where(row >= C - N_META, k_decay, 0.0)
    s_scr[...] = _dot_tn((km_ref[...] * meta_decay).astype(BF16), vm_ref[...].astype(BF16))

    def chunk(c, carry):
        rows = pl.ds(pl.multiple_of(c * C, C), C)
        k = k_ref[rows, :]
        qb = q_ref[rows, :].astype(BF16)
        vb = v_ref[rows, :].astype(BF16)
        s_prev = s_scr[...]
        scores = _dot_nt(qb, k.astype(BF16)) * decay
        inner = _dot(scores.astype(BF16), vb)
        cross = _dot(qb, s_prev.astype(BF16)) * q_decay
        s_scr[...] = chunk_decay * s_prev + _dot_tn((k * k_decay).astype(BF16), vb)
        o_ref[rows, :] = _group_norm_gate(inner + cross, g_ref[rows, :]).astype(BF16)
        return carry

    lax.fori_loop(0, q_ref.shape[0] // C, chunk, 0, unroll=RET_CHUNK_UNROLL)
    s_ref[...] = s_scr[...]


def _ret_prompt(lg, z, zm, bsz, seq):
    col = lambda base: (lambda b, h: (b, base + h))
    mcol = lambda base: (lambda b, h: (0, base + h))
    return pl.pallas_call(
        _ret_prompt_kernel,
        grid=(bsz, RET_HEADS),
        in_specs=[
            pl.BlockSpec(memory_space=pltpu.SMEM),
            pl.BlockSpec((seq, RET_DK), col(0)),
            pl.BlockSpec((seq, RET_DK), col(RET_HEADS)),
            pl.BlockSpec((seq, RET_DV), col(2 * RET_HEADS)),
            pl.BlockSpec((seq, RET_DV), col(3 * RET_HEADS)),
            pl.BlockSpec((RET_CHUNK, RET_DK), mcol(RET_HEADS)),
            pl.BlockSpec((RET_CHUNK, RET_DV), mcol(2 * RET_HEADS)),
        ],
        out_specs=[
            pl.BlockSpec((seq, RET_DV), lambda b, h: (b, h)),
            pl.BlockSpec((None, None, RET_DK, RET_DV), lambda b, h: (b, h, 0, 0)),
        ],
        out_shape=[
            jax.ShapeDtypeStruct((bsz * seq, RET_W), BF16),
            jax.ShapeDtypeStruct((bsz, RET_HEADS, RET_DK, RET_DV), F32),
        ],
        scratch_shapes=[pltpu.VMEM((RET_DK, RET_DV), F32)],
        compiler_params=_cparams(("parallel", "parallel")),
        name="ret_prompt",
    )(lg, z, z, z, z, zm, zm)


def _ret_sample_jobs(lg_ref, z_ref, s_ref, o_ref, so_ref, *, nb, t):
    rows = nb * t
    R = RET_HEADS * rows

    def stack(base):
        return jnp.concatenate([z_ref[:, base + h * RET_DK: base + (h + 1) * RET_DK] for h in range(RET_HEADS)],
                               axis=0)

    q = stack(0)
    k = stack(RET_W)
    v = stack(2 * RET_W)
    gate = stack(3 * RET_W)

    ri = lax.broadcasted_iota(jnp.int32, (R, R), 0)
    ci = lax.broadcasted_iota(jnp.int32, (R, R), 1)
    rcol = lax.broadcasted_iota(jnp.int32, (R, 1), 0)
    lg_col = jnp.zeros((R, 1), F32)
    for h in range(RET_HEADS):
        lg_col = jnp.where((rcol >= h * rows) & (rcol < (h + 1) * rows), lg_ref[h], lg_col)
    tcol = (rcol & (t - 1)).astype(F32)
    rel = (ri - ci).astype(F32)
    same = ((ri & -t) == (ci & -t)) & (ri >= ci)
    decay = jnp.where(same, jnp.exp(jnp.maximum(rel, 0.0) * lg_col), 0.0)
    q_decay = jnp.exp((tcol + 1.0) * lg_col)
    k_decay = jnp.exp((t - 1.0 - tcol) * lg_col)

    qb = q.astype(BF16)
    vb = v.astype(BF16)
    scores = _dot_nt(qb, k.astype(BF16)) * decay
    inner = _dot(scores.astype(BF16), vb)
    kw = k * k_decay

    cross_parts = [None] * (RET_HEADS * nb)

    def state_job(h, db):
        def run():
            r0 = h * rows + db * t
            s_prev = s_ref[db, h]
            cross_parts[h * nb + db] = _dot(q[r0:r0 + t, :].astype(BF16), s_prev.astype(BF16))
            mine = (rcol >= r0) & (rcol < r0 + t)
            upd = _dot_tn(jnp.where(mine, kw, 0.0).astype(BF16), vb)
            step_decay = jnp.exp(jnp.full((1, RET_DV), t * lg_ref[h], F32))
            so_ref[db, h] = step_decay * s_prev + upd
        return run

    def finish():
        cross = jnp.concatenate(cross_parts, axis=0) * q_decay
        out = _group_norm_gate(inner + cross, gate).astype(BF16)
        for h in range(RET_HEADS):
            o_ref[:, h * RET_DV:(h + 1) * RET_DV] = out[h * rows:(h + 1) * rows, :]

    return [state_job(h, db) for h in range(RET_HEADS) for db in range(nb)], finish


def _swa_rope(x, c, s1, s2):
    return x * c + pltpu.roll(x, 8, 1) * s1 + pltpu.roll(x, LANES - 8, 1) * s2


def _dup_head(x, g, low):
    swapped = pltpu.roll(x, SWA_HD, 1)
    return jnp.where(low, x, swapped) if g == 0 else jnp.where(low, swapped, x)


def _to_kv_half(slab, head, low):
    g = head // SWA_GROUP
    src = slab if head % 2 == g else pltpu.roll(slab, SWA_HD, 1)
    return jnp.where(low, src, 0.0) if g == 0 else jnp.where(low, 0.0, src)


def _from_kv_half(o_even, o_odd, g, low):
    if g == 0:
        return jnp.where(low, o_even, pltpu.roll(o_odd, SWA_HD, 1))
    return jnp.where(low, pltpu.roll(o_even, SWA_HD, 1), o_odd)


def _sink_softmax(logits, sink):
    m = jnp.maximum(jnp.max(logits, axis=-1, keepdims=True), sink)
    p = jnp.exp(logits - m)
    return p, jnp.sum(p, axis=-1, keepdims=True) + jnp.exp(sink - m)


def _swa_prompt_kernel(sink_ref, q_ref, kv_ref, kvm_ref, o_ref, kp_ref, vp_ref, kprev, vprev):
    W = WINDOW
    m_id = pl.program_id(1)

    @pl.when(m_id == 0)
    def _():
        kprev[...] = kvm_ref[:, :LANES]
        vprev[...] = kvm_ref[:, LANES:]

    kcur = kv_ref[:, :LANES]
    vcur = kv_ref[:, LANES:]
    k2 = jnp.concatenate([kprev[...], kcur], axis=0)
    v2 = jnp.concatenate([vprev[...], vcur], axis=0)

    r = lax.broadcasted_iota(jnp.int32, (W, 2 * W), 0)
    cc = lax.broadcasted_iota(jnp.int32, (W, 2 * W), 1)
    diff = W + r - cc
    mask = (diff >= 0) & (diff < W) & ((m_id > 0) | (cc >= W - N_META))
    low_k = lax.broadcasted_iota(jnp.int32, (2 * W, LANES), 1) < SWA_HD
    low = lax.broadcasted_iota(jnp.int32, (W, LANES), 1) < SWA_HD

    def attend(q, kd, vd, h):
        p, den = _sink_softmax(jnp.where(mask, _dot_nt(q.astype(BF16), kd), NEG_INF), sink_ref[h])
        return _dot(p.astype(BF16), vd) / den

    for g in range(SWA_KV_HEADS):
        kd = _dup_head(k2, g, low_k).astype(BF16)
        vd = _dup_head(v2, g, low_k).astype(BF16)
        for p in range(g * SWA_GROUP // 2, (g + 1) * SWA_GROUP // 2):
            slab = q_ref[:, p * LANES:(p + 1) * LANES]
            o_even = attend(jnp.where(low, slab, 0.0), kd, vd, 2 * p)
            o_odd = attend(jnp.where(low, 0.0, slab), kd, vd, 2 * p + 1)
            o_ref[:, p * LANES:(p + 1) * LANES] = jnp.where(low, o_even, o_odd).astype(BF16)

    kprev[...] = kcur
    vprev[...] = vcur

    @pl.when(m_id == pl.num_programs(1) - 1)
    def _():
        kp_ref[...] = kcur.T
        vp_ref[...] = vcur.T


def _swa_prompt_ret_sample_kernel(sink_ref, q_ref, kv_ref, kvm_ref, lg_ref, zs_ref, st_ref,
                                 o_ref, kp_ref, vp_ref, os_ref, sto_ref, kprev, vprev, *, nb, t, every):
    _swa_prompt_kernel(sink_ref, q_ref, kv_ref, kvm_ref, o_ref, kp_ref, vp_ref, kprev, vprev)

    @pl.when(lax.rem(pl.program_id(1), every) == every - 1)
    def _():
        state_jobs, finish = _ret_sample_jobs(lg_ref, zs_ref, st_ref, os_ref, sto_ref, nb=nb, t=t)
        for job in state_jobs + [finish]:
            job()


def _swa_prompt_ret_sample(sinks, z, zm, bsz, seq, lg, sample_row, state, t):
    W = WINDOW
    nblk = seq // W
    every = RET_SAMPLE_EVERY
    ret_steps = bsz * nblk // every
    dbs = state.shape[0]
    nb = dbs // ret_steps
    assert nblk % every == 0 and nb * ret_steps == dbs and (nb * t) % 16 == 0
    rows = nb * t
    qw = SWA_HEADS * SWA_HD
    kv_col = SWA_KV_COL // (2 * LANES)
    ret_step =lambda b, m: (b * nblk + m) // every
    state_spec = pl.BlockSpec((nb, RET_HEADS, RET_DK, RET_DV), lambda b, m: (ret_step(b, m), 0, 0, 0))
    return pl.pallas_call(
        functools.partial(_swa_prompt_ret_sample_kernel, nb=nb, t=t, every=every),
        grid=(bsz, nblk),
        in_specs=[
            pl.BlockSpec(memory_space=pltpu.SMEM),
            pl.BlockSpec((W, qw), lambda b, m: (b * nblk + m, SWA_Q_COL // qw)),
            pl.BlockSpec((W, 2 * LANES), lambda b, m: (b * nblk + m, kv_col)),
            pl.BlockSpec((W, 2 * LANES), lambda b, m: (0, kv_col)),
            pl.BlockSpec(memory_space=pltpu.SMEM),
            pl.BlockSpec((rows, 4 * RET_W), lambda b, m: (sample_row // rows + ret_step(b, m), 0)),
            state_spec,
        ],
        out_specs=[
            pl.BlockSpec((W, qw), lambda b, m: (b * nblk + m, 0)),
            pl.BlockSpec((None, W, LANES), lambda b, m: (b, 0, 0)),
            pl.BlockSpec((None, W, LANES), lambda b, m: (b, 0, 0)),
            pl.BlockSpec((rows, RET_W), lambda b, m: (ret_step(b, m), 0)),
            state_spec,
        ],
        out_shape=[
            jax.ShapeDtypeStruct((bsz * seq, qw), BF16),
            jax.ShapeDtypeStruct((bsz, W, LANES), F32),
            jax.ShapeDtypeStruct((bsz, W, LANES), F32),
            jax.ShapeDtypeStruct((dbs * t, RET_W), BF16),
            jax.ShapeDtypeStruct(state.shape, F32),
        ],
        scratch_shapes=[pltpu.VMEM((W, LANES), F32), pltpu.VMEM((W, LANES), F32)],
        compiler_params=_cparams(("parallel", "arbitrary")),
        name="swa_prompt_ret_sample",
    )(sinks, z, z, zm, lg, z, state)


def _swa_sample_kernel(sink_ref, q_ref, kv_ref, kc_ref, vc_ref, o_ref, ko_ref, vo_ref, *, nb, t):
    W = WINDOW
    row_pad = jnp.zeros((LANES - nb * t, LANES), F32)
    knew_t = jnp.concatenate([kv_ref[:, :LANES], row_pad], axis=0).T
    vnew_t = jnp.concatenate([kv_ref[:, LANES:], row_pad], axis=0).T
    knew_b = knew_t.astype(BF16)
    vnew_b = vnew_t.astype(BF16)

    rq = SWA_HEADS * t
    tq = lax.broadcasted_iota(jnp.int32, (rq, 2 * W), 0) & (t - 1)
    cc = lax.broadcasted_iota(jnp.int32, (rq, 2 * W), 1)
    out_lane = lax.broadcasted_iota(jnp.int32, (LANES, LANES), 1)
    low = lax.broadcasted_iota(jnp.int32, (nb * t, LANES), 1) < SWA_HD
    low_t = lax.broadcasted_iota(jnp.int32, (t, LANES), 1) < SWA_HD
    sink_col = jnp.concatenate([jnp.full((t, 1), sink_ref[h], F32) for h in range(SWA_HEADS)], axis=0)
    npair = SWA_HEADS // 2

    slabs = [q_ref[:, p * LANES:(p + 1) * LANES] for p in range(npair)]
    q_heads = [_to_kv_half(slabs[h // 2], h, low) for h in range(SWA_HEADS)]

    logits, values = [], []
    for db in range(nb):
        rows = slice(db * t, (db + 1) * t)
        kc = kc_ref[db]
        vc = vc_ref[db]
        keep = out_lane < W - t
        ko_ref[db] = jnp.where(keep, pltpu.roll(kc, W - t, 1), pltpu.roll(knew_t, (W - t - db * t) % LANES, 1))
        vo_ref[db] = jnp.where(keep, pltpu.roll(vc, W - t, 1), pltpu.roll(vnew_t, (W - t - db * t) % LANES, 1))
        k2 = jnp.concatenate([kc.astype(BF16), knew_b], axis=1)
        values.append(jnp.concatenate([vc.astype(BF16), vnew_b], axis=1))
        t_new = cc - W - db * t
        mask = ((cc < W) & (cc > tq)) | ((t_new >= 0) & (t_new <= tq))
        q_db = jnp.concatenate([qh[rows, :] for qh in q_heads], axis=0).astype(BF16)
        logits.append(jnp.where(mask, _dot(q_db, k2), NEG_INF))
    p, den = _sink_softmax(jnp.concatenate(logits, axis=0), jnp.concatenate([sink_col] * nb, axis=0))
    p = p.astype(BF16)
    outs = [[] for _ in range(npair)]
    for db in range(nb):
        o = _dot_nt(p[db * rq:(db + 1) * rq, :], values[db]) / den[db * rq:(db + 1) * rq, :]
        for pr in range(npair):
            o_even = o[(2 * pr) * t:(2 * pr + 1) * t, :]
            o_odd = o[(2 * pr + 1) * t:(2 * pr + 2) * t, :]
            outs[pr].append(_from_kv_half(o_even, o_odd, 2 * pr // SWA_GROUP, low_t))
    for pr in range(npair):
        o_ref[:, pr * LANES:(pr + 1) * LANES] = jnp.concatenate(outs[pr], axis=0).astype(BF16)


def _swa_sample(sinks, z, row0, kc, vc, nb, t):
    dbs = kc.shape[0]
    n = dbs * t
    rows = nb * t
    W = WINDOW
    qw = SWA_HEADS * SWA_HD
    cache = pl.BlockSpec((nb, W, LANES), lambda i: (i, 0, 0))
    return pl.pallas_call(
        functools.partial(_swa_sample_kernel, nb=nb, t=t),
        grid=(dbs // nb,),
        in_specs=[
            pl.BlockSpec(memory_space=pltpu.SMEM),
            pl.BlockSpec((rows, qw), lambda i: (row0 // rows + i, SWA_Q_COL // qw)),
            pl.BlockSpec((rows, 2 * LANES), lambda i: (row0 // rows + i, SWA_KV_COL // (2 * LANES))),
            cache, cache,
        ],
        out_specs=[pl.BlockSpec((rows, qw), lambda i: (i, 0)), cache, cache],
        out_shape=[
            jax.ShapeDtypeStruct((n, qw), BF16),
            jax.ShapeDtypeStruct((dbs, W, LANES), F32),
            jax.ShapeDtypeStruct((dbs, W, LANES), F32),
        ],
        compiler_params=_cparams(("parallel",)),
        name="swa_sample",
    )(sinks, z, z, kc, vc)


def _out_proj_ln_kernel(rop_ref, sop_ref, xp_ref, ros_ref, sos_ref, xs_ref, eg_ref, eb_ref, w_hbm, mg_ref, mb_ref,
                        h_ref, hb_ref, wb_scr, stage, sem, *, np_tiles):
    i = pl.program_id(0)

    def store_chunk(rows, slot):
        wb_scr[rows, :] = slot[...].astype(BF16)

    @pl.when(i == 0)
    def _():
        _stage_weight(w_hbm, stage, sem, store_chunk)

    def step(ro_ref, so_ref, x_ref):
        sub = h_ref.shape[0] // 2
        for r in range(2):
            rows = slice(r * sub, (r + 1) * sub)
            x_in = _layer_norm(x_ref[rows, :], eg_ref[...], eb_ref[...])
            mixed = _dot(ro_ref[rows, :], wb_scr[:RET_W, :]) + _dot(so_ref[rows, :], wb_scr[RET_W:, :])
            h = _layer_norm(ALPHA * x_in + mixed, mg_ref[...], mb_ref[...])
            h_ref[rows, :] = h
            hb_ref[rows, :] = h.astype(BF16)

    @pl.when(i < np_tiles)
    def _():
        step(rop_ref, sop_ref, xp_ref)

    @pl.when(i >= np_tiles)
    def _():
        step(ros_ref, sos_ref, xs_ref)


def _out_proj_ln(ro_p, so_p, xp, ro_s, so_s, xs, eg, eb, w, mg, mb, tm):
    np_tiles = xp.shape[0] // tm
    ns_tiles = xs.shape[0] // tm
    n = xp.shape[0] + xs.shape[0]
    vec = pl.BlockSpec((1, D_MODEL), lambda i: (0, 0))
    tile = pl.BlockSpec((tm, D_MODEL), lambda i: (i, 0))
    p_idx = lambda i: (jnp.minimum(i, np_tiles - 1), 0)
    s_idx = lambda i: (jnp.clip(i - np_tiles, 0, ns_tiles - 1), 0)
    return pl.pallas_call(
        functools.partial(_out_proj_ln_kernel, np_tiles=np_tiles),
        grid=(np_tiles + ns_tiles,),
        in_specs=[
            pl.BlockSpec((tm, RET_W), p_idx), pl.BlockSpec((tm, RET_W), p_idx), pl.BlockSpec((tm, D_MODEL), p_idx),
            pl.BlockSpec((tm, RET_W), s_idx), pl.BlockSpec((tm, RET_W), s_idx), pl.BlockSpec((tm, D_MODEL), s_idx),
            vec, vec,
            pl.BlockSpec(memory_space=pl.ANY),
            vec, vec,
        ],
        out_specs=[tile, tile],
        out_shape=[jax.ShapeDtypeStruct((n, D_MODEL), F32), jax.ShapeDtypeStruct((n, D_MODEL), BF16)],
        scratch_shapes=[pltpu.VMEM((D_MODEL, D_MODEL), BF16),
                        pltpu.VMEM((WEIGHT_STAGE_SLOTS, WEIGHT_STAGE_ROWS // 4, D_MODEL), F32),
                        pltpu.SemaphoreType.DMA((WEIGHT_STAGE_SLOTS,))],
        compiler_params=pltpu.CompilerParams(dimension_semantics=("arbitrary",), vmem_limit_bytes=VMEM_LIMIT_LARGE),
        name="out_proj_ln",
    )(ro_p, so_p, xp, ro_s, so_s, xs, eg, eb, w, mg, mb)


def _ffn_ln_kernel(hb_ref, h_hbm, wg_ref, wu_ref, wd_ref, g_ref, b_ref, y_ref, *rest, row0):
    *w_out_refs, h_res, sem = rest
    i = pl.program_id(0)
    j = pl.program_id(1)
    tm = h_res.shape[0]
    residual_copy = pltpu.make_async_copy(h_hbm.at[pl.ds(row0 + i * tm, tm), :], h_res, sem)

    @pl.when(j == 0)
    def _():
        residual_copy.start()
        y_ref[...] = jnp.zeros_like(y_ref)

    wg, wu, wd = (w[...].astype(BF16) for w in (wg_ref, wu_ref, wd_ref))
    for out_ref, w in zip(w_out_refs, (wg, wu, wd)):
        out_ref[...] = w
    hb = hb_ref[...]
    act = _silu(_dot(hb, wg)) * _dot(hb, wu)
    y_ref[...] += _dot(act.astype(BF16), wd)

    @pl.when(j == pl.num_programs(1) - 1)
    def _():
        residual_copy.wait()
        y_ref[...] = _layer_norm(ALPHA * h_res[...] + y_ref[...], g_ref[...], b_ref[...])


def _ffn_ln(h, hb, row0, n, wg, wu, wd, g, b, tm, th):
    emit_weights = wg.dtype == F32
    assert not emit_weights or n == tm
    vec = pl.BlockSpec((1, D_MODEL), lambda i, j: (0, 0))
    w_specs = [
        pl.BlockSpec((D_MODEL, th), lambda i, j: (0, j)),
        pl.BlockSpec((D_MODEL, th), lambda i, j: (0, j)),
        pl.BlockSpec((th, D_MODEL), lambda i, j: (j, 0)),
    ]
    out_specs = [pl.BlockSpec((tm, D_MODEL), lambda i, j: (i, 0))]
    out_shape = [jax.ShapeDtypeStruct((n, D_MODEL), F32)]
    if emit_weights:
        out_specs += w_specs
        out_shape += [jax.ShapeDtypeStruct(w.shape, BF16) for w in (wg, wu, wd)]
    outs = pl.pallas_call(
        functools.partial(_ffn_ln_kernel, row0=row0),
        grid=(n // tm, FFN_HIDDEN // th),
        in_specs=[pl.BlockSpec((tm, D_MODEL), lambda i, j: (row0 // tm + i, 0)), pl.BlockSpec(memory_space=pl.ANY)]
        + w_specs + [vec, vec],
        out_specs=out_specs,
        out_shape=out_shape,
        scratch_shapes=[pltpu.VMEM((tm, D_MODEL), F32), pltpu.SemaphoreType.DMA(())],
        compiler_params=pltpu.CompilerParams(dimension_semantics=("arbitrary", "arbitrary"),
                                             vmem_limit_bytes=VMEM_LIMIT_LARGE),
        name="ffn_ln",
    )(hb, h, wg, wu, wd, g, b)
    return tuple(outs) if emit_weights else outs[0]


def _position_tables(seq, t, past_len, sample_rows):
    r = jnp.arange(seq + sample_rows + RET_CHUNK)
    meta_r = r - seq - sample_rows
    pos = jnp.where(r < seq, N_META + r,
                    jnp.where(meta_r < 0, past_len + (r - seq) % t,
                              jnp.maximum(meta_r - (RET_CHUNK - N_META), 0))).astype(F32)[:, None]
    ret_freq = jnp.power(RET_THETA, -jnp.linspace(0.0, 1.0, RET_DK // 2, dtype=F32))[None, :]
    a = (N_META + RET_CHUNK * jnp.arange(seq // RET_CHUNK)).astype(F32)[:, None] * ret_freq
    b = jnp.arange(RET_CHUNK).astype(F32)[:, None] * ret_freq
    ca, sa, cb, sb = jnp.cos(a)[:, None, :], jnp.sin(a)[:, None, :], jnp.cos(b)[None], jnp.sin(b)[None]
    rest_ang = pos[seq:] * ret_freq
    ret_cos = jnp.concatenate([(ca * cb - sa * sb).reshape(seq, LANES), jnp.cos(rest_ang)], axis=0)
    ret_sin = jnp.concatenate([(sa * cb + ca * sb).reshape(seq, LANES), jnp.sin(rest_ang)], axis=0)
    half = ROT_DIM // 2
    swa_freq = jnp.power(ROPE_THETA, -jnp.arange(0, ROT_DIM, 2, dtype=F32) / ROT_DIM)
    swa_ang = pos * swa_freq[None, :]
    cos = jnp.tile(jnp.cos(swa_ang), (1, LANES // half))
    sin = jnp.tile(jnp.sin(swa_ang), (1, LANES // half))
    d = jnp.arange(LANES) % SWA_HD
    c = jnp.where(d < ROT_DIM, cos, 1.0)
    s1 = jnp.where((d >= half) & (d < ROT_DIM), sin, 0.0)
    s2 = jnp.where(d < half, -sin, 0.0)
    return jnp.concatenate([ret_cos, ret_sin, c, s1, s2], axis=1)


def kernel(x_prompt, x_sample, state_ret, cache_swa_k, cache_swa_v, meta_tokens, ln_emb_g, ln_emb_b,
           w_in, w_out, swa_sinks, ln_mix_g, ln_mix_b, w_ffn_gate, w_ffn_up, w_ffn_down, ln_ffn_g, ln_ffn_b):
    bsz, seq, d = x_prompt.shape
    dbs, t, _ = x_sample.shape
    assert w_in.shape[0] == DEPTH and d == D_MODEL and seq % RET_CHUNK == 0 and t & (t - 1) == 0
    past_len = 16384
    row = lambda a: a.reshape(1, -1)

    eg, eb = row(ln_emb_g), row(ln_emb_b)
    sinks = swa_sinks[0]
    lg = jnp.log(1.0 - jnp.power(2.0, -5.0 - jnp.arange(RET_HEADS, dtype=F32)))

    xp = x_prompt.reshape(bsz * seq, d)
    xs = x_sample.reshape(dbs * t, d)

    tabs = _position_tables(seq, t, past_len, LN_PROJ_TILE)
    z, zm = _ln_proj(xp, xs, meta_tokens, eg, eb, w_in[0], tabs, seq, LN_PROJ_TILE, PROJ_W)
    sample_row = bsz * seq

    ret_o_p, ret_state_p = _ret_prompt(lg, z, zm, bsz, seq)
    swa_o_p, k_p, v_p, ret_o_s, ret_state_s = _swa_prompt_ret_sample(
        sinks, z, zm, bsz, seq, lg, sample_row, state_ret[0], t)
    to_kernel = lambda a: jnp.transpose(a[0], (0, 2, 3, 1)).reshape(-1, LANES, WINDOW)
    from_kernel = lambda a: jnp.transpose(a.reshape(-1, SWA_KV_HEADS, SWA_HD, WINDOW), (0, 3, 1, 2))[None]
    kc, vc = to_kernel(cache_swa_k), to_kernel(cache_swa_v)
    swa_o_s, k_s, v_s = _swa_sample(sinks, z, sample_row, kc, vc, SWA_SAMPLE_BATCHES, t)

    mg, mb = row(ln_mix_g[0]), row(ln_mix_b[0])
    fg, fb = row(ln_ffn_g[0]), row(ln_ffn_b[0])
    ffn_w = (w_ffn_gate[0], w_ffn_up[0], w_ffn_down[0])
    h, hb = _out_proj_ln(ret_o_p, swa_o_p, xp, ret_o_s, swa_o_s, xs, eg, eb, w_out[0], mg, mb, 512)
    y_s, *ffn_wb = _ffn_ln(h, hb, sample_row, dbs * t, *ffn_w, fg, fb, TOK_TILE, 256)
    y_p = _ffn_ln(h, hb, 0, bsz * seq, *ffn_wb, fg, fb, TOK_TILE, 512)

    return (y_p.reshape(bsz, seq, d), y_s.reshape(dbs, t, d),
            ret_state_p[None], from_kernel(k_p), from_kernel(v_p),
            ret_state_s[None], from_kernel(k_s), from_kernel(v_s))
```

```python
import functools

import jax
import jax.numpy as jnp
from jax import lax
from jax.experimental import pallas as pl
from jax.experimental.pallas import tpu as pltpu

F32 = jnp.float32
BF16 = jnp.bfloat16

D_MODEL = 2048
N_META = 16
RET_HEADS = 4
RET_DK = 256
RET_DV = 256
RET_CHUNK = 128
RET_THETA = 10000.0
SWA_HD = 64
SWA_HEADS = 16
SWA_KV_HEADS = 2
SWA_GROUP = SWA_HEADS // SWA_KV_HEADS
WINDOW = 128
ROPE_THETA = 500000.0
ROT_DIM = SWA_HD // 4
FFN_HIDDEN = 5632
PROJ_W = 5376
DEPTH = 1
ALPHA = (2.0 * DEPTH) ** 0.25
LN_EPS = 1e-5
NEG_INF = -1e30

LANES = 128
RET_W = RET_HEADS * RET_DK
SWA_Q_COL = 4 * RET_W
SWA_KV_COL = SWA_Q_COL + SWA_HEADS * SWA_HD
VMEM_LIMIT = 56 * 1024 * 1024
VMEM_LIMIT_LARGE = 60 * 1024 * 1024
TOK_TILE = 1024
LN_PROJ_SUBBLOCKS = 2
RET_CHUNK_UNROLL = 16
RET_SAMPLE_EVERY = 2
LN_PROJ_TILE = 256
SWA_SAMPLE_BATCHES = 8
WEIGHT_STAGE_ROWS = 256
WEIGHT_STAGE_SLOTS = 4


def _cparams(sem):
    return pltpu.CompilerParams(dimension_semantics=sem, vmem_limit_bytes=VMEM_LIMIT)


def _layer_norm(x, g, b):
    mu = jnp.mean(x, axis=-1, keepdims=True)
    xc = x - mu
    var = jnp.mean(xc * xc, axis=-1, keepdims=True)
    return xc * lax.rsqrt(var + LN_EPS) * g + b


def _silu(x):
    return x / (1.0 + jnp.exp(-x))


def _dot(a, b):
    return jnp.dot(a, b, preferred_element_type=F32)


def _dot_nt(a, b):
    return lax.dot_general(a, b, (((1,), (1,)), ((), ())), preferred_element_type=F32)


def _dot_tn(a, b):
    return lax.dot_general(a, b, (((0,), (0,)), ((), ())), preferred_element_type=F32)


def _store_rotated(z_ref, rows, res, tab):
    cos, sin, c, s1, s2 = (tab[:, n * LANES:(n + 1) * LANES] for n in range(5))
    for c0 in range(0, 2 * RET_W, RET_DK):
        rot = _ret_rope(res[:, c0:c0 + RET_DK], cos, sin)
        z_ref[rows, c0:c0 + RET_DK] = rot if c0 < RET_W else rot * RET_DK ** -0.5
    z_ref[rows, 2 * RET_W:SWA_Q_COL] = res[:, 2 * RET_W:SWA_Q_COL]
    for c0 in range(SWA_Q_COL, SWA_KV_COL + LANES, LANES):
        rot = _swa_rope(res[:, c0:c0 + LANES], c, s1, s2)
        z_ref[rows, c0:c0 + LANES] = rot * SWA_HD ** -0.5 if c0 < SWA_KV_COL else rot
    z_ref[rows, SWA_KV_COL + LANES:] = res[:, SWA_KV_COL + LANES:]


def _stage_weight(w_hbm, stage, sem, store_chunk):
    slots, chunk = stage.shape[:2]
    n_chunks = w_hbm.shape[0] // chunk

    def weight_copy(k):
        return pltpu.make_async_copy(w_hbm.at[pl.ds(k * chunk, chunk), :], stage.at[k % slots], sem.at[k % slots])

    for k in range(min(slots - 1, n_chunks)):
        weight_copy(k).start()
    for k in range(n_chunks):
        if k + slots - 1 < n_chunks:
            weight_copy(k + slots - 1).start()
        weight_copy(k).wait()
        store_chunk(slice(k * chunk, (k + 1) * chunk), stage.at[k % slots])


def _ln_proj_kernel(xp_ref, xs_ref, xm_ref, g_ref, b_ref, w_hbm, tabp_ref, tabs_ref, tabm_ref, z_ref, zm_ref,
                    hm_scr, wb_scr, stage, sem, *, np_tiles, ns_tiles):
    i = pl.program_id(0)
    j = pl.program_id(1)
    first = j == 0
    is_sample = i >= np_tiles
    is_last = i == np_tiles + ns_tiles - 1
    n_col, _, tn = wb_scr.shape

    def store_chunk(rows, slot):
        for jt in range(n_col):
            wb_scr[jt, rows, :] = slot[:, jt * tn:(jt + 1) * tn].astype(BF16)

    @pl.when(first & (i == 0))
    def _():
        _stage_weight(w_hbm, stage, sem, store_chunk)

    def norm(x):
        return _layer_norm(x, g_ref[...], b_ref[...]).astype(BF16)

    def project(x_ref, tab_ref):
        sub = x_ref.shape[0] // LN_PROJ_SUBBLOCKS
        for r in range(LN_PROJ_SUBBLOCKS):
            rows = slice(r * sub, (r + 1) * sub)
            _store_rotated(z_ref, rows, _dot(norm(x_ref[rows, :]), wb_scr[j]), tab_ref[rows, :])

    @pl.when(jnp.logical_not(is_sample))
    def _():
        project(xp_ref, tabp_ref)

    @pl.when(is_sample)
    def _():
        project(xs_ref, tabs_ref)

    @pl.when(is_last)
    def _():
        lead = hm_scr.shape[0] - xm_ref.shape[0]
        hm_scr[:lead, :] = jnp.zeros((lead, D_MODEL), BF16)
        hm_scr[lead:, :] = norm(xm_ref[...])
        _store_rotated(zm_ref, slice(None), _dot(hm_scr[...], wb_scr[j]), tabm_ref[...])


def _ln_proj(xp, xs, xm, g, b, w, tabs, seq, tm, tn):
    np_tiles = xp.shape[0] // tm
    ns_tiles = xs.shape[0] // tm
    last = np_tiles + ns_tiles - 1
    n_col = PROJ_W // tn
    assert n_col == 1
    tab_w = tabs.shape[1]
    return pl.pallas_call(
        functools.partial(_ln_proj_kernel, np_tiles=np_tiles, ns_tiles=ns_tiles),
        grid=(np_tiles + ns_tiles, n_col),
        in_specs=[
            pl.BlockSpec((tm, D_MODEL), lambda i, j: (jnp.minimum(i, np_tiles - 1), 0)),
            pl.BlockSpec((tm, D_MODEL), lambda i, j: (jnp.clip(i - np_tiles, 0, ns_tiles - 1), 0)),
            pl.BlockSpec(xm.shape, lambda i, j: (0, 0), pipeline_mode=pl.Buffered(1)),
            pl.BlockSpec((1, D_MODEL), lambda i, j: (0, 0)),
            pl.BlockSpec((1, D_MODEL), lambda i, j: (0, 0)),
            pl.BlockSpec(memory_space=pl.ANY),
            pl.BlockSpec((tm, tab_w), lambda i, j: (lax.rem(i, seq // tm), 0)),
            pl.BlockSpec((tm, tab_w), lambda i, j: (seq // tm, 0)),
            pl.BlockSpec((RET_CHUNK, tab_w), lambda i, j: ((seq + tm) // RET_CHUNK, 0)),
        ],
        out_specs=[
            pl.BlockSpec((tm, tn), lambda i, j: (i, j)),
            pl.BlockSpec((RET_CHUNK, tn), lambda i, j: (0, jnp.where(i == last, j, 0))),
        ],
        out_shape=[
            jax.ShapeDtypeStruct((xp.shape[0] + xs.shape[0], PROJ_W), F32),
            jax.ShapeDtypeStruct((RET_CHUNK, PROJ_W), F32),
        ],
        scratch_shapes=[
            pltpu.VMEM((RET_CHUNK, D_MODEL), BF16),
            pltpu.VMEM((n_col, D_MODEL, tn), BF16),
            pltpu.VMEM((WEIGHT_STAGE_SLOTS, WEIGHT_STAGE_ROWS // 4, PROJ_W), F32),
            pltpu.SemaphoreType.DMA((WEIGHT_STAGE_SLOTS,)),
        ],
        compiler_params=pltpu.CompilerParams(dimension_semantics=("arbitrary", "arbitrary"),
                                             vmem_limit_bytes=VMEM_LIMIT_LARGE),
        name="ln_proj",
    )(xp, xs, xm, g, b, w, tabs, tabs, tabs)


def _ret_rope(x, cos, sin):
    x1 = x[:, :LANES]
    x2 = x[:, LANES:]
    return jnp.concatenate([x1 * cos - x2 * sin, x2 * cos + x1 * sin], axis=1)


def _group_norm_gate(o, gate):
    mu = jnp.mean(o, axis=-1, keepdims=True)
    oc = o - mu
    var = jnp.mean(oc * oc, axis=-1, keepdims=True)
    return oc * lax.rsqrt(var + LN_EPS) * _silu(gate)


def _ret_prompt_kernel(lg_ref, q_ref, k_ref, v_ref, g_ref, km_ref, vm_ref, o_ref, s_ref, s_scr):
    C = RET_CHUNK
    lg = lg_ref[pl.program_id(1)]
    ri = lax.broadcasted_iota(jnp.int32, (C, C), 0)
    ci = lax.broadcasted_iota(jnp.int32, (C, C), 1)
    rel = (ri - ci).astype(F32)
    decay = jnp.where(rel >= 0.0, jnp.exp(jnp.maximum(rel, 0.0) * lg), 0.0)
    row = lax.broadcasted_iota(jnp.int32, (C, 1), 0).astype(F32)
    q_decay = jnp.exp((row + 1.0) * lg)
    k_decay = jnp.exp((C - 1.0 - row) * lg)
    chunk_decay = jnp.exp(jnp.full((1, RET_DV), C * lg, F32))

    meta_decay = jnp.where(row >= C - N_META, k_decay, 0.0)
    s_scr[...] = _dot_tn((km_ref[...] * meta_decay).astype(BF16), vm_ref[...].astype(BF16))

    def chunk(c, carry):
        rows = pl.ds(pl.multiple_of(c * C, C), C)
        k = k_ref[rows, :]
        qb = q_ref[rows, :].astype(BF16)
        vb = v_ref[rows, :].astype(BF16)
        s_prev = s_scr[...]
        scores = _dot_nt(qb, k.astype(BF16)) * decay
        inner = _dot(scores.astype(BF16), vb)
        cross = _dot(qb, s_prev.astype(BF16)) * q_decay
        s_scr[...] = chunk_decay * s_prev + _dot_tn((k * k_decay).astype(BF16), vb)
        o_ref[rows, :] = _group_norm_gate(inner + cross, g_ref[rows, :]).astype(BF16)
        return carry

    lax.fori_loop(0, q_ref.shape[0] // C, chunk, 0, unroll=RET_CHUNK_UNROLL)
    s_ref[...] = s_scr[...]


def _ret_prompt(lg, z, zm, bsz, seq):
    col = lambda base: (lambda b, h: (b, base + h))
    mcol = lambda base: (lambda b, h: (0, base + h))
    return pl.pallas_call(
        _ret_prompt_kernel,
        grid=(bsz, RET_HEADS),
        in_specs=[
            pl.BlockSpec(memory_space=pltpu.SMEM),
            pl.BlockSpec((seq, RET_DK), col(0)),
            pl.BlockSpec((seq, RET_DK), col(RET_HEADS)),
            pl.BlockSpec((seq, RET_DV), col(2 * RET_HEADS)),
            pl.BlockSpec((seq, RET_DV), col(3 * RET_HEADS)),
            pl.BlockSpec((RET_CHUNK, RET_DK), mcol(RET_HEADS)),
            pl.BlockSpec((RET_CHUNK, RET_DV), mcol(2 * RET_HEADS)),
        ],
        out_specs=[
            pl.BlockSpec((seq, RET_DV), lambda b, h: (b, h)),
            pl.BlockSpec((None, None, RET_DK, RET_DV), lambda b, h: (b, h, 0, 0)),
        ],
        out_shape=[
            jax.ShapeDtypeStruct((bsz * seq, RET_W), BF16),
            jax.ShapeDtypeStruct((bsz, RET_HEADS, RET_DK, RET_DV), F32),
        ],
        scratch_shapes=[pltpu.VMEM((RET_DK, RET_DV), F32)],
        compiler_params=_cparams(("parallel", "parallel")),
        name="ret_prompt",
    )(lg, z, z, z, z, zm, zm)


def _ret_sample_jobs(lg_ref, z_ref, s_ref, o_ref, so_ref, *, nb, t):
    rows = nb * t
    R = RET_HEADS * rows

    def stack(base):
        return jnp.concatenate([z_ref[:, base + h * RET_DK: base + (h + 1) * RET_DK] for h in range(RET_HEADS)],
                               axis=0)

    q = stack(0)
    k = stack(RET_W)
    v = stack(2 * RET_W)
    gate = stack(3 * RET_W)

    ri = lax.broadcasted_iota(jnp.int32, (R, R), 0)
    ci = lax.broadcasted_iota(jnp.int32, (R, R), 1)
    rcol = lax.broadcasted_iota(jnp.int32, (R, 1), 0)
    lg_col = jnp.zeros((R, 1), F32)
    for h in range(RET_HEADS):
        lg_col = jnp.where((rcol >= h * rows) & (rcol < (h + 1) * rows), lg_ref[h], lg_col)
    tcol = (rcol & (t - 1)).astype(F32)
    rel = (ri - ci).astype(F32)
    same = ((ri & -t) == (ci & -t)) & (ri >= ci)
    decay = jnp.where(same, jnp.exp(jnp.maximum(rel, 0.0) * lg_col), 0.0)
    q_decay = jnp.exp((tcol + 1.0) * lg_col)
    k_decay = jnp.exp((t - 1.0 - tcol) * lg_col)

    qb = q.astype(BF16)
    vb = v.astype(BF16)
    scores = _dot_nt(qb, k.astype(BF16)) * decay
    inner = _dot(scores.astype(BF16), vb)
    kw = k * k_decay

    cross_parts = [None] * (RET_HEADS * nb)

    def state_job(h, db):
        def run():
            r0 = h * rows + db * t
            s_prev = s_ref[db, h]
            cross_parts[h * nb + db] = _dot(q[r0:r0 + t, :].astype(BF16), s_prev.astype(BF16))
            mine = (rcol >= r0) & (rcol < r0 + t)
            upd = _dot_tn(jnp.where(mine, kw, 0.0).astype(BF16), vb)
            step_decay = jnp.exp(jnp.full((1, RET_DV), t * lg_ref[h], F32))
            so_ref[db, h] = step_decay * s_prev + upd
        return run

    def finish():
        cross = jnp.concatenate(cross_parts, axis=0) * q_decay
        out = _group_norm_gate(inner + cross, gate).astype(BF16)
        for h in range(RET_HEADS):
            o_ref[:, h * RET_DV:(h + 1) * RET_DV] = out[h * rows:(h + 1) * rows, :]

    return [state_job(h, db) for h in range(RET_HEADS) for db in range(nb)], finish


def _swa_rope(x, c, s1, s2):
    return x * c + pltpu.roll(x, 8, 1) * s1 + pltpu.roll(x, LANES - 8, 1) * s2


def _dup_head(x, g, low):
    swapped = pltpu.roll(x, SWA_HD, 1)
    return jnp.where(low, x, swapped) if g == 0 else jnp.where(low, swapped, x)


def _to_kv_half(slab, head, low):
    g = head // SWA_GROUP
    src = slab if head % 2 == g else pltpu.roll(slab, SWA_HD, 1)
    return jnp.where(low, src, 0.0) if g == 0 else jnp.where(low, 0.0, src)


def _from_kv_half(o_even, o_odd, g, low):
    if g == 0:
        return jnp.where(low, o_even, pltpu.roll(o_odd, SWA_HD, 1))
    return jnp.where(low, pltpu.roll(o_even, SWA_HD, 1), o_odd)


def _sink_softmax(logits, sink):
    m = jnp.maximum(jnp.max(logits, axis=-1, keepdims=True), sink)
    p = jnp.exp(logits - m)
    return p, jnp.sum(p, axis=-1, keepdims=True) + jnp.exp(sink - m)


def _swa_prompt_kernel(sink_ref, q_ref, kv_ref, kvm_ref, o_ref, kp_ref, vp_ref, kprev, vprev):
    W = WINDOW
    m_id = pl.program_id(1)

    @pl.when(m_id == 0)
    def _():
        kprev[...] = kvm_ref[:, :LANES]
        vprev[...] = kvm_ref[:, LANES:]

    kcur = kv_ref[:, :LANES]
    vcur = kv_ref[:, LANES:]
    k2 = jnp.concatenate([kprev[...], kcur], axis=0)
    v2 = jnp.concatenate([vprev[...], vcur], axis=0)

    r = lax.broadcasted_iota(jnp.int32, (W, W), 0)
    cc = lax.broadcasted_iota(jnp.int32, (W, W), 1)
    from_prev = cc > r
    prev_ok = from_prev & ((m_id > 0) | (cc >= W - N_META))
    low_k = lax.broadcasted_iota(jnp.int32, (2 * W, LANES), 1) < SWA_HD
    low = lax.broadcasted_iota(jnp.int32, (W, LANES), 1) < SWA_HD

    def attend(q, kd, vd, h):
        s = _dot_nt(q.astype(BF16), kd)
        logits = jnp.where(prev_ok, s[:, :W], jnp.where(from_prev, NEG_INF, s[:, W:]))
        p, den = _sink_softmax(logits, sink_ref[h])
        p2 = jnp.concatenate([jnp.where(from_prev, p, 0.0), jnp.where(from_prev, 0.0, p)], axis=1)
        return _dot(p2.astype(BF16), vd) / den

    for g in range(SWA_KV_HEADS):
        kd = _dup_head(k2, g, low_k).astype(BF16)
        vd = _dup_head(v2, g, low_k).astype(BF16)
        for p in range(g * SWA_GROUP // 2, (g + 1) * SWA_GROUP // 2):
            slab = q_ref[:, p * LANES:(p + 1) * LANES]
            o_even = attend(jnp.where(low, slab, 0.0), kd, vd, 2 * p)
            o_odd = attend(jnp.where(low, 0.0, slab), kd, vd, 2 * p + 1)
            o_ref[:, p * LANES:(p + 1) * LANES] = jnp.where(low, o_even, o_odd).astype(BF16)

    kprev[...] = kcur
    vprev[...] = vcur

    @pl.when(m_id == pl.num_programs(1) - 1)
    def _():
        kp_ref[...] = kcur.T
        vp_ref[...] = vcur.T


def _swa_prompt_ret_sample_kernel(sink_ref, q_ref, kv_ref, kvm_ref, lg_ref, zs_ref, st_ref,
                                 o_ref, kp_ref, vp_ref, os_ref, sto_ref, kprev, vprev, *, nb, t, every):
    _swa_prompt_kernel(sink_ref, q_ref, kv_ref, kvm_ref, o_ref, kp_ref, vp_ref, kprev, vprev)

    @pl.when(lax.rem(pl.program_id(1), every) == every - 1)
    def _():
        state_jobs, finish = _ret_sample_jobs(lg_ref, zs_ref, st_ref, os_ref, sto_ref, nb=nb, t=t)
        for job in state_jobs + [finish]:
            job()


def _swa_prompt_ret_sample(sinks, z, zm, bsz, seq, lg, sample_row, state, t):
    W = WINDOW
    nblk = seq // W
    every = RET_SAMPLE_EVERY
    ret_steps = bsz * nblk // every
    dbs = state.shape[0]
    nb = dbs // ret_steps
    assert nblk % every == 0 and nb * ret_steps == dbs and (nb * t) % 16 == 0
    rows = nb * t
    qw = SWA_HEADS * SWA_HD
    kv_col = SWA_KV_COL // (2 * LANES)
    ret_step =lambda b, m: (b * nblk + m) // every
    state_spec = pl.BlockSpec((nb, RET_HEADS, RET_DK, RET_DV), lambda b, m: (ret_step(b, m), 0, 0, 0))
    return pl.pallas_call(
        functools.partial(_swa_prompt_ret_sample_kernel, nb=nb, t=t, every=every),
        grid=(bsz, nblk),
        in_specs=[
            pl.BlockSpec(memory_space=pltpu.SMEM),
            pl.BlockSpec((W, qw), lambda b, m: (b * nblk + m, SWA_Q_COL // qw)),
            pl.BlockSpec((W, 2 * LANES), lambda b, m: (b * nblk + m, kv_col)),
            pl.BlockSpec((W, 2 * LANES), lambda b, m: (0, kv_col)),
            pl.BlockSpec(memory_space=pltpu.SMEM),
            pl.BlockSpec((rows, 4 * RET_W), lambda b, m: (sample_row // rows + ret_step(b, m), 0)),
            state_spec,
        ],
        out_specs=[
            pl.BlockSpec((W, qw), lambda b, m: (b * nblk + m, 0)),
            pl.BlockSpec((None, W, LANES), lambda b, m: (b, 0, 0)),
            pl.BlockSpec((None, W, LANES), lambda b, m: (b, 0, 0)),
            pl.BlockSpec((rows, RET_W), lambda b, m: (ret_step(b, m), 0)),
            state_spec,
        ],
        out_shape=[
            jax.ShapeDtypeStruct((bsz * seq, qw), BF16),
            jax.ShapeDtypeStruct((bsz, W, LANES), F32),
            jax.ShapeDtypeStruct((bsz, W, LANES), F32),
            jax.ShapeDtypeStruct((dbs * t, RET_W), BF16),
            jax.ShapeDtypeStruct(state.shape, F32),
        ],
        scratch_shapes=[pltpu.VMEM((W, LANES), F32), pltpu.VMEM((W, LANES), F32)],
        compiler_params=_cparams(("parallel", "arbitrary")),
        name="swa_prompt_ret_sample",
    )(sinks, z, z, zm, lg, z, state)


def _swa_sample_kernel(sink_ref, q_ref, kv_ref, kc_ref, vc_ref, o_ref, ko_ref, vo_ref, *, nb, t):
    W = WINDOW
    row_pad = jnp.zeros((LANES - nb * t, LANES), F32)
    knew_t = jnp.concatenate([kv_ref[:, :LANES], row_pad], axis=0).T
    vnew_t = jnp.concatenate([kv_ref[:, LANES:], row_pad], axis=0).T
    knew_b = knew_t.astype(BF16)
    vnew_b = vnew_t.astype(BF16)

    rq = SWA_HEADS * t
    tq = lax.broadcasted_iota(jnp.int32, (rq, 2 * W), 0) & (t - 1)
    cc = lax.broadcasted_iota(jnp.int32, (rq, 2 * W), 1)
    out_lane = lax.broadcasted_iota(jnp.int32, (LANES, LANES), 1)
    low = lax.broadcasted_iota(jnp.int32, (nb * t, LANES), 1) < SWA_HD
    low_t = lax.broadcasted_iota(jnp.int32, (t, LANES), 1) < SWA_HD
    sink_col = jnp.concatenate([jnp.full((t, 1), sink_ref[h], F32) for h in range(SWA_HEADS)], axis=0)
    npair = SWA_HEADS // 2

    slabs = [q_ref[:, p * LANES:(p + 1) * LANES] for p in range(npair)]
    q_heads = [_to_kv_half(slabs[h // 2], h, low) for h in range(SWA_HEADS)]

    logits, values = [], []
    for db in range(nb):
        rows = slice(db * t, (db + 1) * t)
        kc = kc_ref[db]
        vc = vc_ref[db]
        keep = out_lane < W - t
        ko_ref[db] = jnp.where(keep, pltpu.roll(kc, W - t, 1), pltpu.roll(knew_t, (W - t - db * t) % LANES, 1))
        vo_ref[db] = jnp.where(keep, pltpu.roll(vc, W - t, 1), pltpu.roll(vnew_t, (W - t - db * t) % LANES, 1))
        k2 = jnp.concatenate([kc.astype(BF16), knew_b], axis=1)
        values.append(jnp.concatenate([vc.astype(BF16), vnew_b], axis=1))
        t_new = cc - W - db * t
        mask = ((cc < W) & (cc > tq)) | ((t_new >= 0) & (t_new <= tq))
        q_db = jnp.concatenate([qh[rows, :] for qh in q_heads], axis=0).astype(BF16)
        logits.append(jnp.where(mask, _dot(q_db, k2), NEG_INF))
    p, den = _sink_softmax(jnp.concatenate(logits, axis=0), jnp.concatenate([sink_col] * nb, axis=0))
    p = p.astype(BF16)
    outs = [[] for _ in range(npair)]
    for db in range(nb):
        o = _dot_nt(p[db * rq:(db + 1) * rq, :], values[db]) / den[db * rq:(db + 1) * rq, :]
        for pr in range(npair):
            o_even = o[(2 * pr) * t:(2 * pr + 1) * t, :]
            o_odd = o[(2 * pr + 1) * t:(2 * pr + 2) * t, :]
            outs[pr].append(_from_kv_half(o_even, o_odd, 2 * pr // SWA_GROUP, low_t))
    for pr in range(npair):
        o_ref[:, pr * LANES:(pr + 1) * LANES] = jnp.concatenate(outs[pr], axis=0).astype(BF16)


def _swa_sample(sinks, z, row0, kc, vc, nb, t):
    dbs = kc.shape[0]
    n = dbs * t
    rows = nb * t
    W = WINDOW
    qw = SWA_HEADS * SWA_HD
    cache = pl.BlockSpec((nb, W, LANES), lambda i: (i, 0, 0))
    return pl.pallas_call(
        functools.partial(_swa_sample_kernel, nb=nb, t=t),
        grid=(dbs // nb,),
        in_specs=[
            pl.BlockSpec(memory_space=pltpu.SMEM),
            pl.BlockSpec((rows, qw), lambda i: (row0 // rows + i, SWA_Q_COL // qw)),
            pl.BlockSpec((rows, 2 * LANES), lambda i: (row0 // rows + i, SWA_KV_COL // (2 * LANES))),
            cache, cache,
        ],
        out_specs=[pl.BlockSpec((rows, qw), lambda i: (i, 0)), cache, cache],
        out_shape=[
            jax.ShapeDtypeStruct((n, qw), BF16),
            jax.ShapeDtypeStruct((dbs, W, LANES), F32),
            jax.ShapeDtypeStruct((dbs, W, LANES), F32),
        ],
        compiler_params=_cparams(("parallel",)),
        name="swa_sample",
    )(sinks, z, z, kc, vc)


def _out_proj_ln_kernel(rop_ref, sop_ref, xp_ref, ros_ref, sos_ref, xs_ref, eg_ref, eb_ref, w_hbm, mg_ref, mb_ref,
                        h_ref, hb_ref, wb_scr, stage, sem, *, np_tiles):
    i = pl.program_id(0)

    def store_chunk(rows, slot):
        wb_scr[rows, :] = slot[...].astype(BF16)

    @pl.when(i == 0)
    def _():
        _stage_weight(w_hbm, stage, sem, store_chunk)

    def step(ro_ref, so_ref, x_ref):
        sub = h_ref.shape[0] // 2
        for r in range(2):
            rows = slice(r * sub, (r + 1) * sub)
            x_in = _layer_norm(x_ref[rows, :], eg_ref[...], eb_ref[...])
            mixed = _dot(ro_ref[rows, :], wb_scr[:RET_W, :]) + _dot(so_ref[rows, :], wb_scr[RET_W:, :])
            h = _layer_norm(ALPHA * x_in + mixed, mg_ref[...], mb_ref[...])
            h_ref[rows, :] = h
            hb_ref[rows, :] = h.astype(BF16)

    @pl.when(i < np_tiles)
    def _():
        step(rop_ref, sop_ref, xp_ref)

    @pl.when(i >= np_tiles)
    def _():
        step(ros_ref, sos_ref, xs_ref)


def _out_proj_ln(ro_p, so_p, xp, ro_s, so_s, xs, eg, eb, w, mg, mb, tm):
    np_tiles = xp.shape[0] // tm
    ns_tiles = xs.shape[0] // tm
    n = xp.shape[0] + xs.shape[0]
    vec = pl.BlockSpec((1, D_MODEL), lambda i: (0, 0))
    tile = pl.BlockSpec((tm, D_MODEL), lambda i: (i, 0))
    p_idx = lambda i: (jnp.minimum(i, np_tiles - 1), 0)
    s_idx = lambda i: (jnp.clip(i - np_tiles, 0, ns_tiles - 1), 0)
    return pl.pallas_call(
        functools.partial(_out_proj_ln_kernel, np_tiles=np_tiles),
        grid=(np_tiles + ns_tiles,),
        in_specs=[
            pl.BlockSpec((tm, RET_W), p_idx), pl.BlockSpec((tm, RET_W), p_idx), pl.BlockSpec((tm, D_MODEL), p_idx),
            pl.BlockSpec((tm, RET_W), s_idx), pl.BlockSpec((tm, RET_W), s_idx), pl.BlockSpec((tm, D_MODEL), s_idx),
            vec, vec,
            pl.BlockSpec(memory_space=pl.ANY),
            vec, vec,
        ],
        out_specs=[tile, tile],
        out_shape=[jax.ShapeDtypeStruct((n, D_MODEL), F32), jax.ShapeDtypeStruct((n, D_MODEL), BF16)],
        scratch_shapes=[pltpu.VMEM((D_MODEL, D_MODEL), BF16),
                        pltpu.VMEM((WEIGHT_STAGE_SLOTS, WEIGHT_STAGE_ROWS // 2, D_MODEL), F32),
                        pltpu.SemaphoreType.DMA((WEIGHT_STAGE_SLOTS,))],
        compiler_params=pltpu.CompilerParams(dimension_semantics=("arbitrary",), vmem_limit_bytes=VMEM_LIMIT_LARGE),
        name="out_proj_ln",
    )(ro_p, so_p, xp, ro_s, so_s, xs, eg, eb, w, mg, mb)


def _ffn_ln_kernel(hb_ref, h_hbm, wg_ref, wu_ref, wd_ref, g_ref, b_ref, y_ref, *rest, row0):
    *w_out_refs, h_res, sem = rest
    i = pl.program_id(0)
    j = pl.program_id(1)
    tm = h_res.shape[0]
    residual_copy = pltpu.make_async_copy(h_hbm.at[pl.ds(row0 + i * tm, tm), :], h_res, sem)

    @pl.when(j == 0)
    def _():
        residual_copy.start()
        y_ref[...] = jnp.zeros_like(y_ref)

    wg, wu, wd = (w[...].astype(BF16) for w in (wg_ref, wu_ref, wd_ref))
    for out_ref, w in zip(w_out_refs, (wg, wu, wd)):
        out_ref[...] = w
    hb = hb_ref[...]
    act = _silu(_dot(hb, wg)) * _dot(hb, wu)
    y_ref[...] += _dot(act.astype(BF16), wd)

    @pl.when(j == pl.num_programs(1) - 1)
    def _():
        residual_copy.wait()
        y_ref[...] = _layer_norm(ALPHA * h_res[...] + y_ref[...], g_ref[...], b_ref[...])


def _ffn_ln(h, hb, row0, n, wg, wu, wd, g, b, tm, th):
    emit_weights = wg.dtype == F32
    assert not emit_weights or n == tm
    vec = pl.BlockSpec((1, D_MODEL), lambda i, j: (0, 0))
    w_specs = [
        pl.BlockSpec((D_MODEL, th), lambda i, j: (0, j)),
        pl.BlockSpec((D_MODEL, th), lambda i, j: (0, j)),
        pl.BlockSpec((th, D_MODEL), lambda i, j: (j, 0)),
    ]
    out_specs = [pl.BlockSpec((tm, D_MODEL), lambda i, j: (i, 0))]
    out_shape = [jax.ShapeDtypeStruct((n, D_MODEL), F32)]
    if emit_weights:
        out_specs += w_specs
        out_shape += [jax.ShapeDtypeStruct(w.shape, BF16) for w in (wg, wu, wd)]
    outs = pl.pallas_call(
        functools.partial(_ffn_ln_kernel, row0=row0),
        grid=(n // tm, FFN_HIDDEN // th),
        in_specs=[pl.BlockSpec((tm, D_MODEL), lambda i, j: (row0 // tm + i, 0)), pl.BlockSpec(memory_space=pl.ANY)]
        + w_specs + [vec, vec],
        out_specs=out_specs,
        out_shape=out_shape,
        scratch_shapes=[pltpu.VMEM((tm, D_MODEL), F32), pltpu.SemaphoreType.DMA(())],
        compiler_params=pltpu.CompilerParams(dimension_semantics=("arbitrary", "arbitrary"),
                                             vmem_limit_bytes=VMEM_LIMIT_LARGE),
        name="ffn_ln",
    )(hb, h, wg, wu, wd, g, b)
    return tuple(outs) if emit_weights else outs[0]


def _position_tables(seq, t, past_len, sample_rows):
    r = jnp.arange(seq + sample_rows + RET_CHUNK)
    meta_r = r - seq - sample_rows
    pos = jnp.where(r < seq, N_META + r,
                    jnp.where(meta_r < 0, past_len + (r - seq) % t,
                              jnp.maximum(meta_r - (RET_CHUNK - N_META), 0))).astype(F32)[:, None]
    ret_freq = jnp.power(RET_THETA, -jnp.linspace(0.0, 1.0, RET_DK // 2, dtype=F32))[None, :]
    a = (N_META + RET_CHUNK * jnp.arange(seq // RET_CHUNK)).astype(F32)[:, None] * ret_freq
    b = jnp.arange(RET_CHUNK).astype(F32)[:, None] * ret_freq
    ca, sa, cb, sb = jnp.cos(a)[:, None, :], jnp.sin(a)[:, None, :], jnp.cos(b)[None], jnp.sin(b)[None]
    rest_ang = pos[seq:] * ret_freq
    ret_cos = jnp.concatenate([(ca * cb - sa * sb).reshape(seq, LANES), jnp.cos(rest_ang)], axis=0)
    ret_sin = jnp.concatenate([(sa * cb + ca * sb).reshape(seq, LANES), jnp.sin(rest_ang)], axis=0)
    half = ROT_DIM // 2
    swa_freq = jnp.power(ROPE_THETA, -jnp.arange(0, ROT_DIM, 2, dtype=F32) / ROT_DIM)
    swa_ang = pos * swa_freq[None, :]
    cos = jnp.tile(jnp.cos(swa_ang), (1, LANES // half))
    sin = jnp.tile(jnp.sin(swa_ang), (1, LANES // half))
    d = jnp.arange(LANES) % SWA_HD
    c = jnp.where(d < ROT_DIM, cos, 1.0)
    s1 = jnp.where((d >= half) & (d < ROT_DIM), sin, 0.0)
    s2 = jnp.where(d < half, -sin, 0.0)
    return jnp.concatenate([ret_cos, ret_sin, c, s1, s2], axis=1)


def kernel(x_prompt, x_sample, state_ret, cache_swa_k, cache_swa_v, meta_tokens, ln_emb_g, ln_emb_b,
           w_in, w_out, swa_sinks, ln_mix_g, ln_mix_b, w_ffn_gate, w_ffn_up, w_ffn_down, ln_ffn_g, ln_ffn_b):
    bsz, seq, d = x_prompt.shape
    dbs, t, _ = x_sample.shape
    assert w_in.shape[0] == DEPTH and d == D_MODEL and seq % RET_CHUNK == 0 and t & (t - 1) == 0
    past_len = 16384
    row = lambda a: a.reshape(1, -1)

    eg, eb = row(ln_emb_g), row(ln_emb_b)
    sinks = swa_sinks[0]
    lg = jnp.log(1.0 - jnp.power(2.0, -5.0 - jnp.arange(RET_HEADS, dtype=F32)))

    xp = x_prompt.reshape(bsz * seq, d)
    xs = x_sample.reshape(dbs * t, d)

    tabs = _position_tables(seq, t, past_len, LN_PROJ_TILE)
    z, zm = _ln_proj(xp, xs, meta_tokens, eg, eb, w_in[0], tabs, seq, LN_PROJ_TILE, PROJ_W)
    sample_row = bsz * seq

    ret_o_p, ret_state_p = _ret_prompt(lg, z, zm, bsz, seq)
    swa_o_p, k_p, v_p, ret_o_s, ret_state_s = _swa_prompt_ret_sample(
        sinks, z, zm, bsz, seq, lg, sample_row, state_ret[0], t)
    to_kernel = lambda a: jnp.transpose(a[0], (0, 2, 3, 1)).reshape(-1, LANES, WINDOW)
    from_kernel = lambda a: jnp.transpose(a.reshape(-1, SWA_KV_HEADS, SWA_HD, WINDOW), (0, 3, 1, 2))[None]
    kc, vc = to_kernel(cache_swa_k), to_kernel(cache_swa_v)
    swa_o_s, k_s, v_s = _swa_sample(sinks, z, sample_row, kc, vc, SWA_SAMPLE_BATCHES, t)

    mg, mb = row(ln_mix_g[0]), row(ln_mix_b[0])
    fg, fb = row(ln_ffn_g[0]), row(ln_ffn_b[0])
    ffn_w = (w_ffn_gate[0], w_ffn_up[0], w_ffn_down[0])
    h, hb = _out_proj_ln(ret_o_p, swa_o_p, xp, ret_o_s, swa_o_s, xs, eg, eb, w_out[0], mg, mb, 512)
    y_s, *ffn_wb = _ffn_ln(h, hb, sample_row, dbs * t, *ffn_w, fg, fb, TOK_TILE, 256)
    y_p = _ffn_ln(h, hb, 0, bsz * seq, *ffn_wb, fg, fb, TOK_TILE, 512)

    return (y_p.reshape(bsz, seq, d), y_s.reshape(dbs, t, d),
            ret_state_p[None], from_kernel(k_p), from_kernel(v_p),
            ret_state_s[None], from_kernel(k_s), from_kernel(v_s))
```

```python
import functools

import jax
import jax.numpy as jnp
from jax import lax
from jax.experimental import pallas as pl
from jax.experimental.pallas import tpu as pltpu

F32 = jnp.float32
BF16 = jnp.bfloat16

D_MODEL = 2048
N_META = 16
RET_HEADS = 4
RET_DK = 256
RET_DV = 256
RET_CHUNK = 128
RET_THETA = 10000.0
SWA_HD = 64
SWA_HEADS = 16
SWA_KV_HEADS = 2
SWA_GROUP = SWA_HEADS // SWA_KV_HEADS
WINDOW = 128
ROPE_THETA = 500000.0
ROT_DIM = SWA_HD // 4
FFN_HIDDEN = 5632
PROJ_W = 5376
DEPTH = 1
ALPHA = (2.0 * DEPTH) ** 0.25
LN_EPS = 1e-5
NEG_INF = -1e30

LANES = 128
RET_W = RET_HEADS * RET_DK
SWA_Q_COL = 4 * RET_W
SWA_KV_COL = SWA_Q_COL + SWA_HEADS * SWA_HD
VMEM_LIMIT = 56 * 1024 * 1024
VMEM_LIMIT_LARGE = 60 * 1024 * 1024
TOK_TILE = 1024
LN_PROJ_SUBBLOCKS = 2
RET_CHUNK_UNROLL = 16
RET_SAMPLE_EVERY = 2
LN_PROJ_TILE = 256
SWA_SAMPLE_BATCHES = 8
WEIGHT_STAGE_ROWS = 256
WEIGHT_STAGE_SLOTS = 4


def _cparams(sem):
    return pltpu.CompilerParams(dimension_semantics=sem, vmem_limit_bytes=VMEM_LIMIT)


def _layer_norm(x, g, b):
    mu = jnp.mean(x, axis=-1, keepdims=True)
    xc = x - mu
    var = jnp.mean(xc * xc, axis=-1, keepdims=True)
    return xc * lax.rsqrt(var + LN_EPS) * g + b


def _silu(x):
    return x / (1.0 + jnp.exp(-x))


def _dot(a, b):
    return jnp.dot(a, b, preferred_element_type=F32)


def _dot_nt(a, b):
    return lax.dot_general(a, b, (((1,), (1,)), ((), ())), preferred_element_type=F32)


def _dot_tn(a, b):
    return lax.dot_general(a, b, (((0,), (0,)), ((), ())), preferred_element_type=F32)


def _store_rotated(z_ref, rows, res, tab):
    cos, sin, c, s1, s2 = (tab[:, n * LANES:(n + 1) * LANES] for n in range(5))
    for c0 in range(0, 2 * RET_W, RET_DK):
        rot = _ret_rope(res[:, c0:c0 + RET_DK], cos, sin)
        z_ref[rows, c0:c0 + RET_DK] = rot if c0 < RET_W else rot * RET_DK ** -0.5
    z_ref[rows, 2 * RET_W:SWA_Q_COL] = res[:, 2 * RET_W:SWA_Q_COL]
    for c0 in range(SWA_Q_COL, SWA_KV_COL + LANES, LANES):
        rot = _swa_rope(res[:, c0:c0 + LANES], c, s1, s2)
        z_ref[rows, c0:c0 + LANES] = rot * SWA_HD ** -0.5 if c0 < SWA_KV_COL else rot
    z_ref[rows, SWA_KV_COL + LANES:] = res[:, SWA_KV_COL + LANES:]


def _stage_weight(w_hbm, stage, sem, store_chunk):
    slots, chunk = stage.shape[:2]
    n_chunks = w_hbm.shape[0] // chunk

    def weight_copy(k):
        return pltpu.make_async_copy(w_hbm.at[pl.ds(k * chunk, chunk), :], stage.at[k % slots], sem.at[k % slots])

    for k in range(min(slots - 1, n_chunks)):
        weight_copy(k).start()
    for k in range(n_chunks):
        if k + slots - 1 < n_chunks:
            weight_copy(k + slots - 1).start()
        weight_copy(k).wait()
        store_chunk(slice(k * chunk, (k + 1) * chunk), stage.at[k % slots])


def _ln_proj_kernel(xp_ref, xs_ref, xm_ref, g_ref, b_ref, w_hbm, tabp_ref, tabs_ref, tabm_ref, z_ref, zm_ref,
                    hm_scr, wb_scr, stage, sem, *, np_tiles, ns_tiles):
    i = pl.program_id(0)
    j = pl.program_id(1)
    first = j == 0
    is_sample = i >= np_tiles
    is_last = i == np_tiles + ns_tiles - 1
    n_col, _, tn = wb_scr.shape

    def store_chunk(rows, slot):
        for jt in range(n_col):
            wb_scr[jt, rows, :] = slot[:, jt * tn:(jt + 1) * tn].astype(BF16)

    @pl.when(first & (i == 0))
    def _():
        _stage_weight(w_hbm, stage, sem, store_chunk)

    def norm(x):
        return _layer_norm(x, g_ref[...], b_ref[...]).astype(BF16)

    def project(x_ref, tab_ref):
        sub = x_ref.shape[0] // LN_PROJ_SUBBLOCKS
        for r in range(LN_PROJ_SUBBLOCKS):
            rows = slice(r * sub, (r + 1) * sub)
            _store_rotated(z_ref, rows, _dot(norm(x_ref[rows, :]), wb_scr[j]), tab_ref[rows, :])

    @pl.when(jnp.logical_not(is_sample))
    def _():
        project(xp_ref, tabp_ref)

    @pl.when(is_sample)
    def _():
        project(xs_ref, tabs_ref)

    @pl.when(is_last)
    def _():
        lead = hm_scr.shape[0] - xm_ref.shape[0]
        hm_scr[:lead, :] = jnp.zeros((lead, D_MODEL), BF16)
        hm_scr[lead:, :] = norm(xm_ref[...])
        _store_rotated(zm_ref, slice(None), _dot(hm_scr[...], wb_scr[j]), tabm_ref[...])


def _ln_proj(xp, xs, xm, g, b, w, tabs, seq, tm, tn):
    np_tiles = xp.shape[0] // tm
    ns_tiles = xs.shape[0] // tm
    last = np_tiles + ns_tiles - 1
    n_col = PROJ_W // tn
    assert n_col == 1
    tab_w = tabs.shape[1]
    return pl.pallas_call(
        functools.partial(_ln_proj_kernel, np_tiles=np_tiles, ns_tiles=ns_tiles),
        grid=(np_tiles + ns_tiles, n_col),
        in_specs=[
            pl.BlockSpec((tm, D_MODEL), lambda i, j: (jnp.minimum(i, np_tiles - 1), 0)),
            pl.BlockSpec((tm, D_MODEL), lambda i, j: (jnp.clip(i - np_tiles, 0, ns_tiles - 1), 0)),
            pl.BlockSpec(xm.shape, lambda i, j: (0, 0), pipeline_mode=pl.Buffered(1)),
            pl.BlockSpec((1, D_MODEL), lambda i, j: (0, 0)),
            pl.BlockSpec((1, D_MODEL), lambda i, j: (0, 0)),
            pl.BlockSpec(memory_space=pl.ANY),
            pl.BlockSpec((tm, tab_w), lambda i, j: (lax.rem(i, seq // tm), 0)),
            pl.BlockSpec((tm, tab_w), lambda i, j: (seq // tm, 0)),
            pl.BlockSpec((RET_CHUNK, tab_w), lambda i, j: ((seq + tm) // RET_CHUNK, 0)),
        ],
        out_specs=[
            pl.BlockSpec((tm, tn), lambda i, j: (i, j)),
            pl.BlockSpec((RET_CHUNK, tn), lambda i, j: (0, jnp.where(i == last, j, 0))),
        ],
        out_shape=[
            jax.ShapeDtypeStruct((xp.shape[0] + xs.shape[0], PROJ_W), F32),
            jax.ShapeDtypeStruct((RET_CHUNK, PROJ_W), F32),
        ],
        scratch_shapes=[
            pltpu.VMEM((RET_CHUNK, D_MODEL), BF16),
            pltpu.VMEM((n_col, D_MODEL, tn), BF16),
            pltpu.VMEM((WEIGHT_STAGE_SLOTS, WEIGHT_STAGE_ROWS // 4, PROJ_W), F32),
            pltpu.SemaphoreType.DMA((WEIGHT_STAGE_SLOTS,)),
        ],
        compiler_params=pltpu.CompilerParams(dimension_semantics=("arbitrary", "arbitrary"),
                                             vmem_limit_bytes=VMEM_LIMIT_LARGE),
        name="ln_proj",
    )(xp, xs, xm, g, b, w, tabs, tabs, tabs)


def _ret_rope(x, cos, sin):
    x1 = x[:, :LANES]
    x2 = x[:, LANES:]
    return jnp.concatenate([x1 * cos - x2 * sin, x2 * cos + x1 * sin], axis=1)


def _group_norm_gate(o, gate):
    mu = jnp.mean(o, axis=-1, keepdims=True)
    oc = o - mu
    var = jnp.mean(oc * oc, axis=-1, keepdims=True)
    return oc * lax.rsqrt(var + LN_EPS) * _silu(gate)


def _ret_prompt_kernel(lg_ref, q_ref, k_ref, v_ref, g_ref, km_ref, vm_ref, o_ref, s_ref, s_scr):
    C = RET_CHUNK
    lg = lg_ref[pl.program_id(1)]
    ri = lax.broadcasted_iota(jnp.int32, (C, C), 0)
    ci = lax.broadcasted_iota(jnp.int32, (C, C), 1)
    rel = (ri - ci).astype(F32)
    decay = jnp.where(rel >= 0.0, jnp.exp(jnp.maximum(rel, 0.0) * lg), 0.0)
    row = lax.broadcasted_iota(jnp.int32, (C, 1), 0).astype(F32)
    q_decay = jnp.exp((row + 1.0) * lg)
    k_decay = jnp.exp((C - 1.0 - row) * lg)
    chunk_decay = jnp.exp(jnp.full((1, RET_DV), C * lg, F32))

    meta_decay = jnp.where(row >= C - N_META, k_decay, 0.0)
    s_scr[...] = _dot_tn((km_ref[...] * meta_decay).astype(BF16), vm_ref[...].astype(BF16))

    def chunk(c, carry):
        rows = pl.ds(pl.multiple_of(c * C, C), C)
        k = k_ref[rows, :]
        qb = q_ref[rows, :].astype(BF16)
        vb = v_ref[rows, :].astype(BF16)
        s_prev = s_scr[...]
        scores = _dot_nt(qb, k.astype(BF16)) * decay
        inner = _dot(scores.astype(BF16), vb)
        cross = _dot(qb, s_prev.astype(BF16)) * q_decay
        s_scr[...] = chunk_decay * s_prev + _dot_tn((k * k_decay).astype(BF16), vb)
        o_ref[rows, :] = _group_norm_gate(inner + cross, g_ref[rows, :]).astype(BF16)
        return carry

    lax.fori_loop(0, q_ref.shape[0] // C, chunk, 0, unroll=RET_CHUNK_UNROLL)
    s_ref[...] = s_scr[...]


def _ret_prompt(lg, z, zm, bsz, seq):
    col = lambda base: (lambda b, h: (b, base + h))
    mcol = lambda base: (lambda b, h: (0, base + h))
    return pl.pallas_call(
        _ret_prompt_kernel,
        grid=(bsz, RET_HEADS),
        in_specs=[
            pl.BlockSpec(memory_space=pltpu.SMEM),
            pl.BlockSpec((seq, RET_DK), col(0)),
            pl.BlockSpec((seq, RET_DK), col(RET_HEADS)),
            pl.BlockSpec((seq, RET_DV), col(2 * RET_HEADS)),
            pl.BlockSpec((seq, RET_DV), col(3 * RET_HEADS)),
            pl.BlockSpec((RET_CHUNK, RET_DK), mcol(RET_HEADS)),
            pl.BlockSpec((RET_CHUNK, RET_DV), mcol(2 * RET_HEADS)),
        ],
        out_specs=[
            pl.BlockSpec((seq, RET_DV), lambda b, h: (b, h)),
            pl.BlockSpec((None, None, RET_DK, RET_DV), lambda b, h: (b, h, 0, 0)),
        ],
        out_shape=[
            jax.ShapeDtypeStruct((bsz * seq, RET_W), BF16),
            jax.ShapeDtypeStruct((bsz, RET_HEADS, RET_DK, RET_DV), F32),
        ],
        scratch_shapes=[pltpu.VMEM((RET_DK, RET_DV), F32)],
        compiler_params=_cparams(("parallel", "parallel")),
        name="ret_prompt",
    )(lg, z, z, z, z, zm, zm)


def _ret_sample_jobs(lg_ref, z_ref, s_ref, o_ref, so_ref, *, nb, t):
    rows = nb * t
    R = RET_HEADS * rows

    def stack(base):
        return jnp.concatenate([z_ref[:, base + h * RET_DK: base + (h + 1) * RET_DK] for h in range(RET_HEADS)],
                               axis=0)

    q = stack(0)
    k = stack(RET_W)
    v = stack(2 * RET_W)
    gate = stack(3 * RET_W)

    ri = lax.broadcasted_iota(jnp.int32, (R, R), 0)
    ci = lax.broadcasted_iota(jnp.int32, (R, R), 1)
    rcol = lax.broadcasted_iota(jnp.int32, (R, 1), 0)
    lg_col = jnp.zeros((R, 1), F32)
    for h in range(RET_HEADS):
        lg_col = jnp.where((rcol >= h * rows) & (rcol < (h + 1) * rows), lg_ref[h], lg_col)
    tcol = (rcol & (t - 1)).astype(F32)
    rel = (ri - ci).astype(F32)
    same = ((ri & -t) == (ci & -t)) & (ri >= ci)
    decay = jnp.where(same, jnp.exp(jnp.maximum(rel, 0.0) * lg_col), 0.0)
    q_decay = jnp.exp((tcol + 1.0) * lg_col)
    k_decay = jnp.exp((t - 1.0 - tcol) * lg_col)

    qb = q.astype(BF16)
    vb = v.astype(BF16)
    scores = _dot_nt(qb, k.astype(BF16)) * decay
    inner = _dot(scores.astype(BF16), vb)
    kw = k * k_decay

    cross_parts = [None] * (RET_HEADS * nb)

    def state_job(h, db):
        def run():
            r0 = h * rows + db * t
            s_prev = s_ref[db, h]
            cross_parts[h * nb + db] = _dot(q[r0:r0 + t, :].astype(BF16), s_prev.astype(BF16))
            mine = (rcol >= r0) & (rcol < r0 + t)
            upd = _dot_tn(jnp.where(mine, kw, 0.0).astype(BF16), vb)
            step_decay = jnp.exp(jnp.full((1, RET_DV), t * lg_ref[h], F32))
            so_ref[db, h] = step_decay * s_prev + upd
        return run

    def finish():
        cross = jnp.concatenate(cross_parts, axis=0) * q_decay
        out = _group_norm_gate(inner + cross, gate).astype(BF16)
        for h in range(RET_HEADS):
            o_ref[:, h * RET_DV:(h + 1) * RET_DV] = out[h * rows:(h + 1) * rows, :]

    return [state_job(h, db) for h in range(RET_HEADS) for db in range(nb)], finish


def _swa_rope(x, c, s1, s2):
    return x * c + pltpu.roll(x, 8, 1) * s1 + pltpu.roll(x, LANES - 8, 1) * s2


def _dup_head(x, g, low):
    swapped = pltpu.roll(x, SWA_HD, 1)
    return jnp.where(low, x, swapped) if g == 0 else jnp.where(low, swapped, x)


def _to_kv_half(slab, head, low):
    g = head // SWA_GROUP
    src = slab if head % 2 == g else pltpu.roll(slab, SWA_HD, 1)
    return jnp.where(low, src, 0.0) if g == 0 else jnp.where(low, 0.0, src)


def _from_kv_half(o_even, o_odd, g, low):
    if g == 0:
        return jnp.where(low, o_even, pltpu.roll(o_odd, SWA_HD, 1))
    return jnp.where(low, pltpu.roll(o_even, SWA_HD, 1), o_odd)


def _sink_softmax(logits, sink):
    m = jnp.maximum(jnp.max(logits, axis=-1, keepdims=True), sink)
    p = jnp.exp(logits - m)
    return p, jnp.sum(p, axis=-1, keepdims=True) + jnp.exp(sink - m)


def _swa_prompt_kernel(sink_ref, q_ref, kv_ref, kvm_ref, o_ref, kp_ref, vp_ref, kprev, vprev):
    W = WINDOW
    m_id = pl.program_id(1)

    @pl.when(m_id == 0)
    def _():
        kprev[...] = kvm_ref[:, :LANES]
        vprev[...] = kvm_ref[:, LANES:]

    kcur = kv_ref[:, :LANES]
    vcur = kv_ref[:, LANES:]
    k2 = jnp.concatenate([kprev[...], kcur], axis=0)
    v2 = jnp.concatenate([vprev[...], vcur], axis=0)

    r = lax.broadcasted_iota(jnp.int32, (W, W), 0)
    cc = lax.broadcasted_iota(jnp.int32, (W, W), 1)
    from_prev = cc > r
    prev_ok = from_prev & ((m_id > 0) | (cc >= W - N_META))
    low_k = lax.broadcasted_iota(jnp.int32, (2 * W, LANES), 1) < SWA_HD
    low = lax.broadcasted_iota(jnp.int32, (W, LANES), 1) < SWA_HD

    def attend(q, kd, vd, h):
        s = _dot_nt(q.astype(BF16), kd)
        logits = jnp.where(prev_ok, s[:, :W], jnp.where(from_prev, NEG_INF, s[:, W:]))
        p, den = _sink_softmax(logits, sink_ref[h])
        p2 = jnp.concatenate([jnp.where(from_prev, p, 0.0), jnp.where(from_prev, 0.0, p)], axis=1)
        return _dot(p2.astype(BF16), vd) / den

    for g in range(SWA_KV_HEADS):
        kd = _dup_head(k2, g, low_k).astype(BF16)
        vd = _dup_head(v2, g, low_k).astype(BF16)
        for p in range(g * SWA_GROUP // 2, (g + 1) * SWA_GROUP // 2):
            slab = q_ref[:, p * LANES:(p + 1) * LANES]
            o_even = attend(jnp.where(low, slab, 0.0), kd, vd, 2 * p)
            o_odd = attend(jnp.where(low, 0.0, slab), kd, vd, 2 * p + 1)
            o_ref[:, p * LANES:(p + 1) * LANES] = jnp.where(low, o_even, o_odd).astype(BF16)

    kprev[...] = kcur
    vprev[...] = vcur

    @pl.when(m_id == pl.num_programs(1) - 1)
    def _():
        kp_ref[...] = kcur.T
        vp_ref[...] = vcur.T


def _swa_prompt_ret_sample_kernel(sink_ref, q_ref, kv_ref, kvm_ref, lg_ref, zs_ref, st_ref,
                                 o_ref, kp_ref, vp_ref, os_ref, sto_ref, kprev, vprev, *, nb, t, every):
    _swa_prompt_kernel(sink_ref, q_ref, kv_ref, kvm_ref, o_ref, kp_ref, vp_ref, kprev, vprev)

    @pl.when(lax.rem(pl.program_id(1), every) == every - 1)
    def _():
        state_jobs, finish = _ret_sample_jobs(lg_ref, zs_ref, st_ref, os_ref, sto_ref, nb=nb, t=t)
        for job in state_jobs + [finish]:
            job()


def _swa_prompt_ret_sample(sinks, z, zm, bsz, seq, lg, sample_row, state, t):
    W = WINDOW
    nblk = seq // W
    every = RET_SAMPLE_EVERY
    ret_steps = bsz * nblk // every
    dbs = state.shape[0]
    nb = dbs // ret_steps
    assert nblk % every == 0 and nb * ret_steps == dbs and (nb * t) % 16 == 0
    rows = nb * t
    qw = SWA_HEADS * SWA_HD
    kv_col = SWA_KV_COL // (2 * LANES)
    ret_step =lambda b, m: (b * nblk + m) // every
    state_spec = pl.BlockSpec((nb, RET_HEADS, RET_DK, RET_DV), lambda b, m: (ret_step(b, m), 0, 0, 0))
    return pl.pallas_call(
        functools.partial(_swa_prompt_ret_sample_kernel, nb=nb, t=t, every=every),
        grid=(bsz, nblk),
        in_specs=[
            pl.BlockSpec(memory_space=pltpu.SMEM),
            pl.BlockSpec((W, qw), lambda b, m: (b * nblk + m, SWA_Q_COL // qw)),
            pl.BlockSpec((W, 2 * LANES), lambda b, m: (b * nblk + m, kv_col)),
            pl.BlockSpec((W, 2 * LANES), lambda b, m: (0, kv_col)),
            pl.BlockSpec(memory_space=pltpu.SMEM),
            pl.BlockSpec((rows, 4 * RET_W), lambda b, m: (sample_row // rows + ret_step(b, m), 0)),
            state_spec,
        ],
        out_specs=[
            pl.BlockSpec((W, qw), lambda b, m: (b * nblk + m, 0)),
            pl.BlockSpec((None, W, LANES), lambda b, m: (b, 0, 0)),
            pl.BlockSpec((None, W, LANES), lambda b, m: (b, 0, 0)),
            pl.BlockSpec((rows, RET_W), lambda b, m: (ret_step(b, m), 0)),
            state_spec,
        ],
        out_shape=[
            jax.ShapeDtypeStruct((bsz * seq, qw), BF16),
            jax.ShapeDtypeStruct((bsz, W, LANES), F32),
            jax.ShapeDtypeStruct((bsz, W, LANES), F32),
            jax.ShapeDtypeStruct((dbs * t, RET_W), BF16),
            jax.ShapeDtypeStruct(state.shape, F32),
        ],
        scratch_shapes=[pltpu.VMEM((W, LANES), F32), pltpu.VMEM((W, LANES), F32)],
        compiler_params=_cparams(("parallel", "arbitrary")),
        name="swa_prompt_ret_sample",
    )(sinks, z, z, zm, lg, z, state)


def _swa_sample_kernel(sink_ref, q_ref, kv_ref, kc_ref, vc_ref, o_ref, ko_ref, vo_ref, *, nb, t):
    W = WINDOW
    row_pad = jnp.zeros((LANES - nb * t, LANES), F32)
    knew_t = jnp.concatenate([kv_ref[:, :LANES], row_pad], axis=0).T
    vnew_t = jnp.concatenate([kv_ref[:, LANES:], row_pad], axis=0).T
    knew_b = knew_t.astype(BF16)
    vnew_b = vnew_t.astype(BF16)

    rq = SWA_HEADS * t
    tq = lax.broadcasted_iota(jnp.int32, (rq, 2 * W), 0) & (t - 1)
    cc = lax.broadcasted_iota(jnp.int32, (rq, 2 * W), 1)
    out_lane = lax.broadcasted_iota(jnp.int32, (LANES, LANES), 1)
    low = lax.broadcasted_iota(jnp.int32, (nb * t, LANES), 1) < SWA_HD
    low_t = lax.broadcasted_iota(jnp.int32, (t, LANES), 1) < SWA_HD
    sink_col = jnp.concatenate([jnp.full((t, 1), sink_ref[h], F32) for h in range(SWA_HEADS)], axis=0)
    npair = SWA_HEADS // 2

    slabs = [q_ref[:, p * LANES:(p + 1) * LANES] for p in range(npair)]
    q_heads = [_to_kv_half(slabs[h // 2], h, low) for h in range(SWA_HEADS)]

    logits, values = [], []
    for db in range(nb):
        rows = slice(db * t, (db + 1) * t)
        kc = kc_ref[db]
        vc = vc_ref[db]
        keep = out_lane < W - t
        ko_ref[db] = jnp.where(keep, pltpu.roll(kc, W - t, 1), pltpu.roll(knew_t, (W - t - db * t) % LANES, 1))
        vo_ref[db] = jnp.where(keep, pltpu.roll(vc, W - t, 1), pltpu.roll(vnew_t, (W - t - db * t) % LANES, 1))
        k2 = jnp.concatenate([kc.astype(BF16), knew_b], axis=1)
        values.append(jnp.concatenate([vc.astype(BF16), vnew_b], axis=1))
        t_new = cc - W - db * t
        mask = ((cc < W) & (cc > tq)) | ((t_new >= 0) & (t_new <= tq))
        q_db = jnp.concatenate([qh[rows, :] for qh in q_heads], axis=0).astype(BF16)
        logits.append(jnp.where(mask, _dot(q_db, k2), NEG_INF))
    p, den = _sink_softmax(jnp.concatenate(logits, axis=0), jnp.concatenate([sink_col] * nb, axis=0))
    p = p.astype(BF16)
    outs = [[] for _ in range(npair)]
    for db in range(nb):
        o = _dot_nt(p[db * rq:(db + 1) * rq, :], values[db]) / den[db * rq:(db + 1) * rq, :]
        for pr in range(npair):
            o_even = o[(2 * pr) * t:(2 * pr + 1) * t, :]
            o_odd = o[(2 * pr + 1) * t:(2 * pr + 2) * t, :]
            outs[pr].append(_from_kv_half(o_even, o_odd, 2 * pr // SWA_GROUP, low_t))
    for pr in range(npair):
        o_ref[:, pr * LANES:(pr + 1) * LANES] = jnp.concatenate(outs[pr], axis=0).astype(BF16)


def _swa_sample(sinks, z, row0, kc, vc, nb, t):
    dbs = kc.shape[0]
    n = dbs * t
    rows = nb * t
    W = WINDOW
    qw = SWA_HEADS * SWA_HD
    cache = pl.BlockSpec((nb, W, LANES), lambda i: (i, 0, 0))
    return pl.pallas_call(
        functools.partial(_swa_sample_kernel, nb=nb, t=t),
        grid=(dbs // nb,),
        in_specs=[
            pl.BlockSpec(memory_space=pltpu.SMEM),
            pl.BlockSpec((rows, qw), lambda i: (row0 // rows + i, SWA_Q_COL // qw)),
            pl.BlockSpec((rows, 2 * LANES), lambda i: (row0 // rows + i, SWA_KV_COL // (2 * LANES))),
            cache, cache,
        ],
        out_specs=[pl.BlockSpec((rows, qw), lambda i: (i, 0)), cache, cache],
        out_shape=[
            jax.ShapeDtypeStruct((n, qw), BF16),
            jax.ShapeDtypeStruct((dbs, W, LANES), F32),
            jax.ShapeDtypeStruct((dbs, W, LANES), F32),
        ],
        compiler_params=_cparams(("parallel",)),
        name="swa_sample",
    )(sinks, z, z, kc, vc)


def _out_proj_ln_kernel(rop_ref, sop_ref, xp_ref, ros_ref, sos_ref, xs_ref, eg_ref, eb_ref, w_hbm, mg_ref, mb_ref,
                        h_ref, hb_ref, wb_scr, stage, sem, *, np_tiles):
    i = pl.program_id(0)

    def store_chunk(rows, slot):
        wb_scr[rows, :] = slot[...].astype(BF16)

    @pl.when(i == 0)
    def _():
        _stage_weight(w_hbm, stage, sem, store_chunk)

    def step(ro_ref, so_ref, x_ref):
        sub = h_ref.shape[0] // 2
        for r in range(2):
            rows = slice(r * sub, (r + 1) * sub)
            x_in = _layer_norm(x_ref[rows, :], eg_ref[...], eb_ref[...])
            mixed = _dot(ro_ref[rows, :], wb_scr[:RET_W, :]) + _dot(so_ref[rows, :], wb_scr[RET_W:, :])
            h = _layer_norm(ALPHA * x_in + mixed, mg_ref[...], mb_ref[...])
            h_ref[rows, :] = h
            hb_ref[rows, :] = h.astype(BF16)

    @pl.when(i < np_tiles)
    def _():
        step(rop_ref, sop_ref, xp_ref)

    @pl.when(i >= np_tiles)
    def _():
        step(ros_ref, sos_ref, xs_ref)


def _out_proj_ln(ro_p, so_p, xp, ro_s, so_s, xs, eg, eb, w, mg, mb, tm):
    np_tiles = xp.shape[0] // tm
    ns_tiles = xs.shape[0] // tm
    n = xp.shape[0] + xs.shape[0]
    vec = pl.BlockSpec((1, D_MODEL), lambda i: (0, 0))
    tile = pl.BlockSpec((tm, D_MODEL), lambda i: (i, 0))
    p_idx = lambda i: (jnp.minimum(i, np_tiles - 1), 0)
    s_idx = lambda i: (jnp.clip(i - np_tiles, 0, ns_tiles - 1), 0)
    return pl.pallas_call(
        functools.partial(_out_proj_ln_kernel, np_tiles=np_tiles),
        grid=(np_tiles + ns_tiles,),
        in_specs=[
            pl.BlockSpec((tm, RET_W), p_idx), pl.BlockSpec((tm, RET_W), p_idx), pl.BlockSpec((tm, D_MODEL), p_idx),
            pl.BlockSpec((tm, RET_W), s_idx), pl.BlockSpec((tm, RET_W), s_idx), pl.BlockSpec((tm, D_MODEL), s_idx),
            vec, vec,
            pl.BlockSpec(memory_space=pl.ANY),
            vec, vec,
        ],
        out_specs=[tile, tile],
        out_shape=[jax.ShapeDtypeStruct((n, D_MODEL), F32), jax.ShapeDtypeStruct((n, D_MODEL), BF16)],
        scratch_shapes=[pltpu.VMEM((D_MODEL, D_MODEL), BF16),
                        pltpu.VMEM((WEIGHT_STAGE_SLOTS, WEIGHT_STAGE_ROWS // 2, D_MODEL), F32),
                        pltpu.SemaphoreType.DMA((WEIGHT_STAGE_SLOTS,))],
        compiler_params=pltpu.CompilerParams(dimension_semantics=("arbitrary",), vmem_limit_bytes=VMEM_LIMIT_LARGE),
        name="out_proj_ln",
    )(ro_p, so_p, xp, ro_s, so_s, xs, eg, eb, w, mg, mb)


def _ffn_ln_kernel(hb_ref, h_hbm, wg_ref, wu_ref, wd_ref, g_ref, b_ref, y_ref, *rest, row0):
    *w_out_refs, h_res, sem = rest
    i = pl.program_id(0)
    j = pl.program_id(1)
    tm = h_res.shape[0]
    residual_copy = pltpu.make_async_copy(h_hbm.at[pl.ds(row0 + i * tm, tm), :], h_res, sem)

    def hidden_tile(accumulate):
        wg, wu, wd = (w[...].astype(BF16) for w in (wg_ref, wu_ref, wd_ref))
        for out_ref, w in zip(w_out_refs, (wg, wu, wd)):
            out_ref[...] = w
        hb = hb_ref[...]
        act = _silu(_dot(hb, wg)) * _dot(hb, wu)
        update = _dot(act.astype(BF16), wd)
        y_ref[...] = y_ref[...] + update if accumulate else update

    @pl.when(j == 0)
    def _():
        residual_copy.start()
        hidden_tile(False)

    @pl.when(j > 0)
    def _():
        hidden_tile(True)

    @pl.when(j == pl.num_programs(1) - 1)
    def _():
        residual_copy.wait()
        y_ref[...] = _layer_norm(ALPHA * h_res[...] + y_ref[...], g_ref[...], b_ref[...])


def _ffn_ln(h, hb, row0, n, wg, wu, wd, g, b, tm, th):
    emit_weights = wg.dtype == F32
    assert not emit_weights or n == tm
    vec = pl.BlockSpec((1, D_MODEL), lambda i, j: (0, 0))
    w_specs = [
        pl.BlockSpec((D_MODEL, th), lambda i, j: (0, j)),
        pl.BlockSpec((D_MODEL, th), lambda i, j: (0, j)),
        pl.BlockSpec((th, D_MODEL), lambda i, j: (j, 0)),
    ]
    out_specs = [pl.BlockSpec((tm, D_MODEL), lambda i, j: (i, 0))]
    out_shape = [jax.ShapeDtypeStruct((n, D_MODEL), F32)]
    if emit_weights:
        out_specs += w_specs
        out_shape += [jax.ShapeDtypeStruct(w.shape, BF16) for w in (wg, wu, wd)]
    outs = pl.pallas_call(
        functools.partial(_ffn_ln_kernel, row0=row0),
        grid=(n // tm, FFN_HIDDEN // th),
        in_specs=[pl.BlockSpec((tm, D_MODEL), lambda i, j: (row0 // tm + i, 0)), pl.BlockSpec(memory_space=pl.ANY)]
        + w_specs + [vec, vec],
        out_specs=out_specs,
        out_shape=out_shape,
        scratch_shapes=[pltpu.VMEM((tm, D_MODEL), F32), pltpu.SemaphoreType.DMA(())],
        compiler_params=pltpu.CompilerParams(dimension_semantics=("arbitrary", "arbitrary"),
                                             vmem_limit_bytes=VMEM_LIMIT_LARGE),
        name="ffn_ln",
    )(hb, h, wg, wu, wd, g, b)
    return tuple(outs) if emit_weights else outs[0]


def _position_tables(seq, t, past_len, sample_rows):
    r = jnp.arange(seq + sample_rows + RET_CHUNK)
    meta_r = r - seq - sample_rows
    pos = jnp.where(r < seq, N_META + r,
                    jnp.where(meta_r < 0, past_len + (r - seq) % t,
                              jnp.maximum(meta_r - (RET_CHUNK - N_META), 0))).astype(F32)[:, None]
    ret_freq = jnp.power(RET_THETA, -jnp.linspace(0.0, 1.0, RET_DK // 2, dtype=F32))[None, :]
    a = (N_META + RET_CHUNK * jnp.arange(seq // RET_CHUNK)).astype(F32)[:, None] * ret_freq
    b = jnp.arange(RET_CHUNK).astype(F32)[:, None] * ret_freq
    ca, sa, cb, sb = jnp.cos(a)[:, None, :], jnp.sin(a)[:, None, :], jnp.cos(b)[None], jnp.sin(b)[None]
    rest_ang = pos[seq:] * ret_freq
    ret_cos = jnp.concatenate([(ca * cb - sa * sb).reshape(seq, LANES), jnp.cos(rest_ang)], axis=0)
    ret_sin = jnp.concatenate([(sa * cb + ca * sb).reshape(seq, LANES), jnp.sin(rest_ang)], axis=0)
    half = ROT_DIM // 2
    swa_freq = jnp.power(ROPE_THETA, -jnp.arange(0, ROT_DIM, 2, dtype=F32) / ROT_DIM)
    swa_ang = pos * swa_freq[None, :]
    cos = jnp.tile(jnp.cos(swa_ang), (1, LANES // half))
    sin = jnp.tile(jnp.sin(swa_ang), (1, LANES // half))
    d = jnp.arange(LANES) % SWA_HD
    c = jnp.where(d < ROT_DIM, cos, 1.0)
    s1 = jnp.where((d >= half) & (d < ROT_DIM), sin, 0.0)
    s2 = jnp.where(d < half, -sin, 0.0)
    return jnp.concatenate([ret_cos, ret_sin, c, s1, s2], axis=1)


def kernel(x_prompt, x_sample, state_ret, cache_swa_k, cache_swa_v, meta_tokens, ln_emb_g, ln_emb_b,
           w_in, w_out, swa_sinks, ln_mix_g, ln_mix_b, w_ffn_gate, w_ffn_up, w_ffn_down, ln_ffn_g, ln_ffn_b):
    bsz, seq, d = x_prompt.shape
    dbs, t, _ = x_sample.shape
    assert w_in.shape[0] == DEPTH and d == D_MODEL and seq % RET_CHUNK == 0 and t & (t - 1) == 0
    past_len = 16384
    row = lambda a: a.reshape(1, -1)

    eg, eb = row(ln_emb_g), row(ln_emb_b)
    sinks = swa_sinks[0]
    lg = jnp.log(1.0 - jnp.power(2.0, -5.0 - jnp.arange(RET_HEADS, dtype=F32)))

    xp = x_prompt.reshape(bsz * seq, d)
    xs = x_sample.reshape(dbs * t, d)

    tabs = _position_tables(seq, t, past_len, LN_PROJ_TILE)
    z, zm = _ln_proj(xp, xs, meta_tokens, eg, eb, w_in[0], tabs, seq, LN_PROJ_TILE, PROJ_W)
    sample_row = bsz * seq

    ret_o_p, ret_state_p = _ret_prompt(lg, z, zm, bsz, seq)
    swa_o_p, k_p, v_p, ret_o_s, ret_state_s = _swa_prompt_ret_sample(
        sinks, z, zm, bsz, seq, lg, sample_row, state_ret[0], t)
    to_kernel = lambda a: jnp.transpose(a[0], (0, 2, 3, 1)).reshape(-1, LANES, WINDOW)
    from_kernel = lambda a: jnp.transpose(a.reshape(-1, SWA_KV_HEADS, SWA_HD, WINDOW), (0, 3, 1, 2))[None]
    kc, vc = to_kernel(cache_swa_k), to_kernel(cache_swa_v)
    swa_o_s, k_s, v_s = _swa_sample(sinks, z, sample_row, kc, vc, SWA_SAMPLE_BATCHES, t)

    mg, mb = row(ln_mix_g[0]), row(ln_mix_b[0])
    fg, fb = row(ln_ffn_g[0]), row(ln_ffn_b[0])
    ffn_w = (w_ffn_gate[0], w_ffn_up[0], w_ffn_down[0])
    h, hb = _out_proj_ln(ret_o_p, swa_o_p, xp, ret_o_s, swa_o_s, xs, eg, eb, w_out[0], mg, mb, 512)
    y_s, *ffn_wb = _ffn_ln(h, hb, sample_row, dbs * t, *ffn_w, fg, fb, TOK_TILE, 256)
    y_p = _ffn_ln(h, hb, 0, bsz * seq, *ffn_wb, fg, fb, TOK_TILE, 512)

    return (y_p.reshape(bsz, seq, d), y_s.reshape(dbs, t, d),
            ret_state_p[None], from_kernel(k_p), from_kernel(v_p),
            ret_state_s[None], from_kernel(k_s), from_kernel(v_s))
```

```python
import functools

import jax
import jax.numpy as jnp
from jax import lax
from jax.experimental import pallas as pl
from jax.experimental.pallas import tpu as pltpu

F32 = jnp.float32
BF16 = jnp.bfloat16

D_MODEL = 2048
N_META = 16
RET_HEADS = 4
RET_DK = 256
RET_DV = 256
RET_CHUNK = 128
RET_THETA = 10000.0
SWA_HD = 64
SWA_HEADS = 16
SWA_KV_HEADS = 2
SWA_GROUP = SWA_HEADS // SWA_KV_HEADS
WINDOW = 128
ROPE_THETA = 500000.0
ROT_DIM = SWA_HD // 4
FFN_HIDDEN = 5632
PROJ_W = 5376
DEPTH = 1
ALPHA = (2.0 * DEPTH) ** 0.25
LN_EPS = 1e-5
NEG_INF = -1e30

LANES = 128
RET_W = RET_HEADS * RET_DK
SWA_Q_COL = 4 * RET_W
SWA_KV_COL = SWA_Q_COL + SWA_HEADS * SWA_HD
VMEM_LIMIT = 56 * 1024 * 1024
VMEM_LIMIT_LARGE = 60 * 1024 * 1024
TOK_TILE = 1024
LN_PROJ_SUBBLOCKS = 2
RET_CHUNK_UNROLL = 16
RET_SAMPLE_EVERY = 2
LN_PROJ_TILE = 256
SWA_SAMPLE_BATCHES = 8
WEIGHT_STAGE_ROWS = 256
WEIGHT_STAGE_SLOTS = 4


def _cparams(sem):
    return pltpu.CompilerParams(dimension_semantics=sem, vmem_limit_bytes=VMEM_LIMIT)


def _layer_norm(x, g, b):
    mu = jnp.mean(x, axis=-1, keepdims=True)
    xc = x - mu
    var = jnp.mean(xc * xc, axis=-1, keepdims=True)
    return xc * lax.rsqrt(var + LN_EPS) * g + b


def _silu(x):
    return x / (1.0 + jnp.exp(-x))


def _dot(a, b):
    return jnp.dot(a, b, preferred_element_type=F32)


def _dot_nt(a, b):
    return lax.dot_general(a, b, (((1,), (1,)), ((), ())), preferred_element_type=F32)


def _dot_tn(a, b):
    return lax.dot_general(a, b, (((0,), (0,)), ((), ())), preferred_element_type=F32)


def _store_rotated(z_ref, rows, res, tab):
    cos, sin, c, s1, s2 = (tab[:, n * LANES:(n + 1) * LANES] for n in range(5))
    for c0 in range(0, 2 * RET_W, RET_DK):
        rot = _ret_rope(res[:, c0:c0 + RET_DK], cos, sin)
        z_ref[rows, c0:c0 + RET_DK] = rot if c0 < RET_W else rot * RET_DK ** -0.5
    z_ref[rows, 2 * RET_W:3 * RET_W] = res[:, 2 * RET_W:3 * RET_W]
    z_ref[rows, 3 * RET_W:SWA_Q_COL] = _silu(res[:, 3 * RET_W:SWA_Q_COL])
    for c0 in range(SWA_Q_COL, SWA_KV_COL + LANES, LANES):
        rot = _swa_rope(res[:, c0:c0 + LANES], c, s1, s2)
        z_ref[rows, c0:c0 + LANES] = rot * SWA_HD ** -0.5 if c0 < SWA_KV_COL else rot
    z_ref[rows, SWA_KV_COL + LANES:] = res[:, SWA_KV_COL + LANES:]


def _stage_weight(w_hbm, stage, sem, store_chunk):
    slots, chunk = stage.shape[:2]
    n_chunks = w_hbm.shape[0] // chunk

    def weight_copy(k):
        return pltpu.make_async_copy(w_hbm.at[pl.ds(k * chunk, chunk), :], stage.at[k % slots], sem.at[k % slots])

    for k in range(min(slots - 1, n_chunks)):
        weight_copy(k).start()
    for k in range(n_chunks):
        if k + slots - 1 < n_chunks:
            weight_copy(k + slots - 1).start()
        weight_copy(k).wait()
        store_chunk(slice(k * chunk, (k + 1) * chunk), stage.at[k % slots])


def _ln_proj_kernel(xp_ref, xs_ref, xm_ref, g_ref, b_ref, w_hbm, tabp_ref, tabs_ref, tabm_ref, z_ref, zm_ref,
                    hm_scr, wb_scr, stage, sem, *, np_tiles, ns_tiles):
    i = pl.program_id(0)
    j = pl.program_id(1)
    first = j == 0
    is_sample = i >= np_tiles
    is_last = i == np_tiles + ns_tiles - 1
    n_col, _, tn = wb_scr.shape

    def store_chunk(rows, slot):
        for jt in range(n_col):
            wb_scr[jt, rows, :] = slot[:, jt * tn:(jt + 1) * tn].astype(BF16)

    @pl.when(first & (i == 0))
    def _():
        _stage_weight(w_hbm, stage, sem, store_chunk)

    def norm(x):
        return _layer_norm(x, g_ref[...], b_ref[...]).astype(BF16)

    def project(x_ref, tab_ref):
        sub = x_ref.shape[0] // LN_PROJ_SUBBLOCKS
        for r in range(LN_PROJ_SUBBLOCKS):
            rows = slice(r * sub, (r + 1) * sub)
            _store_rotated(z_ref, rows, _dot(norm(x_ref[rows, :]), wb_scr[j]), tab_ref[rows, :])

    @pl.when(jnp.logical_not(is_sample))
    def _():
        project(xp_ref, tabp_ref)

    @pl.when(is_sample)
    def _():
        project(xs_ref, tabs_ref)

    @pl.when(is_last)
    def _():
        lead = hm_scr.shape[0] - xm_ref.shape[0]
        hm_scr[:lead, :] = jnp.zeros((lead, D_MODEL), BF16)
        hm_scr[lead:, :] = norm(xm_ref[...])
        _store_rotated(zm_ref, slice(None), _dot(hm_scr[...], wb_scr[j]), tabm_ref[...])


def _ln_proj(xp, xs, xm, g, b, w, tabs, seq, tm, tn):
    np_tiles = xp.shape[0] // tm
    ns_tiles = xs.shape[0] // tm
    last = np_tiles + ns_tiles - 1
    n_col = PROJ_W // tn
    assert n_col == 1
    tab_w = tabs.shape[1]
    return pl.pallas_call(
        functools.partial(_ln_proj_kernel, np_tiles=np_tiles, ns_tiles=ns_tiles),
        grid=(np_tiles + ns_tiles, n_col),
        in_specs=[
            pl.BlockSpec((tm, D_MODEL), lambda i, j: (jnp.minimum(i, np_tiles - 1), 0)),
            pl.BlockSpec((tm, D_MODEL), lambda i, j: (jnp.clip(i - np_tiles, 0, ns_tiles - 1), 0)),
            pl.BlockSpec(xm.shape, lambda i, j: (0, 0), pipeline_mode=pl.Buffered(1)),
            pl.BlockSpec((1, D_MODEL), lambda i, j: (0, 0)),
            pl.BlockSpec((1, D_MODEL), lambda i, j: (0, 0)),
            pl.BlockSpec(memory_space=pl.ANY),
            pl.BlockSpec((tm, tab_w), lambda i, j: (lax.rem(i, seq // tm), 0)),
            pl.BlockSpec((tm, tab_w), lambda i, j: (seq // tm, 0)),
            pl.BlockSpec((RET_CHUNK, tab_w), lambda i, j: ((seq + tm) // RET_CHUNK, 0)),
        ],
        out_specs=[
            pl.BlockSpec((tm, tn), lambda i, j: (i, j)),
            pl.BlockSpec((RET_CHUNK, tn), lambda i, j: (0, jnp.where(i == last, j, 0))),
        ],
        out_shape=[
            jax.ShapeDtypeStruct((xp.shape[0] + xs.shape[0], PROJ_W), F32),
            jax.ShapeDtypeStruct((RET_CHUNK, PROJ_W), F32),
        ],
        scratch_shapes=[
            pltpu.VMEM((RET_CHUNK, D_MODEL), BF16),
            pltpu.VMEM((n_col, D_MODEL, tn), BF16),
            pltpu.VMEM((WEIGHT_STAGE_SLOTS, WEIGHT_STAGE_ROWS // 4, PROJ_W), F32),
            pltpu.SemaphoreType.DMA((WEIGHT_STAGE_SLOTS,)),
        ],
        compiler_params=pltpu.CompilerParams(dimension_semantics=("arbitrary", "arbitrary"),
                                             vmem_limit_bytes=VMEM_LIMIT_LARGE),
        name="ln_proj",
    )(xp, xs, xm, g, b, w, tabs, tabs, tabs)


def _ret_rope(x, cos, sin):
    x1 = x[:, :LANES]
    x2 = x[:, LANES:]
    return jnp.concatenate([x1 * cos - x2 * sin, x2 * cos + x1 * sin], axis=1)


def _group_norm_gate(o, gate):
    mu = jnp.mean(o, axis=-1, keepdims=True)
    oc = o - mu
    var = jnp.mean(oc * oc, axis=-1, keepdims=True)
    return oc * lax.rsqrt(var + LN_EPS) * gate


def _ret_prompt_kernel(lg_ref, q_ref, k_ref, v_ref, g_ref, km_ref, vm_ref, o_ref, s_ref, s_scr):
    C = RET_CHUNK
    lg = lg_ref[pl.program_id(1)]
    ri = lax.broadcasted_iota(jnp.int32, (C, C), 0)
    ci = lax.broadcasted_iota(jnp.int32, (C, C), 1)
    rel = (ri - ci).astype(F32)
    decay = jnp.where(rel >= 0.0, jnp.exp(jnp.maximum(rel, 0.0) * lg), 0.0)
    row = lax.broadcasted_iota(jnp.int32, (C, 1), 0).astype(F32)
    q_decay = jnp.exp((row + 1.0) * lg)
    k_decay = jnp.exp((C - 1.0 - row) * lg)
    chunk_decay = jnp.exp(jnp.full((1, RET_DV), C * lg, F32))

    meta_decay = jnp.where(row >= C - N_META, k_decay, 0.0)
    s_scr[...] = _dot_tn((km_ref[...] * meta_decay).astype(BF16), vm_ref[...].astype(BF16))

    def chunk(c, carry):
        rows = pl.ds(pl.multiple_of(c * C, C), C)
        k = k_ref[rows, :]
        qb = q_ref[rows, :].astype(BF16)
        vb = v_ref[rows, :].astype(BF16)
        s_prev = s_scr[...]
        scores = _dot_nt(qb, k.astype(BF16)) * decay
        inner = _dot(scores.astype(BF16), vb)
        cross = _dot(qb, s_prev.astype(BF16)) * q_decay
        s_scr[...] = chunk_decay * s_prev + _dot_tn((k * k_decay).astype(BF16), vb)
        o_ref[rows, :] = _group_norm_gate(inner + cross, g_ref[rows, :]).astype(BF16)
        return carry

    lax.fori_loop(0, q_ref.shape[0] // C, chunk, 0, unroll=RET_CHUNK_UNROLL)
    s_ref[...] = s_scr[...]


def _ret_prompt(lg, z, zm, bsz, seq):
    col = lambda base: (lambda b, h: (b, base + h))
    mcol = lambda base: (lambda b, h: (0, base + h))
    return pl.pallas_call(
        _ret_prompt_kernel,
        grid=(bsz, RET_HEADS),
        in_specs=[
            pl.BlockSpec(memory_space=pltpu.SMEM),
            pl.BlockSpec((seq, RET_DK), col(0)),
            pl.BlockSpec((seq, RET_DK), col(RET_HEADS)),
            pl.BlockSpec((seq, RET_DV), col(2 * RET_HEADS)),
            pl.BlockSpec((seq, RET_DV), col(3 * RET_HEADS)),
            pl.BlockSpec((RET_CHUNK, RET_DK), mcol(RET_HEADS)),
            pl.BlockSpec((RET_CHUNK, RET_DV), mcol(2 * RET_HEADS)),
        ],
        out_specs=[
            pl.BlockSpec((seq, RET_DV), lambda b, h: (b, h)),
            pl.BlockSpec((None, None, RET_DK, RET_DV), lambda b, h: (b, h, 0, 0)),
        ],
        out_shape=[
            jax.ShapeDtypeStruct((bsz * seq, RET_W), BF16),
            jax.ShapeDtypeStruct((bsz, RET_HEADS, RET_DK, RET_DV), F32),
        ],
        scratch_shapes=[pltpu.VMEM((RET_DK, RET_DV), F32)],
        compiler_params=_cparams(("parallel", "parallel")),
        name="ret_prompt",
    )(lg, z, z, z, z, zm, zm)


def _ret_sample_jobs(lg_ref, z_ref, s_ref, o_ref, so_ref, *, nb, t):
    rows = nb * t
    R = RET_HEADS * rows

    def stack(base):
        return jnp.concatenate([z_ref[:, base + h * RET_DK: base + (h + 1) * RET_DK] for h in range(RET_HEADS)],
                               axis=0)

    q = stack(0)
    k = stack(RET_W)
    v = stack(2 * RET_W)
    gate = stack(3 * RET_W)

    ri = lax.broadcasted_iota(jnp.int32, (R, R), 0)
    ci = lax.broadcasted_iota(jnp.int32, (R, R), 1)
    rcol = lax.broadcasted_iota(jnp.int32, (R, 1), 0)
    lg_col = jnp.zeros((R, 1), F32)
    for h in range(RET_HEADS):
        lg_col = jnp.where((rcol >= h * rows) & (rcol < (h + 1) * rows), lg_ref[h], lg_col)
    tcol = (rcol & (t - 1)).astype(F32)
    rel = (ri - ci).astype(F32)
    same = ((ri & -t) == (ci & -t)) & (ri >= ci)
    decay = jnp.where(same, jnp.exp(jnp.maximum(rel, 0.0) * lg_col), 0.0)
    q_decay = jnp.exp((tcol + 1.0) * lg_col)
    k_decay = jnp.exp((t - 1.0 - tcol) * lg_col)

    qb = q.astype(BF16)
    vb = v.astype(BF16)
    scores = _dot_nt(qb, k.astype(BF16)) * decay
    inner = _dot(scores.astype(BF16), vb)
    kw = k * k_decay

    cross_parts = [None] * (RET_HEADS * nb)

    def state_job(h, db):
        def run():
            r0 = h * rows + db * t
            s_prev = s_ref[db, h]
            cross_parts[h * nb + db] = _dot(q[r0:r0 + t, :].astype(BF16), s_prev.astype(BF16))
            mine = (rcol >= r0) & (rcol < r0 + t)
            upd = _dot_tn(jnp.where(mine, kw, 0.0).astype(BF16), vb)
            step_decay = jnp.exp(jnp.full((1, RET_DV), t * lg_ref[h], F32))
            so_ref[db, h] = step_decay * s_prev + upd
        return run

    def finish():
        cross = jnp.concatenate(cross_parts, axis=0) * q_decay
        out = _group_norm_gate(inner + cross, gate).astype(BF16)
        for h in range(RET_HEADS):
            o_ref[:, h * RET_DV:(h + 1) * RET_DV] = out[h * rows:(h + 1) * rows, :]

    return [state_job(h, db) for h in range(RET_HEADS) for db in range(nb)], finish


def _swa_rope(x, c, s1, s2):
    return x * c + pltpu.roll(x, 8, 1) * s1 + pltpu.roll(x, LANES - 8, 1) * s2


def _dup_head(x, g, low):
    swapped = pltpu.roll(x, SWA_HD, 1)
    return jnp.where(low, x, swapped) if g == 0 else jnp.where(low, swapped, x)


def _to_kv_half(slab, head, low):
    g = head // SWA_GROUP
    src = slab if head % 2 == g else pltpu.roll(slab, SWA_HD, 1)
    return jnp.where(low, src, 0.0) if g == 0 else jnp.where(low, 0.0, src)


def _from_kv_half(o_even, o_odd, g, low):
    if g == 0:
        return jnp.where(low, o_even, pltpu.roll(o_odd, SWA_HD, 1))
    return jnp.where(low, pltpu.roll(o_even, SWA_HD, 1), o_odd)


def _sink_softmax(logits, sink):
    m = jnp.maximum(jnp.max(logits, axis=-1, keepdims=True), sink)
    p = jnp.exp(logits - m)
    return p, jnp.sum(p, axis=-1, keepdims=True) + jnp.exp(sink - m)


def _swa_prompt_kernel(sink_ref, q_ref, kv_ref, kvm_ref, o_ref, kp_ref, vp_ref, kprev, vprev):
    W = WINDOW
    m_id = pl.program_id(1)

    @pl.when(m_id == 0)
    def _():
        kprev[...] = kvm_ref[:, :LANES]
        vprev[...] = kvm_ref[:, LANES:]

    kcur = kv_ref[:, :LANES]
    vcur = kv_ref[:, LANES:]
    k2 = jnp.concatenate([kprev[...], kcur], axis=0)
    v2 = jnp.concatenate([vprev[...], vcur], axis=0)

    r = lax.broadcasted_iota(jnp.int32, (W, W), 0)
    cc = lax.broadcasted_iota(jnp.int32, (W, W), 1)
    from_prev = cc > r
    prev_ok = from_prev & ((m_id > 0) | (cc >= W - N_META))
    low_k = lax.broadcasted_iota(jnp.int32, (2 * W, LANES), 1) < SWA_HD
    low = lax.broadcasted_iota(jnp.int32, (W, LANES), 1) < SWA_HD

    def attend(q, kd, vd, h):
        s = _dot_nt(q.astype(BF16), kd)
        logits = jnp.where(prev_ok, s[:, :W], jnp.where(from_prev, NEG_INF, s[:, W:]))
        p, den = _sink_softmax(logits, sink_ref[h])
        p2 = jnp.concatenate([jnp.where(from_prev, p, 0.0), jnp.where(from_prev, 0.0, p)], axis=1)
        return _dot(p2.astype(BF16), vd) / den

    for g in range(SWA_KV_HEADS):
        kd = _dup_head(k2, g, low_k).astype(BF16)
        vd = _dup_head(v2, g, low_k).astype(BF16)
        for p in range(g * SWA_GROUP // 2, (g + 1) * SWA_GROUP // 2):
            slab = q_ref[:, p * LANES:(p + 1) * LANES]
            o_even = attend(jnp.where(low, slab, 0.0), kd, vd, 2 * p)
            o_odd = attend(jnp.where(low, 0.0, slab), kd, vd, 2 * p + 1)
            o_ref[:, p * LANES:(p + 1) * LANES] = jnp.where(low, o_even, o_odd).astype(BF16)

    kprev[...] = kcur
    vprev[...] = vcur

    @pl.when(m_id == pl.num_programs(1) - 1)
    def _():
        kp_ref[...] = kcur.T
        vp_ref[...] = vcur.T


def _swa_prompt_ret_sample_kernel(sink_ref, q_ref, kv_ref, kvm_ref, lg_ref, zs_ref, st_ref,
                                 o_ref, kp_ref, vp_ref, os_ref, sto_ref, kprev, vprev, *, nb, t, every):
    _swa_prompt_kernel(sink_ref, q_ref, kv_ref, kvm_ref, o_ref, kp_ref, vp_ref, kprev, vprev)

    @pl.when(lax.rem(pl.program_id(1), every) == every - 1)
    def _():
        state_jobs, finish = _ret_sample_jobs(lg_ref, zs_ref, st_ref, os_ref, sto_ref, nb=nb, t=t)
        for job in state_jobs + [finish]:
            job()


def _swa_prompt_ret_sample(sinks, z, zm, bsz, seq, lg, sample_row, state, t):
    W = WINDOW
    nblk = seq // W
    every = RET_SAMPLE_EVERY
    ret_steps = bsz * nblk // every
    dbs = state.shape[0]
    nb = dbs // ret_steps
    assert nblk % every == 0 and nb * ret_steps == dbs and (nb * t) % 16 == 0
    rows = nb * t
    qw = SWA_HEADS * SWA_HD
    kv_col = SWA_KV_COL // (2 * LANES)
    ret_step =lambda b, m: (b * nblk + m) // every
    state_spec = pl.BlockSpec((nb, RET_HEADS, RET_DK, RET_DV), lambda b, m: (ret_step(b, m), 0, 0, 0))
    return pl.pallas_call(
        functools.partial(_swa_prompt_ret_sample_kernel, nb=nb, t=t, every=every),
        grid=(bsz, nblk),
        in_specs=[
            pl.BlockSpec(memory_space=pltpu.SMEM),
            pl.BlockSpec((W, qw), lambda b, m: (b * nblk + m, SWA_Q_COL // qw)),
            pl.BlockSpec((W, 2 * LANES), lambda b, m: (b * nblk + m, kv_col)),
            pl.BlockSpec((W, 2 * LANES), lambda b, m: (0, kv_col)),
            pl.BlockSpec(memory_space=pltpu.SMEM),
            pl.BlockSpec((rows, 4 * RET_W), lambda b, m: (sample_row // rows + ret_step(b, m), 0)),
            state_spec,
        ],
        out_specs=[
            pl.BlockSpec((W, qw), lambda b, m: (b * nblk + m, 0)),
            pl.BlockSpec((None, W, LANES), lambda b, m: (b, 0, 0)),
            pl.BlockSpec((None, W, LANES), lambda b, m: (b, 0, 0)),
            pl.BlockSpec((rows, RET_W), lambda b, m: (ret_step(b, m), 0)),
            state_spec,
        ],
        out_shape=[
            jax.ShapeDtypeStruct((bsz * seq, qw), BF16),
            jax.ShapeDtypeStruct((bsz, W, LANES), F32),
            jax.ShapeDtypeStruct((bsz, W, LANES), F32),
            jax.ShapeDtypeStruct((dbs * t, RET_W), BF16),
            jax.ShapeDtypeStruct(state.shape, F32),
        ],
        scratch_shapes=[pltpu.VMEM((W, LANES), F32), pltpu.VMEM((W, LANES), F32)],
        compiler_params=_cparams(("parallel", "arbitrary")),
        name="swa_prompt_ret_sample",
    )(sinks, z, z, zm, lg, z, state)


def _swa_sample_kernel(sink_ref, q_ref, kv_ref, kc_ref, vc_ref, o_ref, ko_ref, vo_ref, *, nb, t):
    W = WINDOW
    row_pad = jnp.zeros((LANES - nb * t, LANES), F32)
    knew_t = jnp.concatenate([kv_ref[:, :LANES], row_pad], axis=0).T
    vnew_t = jnp.concatenate([kv_ref[:, LANES:], row_pad], axis=0).T
    knew_b = knew_t.astype(BF16)
    vnew_b = vnew_t.astype(BF16)

    rq = SWA_HEADS * t
    tq = lax.broadcasted_iota(jnp.int32, (rq, 2 * W), 0) & (t - 1)
    cc = lax.broadcasted_iota(jnp.int32, (rq, 2 * W), 1)
    out_lane = lax.broadcasted_iota(jnp.int32, (LANES, LANES), 1)
    low = lax.broadcasted_iota(jnp.int32, (nb * t, LANES), 1) < SWA_HD
    low_t = lax.broadcasted_iota(jnp.int32, (t, LANES), 1) < SWA_HD
    sink_col = jnp.concatenate([jnp.full((t, 1), sink_ref[h], F32) for h in range(SWA_HEADS)], axis=0)
    npair = SWA_HEADS // 2

    slabs = [q_ref[:, p * LANES:(p + 1) * LANES] for p in range(npair)]
    q_heads = [_to_kv_half(slabs[h // 2], h, low) for h in range(SWA_HEADS)]

    logits, values = [], []
    for db in range(nb):
        rows = slice(db * t, (db + 1) * t)
        kc = kc_ref[db]
        vc = vc_ref[db]
        keep = out_lane < W - t
        ko_ref[db] = jnp.where(keep, pltpu.roll(kc, W - t, 1), pltpu.roll(knew_t, (W - t - db * t) % LANES, 1))
        vo_ref[db] = jnp.where(keep, pltpu.roll(vc, W - t, 1), pltpu.roll(vnew_t, (W - t - db * t) % LANES, 1))
        k2 = jnp.concatenate([kc.astype(BF16), knew_b], axis=1)
        values.append(jnp.concatenate([vc.astype(BF16), vnew_b], axis=1))
        t_new = cc - W - db * t
        mask = ((cc < W) & (cc > tq)) | ((t_new >= 0) & (t_new <= tq))
        q_db = jnp.concatenate([qh[rows, :] for qh in q_heads], axis=0).astype(BF16)
        logits.append(jnp.where(mask, _dot(q_db, k2), NEG_INF))
    p, den = _sink_softmax(jnp.concatenate(logits, axis=0), jnp.concatenate([sink_col] * nb, axis=0))
    p = p.astype(BF16)
    outs = [[] for _ in range(npair)]
    for db in range(nb):
        o = _dot_nt(p[db * rq:(db + 1) * rq, :], values[db]) / den[db * rq:(db + 1) * rq, :]
        for pr in range(npair):
            o_even = o[(2 * pr) * t:(2 * pr + 1) * t, :]
            o_odd = o[(2 * pr + 1) * t:(2 * pr + 2) * t, :]
            outs[pr].append(_from_kv_half(o_even, o_odd, 2 * pr // SWA_GROUP, low_t))
    for pr in range(npair):
        o_ref[:, pr * LANES:(pr + 1) * LANES] = jnp.concatenate(outs[pr], axis=0).astype(BF16)


def _swa_sample(sinks, z, row0, kc, vc, nb, t):
    dbs = kc.shape[0]
    n = dbs * t
    rows = nb * t
    W = WINDOW
    qw = SWA_HEADS * SWA_HD
    cache = pl.BlockSpec((nb, W, LANES), lambda i: (i, 0, 0))
    return pl.pallas_call(
        functools.partial(_swa_sample_kernel, nb=nb, t=t),
        grid=(dbs // nb,),
        in_specs=[
            pl.BlockSpec(memory_space=pltpu.SMEM),
            pl.BlockSpec((rows, qw), lambda i: (row0 // rows + i, SWA_Q_COL // qw)),
            pl.BlockSpec((rows, 2 * LANES), lambda i: (row0 // rows + i, SWA_KV_COL // (2 * LANES))),
            cache, cache,
        ],
        out_specs=[pl.BlockSpec((rows, qw), lambda i: (i, 0)), cache, cache],
        out_shape=[
            jax.ShapeDtypeStruct((n, qw), BF16),
            jax.ShapeDtypeStruct((dbs, W, LANES), F32),
            jax.ShapeDtypeStruct((dbs, W, LANES), F32),
        ],
        compiler_params=_cparams(("parallel",)),
        name="swa_sample",
    )(sinks, z, z, kc, vc)


def _out_proj_ln_kernel(rop_ref, sop_ref, xp_ref, ros_ref, sos_ref, xs_ref, eg_ref, eb_ref, w_hbm, mg_ref, mb_ref,
                        h_ref, hb_ref, wb_scr, stage, sem, *, np_tiles):
    i = pl.program_id(0)

    def store_chunk(rows, slot):
        wb_scr[rows, :] = slot[...].astype(BF16)

    @pl.when(i == 0)
    def _():
        _stage_weight(w_hbm, stage, sem, store_chunk)

    def step(ro_ref, so_ref, x_ref):
        sub = h_ref.shape[0] // 2
        for r in range(2):
            rows = slice(r * sub, (r + 1) * sub)
            x_in = _layer_norm(x_ref[rows, :], eg_ref[...], eb_ref[...])
            mixed = _dot(ro_ref[rows, :], wb_scr[:RET_W, :]) + _dot(so_ref[rows, :], wb_scr[RET_W:, :])
            h = _layer_norm(ALPHA * x_in + mixed, mg_ref[...], mb_ref[...])
            h_ref[rows, :] = h
            hb_ref[rows, :] = h.astype(BF16)

    @pl.when(i < np_tiles)
    def _():
        step(rop_ref, sop_ref, xp_ref)

    @pl.when(i >= np_tiles)
    def _():
        step(ros_ref, sos_ref, xs_ref)


def _out_proj_ln(ro_p, so_p, xp, ro_s, so_s, xs, eg, eb, w, mg, mb, tm):
    np_tiles = xp.shape[0] // tm
    ns_tiles = xs.shape[0] // tm
    n = xp.shape[0] + xs.shape[0]
    vec = pl.BlockSpec((1, D_MODEL), lambda i: (0, 0))
    tile = pl.BlockSpec((tm, D_MODEL), lambda i: (i, 0))
    p_idx = lambda i: (jnp.minimum(i, np_tiles - 1), 0)
    s_idx = lambda i: (jnp.clip(i - np_tiles, 0, ns_tiles - 1), 0)
    return pl.pallas_call(
        functools.partial(_out_proj_ln_kernel, np_tiles=np_tiles),
        grid=(np_tiles + ns_tiles,),
        in_specs=[
            pl.BlockSpec((tm, RET_W), p_idx), pl.BlockSpec((tm, RET_W), p_idx), pl.BlockSpec((tm, D_MODEL), p_idx),
            pl.BlockSpec((tm, RET_W), s_idx), pl.BlockSpec((tm, RET_W), s_idx), pl.BlockSpec((tm, D_MODEL), s_idx),
            vec, vec,
            pl.BlockSpec(memory_space=pl.ANY),
            vec, vec,
        ],
        out_specs=[tile, tile],
        out_shape=[jax.ShapeDtypeStruct((n, D_MODEL), F32), jax.ShapeDtypeStruct((n, D_MODEL), BF16)],
        scratch_shapes=[pltpu.VMEM((D_MODEL, D_MODEL), BF16),
                        pltpu.VMEM((WEIGHT_STAGE_SLOTS, WEIGHT_STAGE_ROWS // 2, D_MODEL), F32),
                        pltpu.SemaphoreType.DMA((WEIGHT_STAGE_SLOTS,))],
        compiler_params=pltpu.CompilerParams(dimension_semantics=("arbitrary",), vmem_limit_bytes=VMEM_LIMIT_LARGE),
        name="out_proj_ln",
    )(ro_p, so_p, xp, ro_s, so_s, xs, eg, eb, w, mg, mb)


def _ffn_ln_kernel(hb_ref, h_hbm, wg_ref, wu_ref, wd_ref, g_ref, b_ref, y_ref, *rest, row0):
    *w_out_refs, h_res, sem = rest
    i = pl.program_id(0)
    j = pl.program_id(1)
    tm = h_res.shape[0]
    residual_copy = pltpu.make_async_copy(h_hbm.at[pl.ds(row0 + i * tm, tm), :], h_res, sem)

    def hidden_tile(accumulate):
        wg, wu, wd = (w[...].astype(BF16) for w in (wg_ref, wu_ref, wd_ref))
        for out_ref, w in zip(w_out_refs, (wg, wu, wd)):
            out_ref[...] = w
        hb = hb_ref[...]
        act = _silu(_dot(hb, wg)) * _dot(hb, wu)
        update = _dot(act.astype(BF16), wd)
        y_ref[...] = y_ref[...] + update if accumulate else update

    @pl.when(j == 0)
    def _():
        residual_copy.start()
        hidden_tile(False)

    @pl.when(j > 0)
    def _():
        hidden_tile(True)

    @pl.when(j == pl.num_programs(1) - 1)
    def _():
        residual_copy.wait()
        y_ref[...] = _layer_norm(ALPHA * h_res[...] + y_ref[...], g_ref[...], b_ref[...])


def _ffn_ln(h, hb, row0, n, wg, wu, wd, g, b, tm, th):
    emit_weights = wg.dtype == F32
    assert not emit_weights or n == tm
    vec = pl.BlockSpec((1, D_MODEL), lambda i, j: (0, 0))
    w_specs = [
        pl.BlockSpec((D_MODEL, th), lambda i, j: (0, j)),
        pl.BlockSpec((D_MODEL, th), lambda i, j: (0, j)),
        pl.BlockSpec((th, D_MODEL), lambda i, j: (j, 0)),
    ]
    out_specs = [pl.BlockSpec((tm, D_MODEL), lambda i, j: (i, 0))]
    out_shape = [jax.ShapeDtypeStruct((n, D_MODEL), F32)]
    if emit_weights:
        out_specs += w_specs
        out_shape += [jax.ShapeDtypeStruct(w.shape, BF16) for w in (wg, wu, wd)]
    outs = pl.pallas_call(
        functools.partial(_ffn_ln_kernel, row0=row0),
        grid=(n // tm, FFN_HIDDEN // th),
        in_specs=[pl.BlockSpec((tm, D_MODEL), lambda i, j: (row0 // tm + i, 0)), pl.BlockSpec(memory_space=pl.ANY)]
        + w_specs + [vec, vec],
        out_specs=out_specs,
        out_shape=out_shape,
        scratch_shapes=[pltpu.VMEM((tm, D_MODEL), F32), pltpu.SemaphoreType.DMA(())],
        compiler_params=pltpu.CompilerParams(dimension_semantics=("arbitrary", "arbitrary"),
                                             vmem_limit_bytes=VMEM_LIMIT_LARGE),
        name="ffn_ln",
    )(hb, h, wg, wu, wd, g, b)
    return tuple(outs) if emit_weights else outs[0]


def _position_tables(seq, t, past_len, sample_rows):
    r = jnp.arange(seq + sample_rows + RET_CHUNK)
    meta_r = r - seq - sample_rows
    pos = jnp.where(r < seq, N_META + r,
                    jnp.where(meta_r < 0, past_len + (r - seq) % t,
                              jnp.maximum(meta_r - (RET_CHUNK - N_META), 0))).astype(F32)[:, None]
    ret_freq = jnp.power(RET_THETA, -jnp.linspace(0.0, 1.0, RET_DK // 2, dtype=F32))[None, :]
    a = (N_META + RET_CHUNK * jnp.arange(seq // RET_CHUNK)).astype(F32)[:, None] * ret_freq
    b = jnp.arange(RET_CHUNK).astype(F32)[:, None] * ret_freq
    ca, sa, cb, sb = jnp.cos(a)[:, None, :], jnp.sin(a)[:, None, :], jnp.cos(b)[None], jnp.sin(b)[None]
    rest_ang = pos[seq:] * ret_freq
    ret_cos = jnp.concatenate([(ca * cb - sa * sb).reshape(seq, LANES), jnp.cos(rest_ang)], axis=0)
    ret_sin = jnp.concatenate([(sa * cb + ca * sb).reshape(seq, LANES), jnp.sin(rest_ang)], axis=0)
    half = ROT_DIM // 2
    swa_freq = jnp.power(ROPE_THETA, -jnp.arange(0, ROT_DIM, 2, dtype=F32) / ROT_DIM)
    swa_ang = pos * swa_freq[None, :]
    cos = jnp.tile(jnp.cos(swa_ang), (1, LANES // half))
    sin = jnp.tile(jnp.sin(swa_ang), (1, LANES // half))
    d = jnp.arange(LANES) % SWA_HD
    c = jnp.where(d < ROT_DIM, cos, 1.0)
    s1 = jnp.where((d >= half) & (d < ROT_DIM), sin, 0.0)
    s2 = jnp.where(d < half, -sin, 0.0)
    return jnp.concatenate([ret_cos, ret_sin, c, s1, s2], axis=1)


def kernel(x_prompt, x_sample, state_ret, cache_swa_k, cache_swa_v, meta_tokens, ln_emb_g, ln_emb_b,
           w_in, w_out, swa_sinks, ln_mix_g, ln_mix_b, w_ffn_gate, w_ffn_up, w_ffn_down, ln_ffn_g, ln_ffn_b):
    bsz, seq, d = x_prompt.shape
    dbs, t, _ = x_sample.shape
    assert w_in.shape[0] == DEPTH and d == D_MODEL and seq % RET_CHUNK == 0 and t & (t - 1) == 0
    past_len = 16384
    row = lambda a: a.reshape(1, -1)

    eg, eb = row(ln_emb_g), row(ln_emb_b)
    sinks = swa_sinks[0]
    lg = jnp.log(1.0 - jnp.power(2.0, -5.0 - jnp.arange(RET_HEADS, dtype=F32)))

    xp = x_prompt.reshape(bsz * seq, d)
    xs = x_sample.reshape(dbs * t, d)

    tabs = _position_tables(seq, t, past_len, LN_PROJ_TILE)
    z, zm = _ln_proj(xp, xs, meta_tokens, eg, eb, w_in[0], tabs, seq, LN_PROJ_TILE, PROJ_W)
    sample_row = bsz * seq

    ret_o_p, ret_state_p = _ret_prompt(lg, z, zm, bsz, seq)
    swa_o_p, k_p, v_p, ret_o_s, ret_state_s = _swa_prompt_ret_sample(
        sinks, z, zm, bsz, seq, lg, sample_row, state_ret[0], t)
    to_kernel = lambda a: jnp.transpose(a[0], (0, 2, 3, 1)).reshape(-1, LANES, WINDOW)
    from_kernel = lambda a: jnp.transpose(a.reshape(-1, SWA_KV_HEADS, SWA_HD, WINDOW), (0, 3, 1, 2))[None]
    kc, vc = to_kernel(cache_swa_k), to_kernel(cache_swa_v)
    swa_o_s, k_s, v_s = _swa_sample(sinks, z, sample_row, kc, vc, SWA_SAMPLE_BATCHES, t)

    mg, mb = row(ln_mix_g[0]), row(ln_mix_b[0])
    fg, fb = row(ln_ffn_g[0]), row(ln_ffn_b[0])
    ffn_w = (w_ffn_gate[0], w_ffn_up[0], w_ffn_down[0])
    h, hb = _out_proj_ln(ret_o_p, swa_o_p, xp, ret_o_s, swa_o_s, xs, eg, eb, w_out[0], mg, mb, 512)
    y_s, *ffn_wb = _ffn_ln(h, hb, sample_row, dbs * t, *ffn_w, fg, fb, TOK_TILE, 256)
    y_p = _ffn_ln(h, hb, 0, bsz * seq, *ffn_wb, fg, fb, TOK_TILE, 512)

    return (y_p.reshape(bsz, seq, d), y_s.reshape(dbs, t, d),
            ret_state_p[None], from_kernel(k_p), from_kernel(v_p),
            ret_state_s[None], from_kernel(k_s), from_kernel(v_s))
```
